```python
import jax, jax.numpy as jnp
from jax import lax
import numpy as np

D_MODEL = 1024
BATCH = 32
SEQ = 256
DEPTH = 4
DEC_BATCH = 8
DEC_SEQ = 4096
PAST_LEN = 512

N_HEADS = 16
N_KV_HEADS = 4
HEAD_DIM = D_MODEL // N_HEADS
KV_GROUP = N_HEADS // N_KV_HEADS
D_FF = 2816
GRID_W = 64
Q_BLOCK = 128
WINDOW = 128
ROPE_FREQS = HEAD_DIM // 4
ROPE_THETA = 10000.0
N_MOD = 9
N_MIXERS = 2
ATTN_SCALE = HEAD_DIM ** -0.5
EPS = 1e-6
NEG_INF = -1e30

kernel_name = "hybrid_dit_prefix_gqa_swa_macaron_step"


def _rmsnorm(x, g):
    xf = x.astype(jnp.float32)
    y = xf * lax.rsqrt(jnp.mean(xf * xf, axis=-1, keepdims=True) + EPS)
    return (y * g.astype(jnp.float32)).astype(x.dtype)


def _modulation(cond, w, b):
    m = jax.nn.silu(cond) @ w + b
    return jnp.split(m[:, None, :], N_MOD, axis=-1)


def _modln(h, g, shift, scale):
    return _rmsnorm(h, g) * (1 + scale) + shift


def _swiglu(x, w_in, w_out):
    gate, up = jnp.split(x @ w_in, 2, axis=-1)
    return (jax.nn.silu(gate) * up) @ w_out


def _qkv(a, w_qkv, qg, kg):
    b, l, _ = a.shape
    proj = a @ w_qkv
    q, k, v = jnp.split(proj, [N_HEADS * HEAD_DIM, (N_HEADS + N_KV_HEADS) * HEAD_DIM], axis=-1)
    q = _rmsnorm(q.reshape(b, l, N_HEADS, HEAD_DIM), qg)
    k = _rmsnorm(k.reshape(b, l, N_KV_HEADS, HEAD_DIM), kg)
    v = v.reshape(b, l, N_KV_HEADS, HEAD_DIM)
    return q, k, v


def _axial_rope(x):
    n_tok = x.shape[1]
    rows = n_tok // GRID_W
    row = jnp.repeat(jnp.arange(rows), GRID_W).astype(jnp.float32)
    col = jnp.tile(jnp.arange(GRID_W), rows).astype(jnp.float32)
    freqs = 1.0 / jnp.power(ROPE_THETA, jnp.arange(ROPE_FREQS, dtype=jnp.float32) / ROPE_FREQS)
    ang = jnp.stack([row[:, None] * freqs, col[:, None] * freqs], axis=1)
    cos = jnp.cos(ang)[:, None]
    sin = jnp.sin(ang)[:, None]
    xf = x.astype(jnp.float32).reshape(x.shape[:-1] + (2, 2, ROPE_FREQS))
    x1, x2 = xf[..., 0, :], xf[..., 1, :]
    out = jnp.stack([x1 * cos - x2 * sin, x2 * cos + x1 * sin], axis=-2)
    return out.reshape(x.shape).astype(x.dtype)


def _softmax_sink(s, sink):
    if sink is None:
        return jax.nn.softmax(s, axis=-1)
    sk = sink.astype(jnp.float32).reshape(N_KV_HEADS, KV_GROUP)[None, :, :, None, None]
    m = jnp.maximum(jnp.max(s, axis=-1, keepdims=True), sk)
    e = jnp.exp(s - m)
    return e / (jnp.sum(e, axis=-1, keepdims=True) + jnp.exp(sk - m))


def _attend_blocks(q, k, v, sink):
    b, lq, h, d = q.shape
    nb = lq // Q_BLOCK
    qb = q.reshape(b, nb, Q_BLOCK, N_KV_HEADS, KV_GROUP, d).transpose(1, 0, 2, 3, 4, 5)

    def one(qi):
        s = jnp.einsum("bqkgd,bskd->bkgqs", qi, k, preferred_element_type=jnp.float32) * ATTN_SCALE
        p = _softmax_sink(s, sink)
        o = jnp.einsum("bkgqs,bskd->bqkgd", p.astype(v.dtype), v)
        return o.reshape(b, Q_BLOCK, h, d)

    o = lax.map(one, qb)
    return o.transpose(1, 0, 2, 3, 4).reshape(b, lq, h, d)


def _attend_window(q, k_lat, v_lat, k_ctx, v_ctx, sink):
    b, l, h, d = q.shape
    nb = l // Q_BLOCK
    band = Q_BLOCK + 2 * WINDOW
    c_len = k_ctx.shape[1]
    pad = ((0, 0), (WINDOW, WINDOW), (0, 0), (0, 0))
    kp = jnp.pad(k_lat, pad)
    vp = jnp.pad(v_lat, pad)
    qb = q.reshape(b, nb, Q_BLOCK, N_KV_HEADS, KV_GROUP, d).transpose(1, 0, 2, 3, 4, 5)

    def one(args):
        qi, bi = args
        start = bi * Q_BLOCK
        kb = lax.dynamic_slice_in_dim(kp, start, band, axis=1)
        vb = lax.dynamic_slice_in_dim(vp, start, band, axis=1)
        s_band = jnp.einsum("bqkgd,bskd->bkgqs", qi, kb, preferred_element_type=jnp.float32) * ATTN_SCALE
        qpos = start + jnp.arange(Q_BLOCK)
        kpos = start - WINDOW + jnp.arange(band)
        valid = (jnp.abs(qpos[:, None] - kpos[None, :]) <= WINDOW) & (kpos >= 0)[None, :] & (kpos < l)[None, :]
        s_band = jnp.where(valid, s_band, NEG_INF)
        s_ctx = jnp.einsum("bqkgd,bskd->bkgqs", qi, k_ctx, preferred_element_type=jnp.float32) * ATTN_SCALE
        p = _softmax_sink(jnp.concatenate([s_ctx, s_band], axis=-1), sink).astype(v_lat.dtype)
        o = (jnp.einsum("bkgqs,bskd->bqkgd", p[..., :c_len], v_ctx)
             + jnp.einsum("bkgqs,bskd->bqkgd", p[..., c_len:], vb))
        return o.reshape(b, Q_BLOCK, h, d)

    o = lax.map(one, (qb, jnp.arange(nb)))
    return o.transpose(1, 0, 2, 3, 4).reshape(b, l, h, d)


def _macaron_layer(h, mods, norm_g_i, w_in_i, w_out_i, mix):
    sh1, sc1, g1, sh2, sc2, g2, sh3, sc3, g3 = mods
    h = h + 0.5 * g1 * _swiglu(_modln(h, norm_g_i[0], sh1, sc1), w_in_i[0], w_out_i[0])
    out, aux = mix(_modln(h, norm_g_i[1], sh2, sc2))
    h = h + g2 * out
    h = h + 0.5 * g3 * _swiglu(_modln(h, norm_g_i[2], sh3, sc3), w_in_i[1], w_out_i[1])
    return h, aux


def setup_inputs(seed: int = 0) -> dict:
    key = jax.random.key(seed)
    ks = jax.random.split(key, 16)
    f32 = jnp.float32

    def nrm(k, shape, scale):
        return jax.random.normal(k, shape, f32) * scale

    n_win = DEPTH // N_MIXERS
    qkv_cols = (N_HEADS + 2 * N_KV_HEADS) * HEAD_DIM
    kv_shape = (DEC_BATCH, DEPTH, PAST_LEN, N_KV_HEADS, HEAD_DIM)
    return {
        "x_prompt": nrm(ks[0], (BATCH, SEQ, D_MODEL), 1.0),
        "x_sample": nrm(ks[1], (DEC_BATCH, DEC_SEQ, D_MODEL), 1.0),
        "cache_k": nrm(ks[2], kv_shape, 1.0),
        "cache_v": nrm(ks[3], kv_shape, 1.0),
        "c": nrm(ks[4], (DEC_BATCH, D_MODEL), 1.0),
        "c_ctx": nrm(ks[5], (D_MODEL,), 1.0),
        "w_mod": nrm(ks[6], (DEPTH, D_MODEL, N_MOD * D_MODEL), 0.5 * D_MODEL ** -0.5),
        "b_mod": nrm(ks[7], (DEPTH, N_MOD * D_MODEL), 0.01),
        "norm_g": 1.0 + nrm(ks[8], (DEPTH, 3, D_MODEL), 0.02),
        "w_qkv": nrm(ks[9], (DEPTH, D_MODEL, qkv_cols), D_MODEL ** -0.5),
        "w_o": nrm(ks[10], (DEPTH, N_HEADS * HEAD_DIM, D_MODEL), (N_HEADS * HEAD_DIM) ** -0.5),
        "q_norm_g": 1.0 + nrm(ks[11], (DEPTH, HEAD_DIM), 0.02),
        "k_norm_g": 1.0 + nrm(ks[12], (DEPTH, HEAD_DIM), 0.02),
        "sink": nrm(ks[13], (n_win, N_HEADS), 0.5),
        "w_ffn_in": nrm(ks[14], (DEPTH, 2, D_MODEL, 2 * D_FF), D_MODEL ** -0.5),
        "w_ffn_out": nrm(ks[15], (DEPTH, 2, D_FF, D_MODEL), D_FF ** -0.5),
    }


def reference(x_prompt, x_sample, cache_k, cache_v, c, c_ctx, w_mod, b_mod, norm_g, w_qkv, w_o,
              q_norm_g, k_norm_g, sink, w_ffn_in, w_ffn_out):
    h = x_prompt
    bp, lp, _ = x_prompt.shape
    new_ks, new_vs = [], []
    for i in range(DEPTH):
        sink_i = sink[i // N_MIXERS] if i % N_MIXERS == 1 else None
        mods = _modulation(c_ctx[None, :], w_mod[i], b_mod[i])

        def mix_ctx(a, i=i, sink_i=sink_i):
            q, k, v = _qkv(a, w_qkv[i], q_norm_g[i], k_norm_g[i])
            o = _attend_blocks(q, k, v, sink_i)
            return o.reshape(bp, lp, N_HEADS * HEAD_DIM) @ w_o[i], (k, v)

        h, (k_i, v_i) = _macaron_layer(h, mods, norm_g[i], w_ffn_in[i], w_ffn_out[i], mix_ctx)
        new_ks.append(k_i)
        new_vs.append(v_i)
    y_prompt = h
    new_k = jnp.stack(new_ks, axis=1)
    new_v = jnp.stack(new_vs, axis=1)

    h = x_sample
    bs, ls, _ = x_sample.shape
    for i in range(DEPTH):
        mods = _modulation(c, w_mod[i], b_mod[i])
        ck = cache_k[:, i]
        cv = cache_v[:, i]

        def mix_lat(a, i=i, ck=ck, cv=cv):
            q, k, v = _qkv(a, w_qkv[i], q_norm_g[i], k_norm_g[i])
            q = _axial_rope(q)
            k = _axial_rope(k)
            if i % N_MIXERS == 0:
                o = _attend_blocks(q, jnp.concatenate([ck, k], axis=1), jnp.concatenate([cv, v], axis=1), None)
            else:
                o = _attend_window(q, k, v, ck, cv, sink[i // N_MIXERS])
            return o.reshape(bs, ls, N_HEADS * HEAD_DIM) @ w_o[i], None

        h, _ = _macaron_layer(h, mods, norm_g[i], w_ffn_in[i], w_ffn_out[i], mix_lat)
    y_sample = h

    return (y_prompt, y_sample, new_k, new_v)
```

```python
import functools

import jax
import jax.numpy as jnp
from jax import lax
from jax.experimental import pallas as pl
from jax.experimental.pallas import tpu as pltpu

D_MODEL = 1024
N_HEADS = 16
N_KV_HEADS = 4
HEAD_DIM = 64
KV_GROUP = N_HEADS // N_KV_HEADS
GROUP_COLS = KV_GROUP * HEAD_DIM
KV_COLS = N_KV_HEADS * HEAD_DIM
QKV_COLS = (N_HEADS + 2 * N_KV_HEADS) * HEAD_DIM
D_FF = 2816
N_MOD = 9
GRID_W = 64
WINDOW = 128
ROPE_FREQS = HEAD_DIM // 4
ROPE_THETA = 10000.0
ATTN_SCALE = HEAD_DIM ** -0.5
EPS = 1e-6
NEG_INF = -1e30

FF_CHUNK = 256
N_FF_CHUNKS = D_FF // FF_CHUNK
Q_TILE = 256
KV_TILE = 512
VMEM_LIMIT_BYTES = 52 * 1024 * 1024

_BF16 = jnp.bfloat16
_F32 = jnp.float32


def _params(n_axes):
    return pltpu.CompilerParams(dimension_semantics=("arbitrary",) * n_axes,
                                vmem_limit_bytes=VMEM_LIMIT_BYTES)


def _resident(block_shape, index_map):
    return pl.BlockSpec(block_shape, index_map, pipeline_mode=pl.Buffered(1))


def _silu(x):
    return x * jax.nn.sigmoid(x)


def _modln(x, g, shift, scale):
    y = x * lax.rsqrt(jnp.mean(x * x, axis=-1, keepdims=True) + EPS)
    return (y * g) * (1.0 + scale) + shift


def _mod_body(cond_ref, w_ref, b_ref, o_ref):
    a = _silu(cond_ref[...]).astype(_BF16)
    w = w_ref[0].astype(_BF16)
    o_ref[0] = jnp.dot(a, w, preferred_element_type=_F32) + b_ref[0]


def _modulation(cond, w_mod, b_mod):
    depth, d, n = w_mod.shape
    r = cond.shape[0]
    tn = n // 8
    out = pl.pallas_call(
        _mod_body,
        grid=(depth, n // tn),
        in_specs=[
            pl.BlockSpec((r, d), lambda i, j: (0, 0)),
            pl.BlockSpec((1, d, tn), lambda i, j: (i, 0, j)),
            pl.BlockSpec((1, 1, tn), lambda i, j: (i, 0, j)),
        ],
        out_specs=pl.BlockSpec((1, r, tn), lambda i, j: (i, 0, j)),
        out_shape=jax.ShapeDtypeStruct((depth, r, n), _F32),
        compiler_params=_params(2),
        name="modulation",
    )(cond, w_mod, b_mod.reshape(depth, 1, n))
    return out.reshape(depth, r, N_MOD, d)


def _ffn_body(*refs, sub, has_proj):
    if has_proj:
        x_ref, mods_ref, ng_ref, o_ref, wo_ref, win_ref, wout_ref, out_ref, xn_ref, acc_ref = refs
    else:
        x_ref, mods_ref, ng_ref, win_ref, wout_ref, out_ref, xn_ref, acc_ref = refs
    m = mods_ref[0, 0]
    x = x_ref[...]
    if has_proj:
        x = x + m[5:6] * jnp.dot(o_ref[...], wo_ref[0], preferred_element_type=_F32)
    k = 3 * sub
    xn_ref[...] = _modln(x, ng_ref[0, sub:sub + 1], m[k:k + 1], m[k + 1:k + 2]).astype(_BF16)
    out_ref[...] = x
    acc_ref[...] = jnp.zeros_like(acc_ref)

    def chunk(c, carry):
        xn = xn_ref[...]
        gate = jnp.dot(xn, win_ref[0, 0, c], preferred_element_type=_F32)
        up = jnp.dot(xn, win_ref[0, 0, c + N_FF_CHUNKS], preferred_element_type=_F32)
        act = (_silu(gate) * up).astype(_BF16)
        acc_ref[...] += jnp.dot(act, wout_ref[0, 0, c], preferred_element_type=_F32)
        return carry

    lax.fori_loop(0, N_FF_CHUNKS, chunk, 0)
    out_ref[...] = out_ref[...] + (0.5 * m[k + 2:k + 3]) * acc_ref[...]


def _ffn(x, mods, norm_g, w_in, w_out, layer, which, row_of_tile, tm, o=None, w_o=None):
    t, d = x.shape
    sub = 2 * which
    has_proj = o is not None
    in_specs = [
        pl.BlockSpec((tm, d), lambda i: (i, 0)),
        pl.BlockSpec((1, 1, N_MOD, d), lambda i: (layer, row_of_tile(i), 0, 0)),
        pl.BlockSpec((1, 3, d), lambda i: (layer, 0, 0)),
    ]
    args = [x, mods, norm_g]
    if has_proj:
        in_specs += [pl.BlockSpec((tm, d), lambda i: (i, 0)),
                     _resident((1, d, d), lambda i: (layer, 0, 0))]
        args += [o, w_o]
    in_specs += [
        _resident((1, 1, 2 * N_FF_CHUNKS, d, FF_CHUNK), lambda i: (layer, which, 0, 0, 0)),
        _resident((1, 1, N_FF_CHUNKS, FF_CHUNK, d), lambda i: (layer, which, 0, 0, 0)),
    ]
    args += [w_in, w_out]
    return pl.pallas_call(
        functools.partial(_ffn_body, sub=sub, has_proj=has_proj),
        grid=(t // tm,),
        in_specs=in_specs,
        out_specs=pl.BlockSpec((tm, d), lambda i: (i, 0)),
        out_shape=jax.ShapeDtypeStruct((t, d), _F32),
        scratch_shapes=[pltpu.VMEM((tm, d), _BF16), pltpu.VMEM((tm, d), _F32)],
        compiler_params=_params(1),
        name="ffn_proj" if has_proj else "ffn",
    )(*args)


def _swap16(y):
    n = y.shape[-1]
    lane = lax.broadcasted_iota(jnp.int32, y.shape, y.ndim - 1)
    return jnp.where((lane & 16) == 0, pltpu.roll(y, n - 16, y.ndim - 1), pltpu.roll(y, 16, y.ndim - 1))


def _qkv_body(*refs, rope):
    if rope:
        x_ref, mods_ref, ng_ref, w_ref, ones_ref, qg_ref, kg_ref, cos_ref, sin_ref, q_ref, k_ref, v_ref = refs
    else:
        x_ref, mods_ref, ng_ref, w_ref, ones_ref, qg_ref, kg_ref, q_ref, k_ref, v_ref = refs
    m = mods_ref[0, 0]
    a = _modln(x_ref[...], ng_ref[0, 1:2], m[3:4], m[4:5]).astype(_BF16)
    proj = jnp.dot(a, w_ref[0], preferred_element_type=_F32)
    ones = ones_ref[...]
    n_q_chunks = N_HEADS * HEAD_DIM // GROUP_COLS
    for c in range(n_q_chunks + 1):
        y = proj[:, c * GROUP_COLS:(c + 1) * GROUP_COLS]
        y2 = y * y
        hi = y2.astype(_BF16)
        lo = (y2 - hi.astype(_F32)).astype(_BF16)
        ss = (jnp.dot(hi, ones, preferred_element_type=_F32)
              + jnp.dot(lo, ones, preferred_element_type=_F32))
        g = qg_ref[0] if c < n_q_chunks else kg_ref[0]
        yn = (y * lax.rsqrt(ss * (1.0 / HEAD_DIM) + EPS)) * g
        if rope:
            yn = yn * cos_ref[...] + _swap16(yn) * sin_ref[...]
        if c < n_q_chunks:
            q_ref[:, c * GROUP_COLS:(c + 1) * GROUP_COLS] = (yn * ATTN_SCALE).astype(q_ref.dtype)
        else:
            k_ref[...] = yn.astype(k_ref.dtype)
    v_ref[...] = proj[:, (N_HEADS + N_KV_HEADS) * HEAD_DIM:].astype(v_ref.dtype)


def _qkv(x, mods, norm_g, w_qkv, ones, qg, kg, layer, row_of_tile, tm, kv_dtype, rope_tabs=None):
    t, d = x.shape
    rope = rope_tabs is not None
    in_specs = [
        pl.BlockSpec((tm, d), lambda i: (i, 0)),
        pl.BlockSpec((1, 1, N_MOD, d), lambda i: (layer, row_of_tile(i), 0, 0)),
        pl.BlockSpec((1, 3, d), lambda i: (layer, 0, 0)),
        _resident((1, d, QKV_COLS), lambda i: (layer, 0, 0)),
        _resident((GROUP_COLS, GROUP_COLS), lambda i: (0, 0)),
        pl.BlockSpec((1, 1, GROUP_COLS), lambda i: (layer, 0, 0)),
        pl.BlockSpec((1, 1, GROUP_COLS), lambda i: (layer, 0, 0)),
    ]
    args = [x, mods, norm_g, w_qkv, ones, qg, kg]
    if rope:
        cos, sin = rope_tabs
        tiles_per_seq = cos.shape[0] // tm
        in_specs += [pl.BlockSpec((tm, GROUP_COLS), lambda i: (i % tiles_per_seq, 0)),
                     pl.BlockSpec((tm, GROUP_COLS), lambda i: (i % tiles_per_seq, 0))]
        args += [cos, sin]
    return pl.pallas_call(
        functools.partial(_qkv_body, rope=rope),
        grid=(t // tm,),
        in_specs=in_specs,
        out_specs=[pl.BlockSpec((tm, N_HEADS * HEAD_DIM), lambda i: (i, 0)),
                   pl.BlockSpec((tm, KV_COLS), lambda i: (i, 0)),
                   pl.BlockSpec((tm, KV_COLS), lambda i: (i, 0))],
        out_shape=[jax.ShapeDtypeStruct((t, N_HEADS * HEAD_DIM), _BF16),
                   jax.ShapeDtypeStruct((t, KV_COLS), kv_dtype),
                   jax.ShapeDtypeStruct((t, KV_COLS), kv_dtype)],
        compiler_params=_params(1),
        name="qkv_rope" if rope else "qkv",
    )(*args)


def _rope_tables(n_tok):
    pos = jnp.arange(n_tok)
    row = (pos // GRID_W).astype(_F32)
    col = (pos % GRID_W).astype(_F32)
    freqs = 1.0 / jnp.power(ROPE_THETA, jnp.arange(ROPE_FREQS, dtype=_F32) / ROPE_FREQS)
    ang_r = row[:, None] * freqs
    ang_c = col[:, None] * freqs
    cos = jnp.concatenate([jnp.cos(ang_r)] * 2 + [jnp.cos(ang_c)] * 2, axis=1)
    sin = jnp.concatenate([-jnp.sin(ang_r), jnp.sin(ang_r), -jnp.sin(ang_c), jnp.sin(ang_c)], axis=1)
    return jnp.tile(cos, (1, KV_GROUP)), jnp.tile(sin, (1, KV_GROUP))


def _stack_heads(q4, tq):
    return jnp.concatenate([q4[:, h * HEAD_DIM:(h + 1) * HEAD_DIM] for h in range(KV_GROUP)], axis=0)


def _unstack_heads(o, tq):
    return jnp.concatenate([o[h * tq:(h + 1) * tq] for h in range(KV_GROUP)], axis=1)


def _sink_rows(sink_ref, g, tq):
    return jnp.concatenate(
        [jnp.full((tq, 1), sink_ref[g * KV_GROUP + h], _F32) for h in range(KV_GROUP)], axis=0)


def _flash_body(*refs, tq, nk, has_sink):
    if has_sink:
        sink_ref, q_ref, kt_ref, v_ref, o_ref, qs_ref, m_ref, l_ref, acc_ref = refs
    else:
        q_ref, kt_ref, v_ref, o_ref, qs_ref, m_ref, l_ref, acc_ref = refs
    g = pl.program_id(1)
    ki = pl.program_id(3)

    @pl.when(ki == 0)
    def _():
        qs_ref[...] = _stack_heads(q_ref[0], tq)
        if has_sink:
            m_ref[...] = _sink_rows(sink_ref, g, tq)
            l_ref[...] = jnp.ones_like(l_ref)
        else:
            m_ref[...] = jnp.full_like(m_ref, NEG_INF)
            l_ref[...] = jnp.zeros_like(l_ref)
        acc_ref[...] = jnp.zeros_like(acc_ref)

    s = jnp.dot(qs_ref[...], kt_ref[0, 0], preferred_element_type=_F32)
    m_prev = m_ref[...]
    m_new = jnp.maximum(m_prev, jnp.max(s, axis=-1, keepdims=True))
    alpha = jnp.exp(m_prev - m_new)
    p = jnp.exp(s - m_new)
    l_ref[...] = alpha * l_ref[...] + jnp.sum(p, axis=-1, keepdims=True)
    acc_ref[...] = alpha * acc_ref[...] + jnp.dot(p.astype(_BF16), v_ref[0, 0], preferred_element_type=_F32)
    m_ref[...] = m_new

    @pl.when(ki == nk - 1)
    def _():
        o_ref[0] = _unstack_heads(acc_ref[...] / l_ref[...], tq).astype(o_ref.dtype)


def _attend_all(q, kt, v, sink, tq, tk):
    b, lq, _ = q.shape
    lk = kt.shape[-1]
    nq, nk = lq // tq, lk // tk
    has_sink = sink is not None
    in_specs = [
        pl.BlockSpec((1, tq, GROUP_COLS), lambda bi, g, qi, ki: (bi, qi, g)),
        pl.BlockSpec((1, 1, HEAD_DIM, tk), lambda bi, g, qi, ki: (bi, g, 0, ki)),
        pl.BlockSpec((1, 1, tk, HEAD_DIM), lambda bi, g, qi, ki: (bi, g, ki, 0)),
    ]
    args = [q, kt, v]
    if has_sink:
        in_specs = [pl.BlockSpec(memory_space=pltpu.SMEM)] + in_specs
        args = [sink] + args
    rows = KV_GROUP * tq
    return pl.pallas_call(
        functools.partial(_flash_body, tq=tq, nk=nk, has_sink=has_sink),
        grid=(b, N_KV_HEADS, nq, nk),
        in_specs=in_specs,
        out_specs=pl.BlockSpec((1, tq, GROUP_COLS), lambda bi, g, qi, ki: (bi, qi, g)),
        out_shape=jax.ShapeDtypeStruct(q.shape, _BF16),
        scratch_shapes=[pltpu.VMEM((rows, HEAD_DIM), _BF16), pltpu.VMEM((rows, 1), _F32),
                        pltpu.VMEM((rows, 1), _F32), pltpu.VMEM((rows, HEAD_DIM), _F32)],
        compiler_params=_params(4),
        name="attn_sink" if has_sink else "attn",
    )(*args)


def _window_body(sink_ref, q_ref, ktc_ref, vc_ref, kta_ref, ktb_ref, va_ref, vb_ref, bias_ref, o_ref, *, tq):
    g = pl.program_id(1)
    qs = _stack_heads(q_ref[0], tq)
    s_ctx = jnp.dot(qs, ktc_ref[0, 0], preferred_element_type=_F32)
    kt_band = jnp.concatenate([kta_ref[0, 0], ktb_ref[0, 0]], axis=1)
    bias = jnp.concatenate([bias_ref[0]] * KV_GROUP, axis=0)
    s_band = jnp.dot(qs, kt_band, preferred_element_type=_F32) + bias
    sink = _sink_rows(sink_ref, g, tq)
    m = jnp.maximum(jnp.maximum(jnp.max(s_ctx, axis=-1, keepdims=True),
                                jnp.max(s_band, axis=-1, keepdims=True)), sink)
    e_ctx = jnp.exp(s_ctx - m)
    e_band = jnp.exp(s_band - m)
    denom = (jnp.sum(e_ctx, axis=-1, keepdims=True) + jnp.sum(e_band, axis=-1, keepdims=True)
             + jnp.exp(sink - m))
    v_band = jnp.concatenate([va_ref[0, 0], vb_ref[0, 0]], axis=0)
    o = (jnp.dot(e_ctx.astype(_BF16), vc_ref[0, 0], preferred_element_type=_F32)
         + jnp.dot(e_band.astype(_BF16), v_band, preferred_element_type=_F32)) / denom
    o_ref[0] = _unstack_heads(o, tq).astype(o_ref.dtype)


def _window_bias(l, tq):
    nq = l // tq
    q0 = (jnp.arange(nq) * tq)[:, None, None]
    qpos = q0 + jnp.arange(tq)[None, :, None]
    kpos = q0 - WINDOW + jnp.arange(2 * tq)[None, None, :]
    valid = (jnp.abs(qpos - kpos) <= WINDOW) & (kpos >= 0) & (kpos < l)
    return jnp.where(valid, 0.0, NEG_INF).astype(_F32)


def _attend_window(q, kt, v, kt_ctx, v_ctx, sink, tq):
    assert tq >= 2 * WINDOW
    b, l, _ = q.shape
    nq = l // tq
    lc = kt_ctx.shape[-1]
    kt_pad = jnp.pad(kt, ((0, 0), (0, 0), (0, 0), (WINDOW, tq - WINDOW)))
    v_pad = jnp.pad(v, ((0, 0), (0, 0), (WINDOW, tq - WINDOW), (0, 0)))
    bias = _window_bias(l, tq)
    return pl.pallas_call(
        functools.partial(_window_body, tq=tq),
        grid=(b, N_KV_HEADS, nq),
        in_specs=[
            pl.BlockSpec(memory_space=pltpu.SMEM),
            pl.BlockSpec((1, tq, GROUP_COLS), lambda bi, g, qi: (bi, qi, g)),
            pl.BlockSpec((1, 1, HEAD_DIM, lc), lambda bi, g, qi: (bi, g, 0, 0)),
            pl.BlockSpec((1, 1, lc, HEAD_DIM), lambda bi, g, qi: (bi, g, 0, 0)),
            pl.BlockSpec((1, 1, HEAD_DIM, tq), lambda bi, g, qi: (bi, g, 0, qi)),
            pl.BlockSpec((1, 1, HEAD_DIM, tq), lambda bi, g, qi: (bi, g, 0, qi + 1)),
            pl.BlockSpec((1, 1, tq, HEAD_DIM), lambda bi, g, qi: (bi, g, qi, 0)),
            pl.BlockSpec((1, 1, tq, HEAD_DIM), lambda bi, g, qi: (bi, g, qi + 1, 0)),
            pl.BlockSpec((1, tq, 2 * tq), lambda bi, g, qi: (qi, 0, 0)),
        ],
        out_specs=pl.BlockSpec((1, tq, GROUP_COLS), lambda bi, g, qi: (bi, qi, g)),
        out_shape=jax.ShapeDtypeStruct(q.shape, _BF16),
        compiler_params=_params(3),
        name="attn_window",
    )(sink, q, kt_ctx, v_ctx, kt_pad, kt_pad, v_pad, v_pad, bias)


def _split_kv(k, v, b, l):
    kt = k.reshape(b, l, N_KV_HEADS, HEAD_DIM).transpose(0, 2, 3, 1).astype(_BF16)
    vv = v.reshape(b, l, N_KV_HEADS, HEAD_DIM).transpose(0, 2, 1, 3).astype(_BF16)
    return kt, vv


def _token_tile(t, cap):
    tm = min(cap, t)
    assert t % tm == 0
    return tm


def kernel(x_prompt, x_sample, cache_k, cache_v, c, c_ctx, w_mod, b_mod, norm_g, w_qkv, w_o,
           q_norm_g, k_norm_g, sink, w_ffn_in, w_ffn_out):
    depth = w_mod.shape[0]
    d = D_MODEL
    bp, lp, _ = x_prompt.shape
    bs, ls, _ = x_sample.shape
    assert lp % Q_TILE == 0 and ls % KV_TILE == 0 and cache_k.shape[2] % KV_TILE == 0

    w_in = (w_ffn_in.astype(_BF16).reshape(depth, 2, d, 2 * N_FF_CHUNKS, FF_CHUNK)
            .transpose(0, 1, 3, 2, 4))
    w_out = w_ffn_out.astype(_BF16).reshape(depth, 2, N_FF_CHUNKS, FF_CHUNK, d)
    w_qkv_b = w_qkv.astype(_BF16)
    w_o_b = w_o.astype(_BF16)
    qg = jnp.tile(q_norm_g, (1, KV_GROUP)).reshape(depth, 1, GROUP_COLS)
    kg = jnp.tile(k_norm_g, (1, N_KV_HEADS)).reshape(depth, 1, KV_COLS)
    seg = jnp.arange(GROUP_COLS) // HEAD_DIM
    ones = (seg[:, None] == seg[None, :]).astype(_BF16)

    n_rows = -(-(bs + 1) // 8) * 8
    cond = jnp.concatenate([c, c_ctx[None, :], jnp.zeros((n_rows - bs - 1, d), _F32)], axis=0)
    mods = _modulation(cond, w_mod, b_mod)

    tp = bp * lp
    tm_p = _token_tile(tp, 512)
    ctx_row = lambda i: bs
    h = x_prompt.reshape(tp, d)
    new_ks, new_vs = [], []
    for i in range(depth):
        h = _ffn(h, mods, norm_g, w_in, w_out, i, 0, ctx_row, tm_p)
        q, k, v = _qkv(h, mods, norm_g, w_qkv_b, ones, qg, kg, i, ctx_row, tm_p, _F32)
        new_ks.append(k.reshape(bp, lp, N_KV_HEADS, HEAD_DIM))
        new_vs.append(v.reshape(bp, lp, N_KV_HEADS, HEAD_DIM))
        kt, vv = _split_kv(k, v, bp, lp)
        o = _attend_all(q.reshape(bp, lp, d), kt, vv, sink[i // 2] if i % 2 == 1 else None, Q_TILE, lp)
        h = _ffn(h, mods, norm_g, w_in, w_out, i, 1, ctx_row, tm_p, o=o.reshape(tp, d), w_o=w_o_b)
    y_prompt = h.reshape(bp, lp, d)
    new_k = jnp.stack(new_ks, axis=1)
    new_v = jnp.stack(new_vs, axis=1)

    ts = bs * ls
    tm_s = _token_tile(ls, 512)
    tiles_per_seq = ls // tm_s
    lat_row = lambda i: i // tiles_per_seq
    rope_tabs = _rope_tables(ls)
    h = x_sample.reshape(ts, d)
    for i in range(depth):
        h = _ffn(h, mods, norm_g, w_in, w_out, i, 0, lat_row, tm_s)
        q, k, v = _qkv(h, mods, norm_g, w_qkv_b, ones, qg, kg, i, lat_row, tm_s, _BF16, rope_tabs)
        kt, vv = _split_kv(k, v, bs, ls)
        kt_ctx, v_ctx = _split_kv(cache_k[:, i].reshape(-1, KV_COLS), cache_v[:, i].reshape(-1, KV_COLS),
                                  bs, cache_k.shape[2])
        q = q.reshape(bs, ls, d)
        if i % 2 == 0:
            o = _attend_all(q, jnp.concatenate([kt_ctx, kt], axis=3), jnp.concatenate([v_ctx, vv], axis=2),
                            None, Q_TILE, KV_TILE)
        else:
            o = _attend_window(q, kt, vv, kt_ctx, v_ctx, sink[i // 2], Q_TILE)
        h = _ffn(h, mods, norm_g, w_in, w_out, i, 1, lat_row, tm_s, o=o.reshape(ts, d), w_o=w_o_b)
    y_sample = h.reshape(bs, ls, d)

    return (y_prompt, y_sample, new_k, new_v)
```

```python
import functools

import jax
import jax.numpy as jnp
from jax import lax
from jax.experimental import pallas as pl
from jax.experimental.pallas import tpu as pltpu

D_MODEL = 1024
N_HEADS = 16
N_KV_HEADS = 4
HEAD_DIM = 64
KV_GROUP = N_HEADS // N_KV_HEADS
GROUP_COLS = KV_GROUP * HEAD_DIM
KV_COLS = N_KV_HEADS * HEAD_DIM
QKV_COLS = (N_HEADS + 2 * N_KV_HEADS) * HEAD_DIM
D_FF = 2816
N_MOD = 9
GRID_W = 64
WINDOW = 128
ROPE_FREQS = HEAD_DIM // 4
ROPE_THETA = 10000.0
ATTN_SCALE = HEAD_DIM ** -0.5
LOG2_E = 1.4426950408889634
Q_SCALE = ATTN_SCALE * LOG2_E
EPS = 1e-6
NEG_INF = -1e30

FF_CHUNK = 256
N_FF_CHUNKS = D_FF // FF_CHUNK
Q_TILE = 256
KV_TILE = 512
VMEM_LIMIT_BYTES = 52 * 1024 * 1024

_BF16 = jnp.bfloat16
_F32 = jnp.float32


def _params(n_axes):
    return pltpu.CompilerParams(dimension_semantics=("arbitrary",) * n_axes,
                                vmem_limit_bytes=VMEM_LIMIT_BYTES)


def _resident(block_shape, index_map):
    return pl.BlockSpec(block_shape, index_map, pipeline_mode=pl.Buffered(1))


def _silu(x):
    return x * jax.nn.sigmoid(x)


def _modln(x, g, shift, scale):
    y = x * lax.rsqrt(jnp.mean(x * x, axis=-1, keepdims=True) + EPS)
    return y * (g * (1.0 + scale)) + shift


def _mod_body(cond_ref, w_ref, b_ref, o_ref):
    a = _silu(cond_ref[...]).astype(_BF16)
    w = w_ref[0].astype(_BF16)
    o_ref[0] = jnp.dot(a, w, preferred_element_type=_F32) + b_ref[0]


def _modulation(cond, w_mod, b_mod):
    depth, d, n = w_mod.shape
    r = cond.shape[0]
    tn = n // 8
    out = pl.pallas_call(
        _mod_body,
        grid=(depth, n // tn),
        in_specs=[
            pl.BlockSpec((r, d), lambda i, j: (0, 0)),
            pl.BlockSpec((1, d, tn), lambda i, j: (i, 0, j)),
            pl.BlockSpec((1, 1, tn), lambda i, j: (i, 0, j)),
        ],
        out_specs=pl.BlockSpec((1, r, tn), lambda i, j: (i, 0, j)),
        out_shape=jax.ShapeDtypeStruct((depth, r, n), _F32),
        compiler_params=_params(2),
        name="modulation",
    )(cond, w_mod, b_mod.reshape(depth, 1, n))
    return out.reshape(depth, r, N_MOD, d)


def _ffn_body(*refs, sub, has_proj):
    if has_proj:
        x_ref, mods_ref, ng_ref, o_ref, wo_ref, win_ref, wout_ref, out_ref, xn_ref, acc_ref = refs
    else:
        x_ref, mods_ref, ng_ref, win_ref, wout_ref, out_ref, xn_ref, acc_ref = refs
    m = mods_ref[0, 0]
    x = x_ref[...]
    if has_proj:
        x = x + m[5:6] * jnp.dot(o_ref[...], wo_ref[0], preferred_element_type=_F32)
    k = 3 * sub
    xn_ref[...] = _modln(x, ng_ref[0, sub:sub + 1], m[k:k + 1], m[k + 1:k + 2]).astype(_BF16)
    if has_proj:
        out_ref[...] = x

    def ffn_chunk(c):
        xn = xn_ref[...]
        gate = jnp.dot(xn, win_ref[0, 0, c], preferred_element_type=_F32)
        up = jnp.dot(xn, win_ref[0, 0, c + N_FF_CHUNKS], preferred_element_type=_F32)
        act = (_silu(gate) * up).astype(_BF16)
        return jnp.dot(act, wout_ref[0, 0, c], preferred_element_type=_F32)

    acc_ref[...] = ffn_chunk(0)

    def chunk(c, carry):
        acc_ref[...] += ffn_chunk(c)
        return carry

    lax.fori_loop(1, N_FF_CHUNKS, chunk, 0, unroll=5)
    h = out_ref[...] if has_proj else x_ref[...]
    out_ref[...] = h + (0.5 * m[k + 2:k + 3]) * acc_ref[...]


def _ffn(x, mods, norm_g, w_in, w_out, layer, which, row_of_tile, tm, o=None, w_o=None):
    t, d = x.shape
    sub = 2 * which
    has_proj = o is not None
    in_specs = [
        pl.BlockSpec((tm, d), lambda i: (i, 0)),
        pl.BlockSpec((1, 1, N_MOD, d), lambda i: (layer, row_of_tile(i), 0, 0)),
        pl.BlockSpec((1, 3, d), lambda i: (layer, 0, 0)),
    ]
    args = [x, mods, norm_g]
    if has_proj:
        in_specs += [pl.BlockSpec((tm, d), lambda i: (i, 0)),
                     _resident((1, d, d), lambda i: (layer, 0, 0))]
        args += [o, w_o]
    in_specs += [
        _resident((1, 1, 2 * N_FF_CHUNKS, d, FF_CHUNK), lambda i: (layer, which, 0, 0, 0)),
        _resident((1, 1, N_FF_CHUNKS, FF_CHUNK, d), lambda i: (layer, which, 0, 0, 0)),
    ]
    args += [w_in, w_out]
    return pl.pallas_call(
        functools.partial(_ffn_body, sub=sub, has_proj=has_proj),
        grid=(t // tm,),
        in_specs=in_specs,
        out_specs=pl.BlockSpec((tm, d), lambda i: (i, 0)),
        out_shape=jax.ShapeDtypeStruct((t, d), _F32),
        scratch_shapes=[pltpu.VMEM((tm, d), _BF16), pltpu.VMEM((tm, d), _F32)],
        compiler_params=_params(1),
        name="ffn_proj" if has_proj else "ffn",
    )(*args)


def _head_norm_rope_t(yt, g_t, rope_ref):
    ss = jnp.sum(yt * yt, axis=0, keepdims=True)
    yn = (yt * lax.rsqrt(ss * (1.0 / HEAD_DIM) + EPS)) * g_t
    if rope_ref is None:
        return yn
    f = ROPE_FREQS
    x1r, x2r, x1c, x2c = yn[0:f], yn[f:2 * f], yn[2 * f:3 * f], yn[3 * f:4 * f]
    cr, sr, cc, sc = rope_ref[0], rope_ref[1], rope_ref[2], rope_ref[3]
    return jnp.concatenate([x1r * cr - x2r * sr, x2r * cr + x1r * sr,
                            x1c * cc - x2c * sc, x2c * cc + x1c * sc], axis=0)


def _qkv_body(*refs, rope):
    if rope:
        x_ref, mods_ref, ng_ref, w_ref, qg_ref, kg_ref, rope_ref, qt_ref, k_ref, v_ref = refs
    else:
        x_ref, mods_ref, ng_ref, w_ref, qg_ref, kg_ref, qt_ref, k_ref, v_ref = refs
        rope_ref = None
    m = mods_ref[0, 0]
    a = _modln(x_ref[...], ng_ref[0, 1:2], m[3:4], m[4:5]).astype(_BF16)
    proj = jnp.dot(a, w_ref[0], preferred_element_type=_F32)
    qg, kg = qg_ref[0], kg_ref[0]
    for c in range(N_HEADS * HEAD_DIM // GROUP_COLS):
        yt = proj[:, c * GROUP_COLS:(c + 1) * GROUP_COLS].T
        for h in range(KV_GROUP):
            qn = _head_norm_rope_t(yt[h * HEAD_DIM:(h + 1) * HEAD_DIM], qg, rope_ref)
            r0 = c * GROUP_COLS + h * HEAD_DIM
            qt_ref[0, r0:r0 + HEAD_DIM, :] = (qn * Q_SCALE).astype(qt_ref.dtype)
    k0 = N_HEADS * HEAD_DIM
    kt = proj[:, k0:k0 + KV_COLS].T
    kn = jnp.concatenate([_head_norm_rope_t(kt[g * HEAD_DIM:(g + 1) * HEAD_DIM], kg, rope_ref)
                          for g in range(N_KV_HEADS)], axis=0)
    k_ref[...] = kn.T.astype(k_ref.dtype)
    v_ref[...] = proj[:, k0 + KV_COLS:].astype(v_ref.dtype)


def _qkv(x, mods, norm_g, w_qkv, qg_t, kg_t, layer, row_of_tile, tm, seq_len, kv_dtype, rope_t=None):
    t, d = x.shape
    rope = rope_t is not None
    tiles_per_seq = seq_len // tm
    in_specs = [
        pl.BlockSpec((tm, d), lambda i: (i, 0)),
        pl.BlockSpec((1, 1, N_MOD, d), lambda i: (layer, row_of_tile(i), 0, 0)),
        pl.BlockSpec((1, 3, d), lambda i: (layer, 0, 0)),
        _resident((1, d, QKV_COLS), lambda i: (layer, 0, 0)),
        pl.BlockSpec((1, HEAD_DIM, tm), lambda i: (layer, 0, 0)),
        pl.BlockSpec((1, HEAD_DIM, tm), lambda i: (layer, 0, 0)),
    ]
    args = [x, mods, norm_g, w_qkv, qg_t, kg_t]
    if rope:
        in_specs += [pl.BlockSpec((4, ROPE_FREQS, tm), lambda i: (0, 0, i % tiles_per_seq))]
        args += [rope_t]
    return pl.pallas_call(
        functools.partial(_qkv_body, rope=rope),
        grid=(t // tm,),
        in_specs=in_specs,
        out_specs=[pl.BlockSpec((1, N_HEADS * HEAD_DIM, tm), lambda i: (i // tiles_per_seq, 0, i % tiles_per_seq)),
                   pl.BlockSpec((tm, KV_COLS), lambda i: (i, 0)),
                   pl.BlockSpec((tm, KV_COLS), lambda i: (i, 0))],
        out_shape=[jax.ShapeDtypeStruct((t // seq_len, N_HEADS * HEAD_DIM, seq_len), _BF16),
                   jax.ShapeDtypeStruct((t, KV_COLS), kv_dtype),
                   jax.ShapeDtypeStruct((t, KV_COLS), kv_dtype)],
        compiler_params=_params(1),
        name="qkv_rope" if rope else "qkv",
    )(*args)


def _rope_table_t(n_tok):
    pos = jnp.arange(n_tok)
    row = (pos // GRID_W).astype(_F32)
    col = (pos % GRID_W).astype(_F32)
    freqs = 1.0 / jnp.power(ROPE_THETA, jnp.arange(ROPE_FREQS, dtype=_F32) / ROPE_FREQS)
    ang_r = row[:, None] * freqs
    ang_c = col[:, None] * freqs
    return jnp.stack([jnp.cos(ang_r).T, jnp.sin(ang_r).T, jnp.cos(ang_c).T, jnp.sin(ang_c).T])


KV_SUB = 256
SCORE_ROWS = 64
BF16_SUBLANES = 16
ACC_ROWS = HEAD_DIM + BF16_SUBLANES


def _attn_scratch(n_states, tq, tk):
    return [pltpu.VMEM((2, tk, tq), _F32),
            pltpu.VMEM((2, 1, tq), _F32),
            pltpu.VMEM((2, tk, tq), _BF16),
            pltpu.VMEM((2, 1, tq), _F32),
            pltpu.VMEM((n_states, 1, tq), _F32),
            pltpu.VMEM((n_states, ACC_ROWS, tq), _F32)]


def _init_states(scr, sinks=None):
    m_ref, acc_ref = scr[4], scr[5]
    tq = m_ref.shape[-1]
    if sinks is None:
        m_ref[...] = jnp.full(m_ref.shape, NEG_INF, _F32)
        acc_ref[...] = jnp.zeros(acc_ref.shape, _F32)
    else:
        for i, sk in enumerate(sinks):
            m_ref[i] = jnp.full((1, tq), sk * LOG2_E, _F32)
        row = lax.broadcasted_iota(jnp.int32, (ACC_ROWS, tq), 0)
        acc_ref[...] = jnp.broadcast_to(jnp.where(row == HEAD_DIM, 1.0, 0.0).astype(_F32), acc_ref.shape)


def _part(j):
    return slice(j * KV_SUB, (j + 1) * KV_SUB)


def _scores(scr, slot, j, item):
    s_ref, mc_ref = scr[0], scr[1]
    _, k, qt, _, bias = item
    kj = k[_part(j)]
    s = jnp.concatenate([jnp.dot(kj[r:r + SCORE_ROWS], qt, preferred_element_type=_F32)
                         for r in range(0, KV_SUB, SCORE_ROWS)], axis=0)
    if bias is not None:
        s = s + bias[_part(j)]
    s_ref[slot, _part(j)] = s
    mx = jnp.max(s, axis=0, keepdims=True)
    mc_ref[slot] = mx if j == 0 else jnp.maximum(mc_ref[slot], mx)


def _new_max(scr, slot, st):
    mc_ref, al_ref, m_ref = scr[1], scr[3], scr[4]
    m_old = m_ref[st]
    m_new = jnp.maximum(m_old, mc_ref[slot])
    al_ref[slot] = jnp.exp2(m_old - m_new)
    m_ref[st] = m_new


def _softmax(scr, slot, j, st):
    s_ref, p_ref, m_ref = scr[0], scr[2], scr[4]
    p_ref[slot, _part(j)] = jnp.exp2(s_ref[slot, _part(j)] - m_ref[st]).astype(_BF16)


def _values(scr, slot, j, item):
    p_ref, al_ref, acc_ref = scr[2], scr[3], scr[5]
    st, vts = item[0], item[3]
    pv = jnp.dot(vts[j], p_ref[slot, _part(j)], preferred_element_type=_F32)
    acc = acc_ref[st]
    acc_ref[st] = (al_ref[slot] * acc if j == 0 else acc) + pv


def _pipeline(scr, items):
    n = len(items)
    n_parts = len(items[0][3])
    for j in range(n_parts):
        _scores(scr, 0, j, items[0])
    for i in range(n + 1):
        if i < n:
            _new_max(scr, i % 2, items[i][0])
        for j in range(n_parts):
            if i + 1 < n:
                _scores(scr, (i + 1) % 2, j, items[i + 1])
            if i < n:
                _softmax(scr, i % 2, j, items[i][0])
            if i >= 1:
                _values(scr, (i - 1) % 2, j, items[i - 1])


def _finish(scr, first):
    acc_ref = scr[5]
    ot = jnp.concatenate([acc_ref[first + h, 0:HEAD_DIM] / acc_ref[first + h, HEAD_DIM:HEAD_DIM + 1]
                          for h in range(KV_GROUP)], axis=0)
    return ot.T


def _heads(qt_ref, g=None):
    base = 0 if g is None else g * GROUP_COLS
    return [qt_ref[0, base + h * HEAD_DIM: base + (h + 1) * HEAD_DIM, :] for h in range(KV_GROUP)]


def _full_body(qt_ref, k_ref, vt_ref, o_ref, *scr, tk):
    heads = _heads(qt_ref)
    n_sub = tk // KV_SUB
    _init_states(scr)
    items = []
    for c in range(k_ref.shape[2] // tk):
        k = k_ref[0, 0, c * tk:(c + 1) * tk, :]
        vts = [vt_ref[0, 0, c * n_sub + j] for j in range(n_sub)]
        items += [(h, k, heads[h], vts, None) for h in range(KV_GROUP)]
    _pipeline(scr, items)
    o_ref[0] = _finish(scr, 0).astype(o_ref.dtype)


def _attend_full(qt, k, vt, tq, tk):
    b, d, l = qt.shape
    lk = k.shape[2]
    return pl.pallas_call(
        functools.partial(_full_body, tk=tk),
        grid=(b, N_KV_HEADS, l // tq),
        in_specs=[
            pl.BlockSpec((1, GROUP_COLS, tq), lambda bi, g, qi: (bi, g, qi)),
            pl.BlockSpec((1, 1, lk, HEAD_DIM), lambda bi, g, qi: (bi, g, 0, 0)),
            pl.BlockSpec((1, 1, lk // KV_SUB, ACC_ROWS, KV_SUB), lambda bi, g, qi: (bi, g, 0, 0, 0)),
        ],
        out_specs=pl.BlockSpec((1, tq, GROUP_COLS), lambda bi, g, qi: (bi, qi, g)),
        out_shape=jax.ShapeDtypeStruct((b, l, d), _BF16),
        scratch_shapes=_attn_scratch(KV_GROUP, tq, tk),
        compiler_params=_params(3),
        name="attn_full",
    )(qt, k, vt)


def _window_body(sink_ref, qt_ref, kc_ref, vtc_ref, ka_ref, kb_ref, vta_ref, vtb_ref, bias_ref, o_ref, *scr):
    g = pl.program_id(1)
    heads = _heads(qt_ref)
    lc = kc_ref.shape[2]
    k_ctx = kc_ref[0, 0]
    k_band = jnp.concatenate([ka_ref[0, 0], kb_ref[0, 0]], axis=0)
    vt_band = [vta_ref[0, 0, 0], vtb_ref[0, 0, 0]]
    vt_ctx = [vtc_ref[0, 0, j] for j in range(lc // KV_SUB)]
    bias = bias_ref[0]
    _init_states(scr, [sink_ref[g * KV_GROUP + h] for h in range(KV_GROUP)])
    items = ([(h, k_ctx, heads[h], vt_ctx, None) for h in range(KV_GROUP)]
             + [(h, k_band, heads[h], vt_band, bias) for h in range(KV_GROUP)])
    _pipeline(scr, items)
    o_ref[0] = _finish(scr, 0).astype(o_ref.dtype)


def _window_bias(l, tq):
    nq = l // tq
    q0 = (jnp.arange(nq) * tq)[:, None, None]
    kpos = q0 - WINDOW + jnp.arange(2 * tq)[None, :, None]
    qpos = q0 + jnp.arange(tq)[None, None, :]
    valid = (jnp.abs(qpos - kpos) <= WINDOW) & (kpos >= 0) & (kpos < l)
    return jnp.where(valid, 0.0, NEG_INF).astype(_F32)


def _attend_window(qt, k_ctx, vt_ctx, k_pad, vt_pad, sink, tq):
    b, d, l = qt.shape
    lc = k_ctx.shape[2]
    assert tq == KV_SUB and tq >= 2 * WINDOW and lc == 2 * tq
    bias = _window_bias(l, tq)
    return pl.pallas_call(
        _window_body,
        grid=(b, N_KV_HEADS, l // tq),
        in_specs=[
            pl.BlockSpec(memory_space=pltpu.SMEM),
            pl.BlockSpec((1, GROUP_COLS, tq), lambda bi, g, qi: (bi, g, qi)),
            pl.BlockSpec((1, 1, lc, HEAD_DIM), lambda bi, g, qi: (bi, g, 0, 0)),
            pl.BlockSpec((1, 1, lc // KV_SUB, ACC_ROWS, KV_SUB), lambda bi, g, qi: (bi, g, 0, 0, 0)),
            pl.BlockSpec((1, 1, tq, HEAD_DIM), lambda bi, g, qi: (bi, g, qi, 0)),
            pl.BlockSpec((1, 1, tq, HEAD_DIM), lambda bi, g, qi: (bi, g, qi + 1, 0)),
            pl.BlockSpec((1, 1, 1, ACC_ROWS, KV_SUB), lambda bi, g, qi: (bi, g, qi, 0, 0)),
            pl.BlockSpec((1, 1, 1, ACC_ROWS, KV_SUB), lambda bi, g, qi: (bi, g, qi + 1, 0, 0)),
            pl.BlockSpec((1, 2 * tq, tq), lambda bi, g, qi: (qi, 0, 0)),
        ],
        out_specs=pl.BlockSpec((1, tq, GROUP_COLS), lambda bi, g, qi: (bi, qi, g)),
        out_shape=jax.ShapeDtypeStruct((b, l, d), _BF16),
        scratch_shapes=_attn_scratch(KV_GROUP, tq, 2 * tq),
        compiler_params=_params(3),
        name="attn_window",
    )(sink, qt, k_ctx, vt_ctx, k_pad, k_pad, vt_pad, vt_pad, bias)


def _ctx_body(*refs, has_sink):
    if has_sink:
        sink_ref, qt_ref, k_ref, vt_ref, o_ref = refs[:5]
    else:
        qt_ref, k_ref, vt_ref, o_ref = refs[:4]
    scr = refs[5:] if has_sink else refs[4:]
    n_sub = k_ref.shape[2] // KV_SUB
    _init_states(scr, [sink_ref[i] for i in range(N_HEADS)] if has_sink else None)
    items = []
    for g in range(N_KV_HEADS):
        heads = _heads(qt_ref, g)
        k = k_ref[0, g]
        vts = [vt_ref[0, g, j] for j in range(n_sub)]
        items += [(g * KV_GROUP + h, k, heads[h], vts, None) for h in range(KV_GROUP)]
    _pipeline(scr, items)
    for g in range(N_KV_HEADS):
        o_ref[0, :, g * GROUP_COLS:(g + 1) * GROUP_COLS] = _finish(scr, g * KV_GROUP).astype(o_ref.dtype)


def _attend_ctx(qt, k, vt, sink):
    b, d, l = qt.shape
    has_sink = sink is not None
    in_specs = [
        pl.BlockSpec((1, d, l), lambda bi: (bi, 0, 0)),
        pl.BlockSpec((1, N_KV_HEADS, l, HEAD_DIM), lambda bi: (bi, 0, 0, 0)),
        pl.BlockSpec((1, N_KV_HEADS, l // KV_SUB, ACC_ROWS, KV_SUB), lambda bi: (bi, 0, 0, 0, 0)),
    ]
    args = [qt, k, vt]
    if has_sink:
        in_specs = [pl.BlockSpec(memory_space=pltpu.SMEM)] + in_specs
        args = [sink] + args
    return pl.pallas_call(
        functools.partial(_ctx_body, has_sink=has_sink),
        grid=(b,),
        in_specs=in_specs,
        out_specs=pl.BlockSpec((1, l, d), lambda bi: (bi, 0, 0)),
        out_shape=jax.ShapeDtypeStruct((b, l, d), _BF16),
        scratch_shapes=_attn_scratch(N_HEADS, l, l),
        compiler_params=_params(1),
        name="attn_ctx_sink" if has_sink else "attn_ctx",
    )(*args)


def _key_layout(k):
    return k.transpose(0, 2, 1, 3).astype(_BF16)


def _value_layout(v):
    b, l, g, hd = v.shape
    vt = v.reshape(b, l // KV_SUB, KV_SUB, g, hd).transpose(0, 3, 1, 4, 2).astype(_BF16)
    extra = jnp.zeros((ACC_ROWS - hd, KV_SUB), _BF16).at[0].set(1.0)
    return jnp.concatenate([vt, jnp.broadcast_to(extra, vt.shape[:3] + extra.shape)], axis=3)


def _token_tile(t, cap):
    tm = min(cap, t)
    assert t % tm == 0
    return tm


def kernel(x_prompt, x_sample, cache_k, cache_v, c, c_ctx, w_mod, b_mod, norm_g, w_qkv, w_o,
           q_norm_g, k_norm_g, sink, w_ffn_in, w_ffn_out):
    depth = w_mod.shape[0]
    d = D_MODEL
    bp, lp, _ = x_prompt.shape
    bs, ls, _ = x_sample.shape
    past = cache_k.shape[2]
    assert lp % KV_SUB == 0 and ls % KV_TILE == 0 and past % KV_TILE == 0

    w_in = (w_ffn_in.astype(_BF16).reshape(depth, 2, d, 2 * N_FF_CHUNKS, FF_CHUNK)
            .transpose(0, 1, 3, 2, 4))
    w_out = w_ffn_out.astype(_BF16).reshape(depth, 2, N_FF_CHUNKS, FF_CHUNK, d)
    w_qkv_b = w_qkv.astype(_BF16)
    w_o_b = w_o.astype(_BF16)

    n_rows = -(-(bs + 1) // 8) * 8
    cond = jnp.concatenate([c, c_ctx[None, :], jnp.zeros((n_rows - bs - 1, d), _F32)], axis=0)
    mods = _modulation(cond, w_mod, b_mod)

    tp = bp * lp
    tm_p = _token_tile(tp, 512)
    tq_p = _token_tile(lp, 512)
    ctx_row = lambda i: bs
    qg_p = jnp.broadcast_to(q_norm_g[:, :, None], (depth, HEAD_DIM, tq_p))
    kg_p = jnp.broadcast_to(k_norm_g[:, :, None], (depth, HEAD_DIM, tq_p))
    h = x_prompt.reshape(tp, d)
    new_ks, new_vs = [], []
    for i in range(depth):
        h = _ffn(h, mods, norm_g, w_in, w_out, i, 0, ctx_row, tm_p)
        qt, k, v = _qkv(h, mods, norm_g, w_qkv_b, qg_p, kg_p, i, ctx_row, tq_p, lp, _F32)
        k = k.reshape(bp, lp, N_KV_HEADS, HEAD_DIM)
        v = v.reshape(bp, lp, N_KV_HEADS, HEAD_DIM)
        new_ks.append(k)
        new_vs.append(v)
        o = _attend_ctx(qt, _key_layout(k), _value_layout(v), sink[i // 2] if i % 2 == 1 else None)
        h = _ffn(h, mods, norm_g, w_in, w_out, i, 1, ctx_row, tm_p, o=o.reshape(tp, d), w_o=w_o_b)
    y_prompt = h.reshape(bp, lp, d)
    new_k = jnp.stack(new_ks, axis=1)
    new_v = jnp.stack(new_vs, axis=1)

    ts = bs * ls
    tm_s = _token_tile(ls, 512)
    tiles_per_seq = ls // tm_s
    lat_row = lambda i: i // tiles_per_seq
    rope_t = _rope_table_t(ls)
    qg_s = jnp.broadcast_to(q_norm_g[:, :, None], (depth, HEAD_DIM, tm_s))
    kg_s = jnp.broadcast_to(k_norm_g[:, :, None], (depth, HEAD_DIM, tm_s))
    band_pad = ((0, 0), (WINDOW, Q_TILE - WINDOW), (0, 0), (0, 0))
    h = x_sample.reshape(ts, d)
    for i in range(depth):
        h = _ffn(h, mods, norm_g, w_in, w_out, i, 0, lat_row, tm_s)
        qt, k, v = _qkv(h, mods, norm_g, w_qkv_b, qg_s, kg_s, i, lat_row, tm_s, ls, _BF16, rope_t)
        k = k.reshape(bs, ls, N_KV_HEADS, HEAD_DIM)
        v = v.reshape(bs, ls, N_KV_HEADS, HEAD_DIM)
        k_ctx = _key_layout(cache_k[:, i])
        vt_ctx = _value_layout(cache_v[:, i])
        if i % 2 == 0:
            o = _attend_full(qt, jnp.concatenate([k_ctx, _key_layout(k)], axis=2),
                             jnp.concatenate([vt_ctx, _value_layout(v)], axis=2), Q_TILE, KV_TILE)
        else:
            o = _attend_window(qt, k_ctx, vt_ctx, _key_layout(jnp.pad(k, band_pad)),
                               _value_layout(jnp.pad(v, band_pad)), sink[i // 2], Q_TILE)
        h = _ffn(h, mods, norm_g, w_in, w_out, i, 1, lat_row, tm_s, o=o.reshape(ts, d), w_o=w_o_b)
    y_sample = h.reshape(bs, ls, d)

    return (y_prompt, y_sample, new_k, new_v)
```

```python
import functools

import jax
import jax.numpy as jnp
from jax import lax
from jax.experimental import pallas as pl
from jax.experimental.pallas import tpu as pltpu

D_MODEL = 1024
N_HEADS = 16
N_KV_HEADS = 4
HEAD_DIM = 64
KV_GROUP = N_HEADS // N_KV_HEADS
GROUP_COLS = KV_GROUP * HEAD_DIM
KV_COLS = N_KV_HEADS * HEAD_DIM
QKV_COLS = (N_HEADS + 2 * N_KV_HEADS) * HEAD_DIM
D_FF = 2816
N_MOD = 9
GRID_W = 64
WINDOW = 128
ROPE_FREQS = HEAD_DIM // 4
ROPE_THETA = 10000.0
ATTN_SCALE = HEAD_DIM ** -0.5
LOG2_E = 1.4426950408889634
Q_SCALE = ATTN_SCALE * LOG2_E
EPS = 1e-6
NEG_INF = -1e30

FF_CHUNK = 256
N_FF_CHUNKS = D_FF // FF_CHUNK
Q_TILE = 256
KV_TILE = 512
VMEM_LIMIT_BYTES = 52 * 1024 * 1024

_BF16 = jnp.bfloat16
_F32 = jnp.float32


def _params(n_axes):
    return pltpu.CompilerParams(dimension_semantics=("arbitrary",) * n_axes,
                                vmem_limit_bytes=VMEM_LIMIT_BYTES)


def _resident(block_shape, index_map):
    return pl.BlockSpec(block_shape, index_map, pipeline_mode=pl.Buffered(1))


def _silu(x):
    return x * jax.nn.sigmoid(x)


def _modln(x, g, shift, scale):
    y = x * lax.rsqrt(jnp.mean(x * x, axis=-1, keepdims=True) + EPS)
    return y * (g * (1.0 + scale)) + shift


def _mod_body(cond_ref, w_ref, b_ref, o_ref):
    a = _silu(cond_ref[...]).astype(_BF16)
    w = w_ref[0].astype(_BF16)
    o_ref[0] = jnp.dot(a, w, preferred_element_type=_F32) + b_ref[0]


def _modulation(cond, w_mod, b_mod):
    depth, d, n = w_mod.shape
    r = cond.shape[0]
    tn = n // 8
    out = pl.pallas_call(
        _mod_body,
        grid=(depth, n // tn),
        in_specs=[
            pl.BlockSpec((r, d), lambda i, j: (0, 0)),
            pl.BlockSpec((1, d, tn), lambda i, j: (i, 0, j)),
            pl.BlockSpec((1, 1, tn), lambda i, j: (i, 0, j)),
        ],
        out_specs=pl.BlockSpec((1, r, tn), lambda i, j: (i, 0, j)),
        out_shape=jax.ShapeDtypeStruct((depth, r, n), _F32),
        compiler_params=_params(2),
        name="modulation",
    )(cond, w_mod, b_mod.reshape(depth, 1, n))
    return out.reshape(depth, r, N_MOD, d)


def _ffn_body(*refs, sub, has_proj):
    if has_proj:
        x_ref, mods_ref, ng_ref, o_ref, wo_ref, win_ref, wout_ref, out_ref, xn_ref, acc_ref = refs
    else:
        x_ref, mods_ref, ng_ref, win_ref, wout_ref, out_ref, xn_ref, acc_ref = refs
    m = mods_ref[0, 0]
    x = x_ref[...]
    if has_proj:
        x = x + m[5:6] * jnp.dot(o_ref[...], wo_ref[0], preferred_element_type=_F32)
    k = 3 * sub
    xn_ref[...] = _modln(x, ng_ref[0, sub:sub + 1], m[k:k + 1], m[k + 1:k + 2]).astype(_BF16)
    if has_proj:
        out_ref[...] = x

    def ffn_chunk(c):
        xn = xn_ref[...]
        gate = jnp.dot(xn, win_ref[0, 0, c], preferred_element_type=_F32)
        up = jnp.dot(xn, win_ref[0, 0, c + N_FF_CHUNKS], preferred_element_type=_F32)
        act = (_silu(gate) * up).astype(_BF16)
        return jnp.dot(act, wout_ref[0, 0, c], preferred_element_type=_F32)

    acc_ref[...] = ffn_chunk(0)

    def chunk(c, carry):
        acc_ref[...] += ffn_chunk(c)
        return carry

    lax.fori_loop(1, N_FF_CHUNKS, chunk, 0, unroll=5)
    h = out_ref[...] if has_proj else x_ref[...]
    out_ref[...] = h + (0.5 * m[k + 2:k + 3]) * acc_ref[...]


def _ffn(x, mods, norm_g, w_in, w_out, layer, which, row_of_tile, tm, o=None, w_o=None):
    t, d = x.shape
    sub = 2 * which
    has_proj = o is not None
    in_specs = [
        pl.BlockSpec((tm, d), lambda i: (i, 0)),
        pl.BlockSpec((1, 1, N_MOD, d), lambda i: (layer, row_of_tile(i), 0, 0)),
        pl.BlockSpec((1, 3, d), lambda i: (layer, 0, 0)),
    ]
    args = [x, mods, norm_g]
    if has_proj:
        in_specs += [pl.BlockSpec((tm, d), lambda i: (i, 0)),
                     _resident((1, d, d), lambda i: (layer, 0, 0))]
        args += [o, w_o]
    in_specs += [
        _resident((1, 1, 2 * N_FF_CHUNKS, d, FF_CHUNK), lambda i: (layer, which, 0, 0, 0)),
        _resident((1, 1, N_FF_CHUNKS, FF_CHUNK, d), lambda i: (layer, which, 0, 0, 0)),
    ]
    args += [w_in, w_out]
    return pl.pallas_call(
        functools.partial(_ffn_body, sub=sub, has_proj=has_proj),
        grid=(t // tm,),
        in_specs=in_specs,
        out_specs=pl.BlockSpec((tm, d), lambda i: (i, 0)),
        out_shape=jax.ShapeDtypeStruct((t, d), _F32),
        scratch_shapes=[pltpu.VMEM((tm, d), _BF16), pltpu.VMEM((tm, d), _F32)],
        compiler_params=_params(1),
        name="ffn_proj" if has_proj else "ffn",
    )(*args)


def _head_norm_rope_t(yt, g_t, rope_ref):
    ss = jnp.sum(yt * yt, axis=0, keepdims=True)
    yn = (yt * lax.rsqrt(ss * (1.0 / HEAD_DIM) + EPS)) * g_t
    if rope_ref is None:
        return yn
    f = ROPE_FREQS
    x1r, x2r, x1c, x2c = yn[0:f], yn[f:2 * f], yn[2 * f:3 * f], yn[3 * f:4 * f]
    cr, sr, cc, sc = rope_ref[0], rope_ref[1], rope_ref[2], rope_ref[3]
    return jnp.concatenate([x1r * cr - x2r * sr, x2r * cr + x1r * sr,
                            x1c * cc - x2c * sc, x2c * cc + x1c * sc], axis=0)


def _qkv_body(*refs, rope, keep_f32):
    n_in = 7 if rope else 6
    x_ref, mods_ref, ng_ref, w_ref, qg_ref, kg_ref = refs[:6]
    rope_ref = refs[6] if rope else None
    qt_ref, ka_ref, vt_ref = refs[n_in:n_in + 3]
    m = mods_ref[0, 0]
    a = _modln(x_ref[...], ng_ref[0, 1:2], m[3:4], m[4:5]).astype(_BF16)
    proj = jnp.dot(a, w_ref[0], preferred_element_type=_F32)
    qg, kg = qg_ref[0], kg_ref[0]
    for c in range(N_HEADS * HEAD_DIM // GROUP_COLS):
        yt = proj[:, c * GROUP_COLS:(c + 1) * GROUP_COLS].T
        for h in range(KV_GROUP):
            qn = _head_norm_rope_t(yt[h * HEAD_DIM:(h + 1) * HEAD_DIM], qg, rope_ref)
            r0 = c * GROUP_COLS + h * HEAD_DIM
            qt_ref[0, r0:r0 + HEAD_DIM, :] = (qn * Q_SCALE).astype(qt_ref.dtype)
    k0 = N_HEADS * HEAD_DIM
    kt = proj[:, k0:k0 + KV_COLS].T
    kn = jnp.concatenate([_head_norm_rope_t(kt[g * HEAD_DIM:(g + 1) * HEAD_DIM], kg, rope_ref)
                          for g in range(N_KV_HEADS)], axis=0)
    k_tok = kn.T
    v_tok = proj[:, k0 + KV_COLS:]
    vt = v_tok.T
    tm = k_tok.shape[0]
    row = lax.broadcasted_iota(jnp.int32, (ACC_ROWS - HEAD_DIM, V_PIECE), 0)
    extra = jnp.where(row == 0, 1.0, 0.0).astype(vt_ref.dtype)
    for g in range(N_KV_HEADS):
        ka_ref[0, g] = k_tok[:, g * HEAD_DIM:(g + 1) * HEAD_DIM].astype(ka_ref.dtype)
        for j in range(tm // V_PIECE):
            vt_ref[0, g, j, 0:HEAD_DIM, :] = (
                vt[g * HEAD_DIM:(g + 1) * HEAD_DIM, j * V_PIECE:(j + 1) * V_PIECE].astype(vt_ref.dtype))
            vt_ref[0, g, j, HEAD_DIM:ACC_ROWS, :] = extra
    if keep_f32:
        k_ref, v_ref = refs[n_in + 3:n_in + 5]
        k_ref[...] = k_tok
        v_ref[...] = v_tok


def _qkv(x, mods, norm_g, w_qkv, qg_t, kg_t, layer, row_of_tile, tm, seq_len, keep_f32, rope_t=None):
    t, d = x.shape
    rope = rope_t is not None
    tiles_per_seq = seq_len // tm
    n_seq = t // seq_len
    seq_tile = lambda i: (i // tiles_per_seq, 0, i % tiles_per_seq)
    out_specs = [pl.BlockSpec((1, N_HEADS * HEAD_DIM, tm), seq_tile),
                 pl.BlockSpec((1, N_KV_HEADS, tm, HEAD_DIM), lambda i: seq_tile(i) + (0,)),
                 pl.BlockSpec((1, N_KV_HEADS, tm // V_PIECE, ACC_ROWS, V_PIECE), lambda i: seq_tile(i) + (0, 0))]
    out_shape = [jax.ShapeDtypeStruct((n_seq, N_HEADS * HEAD_DIM, seq_len), _BF16),
                 jax.ShapeDtypeStruct((n_seq, N_KV_HEADS, seq_len, HEAD_DIM), _BF16),
                 jax.ShapeDtypeStruct((n_seq, N_KV_HEADS, seq_len // V_PIECE, ACC_ROWS, V_PIECE), _BF16)]
    if keep_f32:
        out_specs += [pl.BlockSpec((tm, KV_COLS), lambda i: (i, 0))] * 2
        out_shape += [jax.ShapeDtypeStruct((t, KV_COLS), _F32)] * 2
    in_specs = [
        pl.BlockSpec((tm, d), lambda i: (i, 0)),
        pl.BlockSpec((1, 1, N_MOD, d), lambda i: (layer, row_of_tile(i), 0, 0)),
        pl.BlockSpec((1, 3, d), lambda i: (layer, 0, 0)),
        _resident((1, d, QKV_COLS), lambda i: (layer, 0, 0)),
        pl.BlockSpec((1, HEAD_DIM, tm), lambda i: (layer, 0, 0)),
        pl.BlockSpec((1, HEAD_DIM, tm), lambda i: (layer, 0, 0)),
    ]
    args = [x, mods, norm_g, w_qkv, qg_t, kg_t]
    if rope:
        in_specs += [pl.BlockSpec((4, ROPE_FREQS, tm), lambda i: (0, 0, i % tiles_per_seq))]
        args += [rope_t]
    return pl.pallas_call(
        functools.partial(_qkv_body, rope=rope, keep_f32=keep_f32),
        grid=(t // tm,),
        in_specs=in_specs,
        out_specs=out_specs,
        out_shape=out_shape,
        compiler_params=_params(1),
        name="qkv_rope" if rope else "qkv",
    )(*args)


def _rope_table_t(n_tok):
    pos = jnp.arange(n_tok)
    row = (pos // GRID_W).astype(_F32)
    col = (pos % GRID_W).astype(_F32)
    freqs = 1.0 / jnp.power(ROPE_THETA, jnp.arange(ROPE_FREQS, dtype=_F32) / ROPE_FREQS)
    ang_r = row[:, None] * freqs
    ang_c = col[:, None] * freqs
    return jnp.stack([jnp.cos(ang_r).T, jnp.sin(ang_r).T, jnp.cos(ang_c).T, jnp.sin(ang_c).T])


KV_SUB = 256
V_PIECE = 128
SCORE_ROWS = 64
BF16_SUBLANES = 16
ACC_ROWS = HEAD_DIM + BF16_SUBLANES


def _attn_scratch(n_states, tq, tk):
    return [pltpu.VMEM((2, tk, tq), _F32),
            pltpu.VMEM((2, 1, tq), _F32),
            pltpu.VMEM((2, tk, tq), _BF16),
            pltpu.VMEM((2, 1, tq), _F32),
            pltpu.VMEM((n_states, 1, tq), _F32),
            pltpu.VMEM((n_states, ACC_ROWS, tq), _F32)]


def _init_states(scr, sinks=None):
    m_ref, acc_ref = scr[4], scr[5]
    tq = m_ref.shape[-1]
    if sinks is None:
        m_ref[...] = jnp.full(m_ref.shape, NEG_INF, _F32)
        acc_ref[...] = jnp.zeros(acc_ref.shape, _F32)
    else:
        for i, sk in enumerate(sinks):
            m_ref[i] = jnp.full((1, tq), sk * LOG2_E, _F32)
        row = lax.broadcasted_iota(jnp.int32, (ACC_ROWS, tq), 0)
        acc_ref[...] = jnp.broadcast_to(jnp.where(row == HEAD_DIM, 1.0, 0.0).astype(_F32), acc_ref.shape)


def _part(j):
    return slice(j * KV_SUB, (j + 1) * KV_SUB)


def _scores(scr, slot, j, item):
    s_ref, mc_ref = scr[0], scr[1]
    _, k, qt, _, bias = item
    kj = k[_part(j)]
    s = jnp.concatenate([jnp.dot(kj[r:r + SCORE_ROWS], qt, preferred_element_type=_F32)
                         for r in range(0, KV_SUB, SCORE_ROWS)], axis=0)
    if bias is not None:
        s = s + bias[_part(j)]
    s_ref[slot, _part(j)] = s
    mx = jnp.max(s, axis=0, keepdims=True)
    mc_ref[slot] = mx if j == 0 else jnp.maximum(mc_ref[slot], mx)


def _new_max(scr, slot, st):
    mc_ref, al_ref, m_ref = scr[1], scr[3], scr[4]
    m_old = m_ref[st]
    m_new = jnp.maximum(m_old, mc_ref[slot])
    al_ref[slot] = jnp.exp2(m_old - m_new)
    m_ref[st] = m_new


def _softmax(scr, slot, j, st):
    s_ref, p_ref, m_ref = scr[0], scr[2], scr[4]
    p_ref[slot, _part(j)] = jnp.exp2(s_ref[slot, _part(j)] - m_ref[st]).astype(_BF16)


def _values(scr, slot, j, item):
    p_ref, al_ref, acc_ref = scr[2], scr[3], scr[5]
    st, vts = item[0], item[3]
    pv = jnp.dot(vts[j], p_ref[slot, _part(j)], preferred_element_type=_F32)
    acc = acc_ref[st]
    acc_ref[st] = (al_ref[slot] * acc if j == 0 else acc) + pv


def _pipeline(scr, items):
    n = len(items)
    n_parts = len(items[0][3])
    for j in range(n_parts):
        _scores(scr, 0, j, items[0])
    for i in range(n + 1):
        if i < n:
            _new_max(scr, i % 2, items[i][0])
        for j in range(n_parts):
            if i + 1 < n:
                _scores(scr, (i + 1) % 2, j, items[i + 1])
            if i < n:
                _softmax(scr, i % 2, j, items[i][0])
            if i >= 1:
                _values(scr, (i - 1) % 2, j, items[i - 1])


def _finish(scr, first):
    acc_ref = scr[5]
    ot = jnp.concatenate([acc_ref[first + h, 0:HEAD_DIM] / acc_ref[first + h, HEAD_DIM:HEAD_DIM + 1]
                          for h in range(KV_GROUP)], axis=0)
    return ot.T


def _heads(qt_ref, g=None):
    base = 0 if g is None else g * GROUP_COLS
    return [qt_ref[0, base + h * HEAD_DIM: base + (h + 1) * HEAD_DIM, :] for h in range(KV_GROUP)]


def _vt_part(vt_ref, lead, piece):
    return jnp.concatenate([vt_ref[lead + (piece,)], vt_ref[lead + (piece + 1,)]], axis=1)


def _chunk_items(k_ref, vt_ref, heads, tk):
    items = []
    for c in range(k_ref.shape[2] // tk):
        k = k_ref[0, 0, c * tk:(c + 1) * tk, :]
        vts = [_vt_part(vt_ref, (0, 0), (c * tk + j * KV_SUB) // V_PIECE) for j in range(tk // KV_SUB)]
        items += [(h, k, heads[h], vts, None) for h in range(KV_GROUP)]
    return items


def _full_body(qt_ref, kc_ref, vtc_ref, kl_ref, vtl_ref, o_ref, *scr, tk):
    heads = _heads(qt_ref)
    _init_states(scr)
    _pipeline(scr, _chunk_items(kc_ref, vtc_ref, heads, tk) + _chunk_items(kl_ref, vtl_ref, heads, tk))
    o_ref[0] = _finish(scr, 0).astype(o_ref.dtype)


def _kv_specs(lk, index_map):
    return [pl.BlockSpec((1, 1, lk, HEAD_DIM), lambda *i: index_map(*i) + (0, 0)),
            pl.BlockSpec((1, 1, lk // V_PIECE, ACC_ROWS, V_PIECE), lambda *i: index_map(*i) + (0, 0, 0))]


def _attend_full(qt, k_ctx, vt_ctx, k_lat, vt_lat, tq, tk):
    b, d, l = qt.shape
    by_group = lambda bi, g, qi: (bi, g)
    return pl.pallas_call(
        functools.partial(_full_body, tk=tk),
        grid=(b, N_KV_HEADS, l // tq),
        in_specs=([pl.BlockSpec((1, GROUP_COLS, tq), lambda bi, g, qi: (bi, g, qi))]
                  + _kv_specs(k_ctx.shape[2], by_group) + _kv_specs(l, by_group)),
        out_specs=pl.BlockSpec((1, tq, GROUP_COLS), lambda bi, g, qi: (bi, qi, g)),
        out_shape=jax.ShapeDtypeStruct((b, l, d), _BF16),
        scratch_shapes=_attn_scratch(KV_GROUP, tq, tk),
        compiler_params=_params(3),
        name="attn_full",
    )(qt, k_ctx, vt_ctx, k_lat, vt_lat)


def _band_start(q0, l, tq):
    return jnp.clip(q0 - WINDOW, 0, l - 2 * tq)


def _window_body(sink_ref, qt_ref, kc_ref, vtc_ref, kl_ref, vtl_ref, bias_ref, o_ref, *scr, tq):
    g = pl.program_id(1)
    heads = _heads(qt_ref)
    start = pl.multiple_of(_band_start(pl.program_id(2) * tq, kl_ref.shape[2], tq), V_PIECE)
    k_band = kl_ref[0, 0, pl.ds(start, 2 * tq), :]
    vt_band = [_vt_part(vtl_ref, (0, 0), start // V_PIECE + j * (KV_SUB // V_PIECE)) for j in range(2)]
    bias = bias_ref[0]
    _init_states(scr, [sink_ref[g * KV_GROUP + h] for h in range(KV_GROUP)])
    items = (_chunk_items(kc_ref, vtc_ref, heads, 2 * tq)
             + [(h, k_band, heads[h], vt_band, bias) for h in range(KV_GROUP)])
    _pipeline(scr, items)
    o_ref[0] = _finish(scr, 0).astype(o_ref.dtype)


def _window_bias(l, tq):
    q0 = (jnp.arange(l // tq) * tq)[:, None, None]
    kpos = _band_start(q0, l, tq) + jnp.arange(2 * tq)[None, :, None]
    qpos = q0 + jnp.arange(tq)[None, None, :]
    return jnp.where(jnp.abs(qpos - kpos) <= WINDOW, 0.0, NEG_INF).astype(_F32)


def _attend_window(qt, k_ctx, vt_ctx, k_lat, vt_lat, sink, tq):
    b, d, l = qt.shape
    lc = k_ctx.shape[2]
    assert tq == KV_SUB and tq >= 2 * WINDOW and lc == 2 * tq and l >= 2 * tq
    by_group = lambda bi, g, qi: (bi, g)
    return pl.pallas_call(
        functools.partial(_window_body, tq=tq),
        grid=(b, N_KV_HEADS, l // tq),
        in_specs=([pl.BlockSpec(memory_space=pltpu.SMEM),
                   pl.BlockSpec((1, GROUP_COLS, tq), lambda bi, g, qi: (bi, g, qi))]
                  + _kv_specs(lc, by_group) + _kv_specs(l, by_group)
                  + [pl.BlockSpec((1, 2 * tq, tq), lambda bi, g, qi: (qi, 0, 0))]),
        out_specs=pl.BlockSpec((1, tq, GROUP_COLS), lambda bi, g, qi: (bi, qi, g)),
        out_shape=jax.ShapeDtypeStruct((b, l, d), _BF16),
        scratch_shapes=_attn_scratch(KV_GROUP, tq, 2 * tq),
        compiler_params=_params(3),
        name="attn_window",
    )(sink, qt, k_ctx, vt_ctx, k_lat, vt_lat, _window_bias(l, tq))


def _ctx_body(*refs, has_sink):
    if has_sink:
        sink_ref, qt_ref, k_ref, vt_ref, o_ref = refs[:5]
    else:
        qt_ref, k_ref, vt_ref, o_ref = refs[:4]
    scr = refs[5:] if has_sink else refs[4:]
    n_sub = k_ref.shape[2] // KV_SUB
    _init_states(scr, [sink_ref[i] for i in range(N_HEADS)] if has_sink else None)
    items = []
    for g in range(N_KV_HEADS):
        heads = _heads(qt_ref, g)
        k = k_ref[0, g]
        vts = [_vt_part(vt_ref, (0, g), j * (KV_SUB // V_PIECE)) for j in range(n_sub)]
        items += [(g * KV_GROUP + h, k, heads[h], vts, None) for h in range(KV_GROUP)]
    _pipeline(scr, items)
    for g in range(N_KV_HEADS):
        o_ref[0, :, g * GROUP_COLS:(g + 1) * GROUP_COLS] = _finish(scr, g * KV_GROUP).astype(o_ref.dtype)


def _attend_ctx(qt, k, vt, sink):
    b, d, l = qt.shape
    has_sink = sink is not None
    in_specs = [
        pl.BlockSpec((1, d, l), lambda bi: (bi, 0, 0)),
        pl.BlockSpec((1, N_KV_HEADS, l, HEAD_DIM), lambda bi: (bi, 0, 0, 0)),
        pl.BlockSpec((1, N_KV_HEADS, l // V_PIECE, ACC_ROWS, V_PIECE), lambda bi: (bi, 0, 0, 0, 0)),
    ]
    args = [qt, k, vt]
    if has_sink:
        in_specs = [pl.BlockSpec(memory_space=pltpu.SMEM)] + in_specs
        args = [sink] + args
    return pl.pallas_call(
        functools.partial(_ctx_body, has_sink=has_sink),
        grid=(b,),
        in_specs=in_specs,
        out_specs=pl.BlockSpec((1, l, d), lambda bi: (bi, 0, 0)),
        out_shape=jax.ShapeDtypeStruct((b, l, d), _BF16),
        scratch_shapes=_attn_scratch(N_HEADS, l, l),
        compiler_params=_params(1),
        name="attn_ctx_sink" if has_sink else "attn_ctx",
    )(*args)


def _cache_layout(cache_k, cache_v):
    b, depth, p, g, hd = cache_v.shape
    k = cache_k.transpose(1, 0, 3, 2, 4).astype(_BF16)
    vt = cache_v.reshape(b, depth, p // V_PIECE, V_PIECE, g, hd).transpose(1, 0, 4, 2, 5, 3).astype(_BF16)
    extra = jnp.zeros((ACC_ROWS - hd, V_PIECE), _BF16).at[0].set(1.0)
    return k, jnp.concatenate([vt, jnp.broadcast_to(extra, vt.shape[:4] + extra.shape)], axis=4)


def _token_tile(t, cap):
    tm = min(cap, t)
    assert t % tm == 0
    return tm


def kernel(x_prompt, x_sample, cache_k, cache_v, c, c_ctx, w_mod, b_mod, norm_g, w_qkv, w_o,
           q_norm_g, k_norm_g, sink, w_ffn_in, w_ffn_out):
    depth = w_mod.shape[0]
    d = D_MODEL
    bp, lp, _ = x_prompt.shape
    bs, ls, _ = x_sample.shape
    past = cache_k.shape[2]
    assert lp % KV_SUB == 0 and ls % KV_TILE == 0 and past % KV_TILE == 0

    w_in = (w_ffn_in.astype(_BF16).reshape(depth, 2, d, 2 * N_FF_CHUNKS, FF_CHUNK)
            .transpose(0, 1, 3, 2, 4))
    w_out = w_ffn_out.astype(_BF16).reshape(depth, 2, N_FF_CHUNKS, FF_CHUNK, d)
    w_qkv_b = w_qkv.astype(_BF16)
    w_o_b = w_o.astype(_BF16)

    n_rows = -(-(bs + 1) // 8) * 8
    cond = jnp.concatenate([c, c_ctx[None, :], jnp.zeros((n_rows - bs - 1, d), _F32)], axis=0)
    mods = _modulation(cond, w_mod, b_mod)

    tp = bp * lp
    tm_p = _token_tile(tp, 512)
    tq_p = _token_tile(lp, 512)
    ctx_row = lambda i: bs
    qg_p = jnp.broadcast_to(q_norm_g[:, :, None], (depth, HEAD_DIM, tq_p))
    kg_p = jnp.broadcast_to(k_norm_g[:, :, None], (depth, HEAD_DIM, tq_p))
    h = x_prompt.reshape(tp, d)
    new_ks, new_vs = [], []
    for i in range(depth):
        h = _ffn(h, mods, norm_g, w_in, w_out, i, 0, ctx_row, tm_p)
        qt, ka, vt, k, v = _qkv(h, mods, norm_g, w_qkv_b, qg_p, kg_p, i, ctx_row, tq_p, lp, True)
        new_ks.append(k.reshape(bp, lp, N_KV_HEADS, HEAD_DIM))
        new_vs.append(v.reshape(bp, lp, N_KV_HEADS, HEAD_DIM))
        o = _attend_ctx(qt, ka, vt, sink[i // 2] if i % 2 == 1 else None)
        h = _ffn(h, mods, norm_g, w_in, w_out, i, 1, ctx_row, tm_p, o=o.reshape(tp, d), w_o=w_o_b)
    y_prompt = h.reshape(bp, lp, d)
    new_k = jnp.stack(new_ks, axis=1)
    new_v = jnp.stack(new_vs, axis=1)

    ts = bs * ls
    tm_s = _token_tile(ls, 512)
    tiles_per_seq = ls // tm_s
    lat_row = lambda i: i // tiles_per_seq
    rope_t = _rope_table_t(ls)
    qg_s = jnp.broadcast_to(q_norm_g[:, :, None], (depth, HEAD_DIM, tm_s))
    kg_s = jnp.broadcast_to(k_norm_g[:, :, None], (depth, HEAD_DIM, tm_s))
    k_ctx, vt_ctx = _cache_layout(cache_k, cache_v)
    h = x_sample.reshape(ts, d)
    for i in range(depth):
        h = _ffn(h, mods, norm_g, w_in, w_out, i, 0, lat_row, tm_s)
        qt, ka, vt = _qkv(h, mods, norm_g, w_qkv_b, qg_s, kg_s, i, lat_row, tm_s, ls, False, rope_t)
        if i % 2 == 0:
            o = _attend_full(qt, k_ctx[i], vt_ctx[i], ka, vt, Q_TILE, KV_TILE)
        else:
            o = _attend_window(qt, k_ctx[i], vt_ctx[i], ka, vt, sink[i // 2], Q_TILE)
        h = _ffn(h, mods, norm_g, w_in, w_out, i, 1, lat_row, tm_s, o=o.reshape(ts, d), w_o=w_o_b)
    y_sample = h.reshape(bs, ls, d)

    return (y_prompt, y_sample, new_k, new_v)
```

```python
import functools

import jax
import jax.numpy as jnp
from jax import lax
from jax.experimental import pallas as pl
from jax.experimental.pallas import tpu as pltpu

D_MODEL = 1024
N_HEADS = 16
N_KV_HEADS = 4
HEAD_DIM = 64
KV_GROUP = N_HEADS // N_KV_HEADS
GROUP_COLS = KV_GROUP * HEAD_DIM
KV_COLS = N_KV_HEADS * HEAD_DIM
QKV_COLS = (N_HEADS + 2 * N_KV_HEADS) * HEAD_DIM
D_FF = 2816
N_MOD = 9
GRID_W = 64
WINDOW = 128
ROPE_FREQS = HEAD_DIM // 4
ROPE_THETA = 10000.0
ATTN_SCALE = HEAD_DIM ** -0.5
LOG2_E = 1.4426950408889634
Q_SCALE = ATTN_SCALE * LOG2_E
EPS = 1e-6
NEG_INF = -1e30

FF_CHUNK = 256
N_FF_CHUNKS = D_FF // FF_CHUNK
Q_TILE = 256
KV_TILE = 512
VMEM_LIMIT_BYTES = 52 * 1024 * 1024

_BF16 = jnp.bfloat16
_F32 = jnp.float32


def _params(n_axes):
    return pltpu.CompilerParams(dimension_semantics=("arbitrary",) * n_axes,
                                vmem_limit_bytes=VMEM_LIMIT_BYTES)


def _resident(block_shape, index_map):
    return pl.BlockSpec(block_shape, index_map, pipeline_mode=pl.Buffered(1))


def _silu(x):
    return x * jax.nn.sigmoid(x)


def _modln(x, g, shift, scale):
    y = x * lax.rsqrt(jnp.mean(x * x, axis=-1, keepdims=True) + EPS)
    return y * (g * (1.0 + scale)) + shift


def _mod_body(cond_ref, w_ref, b_ref, o_ref):
    a = _silu(cond_ref[...]).astype(_BF16)
    w = w_ref[0].astype(_BF16)
    o_ref[0] = jnp.dot(a, w, preferred_element_type=_F32) + b_ref[0]


def _modulation(cond, w_mod, b_mod):
    depth, d, n = w_mod.shape
    r = cond.shape[0]
    tn = n // 8
    out = pl.pallas_call(
        _mod_body,
        grid=(depth, n // tn),
        in_specs=[
            pl.BlockSpec((r, d), lambda i, j: (0, 0)),
            pl.BlockSpec((1, d, tn), lambda i, j: (i, 0, j)),
            pl.BlockSpec((1, 1, tn), lambda i, j: (i, 0, j)),
        ],
        out_specs=pl.BlockSpec((1, r, tn), lambda i, j: (i, 0, j)),
        out_shape=jax.ShapeDtypeStruct((depth, r, n), _F32),
        compiler_params=_params(2),
        name="modulation",
    )(cond, w_mod, b_mod.reshape(depth, 1, n))
    return out.reshape(depth, r, N_MOD, d)


def _ffn_body(*refs, sub, has_proj):
    if has_proj:
        x_ref, mods_ref, ng_ref, o_ref, wo_ref, win_ref, wout_ref, out_ref, xn_ref, acc_ref = refs
    else:
        x_ref, mods_ref, ng_ref, win_ref, wout_ref, out_ref, xn_ref, acc_ref = refs
    m = mods_ref[0, 0]
    x = x_ref[...]
    if has_proj:
        x = x + m[5:6] * jnp.dot(o_ref[...], wo_ref[0], preferred_element_type=_F32)
    k = 3 * sub
    xn_ref[...] = _modln(x, ng_ref[0, sub:sub + 1], m[k:k + 1], m[k + 1:k + 2]).astype(_BF16)
    if has_proj:
        out_ref[...] = x

    def ffn_chunk(c):
        xn = xn_ref[...]
        gate = jnp.dot(xn, win_ref[0, 0, c], preferred_element_type=_F32)
        up = jnp.dot(xn, win_ref[0, 0, c + N_FF_CHUNKS], preferred_element_type=_F32)
        act = (_silu(gate) * up).astype(_BF16)
        return jnp.dot(act, wout_ref[0, 0, c], preferred_element_type=_F32)

    acc_ref[...] = ffn_chunk(0)

    def chunk(c, carry):
        acc_ref[...] += ffn_chunk(c)
        return carry

    lax.fori_loop(1, N_FF_CHUNKS, chunk, 0, unroll=5)
    h = out_ref[...] if has_proj else x_ref[...]
    out_ref[...] = h + (0.5 * m[k + 2:k + 3]) * acc_ref[...]


def _ffn(x, mods, norm_g, w_in, w_out, layer, which, row_of_tile, tm, o=None, w_o=None):
    t, d = x.shape
    sub = 2 * which
    has_proj = o is not None
    in_specs = [
        pl.BlockSpec((tm, d), lambda i: (i, 0)),
        pl.BlockSpec((1, 1, N_MOD, d), lambda i: (layer, row_of_tile(i), 0, 0)),
        pl.BlockSpec((1, 3, d), lambda i: (layer, 0, 0)),
    ]
    args = [x, mods, norm_g]
    if has_proj:
        in_specs += [pl.BlockSpec((tm, d), lambda i: (i, 0)),
                     _resident((1, d, d), lambda i: (layer, 0, 0))]
        args += [o, w_o]
    in_specs += [
        _resident((1, 1, 2 * N_FF_CHUNKS, d, FF_CHUNK), lambda i: (layer, which, 0, 0, 0)),
        _resident((1, 1, N_FF_CHUNKS, FF_CHUNK, d), lambda i: (layer, which, 0, 0, 0)),
    ]
    args += [w_in, w_out]
    return pl.pallas_call(
        functools.partial(_ffn_body, sub=sub, has_proj=has_proj),
        grid=(t // tm,),
        in_specs=in_specs,
        out_specs=pl.BlockSpec((tm, d), lambda i: (i, 0)),
        out_shape=jax.ShapeDtypeStruct((t, d), _F32),
        scratch_shapes=[pltpu.VMEM((tm, d), _BF16), pltpu.VMEM((tm, d), _F32)],
        compiler_params=_params(1),
        name="ffn_proj" if has_proj else "ffn",
    )(*args)


def _head_norm_rope_t(yt, g_t, rope_ref):
    ss = jnp.sum(yt * yt, axis=0, keepdims=True)
    yn = (yt * lax.rsqrt(ss * (1.0 / HEAD_DIM) + EPS)) * g_t
    if rope_ref is None:
        return yn
    f = ROPE_FREQS
    x1r, x2r, x1c, x2c = yn[0:f], yn[f:2 * f], yn[2 * f:3 * f], yn[3 * f:4 * f]
    cr, sr, cc, sc = rope_ref[0], rope_ref[1], rope_ref[2], rope_ref[3]
    return jnp.concatenate([x1r * cr - x2r * sr, x2r * cr + x1r * sr,
                            x1c * cc - x2c * sc, x2c * cc + x1c * sc], axis=0)


def _qkv_body(*refs, rope, keep_f32):
    n_in = 7 if rope else 6
    x_ref, mods_ref, ng_ref, w_ref, qg_ref, kg_ref = refs[:6]
    rope_ref = refs[6] if rope else None
    qt_ref, ka_ref, vt_ref = refs[n_in:n_in + 3]
    m = mods_ref[0, 0]
    a = _modln(x_ref[...], ng_ref[0, 1:2], m[3:4], m[4:5]).astype(_BF16)
    proj = jnp.dot(a, w_ref[0], preferred_element_type=_F32)
    qg, kg = qg_ref[0], kg_ref[0]
    for c in range(N_HEADS * HEAD_DIM // GROUP_COLS):
        yt = proj[:, c * GROUP_COLS:(c + 1) * GROUP_COLS].T
        for h in range(KV_GROUP):
            qn = _head_norm_rope_t(yt[h * HEAD_DIM:(h + 1) * HEAD_DIM], qg, rope_ref)
            r0 = c * GROUP_COLS + h * HEAD_DIM
            qt_ref[0, r0:r0 + HEAD_DIM, :] = (qn * Q_SCALE).astype(qt_ref.dtype)
    k0 = N_HEADS * HEAD_DIM
    kt = proj[:, k0:k0 + KV_COLS].T
    kn = jnp.concatenate([_head_norm_rope_t(kt[g * HEAD_DIM:(g + 1) * HEAD_DIM], kg, rope_ref)
                          for g in range(N_KV_HEADS)], axis=0)
    k_tok = kn.T
    v_tok = proj[:, k0 + KV_COLS:]
    vt = v_tok.T
    tm = k_tok.shape[0]
    row = lax.broadcasted_iota(jnp.int32, (ACC_ROWS - HEAD_DIM, V_PIECE), 0)
    extra = jnp.where(row == 0, 1.0, 0.0).astype(vt_ref.dtype)
    for g in range(N_KV_HEADS):
        ka_ref[0, g] = k_tok[:, g * HEAD_DIM:(g + 1) * HEAD_DIM].astype(ka_ref.dtype)
        for j in range(tm // V_PIECE):
            vt_ref[0, g, j, 0:HEAD_DIM, :] = (
                vt[g * HEAD_DIM:(g + 1) * HEAD_DIM, j * V_PIECE:(j + 1) * V_PIECE].astype(vt_ref.dtype))
            vt_ref[0, g, j, HEAD_DIM:ACC_ROWS, :] = extra
    if keep_f32:
        k_ref, v_ref = refs[n_in + 3:n_in + 5]
        k_ref[...] = k_tok
        v_ref[...] = v_tok


def _qkv(x, mods, norm_g, w_qkv, qg_t, kg_t, layer, row_of_tile, tm, seq_len, keep_f32, rope_t=None):
    t, d = x.shape
    rope = rope_t is not None
    tiles_per_seq = seq_len // tm
    n_seq = t // seq_len
    seq_tile = lambda i: (i // tiles_per_seq, 0, i % tiles_per_seq)
    out_specs = [pl.BlockSpec((1, N_HEADS * HEAD_DIM, tm), seq_tile),
                 pl.BlockSpec((1, N_KV_HEADS, tm, HEAD_DIM), lambda i: seq_tile(i) + (0,)),
                 pl.BlockSpec((1, N_KV_HEADS, tm // V_PIECE, ACC_ROWS, V_PIECE), lambda i: seq_tile(i) + (0, 0))]
    out_shape = [jax.ShapeDtypeStruct((n_seq, N_HEADS * HEAD_DIM, seq_len), _BF16),
                 jax.ShapeDtypeStruct((n_seq, N_KV_HEADS, seq_len, HEAD_DIM), _BF16),
                 jax.ShapeDtypeStruct((n_seq, N_KV_HEADS, seq_len // V_PIECE, ACC_ROWS, V_PIECE), _BF16)]
    if keep_f32:
        out_specs += [pl.BlockSpec((tm, KV_COLS), lambda i: (i, 0))] * 2
        out_shape += [jax.ShapeDtypeStruct((t, KV_COLS), _F32)] * 2
    in_specs = [
        pl.BlockSpec((tm, d), lambda i: (i, 0)),
        pl.BlockSpec((1, 1, N_MOD, d), lambda i: (layer, row_of_tile(i), 0, 0)),
        pl.BlockSpec((1, 3, d), lambda i: (layer, 0, 0)),
        _resident((1, d, QKV_COLS), lambda i: (layer, 0, 0)),
        pl.BlockSpec((1, HEAD_DIM, tm), lambda i: (layer, 0, 0)),
        pl.BlockSpec((1, HEAD_DIM, tm), lambda i: (layer, 0, 0)),
    ]
    args = [x, mods, norm_g, w_qkv, qg_t, kg_t]
    if rope:
        in_specs += [pl.BlockSpec((4, ROPE_FREQS, tm), lambda i: (0, 0, i % tiles_per_seq))]
        args += [rope_t]
    return pl.pallas_call(
        functools.partial(_qkv_body, rope=rope, keep_f32=keep_f32),
        grid=(t // tm,),
        in_specs=in_specs,
        out_specs=out_specs,
        out_shape=out_shape,
        compiler_params=_params(1),
        name="qkv_rope" if rope else "qkv",
    )(*args)


def _rope_table_t(n_tok):
    pos = jnp.arange(n_tok)
    row = (pos // GRID_W).astype(_F32)
    col = (pos % GRID_W).astype(_F32)
    freqs = 1.0 / jnp.power(ROPE_THETA, jnp.arange(ROPE_FREQS, dtype=_F32) / ROPE_FREQS)
    ang_r = row[:, None] * freqs
    ang_c = col[:, None] * freqs
    return jnp.stack([jnp.cos(ang_r).T, jnp.sin(ang_r).T, jnp.cos(ang_c).T, jnp.sin(ang_c).T])


KV_SUB = 256
V_PIECE = 128
SCORE_ROWS = 64
BF16_SUBLANES = 16
ACC_ROWS = HEAD_DIM + BF16_SUBLANES


SCORE_AHEAD = 2
VALUE_BEHIND = 1
S_SLOTS = SCORE_AHEAD + 1
P_SLOTS = VALUE_BEHIND + 1


def _attn_scratch(n_states, tq, tk):
    return [pltpu.VMEM((S_SLOTS, tk, tq), _F32),
            pltpu.VMEM((S_SLOTS, 1, tq), _F32),
            pltpu.VMEM((P_SLOTS, tk, tq), _BF16),
            pltpu.VMEM((P_SLOTS, 1, tq), _F32),
            pltpu.VMEM((n_states, 1, tq), _F32),
            pltpu.VMEM((n_states, ACC_ROWS, tq), _F32)]


def _init_states(scr, sinks=None):
    m_ref, acc_ref = scr[4], scr[5]
    tq = m_ref.shape[-1]
    if sinks is None:
        m_ref[...] = jnp.full(m_ref.shape, NEG_INF, _F32)
        acc_ref[...] = jnp.zeros(acc_ref.shape, _F32)
    else:
        for i, sk in enumerate(sinks):
            m_ref[i] = jnp.full((1, tq), sk * LOG2_E, _F32)
        row = lax.broadcasted_iota(jnp.int32, (ACC_ROWS, tq), 0)
        acc_ref[...] = jnp.broadcast_to(jnp.where(row == HEAD_DIM, 1.0, 0.0).astype(_F32), acc_ref.shape)


def _part(j):
    return slice(j * KV_SUB, (j + 1) * KV_SUB)


def _scores(scr, slot, j, item):
    s_ref, mc_ref = scr[0], scr[1]
    _, k, qt, _, bias = item
    kj = k[_part(j)]
    s = jnp.concatenate([jnp.dot(kj[r:r + SCORE_ROWS], qt, preferred_element_type=_F32)
                         for r in range(0, KV_SUB, SCORE_ROWS)], axis=0)
    if bias is not None:
        s = s + bias[_part(j)]
    s_ref[slot, _part(j)] = s
    mx = jnp.max(s, axis=0, keepdims=True)
    mc_ref[slot] = mx if j == 0 else jnp.maximum(mc_ref[slot], mx)


def _new_max(scr, s_slot, p_slot, st):
    mc_ref, al_ref, m_ref = scr[1], scr[3], scr[4]
    m_old = m_ref[st]
    m_new = jnp.maximum(m_old, mc_ref[s_slot])
    al_ref[p_slot] = jnp.exp2(m_old - m_new)
    m_ref[st] = m_new


def _softmax(scr, s_slot, p_slot, j, st):
    s_ref, p_ref, m_ref = scr[0], scr[2], scr[4]
    p_ref[p_slot, _part(j)] = jnp.exp2(s_ref[s_slot, _part(j)] - m_ref[st]).astype(_BF16)


def _values(scr, slot, j, item):
    p_ref, al_ref, acc_ref = scr[2], scr[3], scr[5]
    st, vts = item[0], item[3]
    pv = jnp.dot(vts[j], p_ref[slot, _part(j)], preferred_element_type=_F32)
    acc = acc_ref[st]
    acc_ref[st] = (al_ref[slot] * acc if j == 0 else acc) + pv


def _pipeline(scr, items):
    n = len(items)
    n_parts = len(items[0][3])
    for i0 in range(min(SCORE_AHEAD, n)):
        for j in range(n_parts):
            _scores(scr, i0 % S_SLOTS, j, items[i0])
    for i in range(n + VALUE_BEHIND):
        if i < n:
            _new_max(scr, i % S_SLOTS, i % P_SLOTS, items[i][0])
        for j in range(n_parts):
            if i + SCORE_AHEAD < n:
                _scores(scr, (i + SCORE_AHEAD) % S_SLOTS, j, items[i + SCORE_AHEAD])
            if i < n:
                _softmax(scr, i % S_SLOTS, i % P_SLOTS, j, items[i][0])
            if i >= VALUE_BEHIND:
                _values(scr, (i - VALUE_BEHIND) % P_SLOTS, j, items[i - VALUE_BEHIND])


def _finish(scr, first):
    acc_ref = scr[5]
    ot = jnp.concatenate([acc_ref[first + h, 0:HEAD_DIM] / acc_ref[first + h, HEAD_DIM:HEAD_DIM + 1]
                          for h in range(KV_GROUP)], axis=0)
    return ot.T


def _heads(qt_ref, g=None):
    base = 0 if g is None else g * GROUP_COLS
    return [qt_ref[0, base + h * HEAD_DIM: base + (h + 1) * HEAD_DIM, :] for h in range(KV_GROUP)]


def _vt_part(vt_ref, lead, piece):
    return jnp.concatenate([vt_ref[lead + (piece,)], vt_ref[lead + (piece + 1,)]], axis=1)


def _chunk_items(k_ref, vt_ref, heads, tk):
    items = []
    for c in range(k_ref.shape[2] // tk):
        k = k_ref[0, 0, c * tk:(c + 1) * tk, :]
        vts = [_vt_part(vt_ref, (0, 0), (c * tk + j * KV_SUB) // V_PIECE) for j in range(tk // KV_SUB)]
        items += [(h, k, heads[h], vts, None) for h in range(KV_GROUP)]
    return items


def _full_body(qt_ref, kc_ref, vtc_ref, kl_ref, vtl_ref, o_ref, *scr, tk):
    heads = _heads(qt_ref)
    _init_states(scr)
    _pipeline(scr, _chunk_items(kc_ref, vtc_ref, heads, tk) + _chunk_items(kl_ref, vtl_ref, heads, tk))
    o_ref[0] = _finish(scr, 0).astype(o_ref.dtype)


def _kv_specs(lk, index_map):
    return [pl.BlockSpec((1, 1, lk, HEAD_DIM), lambda *i: index_map(*i) + (0, 0)),
            pl.BlockSpec((1, 1, lk // V_PIECE, ACC_ROWS, V_PIECE), lambda *i: index_map(*i) + (0, 0, 0))]


def _attend_full(qt, k_ctx, vt_ctx, k_lat, vt_lat, tq, tk):
    b, d, l = qt.shape
    by_group = lambda bi, g, qi: (bi, g)
    return pl.pallas_call(
        functools.partial(_full_body, tk=tk),
        grid=(b, N_KV_HEADS, l // tq),
        in_specs=([pl.BlockSpec((1, GROUP_COLS, tq), lambda bi, g, qi: (bi, g, qi))]
                  + _kv_specs(k_ctx.shape[2], by_group) + _kv_specs(l, by_group)),
        out_specs=pl.BlockSpec((1, tq, GROUP_COLS), lambda bi, g, qi: (bi, qi, g)),
        out_shape=jax.ShapeDtypeStruct((b, l, d), _BF16),
        scratch_shapes=_attn_scratch(KV_GROUP, tq, tk),
        compiler_params=_params(3),
        name="attn_full",
    )(qt, k_ctx, vt_ctx, k_lat, vt_lat)


def _band_start(q0, l, tq):
    return jnp.clip(q0 - WINDOW, 0, l - 2 * tq)


def _window_body(sink_ref, qt_ref, kc_ref, vtc_ref, kl_ref, vtl_ref, bias_ref, o_ref, *scr, tq):
    g = pl.program_id(1)
    heads = _heads(qt_ref)
    start = pl.multiple_of(_band_start(pl.program_id(2) * tq, kl_ref.shape[2], tq), V_PIECE)
    k_band = kl_ref[0, 0, pl.ds(start, 2 * tq), :]
    vt_band = [_vt_part(vtl_ref, (0, 0), start // V_PIECE + j * (KV_SUB // V_PIECE)) for j in range(2)]
    bias = bias_ref[0]
    _init_states(scr, [sink_ref[g * KV_GROUP + h] for h in range(KV_GROUP)])
    items = (_chunk_items(kc_ref, vtc_ref, heads, 2 * tq)
             + [(h, k_band, heads[h], vt_band, bias) for h in range(KV_GROUP)])
    _pipeline(scr, items)
    o_ref[0] = _finish(scr, 0).astype(o_ref.dtype)


def _window_bias(l, tq):
    q0 = (jnp.arange(l // tq) * tq)[:, None, None]
    kpos = _band_start(q0, l, tq) + jnp.arange(2 * tq)[None, :, None]
    qpos = q0 + jnp.arange(tq)[None, None, :]
    return jnp.where(jnp.abs(qpos - kpos) <= WINDOW, 0.0, NEG_INF).astype(_F32)


def _attend_window(qt, k_ctx, vt_ctx, k_lat, vt_lat, sink, tq):
    b, d, l = qt.shape
    lc = k_ctx.shape[2]
    assert tq == KV_SUB and tq >= 2 * WINDOW and lc == 2 * tq and l >= 2 * tq
    by_group = lambda bi, g, qi: (bi, g)
    return pl.pallas_call(
        functools.partial(_window_body, tq=tq),
        grid=(b, N_KV_HEADS, l // tq),
        in_specs=([pl.BlockSpec(memory_space=pltpu.SMEM),
                   pl.BlockSpec((1, GROUP_COLS, tq), lambda bi, g, qi: (bi, g, qi))]
                  + _kv_specs(lc, by_group) + _kv_specs(l, by_group)
                  + [pl.BlockSpec((1, 2 * tq, tq), lambda bi, g, qi: (qi, 0, 0))]),
        out_specs=pl.BlockSpec((1, tq, GROUP_COLS), lambda bi, g, qi: (bi, qi, g)),
        out_shape=jax.ShapeDtypeStruct((b, l, d), _BF16),
        scratch_shapes=_attn_scratch(KV_GROUP, tq, 2 * tq),
        compiler_params=_params(3),
        name="attn_window",
    )(sink, qt, k_ctx, vt_ctx, k_lat, vt_lat, _window_bias(l, tq))


def _ctx_body(*refs, has_sink):
    if has_sink:
        sink_ref, qt_ref, k_ref, vt_ref, o_ref = refs[:5]
    else:
        qt_ref, k_ref, vt_ref, o_ref = refs[:4]
    scr = refs[5:] if has_sink else refs[4:]
    n_sub = k_ref.shape[2] // KV_SUB
    _init_states(scr, [sink_ref[i] for i in range(N_HEADS)] if has_sink else None)
    items = []
    for g in range(N_KV_HEADS):
        heads = _heads(qt_ref, g)
        k = k_ref[0, g]
        vts = [_vt_part(vt_ref, (0, g), j * (KV_SUB // V_PIECE)) for j in range(n_sub)]
        items += [(g * KV_GROUP + h, k, heads[h], vts, None) for h in range(KV_GROUP)]
    _pipeline(scr, items)
    for g in range(N_KV_HEADS):
        o_ref[0, :, g * GROUP_COLS:(g + 1) * GROUP_COLS] = _finish(scr, g * KV_GROUP).astype(o_ref.dtype)


def _attend_ctx(qt, k, vt, sink):
    b, d, l = qt.shape
    has_sink = sink is not None
    in_specs = [
        pl.BlockSpec((1, d, l), lambda bi: (bi, 0, 0)),
        pl.BlockSpec((1, N_KV_HEADS, l, HEAD_DIM), lambda bi: (bi, 0, 0, 0)),
        pl.BlockSpec((1, N_KV_HEADS, l // V_PIECE, ACC_ROWS, V_PIECE), lambda bi: (bi, 0, 0, 0, 0)),
    ]
    args = [qt, k, vt]
    if has_sink:
        in_specs = [pl.BlockSpec(memory_space=pltpu.SMEM)] + in_specs
        args = [sink] + args
    return pl.pallas_call(
        functools.partial(_ctx_body, has_sink=has_sink),
        grid=(b,),
        in_specs=in_specs,
        out_specs=pl.BlockSpec((1, l, d), lambda bi: (bi, 0, 0)),
        out_shape=jax.ShapeDtypeStruct((b, l, d), _BF16),
        scratch_shapes=_attn_scratch(N_HEADS, l, l),
        compiler_params=_params(1),
        name="attn_ctx_sink" if has_sink else "attn_ctx",
    )(*args)


def _cache_layout(cache_k, cache_v):
    b, depth, p, g, hd = cache_v.shape
    k = cache_k.transpose(1, 0, 3, 2, 4).astype(_BF16)
    vt = cache_v.reshape(b, depth, p // V_PIECE, V_PIECE, g, hd).transpose(1, 0, 4, 2, 5, 3).astype(_BF16)
    extra = jnp.zeros((ACC_ROWS - hd, V_PIECE), _BF16).at[0].set(1.0)
    return k, jnp.concatenate([vt, jnp.broadcast_to(extra, vt.shape[:4] + extra.shape)], axis=4)


def _token_tile(t, cap):
    tm = min(cap, t)
    assert t % tm == 0
    return tm


def kernel(x_prompt, x_sample, cache_k, cache_v, c, c_ctx, w_mod, b_mod, norm_g, w_qkv, w_o,
           q_norm_g, k_norm_g, sink, w_ffn_in, w_ffn_out):
    depth = w_mod.shape[0]
    d = D_MODEL
    bp, lp, _ = x_prompt.shape
    bs, ls, _ = x_sample.shape
    past = cache_k.shape[2]
    assert lp % KV_SUB == 0 and ls % KV_TILE == 0 and past % KV_TILE == 0

    w_in = (w_ffn_in.astype(_BF16).reshape(depth, 2, d, 2 * N_FF_CHUNKS, FF_CHUNK)
            .transpose(0, 1, 3, 2, 4))
    w_out = w_ffn_out.astype(_BF16).reshape(depth, 2, N_FF_CHUNKS, FF_CHUNK, d)
    w_qkv_b = w_qkv.astype(_BF16)
    w_o_b = w_o.astype(_BF16)

    n_rows = -(-(bs + 1) // 8) * 8
    cond = jnp.concatenate([c, c_ctx[None, :], jnp.zeros((n_rows - bs - 1, d), _F32)], axis=0)
    mods = _modulation(cond, w_mod, b_mod)

    tp = bp * lp
    tm_p = _token_tile(tp, 512)
    tq_p = _token_tile(lp, 512)
    ctx_row = lambda i: bs
    qg_p = jnp.broadcast_to(q_norm_g[:, :, None], (depth, HEAD_DIM, tq_p))
    kg_p = jnp.broadcast_to(k_norm_g[:, :, None], (depth, HEAD_DIM, tq_p))
    h = x_prompt.reshape(tp, d)
    new_ks, new_vs = [], []
    for i in range(depth):
        h = _ffn(h, mods, norm_g, w_in, w_out, i, 0, ctx_row, tm_p)
        qt, ka, vt, k, v = _qkv(h, mods, norm_g, w_qkv_b, qg_p, kg_p, i, ctx_row, tq_p, lp, True)
        new_ks.append(k.reshape(bp, lp, N_KV_HEADS, HEAD_DIM))
        new_vs.append(v.reshape(bp, lp, N_KV_HEADS, HEAD_DIM))
        o = _attend_ctx(qt, ka, vt, sink[i // 2] if i % 2 == 1 else None)
        h = _ffn(h, mods, norm_g, w_in, w_out, i, 1, ctx_row, tm_p, o=o.reshape(tp, d), w_o=w_o_b)
    y_prompt = h.reshape(bp, lp, d)
    new_k = jnp.stack(new_ks, axis=1)
    new_v = jnp.stack(new_vs, axis=1)

    ts = bs * ls
    tm_s = _token_tile(ls, 512)
    tiles_per_seq = ls // tm_s
    lat_row = lambda i: i // tiles_per_seq
    rope_t = _rope_table_t(ls)
    qg_s = jnp.broadcast_to(q_norm_g[:, :, None], (depth, HEAD_DIM, tm_s))
    kg_s = jnp.broadcast_to(k_norm_g[:, :, None], (depth, HEAD_DIM, tm_s))
    k_ctx, vt_ctx = _cache_layout(cache_k, cache_v)
    h = x_sample.reshape(ts, d)
    for i in range(depth):
        h = _ffn(h, mods, norm_g, w_in, w_out, i, 0, lat_row, tm_s)
        qt, ka, vt = _qkv(h, mods, norm_g, w_qkv_b, qg_s, kg_s, i, lat_row, tm_s, ls, False, rope_t)
        if i % 2 == 0:
            o = _attend_full(qt, k_ctx[i], vt_ctx[i], ka, vt, Q_TILE, KV_TILE)
        else:
            o = _attend_window(qt, k_ctx[i], vt_ctx[i], ka, vt, sink[i // 2], Q_TILE)
        h = _ffn(h, mods, norm_g, w_in, w_out, i, 1, lat_row, tm_s, o=o.reshape(ts, d), w_o=w_o_b)
    y_sample = h.reshape(bs, ls, d)

    return (y_prompt, y_sample, new_k, new_v)
```

```python
import functools

import jax
import jax.numpy as jnp
from jax import lax
from jax.experimental import pallas as pl
from jax.experimental.pallas import tpu as pltpu

D_MODEL = 1024
N_HEADS = 16
N_KV_HEADS = 4
HEAD_DIM = 64
KV_GROUP = N_HEADS // N_KV_HEADS
GROUP_COLS = KV_GROUP * HEAD_DIM
KV_COLS = N_KV_HEADS * HEAD_DIM
QKV_COLS = (N_HEADS + 2 * N_KV_HEADS) * HEAD_DIM
D_FF = 2816
N_MOD = 9
GRID_W = 64
WINDOW = 128
ROPE_FREQS = HEAD_DIM // 4
ROPE_THETA = 10000.0
ATTN_SCALE = HEAD_DIM ** -0.5
LOG2_E = 1.4426950408889634
Q_SCALE = ATTN_SCALE * LOG2_E
EPS = 1e-6
NEG_INF = -1e30

FF_CHUNK = 256
N_FF_CHUNKS = D_FF // FF_CHUNK
FFN_TILE = 1024
QKV_TILE = 512
Q_TILE = 256
Q_BLOCKS_PER_STEP = 2
KV_TILE = 512
VMEM_LIMIT_BYTES = 52 * 1024 * 1024

_BF16 = jnp.bfloat16
_F32 = jnp.float32


def _params(n_axes):
    return pltpu.CompilerParams(dimension_semantics=("arbitrary",) * n_axes,
                                vmem_limit_bytes=VMEM_LIMIT_BYTES)


def _resident(block_shape, index_map):
    return pl.BlockSpec(block_shape, index_map, pipeline_mode=pl.Buffered(1))


def _silu(x):
    return x * jax.nn.sigmoid(x)


def _modln(x, g, shift, scale):
    y = x * lax.rsqrt(jnp.mean(x * x, axis=-1, keepdims=True) + EPS)
    return y * (g * (1.0 + scale)) + shift


def _mod_body(cond_ref, w_ref, b_ref, o_ref):
    a = _silu(cond_ref[...]).astype(_BF16)
    w = w_ref[0].astype(_BF16)
    o_ref[0] = jnp.dot(a, w, preferred_element_type=_F32) + b_ref[0]


def _modulation(cond, w_mod, b_mod):
    depth, d, n = w_mod.shape
    r = cond.shape[0]
    tn = n // 8
    out = pl.pallas_call(
        _mod_body,
        grid=(depth, n // tn),
        in_specs=[
            pl.BlockSpec((r, d), lambda i, j: (0, 0)),
            pl.BlockSpec((1, d, tn), lambda i, j: (i, 0, j)),
            pl.BlockSpec((1, 1, tn), lambda i, j: (i, 0, j)),
        ],
        out_specs=pl.BlockSpec((1, r, tn), lambda i, j: (i, 0, j)),
        out_shape=jax.ShapeDtypeStruct((depth, r, n), _F32),
        compiler_params=_params(2),
        name="modulation",
    )(cond, w_mod, b_mod.reshape(depth, 1, n))
    return out.reshape(depth, r, N_MOD, d)


def _ffn_body(*refs, sub, has_proj):
    if has_proj:
        x_ref, mods_ref, ng_ref, o_ref, wo_ref, win_ref, wout_ref, out_ref, xn_ref, acc_ref = refs
    else:
        x_ref, mods_ref, ng_ref, win_ref, wout_ref, out_ref, xn_ref, acc_ref = refs
    m = mods_ref[0, 0]
    x = x_ref[...]
    if has_proj:
        x = x + m[5:6] * jnp.dot(o_ref[...], wo_ref[0], preferred_element_type=_F32)
    k = 3 * sub
    xn_ref[...] = _modln(x, ng_ref[0, sub:sub + 1], m[k:k + 1], m[k + 1:k + 2]).astype(_BF16)
    if has_proj:
        out_ref[...] = x

    def ffn_chunk(c):
        xn = xn_ref[...]
        gate = jnp.dot(xn, win_ref[0, 0, c], preferred_element_type=_F32)
        up = jnp.dot(xn, win_ref[0, 0, c + N_FF_CHUNKS], preferred_element_type=_F32)
        act = (_silu(gate) * up).astype(_BF16)
        return jnp.dot(act, wout_ref[0, 0, c], preferred_element_type=_F32)

    acc_ref[...] = ffn_chunk(0)

    def chunk(c, carry):
        acc_ref[...] += ffn_chunk(c)
        return carry

    lax.fori_loop(1, N_FF_CHUNKS, chunk, 0, unroll=5)
    h = out_ref[...] if has_proj else x_ref[...]
    out_ref[...] = h + (0.5 * m[k + 2:k + 3]) * acc_ref[...]


def _ffn(x, mods, norm_g, w_in, w_out, layer, which, row_of_tile, tm, o=None, w_o=None):
    t, d = x.shape
    sub = 2 * which
    has_proj = o is not None
    in_specs = [
        pl.BlockSpec((tm, d), lambda i: (i, 0)),
        pl.BlockSpec((1, 1, N_MOD, d), lambda i: (layer, row_of_tile(i), 0, 0)),
        pl.BlockSpec((1, 3, d), lambda i: (layer, 0, 0)),
    ]
    args = [x, mods, norm_g]
    if has_proj:
        in_specs += [pl.BlockSpec((tm, d), lambda i: (i, 0)),
                     _resident((1, d, d), lambda i: (layer, 0, 0))]
        args += [o, w_o]
    in_specs += [
        _resident((1, 1, 2 * N_FF_CHUNKS, d, FF_CHUNK), lambda i: (layer, which, 0, 0, 0)),
        _resident((1, 1, N_FF_CHUNKS, FF_CHUNK, d), lambda i: (layer, which, 0, 0, 0)),
    ]
    args += [w_in, w_out]
    return pl.pallas_call(
        functools.partial(_ffn_body, sub=sub, has_proj=has_proj),
        grid=(t // tm,),
        in_specs=in_specs,
        out_specs=pl.BlockSpec((tm, d), lambda i: (i, 0)),
        out_shape=jax.ShapeDtypeStruct((t, d), _F32),
        scratch_shapes=[pltpu.VMEM((tm, d), _BF16), pltpu.VMEM((tm, d), _F32)],
        compiler_params=_params(1),
        name="ffn_proj" if has_proj else "ffn",
    )(*args)


def _head_norm_rope_t(yt, g_t, rope_ref):
    ss = jnp.sum(yt * yt, axis=0, keepdims=True)
    yn = (yt * lax.rsqrt(ss * (1.0 / HEAD_DIM) + EPS)) * g_t
    if rope_ref is None:
        return yn
    f = ROPE_FREQS
    x1r, x2r, x1c, x2c = yn[0:f], yn[f:2 * f], yn[2 * f:3 * f], yn[3 * f:4 * f]
    cr, sr, cc, sc = rope_ref[0], rope_ref[1], rope_ref[2], rope_ref[3]
    return jnp.concatenate([x1r * cr - x2r * sr, x2r * cr + x1r * sr,
                            x1c * cc - x2c * sc, x2c * cc + x1c * sc], axis=0)


def _qkv_body(*refs, rope, keep_f32):
    n_in = 7 if rope else 6
    x_ref, mods_ref, ng_ref, w_ref, qg_ref, kg_ref = refs[:6]
    rope_ref = refs[6] if rope else None
    qt_ref, ka_ref, vt_ref = refs[n_in:n_in + 3]
    m = mods_ref[0, 0]
    a = _modln(x_ref[...], ng_ref[0, 1:2], m[3:4], m[4:5]).astype(_BF16)
    proj = jnp.dot(a, w_ref[0], preferred_element_type=_F32)
    qg, kg = qg_ref[0], kg_ref[0]
    for c in range(N_HEADS * HEAD_DIM // GROUP_COLS):
        yt = proj[:, c * GROUP_COLS:(c + 1) * GROUP_COLS].T
        for h in range(KV_GROUP):
            qn = _head_norm_rope_t(yt[h * HEAD_DIM:(h + 1) * HEAD_DIM], qg, rope_ref)
            r0 = c * GROUP_COLS + h * HEAD_DIM
            qt_ref[0, r0:r0 + HEAD_DIM, :] = (qn * Q_SCALE).astype(qt_ref.dtype)
    k0 = N_HEADS * HEAD_DIM
    kt = proj[:, k0:k0 + KV_COLS].T
    kn = jnp.concatenate([_head_norm_rope_t(kt[g * HEAD_DIM:(g + 1) * HEAD_DIM], kg, rope_ref)
                          for g in range(N_KV_HEADS)], axis=0)
    k_tok = kn.T
    v_tok = proj[:, k0 + KV_COLS:]
    vt = v_tok.T
    tm = k_tok.shape[0]
    row = lax.broadcasted_iota(jnp.int32, (ACC_ROWS - HEAD_DIM, V_PIECE), 0)
    extra = jnp.where(row == 0, 1.0, 0.0).astype(vt_ref.dtype)
    for g in range(N_KV_HEADS):
        ka_ref[0, g] = k_tok[:, g * HEAD_DIM:(g + 1) * HEAD_DIM].astype(ka_ref.dtype)
        for j in range(tm // V_PIECE):
            vt_ref[0, g, j, 0:HEAD_DIM, :] = (
                vt[g * HEAD_DIM:(g + 1) * HEAD_DIM, j * V_PIECE:(j + 1) * V_PIECE].astype(vt_ref.dtype))
            vt_ref[0, g, j, HEAD_DIM:ACC_ROWS, :] = extra
    if keep_f32:
        k_ref, v_ref = refs[n_in + 3:n_in + 5]
        k_ref[...] = k_tok
        v_ref[...] = v_tok


def _qkv(x, mods, norm_g, w_qkv, qg_t, kg_t, layer, row_of_tile, tm, seq_len, keep_f32, rope_t=None):
    t, d = x.shape
    rope = rope_t is not None
    tiles_per_seq = seq_len // tm
    n_seq = t // seq_len
    seq_tile = lambda i: (i // tiles_per_seq, 0, i % tiles_per_seq)
    out_specs = [pl.BlockSpec((1, N_HEADS * HEAD_DIM, tm), seq_tile),
                 pl.BlockSpec((1, N_KV_HEADS, tm, HEAD_DIM), lambda i: seq_tile(i) + (0,)),
                 pl.BlockSpec((1, N_KV_HEADS, tm // V_PIECE, ACC_ROWS, V_PIECE), lambda i: seq_tile(i) + (0, 0))]
    out_shape = [jax.ShapeDtypeStruct((n_seq, N_HEADS * HEAD_DIM, seq_len), _BF16),
                 jax.ShapeDtypeStruct((n_seq, N_KV_HEADS, seq_len, HEAD_DIM), _BF16),
                 jax.ShapeDtypeStruct((n_seq, N_KV_HEADS, seq_len // V_PIECE, ACC_ROWS, V_PIECE), _BF16)]
    if keep_f32:
        out_specs += [pl.BlockSpec((tm, KV_COLS), lambda i: (i, 0))] * 2
        out_shape += [jax.ShapeDtypeStruct((t, KV_COLS), _F32)] * 2
    in_specs = [
        pl.BlockSpec((tm, d), lambda i: (i, 0)),
        pl.BlockSpec((1, 1, N_MOD, d), lambda i: (layer, row_of_tile(i), 0, 0)),
        pl.BlockSpec((1, 3, d), lambda i: (layer, 0, 0)),
        _resident((1, d, QKV_COLS), lambda i: (layer, 0, 0)),
        pl.BlockSpec((1, HEAD_DIM, tm), lambda i: (layer, 0, 0)),
        pl.BlockSpec((1, HEAD_DIM, tm), lambda i: (layer, 0, 0)),
    ]
    args = [x, mods, norm_g, w_qkv, qg_t, kg_t]
    if rope:
        in_specs += [pl.BlockSpec((4, ROPE_FREQS, tm), lambda i: (0, 0, i % tiles_per_seq))]
        args += [rope_t]
    return pl.pallas_call(
        functools.partial(_qkv_body, rope=rope, keep_f32=keep_f32),
        grid=(t // tm,),
        in_specs=in_specs,
        out_specs=out_specs,
        out_shape=out_shape,
        compiler_params=_params(1),
        name="qkv_rope" if rope else "qkv",
    )(*args)


def _rope_table_t(n_tok):
    pos = jnp.arange(n_tok)
    row = (pos // GRID_W).astype(_F32)
    col = (pos % GRID_W).astype(_F32)
    freqs = 1.0 / jnp.power(ROPE_THETA, jnp.arange(ROPE_FREQS, dtype=_F32) / ROPE_FREQS)
    ang_r = row[:, None] * freqs
    ang_c = col[:, None] * freqs
    return jnp.stack([jnp.cos(ang_r).T, jnp.sin(ang_r).T, jnp.cos(ang_c).T, jnp.sin(ang_c).T])


KV_SUB = 256
V_PIECE = 128
SCORE_ROWS = 64
BF16_SUBLANES = 16
ACC_ROWS = HEAD_DIM + BF16_SUBLANES


SCORE_AHEAD = 2
VALUE_BEHIND = 1
S_SLOTS = SCORE_AHEAD + 1
P_SLOTS = VALUE_BEHIND + 1


def _attn_scratch(n_states, tq, tk):
    return [pltpu.VMEM((S_SLOTS, tk, tq), _F32),
            pltpu.VMEM((S_SLOTS, 1, tq), _F32),
            pltpu.VMEM((P_SLOTS, tk, tq), _BF16),
            pltpu.VMEM((P_SLOTS, 1, tq), _F32),
            pltpu.VMEM((n_states, 1, tq), _F32),
            pltpu.VMEM((n_states, ACC_ROWS, tq), _F32)]


def _init_states(scr, sinks=None):
    m_ref, acc_ref = scr[4], scr[5]
    tq = m_ref.shape[-1]
    if sinks is None:
        m_ref[...] = jnp.full(m_ref.shape, NEG_INF, _F32)
        acc_ref[...] = jnp.zeros(acc_ref.shape, _F32)
    else:
        for i, sk in enumerate(sinks):
            m_ref[i] = jnp.full((1, tq), sk * LOG2_E, _F32)
        row = lax.broadcasted_iota(jnp.int32, (ACC_ROWS, tq), 0)
        acc_ref[...] = jnp.broadcast_to(jnp.where(row == HEAD_DIM, 1.0, 0.0).astype(_F32), acc_ref.shape)


def _part(j):
    return slice(j * KV_SUB, (j + 1) * KV_SUB)


def _scores(scr, slot, j, item):
    s_ref, mc_ref = scr[0], scr[1]
    _, k, qt, _, bias = item
    kj = k[_part(j)]
    s = jnp.concatenate([jnp.dot(kj[r:r + SCORE_ROWS], qt, preferred_element_type=_F32)
                         for r in range(0, KV_SUB, SCORE_ROWS)], axis=0)
    if bias is not None:
        s = s + bias[_part(j)]
    s_ref[slot, _part(j)] = s
    mx = jnp.max(s, axis=0, keepdims=True)
    mc_ref[slot] = mx if j == 0 else jnp.maximum(mc_ref[slot], mx)


def _new_max(scr, s_slot, p_slot, st):
    mc_ref, al_ref, m_ref = scr[1], scr[3], scr[4]
    m_old = m_ref[st]
    m_new = jnp.maximum(m_old, mc_ref[s_slot])
    al_ref[p_slot] = jnp.exp2(m_old - m_new)
    m_ref[st] = m_new


def _softmax(scr, s_slot, p_slot, j, st):
    s_ref, p_ref, m_ref = scr[0], scr[2], scr[4]
    p_ref[p_slot, _part(j)] = jnp.exp2(s_ref[s_slot, _part(j)] - m_ref[st]).astype(_BF16)


def _values(scr, slot, j, item):
    p_ref, al_ref, acc_ref = scr[2], scr[3], scr[5]
    st, vts = item[0], item[3]
    pv = jnp.dot(vts[j], p_ref[slot, _part(j)], preferred_element_type=_F32)
    acc = acc_ref[st]
    acc_ref[st] = (al_ref[slot] * acc if j == 0 else acc) + pv


def _pipeline(scr, items):
    n = len(items)
    n_parts = len(items[0][3])
    for i0 in range(min(SCORE_AHEAD, n)):
        for j in range(n_parts):
            _scores(scr, i0 % S_SLOTS, j, items[i0])
    for i in range(n + VALUE_BEHIND):
        if i < n:
            _new_max(scr, i % S_SLOTS, i % P_SLOTS, items[i][0])
        for j in range(n_parts):
            if i + SCORE_AHEAD < n:
                _scores(scr, (i + SCORE_AHEAD) % S_SLOTS, j, items[i + SCORE_AHEAD])
            if i < n:
                _softmax(scr, i % S_SLOTS, i % P_SLOTS, j, items[i][0])
            if i >= VALUE_BEHIND:
                _values(scr, (i - VALUE_BEHIND) % P_SLOTS, j, items[i - VALUE_BEHIND])


def _finish(scr, first):
    acc_ref = scr[5]
    ot = jnp.concatenate([acc_ref[first + h, 0:HEAD_DIM] / acc_ref[first + h, HEAD_DIM:HEAD_DIM + 1]
                          for h in range(KV_GROUP)], axis=0)
    return ot.T


def _heads(qt_ref, g=None):
    base = 0 if g is None else g * GROUP_COLS
    return [qt_ref[0, base + h * HEAD_DIM: base + (h + 1) * HEAD_DIM, :] for h in range(KV_GROUP)]


def _vt_part(vt_ref, lead, piece):
    return jnp.concatenate([vt_ref[lead + (piece,)], vt_ref[lead + (piece + 1,)]], axis=1)


def _query_blocks(qt_ref, tq):
    return [[qt_ref[0, h * HEAD_DIM:(h + 1) * HEAD_DIM, b * tq:(b + 1) * tq] for h in range(KV_GROUP)]
            for b in range(qt_ref.shape[2] // tq)]


def _chunk_items(k_ref, vt_ref, blocks, tk):
    items = []
    for c in range(k_ref.shape[2] // tk):
        k = k_ref[0, 0, c * tk:(c + 1) * tk, :]
        vts = [_vt_part(vt_ref, (0, 0), (c * tk + j * KV_SUB) // V_PIECE) for j in range(tk // KV_SUB)]
        for b, heads in enumerate(blocks):
            items += [(b * KV_GROUP + h, k, heads[h], vts, None) for h in range(KV_GROUP)]
    return items


def _store_blocks(scr, o_ref, tq, n_blocks):
    for b in range(n_blocks):
        o_ref[0, b * tq:(b + 1) * tq, :] = _finish(scr, b * KV_GROUP).astype(o_ref.dtype)


def _full_body(qt_ref, kc_ref, vtc_ref, kl_ref, vtl_ref, o_ref, *scr, tq, tk):
    blocks = _query_blocks(qt_ref, tq)
    _init_states(scr)
    _pipeline(scr, _chunk_items(kc_ref, vtc_ref, blocks, tk) + _chunk_items(kl_ref, vtl_ref, blocks, tk))
    _store_blocks(scr, o_ref, tq, len(blocks))


def _kv_specs(lk, index_map):
    return [pl.BlockSpec((1, 1, lk, HEAD_DIM), lambda *i: index_map(*i) + (0, 0)),
            pl.BlockSpec((1, 1, lk // V_PIECE, ACC_ROWS, V_PIECE), lambda *i: index_map(*i) + (0, 0, 0))]


def _attend_full(qt, k_ctx, vt_ctx, k_lat, vt_lat, tq, tk, n_blocks):
    b, d, l = qt.shape
    ts = tq * n_blocks
    by_group = lambda bi, g, qi: (bi, g)
    return pl.pallas_call(
        functools.partial(_full_body, tq=tq, tk=tk),
        grid=(b, N_KV_HEADS, l // ts),
        in_specs=([pl.BlockSpec((1, GROUP_COLS, ts), lambda bi, g, qi: (bi, g, qi))]
                  + _kv_specs(k_ctx.shape[2], by_group) + _kv_specs(l, by_group)),
        out_specs=pl.BlockSpec((1, ts, GROUP_COLS), lambda bi, g, qi: (bi, qi, g)),
        out_shape=jax.ShapeDtypeStruct((b, l, d), _BF16),
        scratch_shapes=_attn_scratch(KV_GROUP * n_blocks, tq, tk),
        compiler_params=_params(3),
        name="attn_full",
    )(qt, k_ctx, vt_ctx, k_lat, vt_lat)


def _band_start(q0, l, tq):
    return jnp.clip(q0 - WINDOW, 0, l - 2 * tq)


def _window_body(sink_ref, qt_ref, kc_ref, vtc_ref, kl_ref, vtl_ref, bias_ref, o_ref, *scr, tq):
    g = pl.program_id(1)
    blocks = _query_blocks(qt_ref, tq)
    _init_states(scr, [sink_ref[g * KV_GROUP + h] for _ in blocks for h in range(KV_GROUP)])
    items = _chunk_items(kc_ref, vtc_ref, blocks, 2 * tq)
    for b, heads in enumerate(blocks):
        q0 = (pl.program_id(2) * len(blocks) + b) * tq
        start = pl.multiple_of(_band_start(q0, kl_ref.shape[2], tq), V_PIECE)
        k_band = kl_ref[0, 0, pl.ds(start, 2 * tq), :]
        vt_band = [_vt_part(vtl_ref, (0, 0), start // V_PIECE + j * (KV_SUB // V_PIECE)) for j in range(2)]
        items += [(b * KV_GROUP + h, k_band, heads[h], vt_band, bias_ref[b]) for h in range(KV_GROUP)]
    _pipeline(scr, items)
    _store_blocks(scr, o_ref, tq, len(blocks))


def _window_bias(l, tq):
    q0 = (jnp.arange(l // tq) * tq)[:, None, None]
    kpos = _band_start(q0, l, tq) + jnp.arange(2 * tq)[None, :, None]
    qpos = q0 + jnp.arange(tq)[None, None, :]
    return jnp.where(jnp.abs(qpos - kpos) <= WINDOW, 0.0, NEG_INF).astype(_F32)


def _attend_window(qt, k_ctx, vt_ctx, k_lat, vt_lat, sink, tq, n_blocks):
    b, d, l = qt.shape
    lc = k_ctx.shape[2]
    ts = tq * n_blocks
    assert tq == KV_SUB and tq >= 2 * WINDOW and lc == 2 * tq and l >= 2 * tq
    by_group = lambda bi, g, qi: (bi, g)
    return pl.pallas_call(
        functools.partial(_window_body, tq=tq),
        grid=(b, N_KV_HEADS, l // ts),
        in_specs=([pl.BlockSpec(memory_space=pltpu.SMEM),
                   pl.BlockSpec((1, GROUP_COLS, ts), lambda bi, g, qi: (bi, g, qi))]
                  + _kv_specs(lc, by_group) + _kv_specs(l, by_group)
                  + [pl.BlockSpec((n_blocks, 2 * tq, tq), lambda bi, g, qi: (qi, 0, 0))]),
        out_specs=pl.BlockSpec((1, ts, GROUP_COLS), lambda bi, g, qi: (bi, qi, g)),
        out_shape=jax.ShapeDtypeStruct((b, l, d), _BF16),
        scratch_shapes=_attn_scratch(KV_GROUP * n_blocks, tq, 2 * tq),
        compiler_params=_params(3),
        name="attn_window",
    )(sink, qt, k_ctx, vt_ctx, k_lat, vt_lat, _window_bias(l, tq))


def _ctx_body(*refs, has_sink):
    if has_sink:
        sink_ref, qt_ref, k_ref, vt_ref, o_ref = refs[:5]
    else:
        qt_ref, k_ref, vt_ref, o_ref = refs[:4]
    scr = refs[5:] if has_sink else refs[4:]
    n_sub = k_ref.shape[2] // KV_SUB
    _init_states(scr, [sink_ref[i] for i in range(N_HEADS)] if has_sink else None)
    items = []
    for g in range(N_KV_HEADS):
        heads = _heads(qt_ref, g)
        k = k_ref[0, g]
        vts = [_vt_part(vt_ref, (0, g), j * (KV_SUB // V_PIECE)) for j in range(n_sub)]
        items += [(g * KV_GROUP + h, k, heads[h], vts, None) for h in range(KV_GROUP)]
    _pipeline(scr, items)
    for g in range(N_KV_HEADS):
        o_ref[0, :, g * GROUP_COLS:(g + 1) * GROUP_COLS] = _finish(scr, g * KV_GROUP).astype(o_ref.dtype)


def _attend_ctx(qt, k, vt, sink):
    b, d, l = qt.shape
    has_sink = sink is not None
    in_specs = [
        pl.BlockSpec((1, d, l), lambda bi: (bi, 0, 0)),
        pl.BlockSpec((1, N_KV_HEADS, l, HEAD_DIM), lambda bi: (bi, 0, 0, 0)),
        pl.BlockSpec((1, N_KV_HEADS, l // V_PIECE, ACC_ROWS, V_PIECE), lambda bi: (bi, 0, 0, 0, 0)),
    ]
    args = [qt, k, vt]
    if has_sink:
        in_specs = [pl.BlockSpec(memory_space=pltpu.SMEM)] + in_specs
        args = [sink] + args
    return pl.pallas_call(
        functools.partial(_ctx_body, has_sink=has_sink),
        grid=(b,),
        in_specs=in_specs,
        out_specs=pl.BlockSpec((1, l, d), lambda bi: (bi, 0, 0)),
        out_shape=jax.ShapeDtypeStruct((b, l, d), _BF16),
        scratch_shapes=_attn_scratch(N_HEADS, l, l),
        compiler_params=_params(1),
        name="attn_ctx_sink" if has_sink else "attn_ctx",
    )(*args)


def _cache_layout(cache_k, cache_v):
    b, depth, p, g, hd = cache_v.shape
    k = cache_k.transpose(1, 0, 3, 2, 4).astype(_BF16)
    vt = cache_v.reshape(b, depth, p // V_PIECE, V_PIECE, g, hd).transpose(1, 0, 4, 2, 5, 3).astype(_BF16)
    extra = jnp.zeros((ACC_ROWS - hd, V_PIECE), _BF16).at[0].set(1.0)
    return k, jnp.concatenate([vt, jnp.broadcast_to(extra, vt.shape[:4] + extra.shape)], axis=4)


def _token_tile(t, cap):
    tm = min(cap, t)
    assert t % tm == 0
    return tm


def kernel(x_prompt, x_sample, cache_k, cache_v, c, c_ctx, w_mod, b_mod, norm_g, w_qkv, w_o,
           q_norm_g, k_norm_g, sink, w_ffn_in, w_ffn_out):
    depth = w_mod.shape[0]
    d = D_MODEL
    bp, lp, _ = x_prompt.shape
    bs, ls, _ = x_sample.shape
    past = cache_k.shape[2]
    assert lp % KV_SUB == 0 and ls % KV_TILE == 0 and past % KV_TILE == 0

    w_in = (w_ffn_in.astype(_BF16).reshape(depth, 2, d, 2 * N_FF_CHUNKS, FF_CHUNK)
            .transpose(0, 1, 3, 2, 4))
    w_out = w_ffn_out.astype(_BF16).reshape(depth, 2, N_FF_CHUNKS, FF_CHUNK, d)
    w_qkv_b = w_qkv.astype(_BF16)
    w_o_b = w_o.astype(_BF16)

    n_rows = -(-(bs + 1) // 8) * 8
    cond = jnp.concatenate([c, c_ctx[None, :], jnp.zeros((n_rows - bs - 1, d), _F32)], axis=0)
    mods = _modulation(cond, w_mod, b_mod)

    tp = bp * lp
    tm_p = _token_tile(tp, FFN_TILE)
    tq_p = _token_tile(lp, QKV_TILE)
    ctx_row = lambda i: bs
    qg_p = jnp.broadcast_to(q_norm_g[:, :, None], (depth, HEAD_DIM, tq_p))
    kg_p = jnp.broadcast_to(k_norm_g[:, :, None], (depth, HEAD_DIM, tq_p))
    h = x_prompt.reshape(tp, d)
    new_ks, new_vs = [], []
    for i in range(depth):
        h = _ffn(h, mods, norm_g, w_in, w_out, i, 0, ctx_row, tm_p)
        qt, ka, vt, k, v = _qkv(h, mods, norm_g, w_qkv_b, qg_p, kg_p, i, ctx_row, tq_p, lp, True)
        new_ks.append(k.reshape(bp, lp, N_KV_HEADS, HEAD_DIM))
        new_vs.append(v.reshape(bp, lp, N_KV_HEADS, HEAD_DIM))
        o = _attend_ctx(qt, ka, vt, sink[i // 2] if i % 2 == 1 else None)
        h = _ffn(h, mods, norm_g, w_in, w_out, i, 1, ctx_row, tm_p, o=o.reshape(tp, d), w_o=w_o_b)
    y_prompt = h.reshape(bp, lp, d)
    new_k = jnp.stack(new_ks, axis=1)
    new_v = jnp.stack(new_vs, axis=1)

    ts = bs * ls
    tm_s = _token_tile(ls, FFN_TILE)
    tq_s = _token_tile(ls, QKV_TILE)
    lat_row = lambda i: i // (ls // tm_s)
    lat_row_q = lambda i: i // (ls // tq_s)
    rope_t = _rope_table_t(ls)
    qg_s = jnp.broadcast_to(q_norm_g[:, :, None], (depth, HEAD_DIM, tq_s))
    kg_s = jnp.broadcast_to(k_norm_g[:, :, None], (depth, HEAD_DIM, tq_s))
    k_ctx, vt_ctx = _cache_layout(cache_k, cache_v)
    nqb = Q_BLOCKS_PER_STEP if ls % (Q_BLOCKS_PER_STEP * Q_TILE) == 0 else 1
    h = x_sample.reshape(ts, d)
    for i in range(depth):
        h = _ffn(h, mods, norm_g, w_in, w_out, i, 0, lat_row, tm_s)
        qt, ka, vt = _qkv(h, mods, norm_g, w_qkv_b, qg_s, kg_s, i, lat_row_q, tq_s, ls, False, rope_t)
        if i % 2 == 0:
            o = _attend_full(qt, k_ctx[i], vt_ctx[i], ka, vt, Q_TILE, KV_TILE, nqb)
        else:
            o = _attend_window(qt, k_ctx[i], vt_ctx[i], ka, vt, sink[i // 2], Q_TILE, nqb)
        h = _ffn(h, mods, norm_g, w_in, w_out, i, 1, lat_row, tm_s, o=o.reshape(ts, d), w_o=w_o_b)
    y_sample = h.reshape(bs, ls, d)

    return (y_prompt, y_sample, new_k, new_v)
```

```python
import functools

import jax
import jax.numpy as jnp
from jax import lax
from jax.experimental import pallas as pl
from jax.experimental.pallas import tpu as pltpu

D_MODEL = 1024
N_HEADS = 16
N_KV_HEADS = 4
HEAD_DIM = 64
KV_GROUP = N_HEADS // N_KV_HEADS
GROUP_COLS = KV_GROUP * HEAD_DIM
KV_COLS = N_KV_HEADS * HEAD_DIM
QKV_COLS = (N_HEADS + 2 * N_KV_HEADS) * HEAD_DIM
D_FF = 2816
N_MOD = 9
GRID_W = 64
WINDOW = 128
ROPE_FREQS = HEAD_DIM // 4
ROPE_THETA = 10000.0
ATTN_SCALE = HEAD_DIM ** -0.5
LOG2_E = 1.4426950408889634
Q_SCALE = ATTN_SCALE * LOG2_E
EPS = 1e-6
NEG_INF = -1e30

FF_CHUNK = 256
N_FF_CHUNKS = D_FF // FF_CHUNK
FFN_TILE = 1024
QKV_TILE = 512
Q_TILE = 256
FULL_Q_BLOCKS = 2
WINDOW_Q_BLOCKS = 4
KV_TILE = 512
VMEM_LIMIT_BYTES = 52 * 1024 * 1024

_BF16 = jnp.bfloat16
_F32 = jnp.float32


def _params(n_axes):
    return pltpu.CompilerParams(dimension_semantics=("arbitrary",) * n_axes,
                                vmem_limit_bytes=VMEM_LIMIT_BYTES)


def _resident(block_shape, index_map):
    return pl.BlockSpec(block_shape, index_map, pipeline_mode=pl.Buffered(1))


def _silu(x):
    return x * jax.nn.sigmoid(x)


def _modln(x, g, shift, scale):
    y = x * lax.rsqrt(jnp.mean(x * x, axis=-1, keepdims=True) + EPS)
    return y * (g * (1.0 + scale)) + shift


def _mod_body(cond_ref, w_ref, b_ref, o_ref):
    a = _silu(cond_ref[...]).astype(_BF16)
    w = w_ref[0].astype(_BF16)
    o_ref[0] = jnp.dot(a, w, preferred_element_type=_F32) + b_ref[0]


def _modulation(cond, w_mod, b_mod):
    depth, d, n = w_mod.shape
    r = cond.shape[0]
    tn = n // 8
    out = pl.pallas_call(
        _mod_body,
        grid=(depth, n // tn),
        in_specs=[
            pl.BlockSpec((r, d), lambda i, j: (0, 0)),
            pl.BlockSpec((1, d, tn), lambda i, j: (i, 0, j)),
            pl.BlockSpec((1, 1, tn), lambda i, j: (i, 0, j)),
        ],
        out_specs=pl.BlockSpec((1, r, tn), lambda i, j: (i, 0, j)),
        out_shape=jax.ShapeDtypeStruct((depth, r, n), _F32),
        compiler_params=_params(2),
        name="modulation",
    )(cond, w_mod, b_mod.reshape(depth, 1, n))
    return out.reshape(depth, r, N_MOD, d)


def _ffn_body(*refs, sub, has_proj):
    if has_proj:
        x_ref, mods_ref, ng_ref, o_ref, wo_ref, win_ref, wout_ref, out_ref, xn_ref, acc_ref = refs
    else:
        x_ref, mods_ref, ng_ref, win_ref, wout_ref, out_ref, xn_ref, acc_ref = refs
    m = mods_ref[0, 0]
    x = x_ref[...]
    if has_proj:
        x = x + m[5:6] * jnp.dot(o_ref[...], wo_ref[0], preferred_element_type=_F32)
    k = 3 * sub
    xn_ref[...] = _modln(x, ng_ref[0, sub:sub + 1], m[k:k + 1], m[k + 1:k + 2]).astype(_BF16)
    if has_proj:
        out_ref[...] = x

    def ffn_chunk(c):
        xn = xn_ref[...]
        gate = jnp.dot(xn, win_ref[0, 0, c], preferred_element_type=_F32)
        up = jnp.dot(xn, win_ref[0, 0, c + N_FF_CHUNKS], preferred_element_type=_F32)
        act = (_silu(gate) * up).astype(_BF16)
        return jnp.dot(act, wout_ref[0, 0, c], preferred_element_type=_F32)

    acc_ref[...] = ffn_chunk(0)

    def chunk(c, carry):
        acc_ref[...] += ffn_chunk(c)
        return carry

    lax.fori_loop(1, N_FF_CHUNKS, chunk, 0, unroll=5)
    h = out_ref[...] if has_proj else x_ref[...]
    out_ref[...] = h + (0.5 * m[k + 2:k + 3]) * acc_ref[...]


def _ffn(x, mods, norm_g, w_in, w_out, layer, which, row_of_tile, tm, o=None, w_o=None):
    t, d = x.shape
    sub = 2 * which
    has_proj = o is not None
    in_specs = [
        pl.BlockSpec((tm, d), lambda i: (i, 0)),
        pl.BlockSpec((1, 1, N_MOD, d), lambda i: (layer, row_of_tile(i), 0, 0)),
        pl.BlockSpec((1, 3, d), lambda i: (layer, 0, 0)),
    ]
    args = [x, mods, norm_g]
    if has_proj:
        in_specs += [pl.BlockSpec((tm, d), lambda i: (i, 0)),
                     _resident((1, d, d), lambda i: (layer, 0, 0))]
        args += [o, w_o]
    in_specs += [
        _resident((1, 1, 2 * N_FF_CHUNKS, d, FF_CHUNK), lambda i: (layer, which, 0, 0, 0)),
        _resident((1, 1, N_FF_CHUNKS, FF_CHUNK, d), lambda i: (layer, which, 0, 0, 0)),
    ]
    args += [w_in, w_out]
    return pl.pallas_call(
        functools.partial(_ffn_body, sub=sub, has_proj=has_proj),
        grid=(t // tm,),
        in_specs=in_specs,
        out_specs=pl.BlockSpec((tm, d), lambda i: (i, 0)),
        out_shape=jax.ShapeDtypeStruct((t, d), _F32),
        scratch_shapes=[pltpu.VMEM((tm, d), _BF16), pltpu.VMEM((tm, d), _F32)],
        compiler_params=_params(1),
        name="ffn_proj" if has_proj else "ffn",
    )(*args)


def _head_norm_rope_t(yt, g_t, rope_ref):
    ss = jnp.sum(yt * yt, axis=0, keepdims=True)
    yn = (yt * lax.rsqrt(ss * (1.0 / HEAD_DIM) + EPS)) * g_t
    if rope_ref is None:
        return yn
    f = ROPE_FREQS
    x1r, x2r, x1c, x2c = yn[0:f], yn[f:2 * f], yn[2 * f:3 * f], yn[3 * f:4 * f]
    cr, sr, cc, sc = rope_ref[0], rope_ref[1], rope_ref[2], rope_ref[3]
    return jnp.concatenate([x1r * cr - x2r * sr, x2r * cr + x1r * sr,
                            x1c * cc - x2c * sc, x2c * cc + x1c * sc], axis=0)


def _qkv_body(*refs, rope, keep_f32):
    n_in = 7 if rope else 6
    x_ref, mods_ref, ng_ref, w_ref, qg_ref, kg_ref = refs[:6]
    rope_ref = refs[6] if rope else None
    qt_ref, ka_ref, vt_ref = refs[n_in:n_in + 3]
    m = mods_ref[0, 0]
    a = _modln(x_ref[...], ng_ref[0, 1:2], m[3:4], m[4:5]).astype(_BF16)
    proj = jnp.dot(a, w_ref[0], preferred_element_type=_F32)
    qg, kg = qg_ref[0], kg_ref[0]
    for c in range(N_HEADS * HEAD_DIM // GROUP_COLS):
        yt = proj[:, c * GROUP_COLS:(c + 1) * GROUP_COLS].T
        for h in range(KV_GROUP):
            qn = _head_norm_rope_t(yt[h * HEAD_DIM:(h + 1) * HEAD_DIM], qg, rope_ref)
            r0 = c * GROUP_COLS + h * HEAD_DIM
            qt_ref[0, r0:r0 + HEAD_DIM, :] = (qn * Q_SCALE).astype(qt_ref.dtype)
    k0 = N_HEADS * HEAD_DIM
    kt = proj[:, k0:k0 + KV_COLS].T
    kn = jnp.concatenate([_head_norm_rope_t(kt[g * HEAD_DIM:(g + 1) * HEAD_DIM], kg, rope_ref)
                          for g in range(N_KV_HEADS)], axis=0)
    k_tok = kn.T
    v_tok = proj[:, k0 + KV_COLS:]
    vt = v_tok.T
    tm = k_tok.shape[0]
    row = lax.broadcasted_iota(jnp.int32, (ACC_ROWS - HEAD_DIM, V_PIECE), 0)
    extra = jnp.where(row == 0, 1.0, 0.0).astype(vt_ref.dtype)
    for g in range(N_KV_HEADS):
        ka_ref[0, g] = k_tok[:, g * HEAD_DIM:(g + 1) * HEAD_DIM].astype(ka_ref.dtype)
        for j in range(tm // V_PIECE):
            vt_ref[0, g, j, 0:HEAD_DIM, :] = (
                vt[g * HEAD_DIM:(g + 1) * HEAD_DIM, j * V_PIECE:(j + 1) * V_PIECE].astype(vt_ref.dtype))
            vt_ref[0, g, j, HEAD_DIM:ACC_ROWS, :] = extra
    if keep_f32:
        k_ref, v_ref = refs[n_in + 3:n_in + 5]
        k_ref[...] = k_tok
        v_ref[...] = v_tok


def _qkv(x, mods, norm_g, w_qkv, qg_t, kg_t, layer, row_of_tile, tm, seq_len, keep_f32, rope_t=None):
    t, d = x.shape
    rope = rope_t is not None
    tiles_per_seq = seq_len // tm
    n_seq = t // seq_len
    seq_tile = lambda i: (i // tiles_per_seq, 0, i % tiles_per_seq)
    out_specs = [pl.BlockSpec((1, N_HEADS * HEAD_DIM, tm), seq_tile),
                 pl.BlockSpec((1, N_KV_HEADS, tm, HEAD_DIM), lambda i: seq_tile(i) + (0,)),
                 pl.BlockSpec((1, N_KV_HEADS, tm // V_PIECE, ACC_ROWS, V_PIECE), lambda i: seq_tile(i) + (0, 0))]
    out_shape = [jax.ShapeDtypeStruct((n_seq, N_HEADS * HEAD_DIM, seq_len), _BF16),
                 jax.ShapeDtypeStruct((n_seq, N_KV_HEADS, seq_len, HEAD_DIM), _BF16),
                 jax.ShapeDtypeStruct((n_seq, N_KV_HEADS, seq_len // V_PIECE, ACC_ROWS, V_PIECE), _BF16)]
    if keep_f32:
        out_specs += [pl.BlockSpec((tm, KV_COLS), lambda i: (i, 0))] * 2
        out_shape += [jax.ShapeDtypeStruct((t, KV_COLS), _F32)] * 2
    in_specs = [
        pl.BlockSpec((tm, d), lambda i: (i, 0)),
        pl.BlockSpec((1, 1, N_MOD, d), lambda i: (layer, row_of_tile(i), 0, 0)),
        pl.BlockSpec((1, 3, d), lambda i: (layer, 0, 0)),
        _resident((1, d, QKV_COLS), lambda i: (layer, 0, 0)),
        pl.BlockSpec((1, HEAD_DIM, tm), lambda i: (layer, 0, 0)),
        pl.BlockSpec((1, HEAD_DIM, tm), lambda i: (layer, 0, 0)),
    ]
    args = [x, mods, norm_g, w_qkv, qg_t, kg_t]
    if rope:
        in_specs += [pl.BlockSpec((4, ROPE_FREQS, tm), lambda i: (0, 0, i % tiles_per_seq))]
        args += [rope_t]
    return pl.pallas_call(
        functools.partial(_qkv_body, rope=rope, keep_f32=keep_f32),
        grid=(t // tm,),
        in_specs=in_specs,
        out_specs=out_specs,
        out_shape=out_shape,
        compiler_params=_params(1),
        name="qkv_rope" if rope else "qkv",
    )(*args)


def _rope_table_t(n_tok):
    pos = jnp.arange(n_tok)
    row = (pos // GRID_W).astype(_F32)
    col = (pos % GRID_W).astype(_F32)
    freqs = 1.0 / jnp.power(ROPE_THETA, jnp.arange(ROPE_FREQS, dtype=_F32) / ROPE_FREQS)
    ang_r = row[:, None] * freqs
    ang_c = col[:, None] * freqs
    return jnp.stack([jnp.cos(ang_r).T, jnp.sin(ang_r).T, jnp.cos(ang_c).T, jnp.sin(ang_c).T])


KV_SUB = 256
V_PIECE = 128
SCORE_ROWS = 64
BF16_SUBLANES = 16
ACC_ROWS = HEAD_DIM + BF16_SUBLANES


SCORE_AHEAD = 2
VALUE_BEHIND = 1
S_SLOTS = SCORE_AHEAD + 1
P_SLOTS = VALUE_BEHIND + 1


def _attn_scratch(n_states, tq, tk):
    return [pltpu.VMEM((S_SLOTS, tk, tq), _F32),
            pltpu.VMEM((S_SLOTS, 1, tq), _F32),
            pltpu.VMEM((P_SLOTS, tk, tq), _BF16),
            pltpu.VMEM((P_SLOTS, 1, tq), _F32),
            pltpu.VMEM((n_states, 1, tq), _F32),
            pltpu.VMEM((n_states, ACC_ROWS, tq), _F32)]


def _init_states(scr, sinks=None):
    m_ref, acc_ref = scr[4], scr[5]
    tq = m_ref.shape[-1]
    if sinks is None:
        m_ref[...] = jnp.full(m_ref.shape, NEG_INF, _F32)
        acc_ref[...] = jnp.zeros(acc_ref.shape, _F32)
    else:
        for i, sk in enumerate(sinks):
            m_ref[i] = jnp.full((1, tq), sk * LOG2_E, _F32)
        row = lax.broadcasted_iota(jnp.int32, (ACC_ROWS, tq), 0)
        acc_ref[...] = jnp.broadcast_to(jnp.where(row == HEAD_DIM, 1.0, 0.0).astype(_F32), acc_ref.shape)


def _part(j):
    return slice(j * KV_SUB, (j + 1) * KV_SUB)


def _scores(scr, slot, j, item):
    s_ref, mc_ref = scr[0], scr[1]
    _, k, qt, _, bias = item
    kj = k[_part(j)]
    s = jnp.concatenate([jnp.dot(kj[r:r + SCORE_ROWS], qt, preferred_element_type=_F32)
                         for r in range(0, KV_SUB, SCORE_ROWS)], axis=0)
    if bias is not None:
        s = s + bias[_part(j)]
    s_ref[slot, _part(j)] = s
    mx = jnp.max(s, axis=0, keepdims=True)
    mc_ref[slot] = mx if j == 0 else jnp.maximum(mc_ref[slot], mx)


def _new_max(scr, s_slot, p_slot, st):
    mc_ref, al_ref, m_ref = scr[1], scr[3], scr[4]
    m_old = m_ref[st]
    m_new = jnp.maximum(m_old, mc_ref[s_slot])
    al_ref[p_slot] = jnp.exp2(m_old - m_new)
    m_ref[st] = m_new


def _softmax(scr, s_slot, p_slot, j, st):
    s_ref, p_ref, m_ref = scr[0], scr[2], scr[4]
    p_ref[p_slot, _part(j)] = jnp.exp2(s_ref[s_slot, _part(j)] - m_ref[st]).astype(_BF16)


def _values(scr, slot, j, item):
    p_ref, al_ref, acc_ref = scr[2], scr[3], scr[5]
    st, vts = item[0], item[3]
    pv = jnp.dot(vts[j], p_ref[slot, _part(j)], preferred_element_type=_F32)
    acc = acc_ref[st]
    acc_ref[st] = (al_ref[slot] * acc if j == 0 else acc) + pv


def _pipeline(scr, items):
    n = len(items)
    n_parts = len(items[0][3])
    for i0 in range(min(SCORE_AHEAD, n)):
        for j in range(n_parts):
            _scores(scr, i0 % S_SLOTS, j, items[i0])
    for i in range(n + VALUE_BEHIND):
        if i < n:
            _new_max(scr, i % S_SLOTS, i % P_SLOTS, items[i][0])
        for j in range(n_parts):
            if i + SCORE_AHEAD < n:
                _scores(scr, (i + SCORE_AHEAD) % S_SLOTS, j, items[i + SCORE_AHEAD])
            if i < n:
                _softmax(scr, i % S_SLOTS, i % P_SLOTS, j, items[i][0])
            if i >= VALUE_BEHIND:
                _values(scr, (i - VALUE_BEHIND) % P_SLOTS, j, items[i - VALUE_BEHIND])


def _finish(scr, first):
    acc_ref = scr[5]
    ot = jnp.concatenate([acc_ref[first + h, 0:HEAD_DIM] / acc_ref[first + h, HEAD_DIM:HEAD_DIM + 1]
                          for h in range(KV_GROUP)], axis=0)
    return ot.T


def _heads(qt_ref, g=None):
    base = 0 if g is None else g * GROUP_COLS
    return [qt_ref[0, base + h * HEAD_DIM: base + (h + 1) * HEAD_DIM, :] for h in range(KV_GROUP)]


def _vt_part(vt_ref, lead, piece):
    return jnp.concatenate([vt_ref[lead + (piece,)], vt_ref[lead + (piece + 1,)]], axis=1)


def _query_blocks(qt_ref, tq):
    return [[qt_ref[0, h * HEAD_DIM:(h + 1) * HEAD_DIM, b * tq:(b + 1) * tq] for h in range(KV_GROUP)]
            for b in range(qt_ref.shape[2] // tq)]


def _chunk_items(k_ref, vt_ref, blocks, tk):
    items = []
    for c in range(k_ref.shape[2] // tk):
        k = k_ref[0, 0, c * tk:(c + 1) * tk, :]
        vts = [_vt_part(vt_ref, (0, 0), (c * tk + j * KV_SUB) // V_PIECE) for j in range(tk // KV_SUB)]
        for b, heads in enumerate(blocks):
            items += [(b * KV_GROUP + h, k, heads[h], vts, None) for h in range(KV_GROUP)]
    return items


def _store_blocks(scr, o_ref, tq, n_blocks):
    for b in range(n_blocks):
        o_ref[0, b * tq:(b + 1) * tq, :] = _finish(scr, b * KV_GROUP).astype(o_ref.dtype)


def _full_body(qt_ref, kc_ref, vtc_ref, kl_ref, vtl_ref, o_ref, *scr, tq, tk):
    blocks = _query_blocks(qt_ref, tq)
    _init_states(scr)
    _pipeline(scr, _chunk_items(kc_ref, vtc_ref, blocks, tk) + _chunk_items(kl_ref, vtl_ref, blocks, tk))
    _store_blocks(scr, o_ref, tq, len(blocks))


def _kv_specs(lk, index_map):
    return [pl.BlockSpec((1, 1, lk, HEAD_DIM), lambda *i: index_map(*i) + (0, 0)),
            pl.BlockSpec((1, 1, lk // V_PIECE, ACC_ROWS, V_PIECE), lambda *i: index_map(*i) + (0, 0, 0))]


def _attend_full(qt, k_ctx, vt_ctx, k_lat, vt_lat, tq, tk, n_blocks):
    b, d, l = qt.shape
    ts = tq * n_blocks
    by_group = lambda bi, g, qi: (bi, g)
    return pl.pallas_call(
        functools.partial(_full_body, tq=tq, tk=tk),
        grid=(b, N_KV_HEADS, l // ts),
        in_specs=([pl.BlockSpec((1, GROUP_COLS, ts), lambda bi, g, qi: (bi, g, qi))]
                  + _kv_specs(k_ctx.shape[2], by_group) + _kv_specs(l, by_group)),
        out_specs=pl.BlockSpec((1, ts, GROUP_COLS), lambda bi, g, qi: (bi, qi, g)),
        out_shape=jax.ShapeDtypeStruct((b, l, d), _BF16),
        scratch_shapes=_attn_scratch(KV_GROUP * n_blocks, tq, tk),
        compiler_params=_params(3),
        name="attn_full",
    )(qt, k_ctx, vt_ctx, k_lat, vt_lat)


def _band_start(q0, l, tq):
    return jnp.clip(q0 - WINDOW, 0, l - 2 * tq)


def _window_body(sink_ref, qt_ref, kc_ref, vtc_ref, kl_ref, vtl_ref, bias_ref, o_ref, *scr, tq):
    g = pl.program_id(1)
    blocks = _query_blocks(qt_ref, tq)
    _init_states(scr, [sink_ref[g * KV_GROUP + h] for _ in blocks for h in range(KV_GROUP)])
    items = _chunk_items(kc_ref, vtc_ref, blocks, 2 * tq)
    for b, heads in enumerate(blocks):
        q0 = (pl.program_id(2) * len(blocks) + b) * tq
        start = pl.multiple_of(_band_start(q0, kl_ref.shape[2], tq), V_PIECE)
        k_band = kl_ref[0, 0, pl.ds(start, 2 * tq), :]
        vt_band = [_vt_part(vtl_ref, (0, 0), start // V_PIECE + j * (KV_SUB // V_PIECE)) for j in range(2)]
        items += [(b * KV_GROUP + h, k_band, heads[h], vt_band, bias_ref[b]) for h in range(KV_GROUP)]
    _pipeline(scr, items)
    _store_blocks(scr, o_ref, tq, len(blocks))


def _window_bias(l, tq):
    q0 = (jnp.arange(l // tq) * tq)[:, None, None]
    kpos = _band_start(q0, l, tq) + jnp.arange(2 * tq)[None, :, None]
    qpos = q0 + jnp.arange(tq)[None, None, :]
    return jnp.where(jnp.abs(qpos - kpos) <= WINDOW, 0.0, NEG_INF).astype(_F32)


def _attend_window(qt, k_ctx, vt_ctx, k_lat, vt_lat, sink, tq, n_blocks):
    b, d, l = qt.shape
    lc = k_ctx.shape[2]
    ts = tq * n_blocks
    assert tq == KV_SUB and tq >= 2 * WINDOW and lc == 2 * tq and l >= 2 * tq
    by_group = lambda bi, g, qi: (bi, g)
    return pl.pallas_call(
        functools.partial(_window_body, tq=tq),
        grid=(b, N_KV_HEADS, l // ts),
        in_specs=([pl.BlockSpec(memory_space=pltpu.SMEM),
                   pl.BlockSpec((1, GROUP_COLS, ts), lambda bi, g, qi: (bi, g, qi))]
                  + _kv_specs(lc, by_group) + _kv_specs(l, by_group)
                  + [pl.BlockSpec((n_blocks, 2 * tq, tq), lambda bi, g, qi: (qi, 0, 0))]),
        out_specs=pl.BlockSpec((1, ts, GROUP_COLS), lambda bi, g, qi: (bi, qi, g)),
        out_shape=jax.ShapeDtypeStruct((b, l, d), _BF16),
        scratch_shapes=_attn_scratch(KV_GROUP * n_blocks, tq, 2 * tq),
        compiler_params=_params(3),
        name="attn_window",
    )(sink, qt, k_ctx, vt_ctx, k_lat, vt_lat, _window_bias(l, tq))


def _ctx_body(*refs, has_sink):
    if has_sink:
        sink_ref, qt_ref, k_ref, vt_ref, o_ref = refs[:5]
    else:
        qt_ref, k_ref, vt_ref, o_ref = refs[:4]
    scr = refs[5:] if has_sink else refs[4:]
    n_sub = k_ref.shape[2] // KV_SUB
    _init_states(scr, [sink_ref[i] for i in range(N_HEADS)] if has_sink else None)
    items = []
    for g in range(N_KV_HEADS):
        heads = _heads(qt_ref, g)
        k = k_ref[0, g]
        vts = [_vt_part(vt_ref, (0, g), j * (KV_SUB // V_PIECE)) for j in range(n_sub)]
        items += [(g * KV_GROUP + h, k, heads[h], vts, None) for h in range(KV_GROUP)]
    _pipeline(scr, items)
    for g in range(N_KV_HEADS):
        o_ref[0, :, g * GROUP_COLS:(g + 1) * GROUP_COLS] = _finish(scr, g * KV_GROUP).astype(o_ref.dtype)


def _attend_ctx(qt, k, vt, sink):
    b, d, l = qt.shape
    has_sink = sink is not None
    in_specs = [
        pl.BlockSpec((1, d, l), lambda bi: (bi, 0, 0)),
        pl.BlockSpec((1, N_KV_HEADS, l, HEAD_DIM), lambda bi: (bi, 0, 0, 0)),
        pl.BlockSpec((1, N_KV_HEADS, l // V_PIECE, ACC_ROWS, V_PIECE), lambda bi: (bi, 0, 0, 0, 0)),
    ]
    args = [qt, k, vt]
    if has_sink:
        in_specs = [pl.BlockSpec(memory_space=pltpu.SMEM)] + in_specs
        args = [sink] + args
    return pl.pallas_call(
        functools.partial(_ctx_body, has_sink=has_sink),
        grid=(b,),
        in_specs=in_specs,
        out_specs=pl.BlockSpec((1, l, d), lambda bi: (bi, 0, 0)),
        out_shape=jax.ShapeDtypeStruct((b, l, d), _BF16),
        scratch_shapes=_attn_scratch(N_HEADS, l, l),
        compiler_params=_params(1),
        name="attn_ctx_sink" if has_sink else "attn_ctx",
    )(*args)


def _cache_layout(cache_k, cache_v):
    b, depth, p, g, hd = cache_v.shape
    k = cache_k.transpose(1, 0, 3, 2, 4).astype(_BF16)
    vt = cache_v.reshape(b, depth, p // V_PIECE, V_PIECE, g, hd).transpose(1, 0, 4, 2, 5, 3).astype(_BF16)
    extra = jnp.zeros((ACC_ROWS - hd, V_PIECE), _BF16).at[0].set(1.0)
    return k, jnp.concatenate([vt, jnp.broadcast_to(extra, vt.shape[:4] + extra.shape)], axis=4)


def _token_tile(t, cap):
    tm = min(cap, t)
    assert t % tm == 0
    return tm


def kernel(x_prompt, x_sample, cache_k, cache_v, c, c_ctx, w_mod, b_mod, norm_g, w_qkv, w_o,
           q_norm_g, k_norm_g, sink, w_ffn_in, w_ffn_out):
    depth = w_mod.shape[0]
    d = D_MODEL
    bp, lp, _ = x_prompt.shape
    bs, ls, _ = x_sample.shape
    past = cache_k.shape[2]
    assert lp % KV_SUB == 0 and ls % KV_TILE == 0 and past % KV_TILE == 0

    w_in = (w_ffn_in.astype(_BF16).reshape(depth, 2, d, 2 * N_FF_CHUNKS, FF_CHUNK)
            .transpose(0, 1, 3, 2, 4))
    w_out = w_ffn_out.astype(_BF16).reshape(depth, 2, N_FF_CHUNKS, FF_CHUNK, d)
    w_qkv_b = w_qkv.astype(_BF16)
    w_o_b = w_o.astype(_BF16)

    n_rows = -(-(bs + 1) // 8) * 8
    cond = jnp.concatenate([c, c_ctx[None, :], jnp.zeros((n_rows - bs - 1, d), _F32)], axis=0)
    mods = _modulation(cond, w_mod, b_mod)

    tp = bp * lp
    tm_p = _token_tile(tp, FFN_TILE)
    tq_p = _token_tile(lp, QKV_TILE)
    ctx_row = lambda i: bs
    qg_p = jnp.broadcast_to(q_norm_g[:, :, None], (depth, HEAD_DIM, tq_p))
    kg_p = jnp.broadcast_to(k_norm_g[:, :, None], (depth, HEAD_DIM, tq_p))
    h = x_prompt.reshape(tp, d)
    new_ks, new_vs = [], []
    for i in range(depth):
        h = _ffn(h, mods, norm_g, w_in, w_out, i, 0, ctx_row, tm_p)
        qt, ka, vt, k, v = _qkv(h, mods, norm_g, w_qkv_b, qg_p, kg_p, i, ctx_row, tq_p, lp, True)
        new_ks.append(k.reshape(bp, lp, N_KV_HEADS, HEAD_DIM))
        new_vs.append(v.reshape(bp, lp, N_KV_HEADS, HEAD_DIM))
        o = _attend_ctx(qt, ka, vt, sink[i // 2] if i % 2 == 1 else None)
        h = _ffn(h, mods, norm_g, w_in, w_out, i, 1, ctx_row, tm_p, o=o.reshape(tp, d), w_o=w_o_b)
    y_prompt = h.reshape(bp, lp, d)
    new_k = jnp.stack(new_ks, axis=1)
    new_v = jnp.stack(new_vs, axis=1)

    ts = bs * ls
    tm_s = _token_tile(ls, FFN_TILE)
    tq_s = _token_tile(ls, QKV_TILE)
    lat_row = lambda i: i // (ls // tm_s)
    lat_row_q = lambda i: i // (ls // tq_s)
    rope_t = _rope_table_t(ls)
    qg_s = jnp.broadcast_to(q_norm_g[:, :, None], (depth, HEAD_DIM, tq_s))
    kg_s = jnp.broadcast_to(k_norm_g[:, :, None], (depth, HEAD_DIM, tq_s))
    k_ctx, vt_ctx = _cache_layout(cache_k, cache_v)
    blocks_per_step = lambda want: want if ls % (want * Q_TILE) == 0 else 1
    h = x_sample.reshape(ts, d)
    for i in range(depth):
        h = _ffn(h, mods, norm_g, w_in, w_out, i, 0, lat_row, tm_s)
        qt, ka, vt = _qkv(h, mods, norm_g, w_qkv_b, qg_s, kg_s, i, lat_row_q, tq_s, ls, False, rope_t)
        if i % 2 == 0:
            o = _attend_full(qt, k_ctx[i], vt_ctx[i], ka, vt, Q_TILE, KV_TILE, blocks_per_step(FULL_Q_BLOCKS))
        else:
            o = _attend_window(qt, k_ctx[i], vt_ctx[i], ka, vt, sink[i // 2], Q_TILE,
                               blocks_per_step(WINDOW_Q_BLOCKS))
        h = _ffn(h, mods, norm_g, w_in, w_out, i, 1, lat_row, tm_s, o=o.reshape(ts, d), w_o=w_o_b)
    y_sample = h.reshape(bs, ls, d)

    return (y_prompt, y_sample, new_k, new_v)
```

```python
import functools

import jax
import jax.numpy as jnp
from jax import lax
from jax.experimental import pallas as pl
from jax.experimental.pallas import tpu as pltpu

D_MODEL = 1024
N_HEADS = 16
N_KV_HEADS = 4
HEAD_DIM = 64
KV_GROUP = N_HEADS // N_KV_HEADS
GROUP_COLS = KV_GROUP * HEAD_DIM
KV_COLS = N_KV_HEADS * HEAD_DIM
QKV_COLS = (N_HEADS + 2 * N_KV_HEADS) * HEAD_DIM
D_FF = 2816
N_MOD = 9
GRID_W = 64
WINDOW = 128
ROPE_FREQS = HEAD_DIM // 4
ROPE_THETA = 10000.0
ATTN_SCALE = HEAD_DIM ** -0.5
LOG2_E = 1.4426950408889634
Q_SCALE = ATTN_SCALE * LOG2_E
EPS = 1e-6
NEG_INF = -1e30

FF_CHUNK = 256
N_FF_CHUNKS = D_FF // FF_CHUNK
FFN_TILE = 1024
QKV_TILE = 1024
QKV_SUB = 512
Q_TILE = 256
FULL_Q_BLOCKS = 2
WINDOW_Q_BLOCKS = 4
KV_TILE = 512
VMEM_LIMIT_BYTES = 52 * 1024 * 1024

_BF16 = jnp.bfloat16
_F32 = jnp.float32


def _params(n_axes):
    return pltpu.CompilerParams(dimension_semantics=("arbitrary",) * n_axes,
                                vmem_limit_bytes=VMEM_LIMIT_BYTES)


def _resident(block_shape, index_map):
    return pl.BlockSpec(block_shape, index_map, pipeline_mode=pl.Buffered(1))


def _silu(x):
    return x * jax.nn.sigmoid(x)


def _modln(x, g, shift, scale):
    y = x * lax.rsqrt(jnp.mean(x * x, axis=-1, keepdims=True) + EPS)
    return y * (g * (1.0 + scale)) + shift


def _mod_body(cond_ref, w_ref, b_ref, o_ref):
    a = _silu(cond_ref[...]).astype(_BF16)
    w = w_ref[0].astype(_BF16)
    o_ref[0] = jnp.dot(a, w, preferred_element_type=_F32) + b_ref[0]


def _modulation(cond, w_mod, b_mod):
    depth, d, n = w_mod.shape
    r = cond.shape[0]
    tn = n // 8
    out = pl.pallas_call(
        _mod_body,
        grid=(depth, n // tn),
        in_specs=[
            pl.BlockSpec((r, d), lambda i, j: (0, 0)),
            pl.BlockSpec((1, d, tn), lambda i, j: (i, 0, j)),
            pl.BlockSpec((1, 1, tn), lambda i, j: (i, 0, j)),
        ],
        out_specs=pl.BlockSpec((1, r, tn), lambda i, j: (i, 0, j)),
        out_shape=jax.ShapeDtypeStruct((depth, r, n), _F32),
        compiler_params=_params(2),
        name="modulation",
    )(cond, w_mod, b_mod.reshape(depth, 1, n))
    return out.reshape(depth, r, N_MOD, d)


def _ffn_body(*refs, sub, has_proj):
    if has_proj:
        x_ref, mods_ref, ng_ref, o_ref, wo_ref, win_ref, wout_ref, out_ref, xn_ref, acc_ref = refs
    else:
        x_ref, mods_ref, ng_ref, win_ref, wout_ref, out_ref, xn_ref, acc_ref = refs
    m = mods_ref[0, 0]
    x = x_ref[...]
    if has_proj:
        x = x + m[5:6] * jnp.dot(o_ref[...], wo_ref[0], preferred_element_type=_F32)
    k = 3 * sub
    xn_ref[...] = _modln(x, ng_ref[0, sub:sub + 1], m[k:k + 1], m[k + 1:k + 2]).astype(_BF16)
    if has_proj:
        out_ref[...] = x

    def ffn_chunk(c):
        xn = xn_ref[...]
        gate = jnp.dot(xn, win_ref[0, 0, c], preferred_element_type=_F32)
        up = jnp.dot(xn, win_ref[0, 0, c + N_FF_CHUNKS], preferred_element_type=_F32)
        act = (_silu(gate) * up).astype(_BF16)
        return jnp.dot(act, wout_ref[0, 0, c], preferred_element_type=_F32)

    acc_ref[...] = ffn_chunk(0)

    def chunk(c, carry):
        acc_ref[...] += ffn_chunk(c)
        return carry

    lax.fori_loop(1, N_FF_CHUNKS, chunk, 0, unroll=5)
    h = out_ref[...] if has_proj else x_ref[...]
    out_ref[...] = h + (0.5 * m[k + 2:k + 3]) * acc_ref[...]


def _ffn(x, mods, norm_g, w_in, w_out, layer, which, row_of_tile, tm, o=None, w_o=None):
    t, d = x.shape
    sub = 2 * which
    has_proj = o is not None
    in_specs = [
        pl.BlockSpec((tm, d), lambda i: (i, 0)),
        pl.BlockSpec((1, 1, N_MOD, d), lambda i: (layer, row_of_tile(i), 0, 0)),
        pl.BlockSpec((1, 3, d), lambda i: (layer, 0, 0)),
    ]
    args = [x, mods, norm_g]
    if has_proj:
        in_specs += [pl.BlockSpec((tm, d), lambda i: (i, 0)),
                     _resident((1, d, d), lambda i: (layer, 0, 0))]
        args += [o, w_o]
    in_specs += [
        _resident((1, 1, 2 * N_FF_CHUNKS, d, FF_CHUNK), lambda i: (layer, which, 0, 0, 0)),
        _resident((1, 1, N_FF_CHUNKS, FF_CHUNK, d), lambda i: (layer, which, 0, 0, 0)),
    ]
    args += [w_in, w_out]
    return pl.pallas_call(
        functools.partial(_ffn_body, sub=sub, has_proj=has_proj),
        grid=(t // tm,),
        in_specs=in_specs,
        out_specs=pl.BlockSpec((tm, d), lambda i: (i, 0)),
        out_shape=jax.ShapeDtypeStruct((t, d), _F32),
        scratch_shapes=[pltpu.VMEM((tm, d), _BF16), pltpu.VMEM((tm, d), _F32)],
        compiler_params=_params(1),
        name="ffn_proj" if has_proj else "ffn",
    )(*args)


def _head_norm_rope_t(yt, g_t, rope_ref):
    ss = jnp.sum(yt * yt, axis=0, keepdims=True)
    yn = (yt * lax.rsqrt(ss * (1.0 / HEAD_DIM) + EPS)) * g_t
    if rope_ref is None:
        return yn
    f = ROPE_FREQS
    x1r, x2r, x1c, x2c = yn[0:f], yn[f:2 * f], yn[2 * f:3 * f], yn[3 * f:4 * f]
    cr, sr, cc, sc = rope_ref[0], rope_ref[1], rope_ref[2], rope_ref[3]
    return jnp.concatenate([x1r * cr - x2r * sr, x2r * cr + x1r * sr,
                            x1c * cc - x2c * sc, x2c * cc + x1c * sc], axis=0)


def _qkv_body(*refs, rope, keep_f32, sub):
    n_in = 7 if rope else 6
    x_ref, mods_ref, ng_ref, w_ref, qg_ref, kg_ref = refs[:6]
    rope_ref = refs[6] if rope else None
    qt_ref, ka_ref, vt_ref = refs[n_in:n_in + 3]
    m = mods_ref[0, 0]
    tm = x_ref.shape[0]
    k0 = N_HEADS * HEAD_DIM
    row = lax.broadcasted_iota(jnp.int32, (ACC_ROWS - HEAD_DIM, V_PIECE), 0)
    extra = jnp.where(row == 0, 1.0, 0.0).astype(vt_ref.dtype)

    def project(r):
        rows = slice(r * sub, (r + 1) * sub)
        a = _modln(x_ref[rows, :], ng_ref[0, 1:2], m[3:4], m[4:5]).astype(_BF16)
        return jnp.dot(a, w_ref[0], preferred_element_type=_F32)

    def finish(r, proj):
        toks = slice(r * sub, (r + 1) * sub)
        qg, kg = qg_ref[0, :, toks], kg_ref[0, :, toks]
        rope_r = None if rope_ref is None else rope_ref.at[:, :, toks]
        for c in range(k0 // GROUP_COLS):
            yt = proj[:, c * GROUP_COLS:(c + 1) * GROUP_COLS].T
            for h in range(KV_GROUP):
                qn = _head_norm_rope_t(yt[h * HEAD_DIM:(h + 1) * HEAD_DIM], qg, rope_r)
                r0 = c * GROUP_COLS + h * HEAD_DIM
                qt_ref[0, r0:r0 + HEAD_DIM, toks] = (qn * Q_SCALE).astype(qt_ref.dtype)
        kt = proj[:, k0:k0 + KV_COLS].T
        kn = jnp.concatenate([_head_norm_rope_t(kt[g * HEAD_DIM:(g + 1) * HEAD_DIM], kg, rope_r)
                              for g in range(N_KV_HEADS)], axis=0)
        k_tok = kn.T
        v_tok = proj[:, k0 + KV_COLS:]
        vt = v_tok.T
        for g in range(N_KV_HEADS):
            ka_ref[0, g, toks, :] = k_tok[:, g * HEAD_DIM:(g + 1) * HEAD_DIM].astype(ka_ref.dtype)
            for j in range(sub // V_PIECE):
                piece = r * (sub // V_PIECE) + j
                vt_ref[0, g, piece, 0:HEAD_DIM, :] = (
                    vt[g * HEAD_DIM:(g + 1) * HEAD_DIM, j * V_PIECE:(j + 1) * V_PIECE].astype(vt_ref.dtype))
                vt_ref[0, g, piece, HEAD_DIM:ACC_ROWS, :] = extra
        if keep_f32:
            k_ref, v_ref = refs[n_in + 3:n_in + 5]
            k_ref[toks, :] = k_tok
            v_ref[toks, :] = v_tok

    n_sub = tm // sub
    proj = project(0)
    for r in range(n_sub):
        nxt = project(r + 1) if r + 1 < n_sub else None
        finish(r, proj)
        proj = nxt


def _qkv(x, mods, norm_g, w_qkv, qg_t, kg_t, layer, row_of_tile, tm, seq_len, keep_f32, rope_t=None):
    t, d = x.shape
    rope = rope_t is not None
    tiles_per_seq = seq_len // tm
    n_seq = t // seq_len
    seq_tile = lambda i: (i // tiles_per_seq, 0, i % tiles_per_seq)
    out_specs = [pl.BlockSpec((1, N_HEADS * HEAD_DIM, tm), seq_tile),
                 pl.BlockSpec((1, N_KV_HEADS, tm, HEAD_DIM), lambda i: seq_tile(i) + (0,)),
                 pl.BlockSpec((1, N_KV_HEADS, tm // V_PIECE, ACC_ROWS, V_PIECE), lambda i: seq_tile(i) + (0, 0))]
    out_shape = [jax.ShapeDtypeStruct((n_seq, N_HEADS * HEAD_DIM, seq_len), _BF16),
                 jax.ShapeDtypeStruct((n_seq, N_KV_HEADS, seq_len, HEAD_DIM), _BF16),
                 jax.ShapeDtypeStruct((n_seq, N_KV_HEADS, seq_len // V_PIECE, ACC_ROWS, V_PIECE), _BF16)]
    if keep_f32:
        out_specs += [pl.BlockSpec((tm, KV_COLS), lambda i: (i, 0))] * 2
        out_shape += [jax.ShapeDtypeStruct((t, KV_COLS), _F32)] * 2
    in_specs = [
        pl.BlockSpec((tm, d), lambda i: (i, 0)),
        pl.BlockSpec((1, 1, N_MOD, d), lambda i: (layer, row_of_tile(i), 0, 0)),
        pl.BlockSpec((1, 3, d), lambda i: (layer, 0, 0)),
        _resident((1, d, QKV_COLS), lambda i: (layer, 0, 0)),
        pl.BlockSpec((1, HEAD_DIM, tm), lambda i: (layer, 0, 0)),
        pl.BlockSpec((1, HEAD_DIM, tm), lambda i: (layer, 0, 0)),
    ]
    args = [x, mods, norm_g, w_qkv, qg_t, kg_t]
    if rope:
        in_specs += [pl.BlockSpec((4, ROPE_FREQS, tm), lambda i: (0, 0, i % tiles_per_seq))]
        args += [rope_t]
    return pl.pallas_call(
        functools.partial(_qkv_body, rope=rope, keep_f32=keep_f32, sub=min(tm, QKV_SUB)),
        grid=(t // tm,),
        in_specs=in_specs,
        out_specs=out_specs,
        out_shape=out_shape,
        compiler_params=_params(1),
        name="qkv_rope" if rope else "qkv",
    )(*args)


def _rope_table_t(n_tok):
    pos = jnp.arange(n_tok)
    row = (pos // GRID_W).astype(_F32)
    col = (pos % GRID_W).astype(_F32)
    freqs = 1.0 / jnp.power(ROPE_THETA, jnp.arange(ROPE_FREQS, dtype=_F32) / ROPE_FREQS)
    ang_r = row[:, None] * freqs
    ang_c = col[:, None] * freqs
    return jnp.stack([jnp.cos(ang_r).T, jnp.sin(ang_r).T, jnp.cos(ang_c).T, jnp.sin(ang_c).T])


KV_SUB = 256
V_PIECE = 128
SCORE_ROWS = 64
BF16_SUBLANES = 16
ACC_ROWS = HEAD_DIM + BF16_SUBLANES


SCORE_AHEAD = 2
VALUE_BEHIND = 1
S_SLOTS = SCORE_AHEAD + 1
P_SLOTS = VALUE_BEHIND + 1


def _attn_scratch(n_states, tq, tk):
    return [pltpu.VMEM((S_SLOTS, tk, tq), _F32),
            pltpu.VMEM((S_SLOTS, 1, tq), _F32),
            pltpu.VMEM((P_SLOTS, tk, tq), _BF16),
            pltpu.VMEM((P_SLOTS, 1, tq), _F32),
            pltpu.VMEM((n_states, 1, tq), _F32),
            pltpu.VMEM((n_states, ACC_ROWS, tq), _F32)]


def _init_states(scr, sinks=None):
    m_ref, acc_ref = scr[4], scr[5]
    tq = m_ref.shape[-1]
    if sinks is None:
        m_ref[...] = jnp.full(m_ref.shape, NEG_INF, _F32)
        acc_ref[...] = jnp.zeros(acc_ref.shape, _F32)
    else:
        for i, sk in enumerate(sinks):
            m_ref[i] = jnp.full((1, tq), sk * LOG2_E, _F32)
        row = lax.broadcasted_iota(jnp.int32, (ACC_ROWS, tq), 0)
        acc_ref[...] = jnp.broadcast_to(jnp.where(row == HEAD_DIM, 1.0, 0.0).astype(_F32), acc_ref.shape)


def _part(j):
    return slice(j * KV_SUB, (j + 1) * KV_SUB)


def _scores(scr, slot, j, item):
    s_ref, mc_ref = scr[0], scr[1]
    _, k, qt, _, bias = item
    kj = k[_part(j)]
    s = jnp.concatenate([jnp.dot(kj[r:r + SCORE_ROWS], qt, preferred_element_type=_F32)
                         for r in range(0, KV_SUB, SCORE_ROWS)], axis=0)
    if bias is not None:
        s = s + bias[_part(j)]
    s_ref[slot, _part(j)] = s
    mx = jnp.max(s, axis=0, keepdims=True)
    mc_ref[slot] = mx if j == 0 else jnp.maximum(mc_ref[slot], mx)


def _new_max(scr, s_slot, p_slot, st):
    mc_ref, al_ref, m_ref = scr[1], scr[3], scr[4]
    m_old = m_ref[st]
    m_new = jnp.maximum(m_old, mc_ref[s_slot])
    al_ref[p_slot] = jnp.exp2(m_old - m_new)
    m_ref[st] = m_new


def _softmax(scr, s_slot, p_slot, j, st):
    s_ref, p_ref, m_ref = scr[0], scr[2], scr[4]
    p_ref[p_slot, _part(j)] = jnp.exp2(s_ref[s_slot, _part(j)] - m_ref[st]).astype(_BF16)


def _values(scr, slot, j, item):
    p_ref, al_ref, acc_ref = scr[2], scr[3], scr[5]
    st, vts = item[0], item[3]
    pv = jnp.dot(vts[j], p_ref[slot, _part(j)], preferred_element_type=_F32)
    acc = acc_ref[st]
    acc_ref[st] = (al_ref[slot] * acc if j == 0 else acc) + pv


def _pipeline(scr, items):
    n = len(items)
    n_parts = len(items[0][3])
    for i0 in range(min(SCORE_AHEAD, n)):
        for j in range(n_parts):
            _scores(scr, i0 % S_SLOTS, j, items[i0])
    for i in range(n + VALUE_BEHIND):
        if i < n:
            _new_max(scr, i % S_SLOTS, i % P_SLOTS, items[i][0])
        for j in range(n_parts):
            if i + SCORE_AHEAD < n:
                _scores(scr, (i + SCORE_AHEAD) % S_SLOTS, j, items[i + SCORE_AHEAD])
            if i < n:
                _softmax(scr, i % S_SLOTS, i % P_SLOTS, j, items[i][0])
            if i >= VALUE_BEHIND:
                _values(scr, (i - VALUE_BEHIND) % P_SLOTS, j, items[i - VALUE_BEHIND])


def _finish(scr, first):
    acc_ref = scr[5]
    ot = jnp.concatenate([acc_ref[first + h, 0:HEAD_DIM] / acc_ref[first + h, HEAD_DIM:HEAD_DIM + 1]
                          for h in range(KV_GROUP)], axis=0)
    return ot.T


def _heads(qt_ref, g=None):
    base = 0 if g is None else g * GROUP_COLS
    return [qt_ref[0, base + h * HEAD_DIM: base + (h + 1) * HEAD_DIM, :] for h in range(KV_GROUP)]


def _vt_part(vt_ref, lead, piece):
    return jnp.concatenate([vt_ref[lead + (piece,)], vt_ref[lead + (piece + 1,)]], axis=1)


def _query_blocks(qt_ref, tq):
    return [[qt_ref[0, h * HEAD_DIM:(h + 1) * HEAD_DIM, b * tq:(b + 1) * tq] for h in range(KV_GROUP)]
            for b in range(qt_ref.shape[2] // tq)]


def _chunk_items(k_ref, vt_ref, blocks, tk):
    items = []
    for c in range(k_ref.shape[2] // tk):
        k = k_ref[0, 0, c * tk:(c + 1) * tk, :]
        vts = [_vt_part(vt_ref, (0, 0), (c * tk + j * KV_SUB) // V_PIECE) for j in range(tk // KV_SUB)]
        for b, heads in enumerate(blocks):
            items += [(b * KV_GROUP + h, k, heads[h], vts, None) for h in range(KV_GROUP)]
    return items


def _store_blocks(scr, o_ref, tq, n_blocks):
    for b in range(n_blocks):
        o_ref[0, b * tq:(b + 1) * tq, :] = _finish(scr, b * KV_GROUP).astype(o_ref.dtype)


def _full_body(qt_ref, kc_ref, vtc_ref, kl_ref, vtl_ref, o_ref, *scr, tq, tk):
    blocks = _query_blocks(qt_ref, tq)
    _init_states(scr)
    _pipeline(scr, _chunk_items(kc_ref, vtc_ref, blocks, tk) + _chunk_items(kl_ref, vtl_ref, blocks, tk))
    _store_blocks(scr, o_ref, tq, len(blocks))


def _kv_specs(lk, index_map):
    return [pl.BlockSpec((1, 1, lk, HEAD_DIM), lambda *i: index_map(*i) + (0, 0)),
            pl.BlockSpec((1, 1, lk // V_PIECE, ACC_ROWS, V_PIECE), lambda *i: index_map(*i) + (0, 0, 0))]


def _attend_full(qt, k_ctx, vt_ctx, k_lat, vt_lat, tq, tk, n_blocks):
    b, d, l = qt.shape
    ts = tq * n_blocks
    by_group = lambda bi, g, qi: (bi, g)
    return pl.pallas_call(
        functools.partial(_full_body, tq=tq, tk=tk),
        grid=(b, N_KV_HEADS, l // ts),
        in_specs=([pl.BlockSpec((1, GROUP_COLS, ts), lambda bi, g, qi: (bi, g, qi))]
                  + _kv_specs(k_ctx.shape[2], by_group) + _kv_specs(l, by_group)),
        out_specs=pl.BlockSpec((1, ts, GROUP_COLS), lambda bi, g, qi: (bi, qi, g)),
        out_shape=jax.ShapeDtypeStruct((b, l, d), _BF16),
        scratch_shapes=_attn_scratch(KV_GROUP * n_blocks, tq, tk),
        compiler_params=_params(3),
        name="attn_full",
    )(qt, k_ctx, vt_ctx, k_lat, vt_lat)


def _band_start(q0, l, tq):
    return jnp.clip(q0 - WINDOW, 0, l - 2 * tq)


def _window_body(sink_ref, qt_ref, kc_ref, vtc_ref, kl_ref, vtl_ref, bias_ref, o_ref, *scr, tq):
    g = pl.program_id(1)
    blocks = _query_blocks(qt_ref, tq)
    _init_states(scr, [sink_ref[g * KV_GROUP + h] for _ in blocks for h in range(KV_GROUP)])
    items = _chunk_items(kc_ref, vtc_ref, blocks, 2 * tq)
    for b, heads in enumerate(blocks):
        q0 = (pl.program_id(2) * len(blocks) + b) * tq
        start = pl.multiple_of(_band_start(q0, kl_ref.shape[2], tq), V_PIECE)
        k_band = kl_ref[0, 0, pl.ds(start, 2 * tq), :]
        vt_band = [_vt_part(vtl_ref, (0, 0), start // V_PIECE + j * (KV_SUB // V_PIECE)) for j in range(2)]
        items += [(b * KV_GROUP + h, k_band, heads[h], vt_band, bias_ref[b]) for h in range(KV_GROUP)]
    _pipeline(scr, items)
    _store_blocks(scr, o_ref, tq, len(blocks))


def _window_bias(l, tq):
    q0 = (jnp.arange(l // tq) * tq)[:, None, None]
    kpos = _band_start(q0, l, tq) + jnp.arange(2 * tq)[None, :, None]
    qpos = q0 + jnp.arange(tq)[None, None, :]
    return jnp.where(jnp.abs(qpos - kpos) <= WINDOW, 0.0, NEG_INF).astype(_F32)


def _attend_window(qt, k_ctx, vt_ctx, k_lat, vt_lat, sink, tq, n_blocks):
    b, d, l = qt.shape
    lc = k_ctx.shape[2]
    ts = tq * n_blocks
    assert tq == KV_SUB and tq >= 2 * WINDOW and lc == 2 * tq and l >= 2 * tq
    by_group = lambda bi, g, qi: (bi, g)
    return pl.pallas_call(
        functools.partial(_window_body, tq=tq),
        grid=(b, N_KV_HEADS, l // ts),
        in_specs=([pl.BlockSpec(memory_space=pltpu.SMEM),
                   pl.BlockSpec((1, GROUP_COLS, ts), lambda bi, g, qi: (bi, g, qi))]
                  + _kv_specs(lc, by_group) + _kv_specs(l, by_group)
                  + [pl.BlockSpec((n_blocks, 2 * tq, tq), lambda bi, g, qi: (qi, 0, 0))]),
        out_specs=pl.BlockSpec((1, ts, GROUP_COLS), lambda bi, g, qi: (bi, qi, g)),
        out_shape=jax.ShapeDtypeStruct((b, l, d), _BF16),
        scratch_shapes=_attn_scratch(KV_GROUP * n_blocks, tq, 2 * tq),
        compiler_params=_params(3),
        name="attn_window",
    )(sink, qt, k_ctx, vt_ctx, k_lat, vt_lat, _window_bias(l, tq))


def _ctx_body(*refs, has_sink):
    if has_sink:
        sink_ref, qt_ref, k_ref, vt_ref, o_ref = refs[:5]
    else:
        qt_ref, k_ref, vt_ref, o_ref = refs[:4]
    scr = refs[5:] if has_sink else refs[4:]
    n_sub = k_ref.shape[2] // KV_SUB
    _init_states(scr, [sink_ref[i] for i in range(N_HEADS)] if has_sink else None)
    items = []
    for g in range(N_KV_HEADS):
        heads = _heads(qt_ref, g)
        k = k_ref[0, g]
        vts = [_vt_part(vt_ref, (0, g), j * (KV_SUB // V_PIECE)) for j in range(n_sub)]
        items += [(g * KV_GROUP + h, k, heads[h], vts, None) for h in range(KV_GROUP)]
    _pipeline(scr, items)
    for g in range(N_KV_HEADS):
        o_ref[0, :, g * GROUP_COLS:(g + 1) * GROUP_COLS] = _finish(scr, g * KV_GROUP).astype(o_ref.dtype)


def _attend_ctx(qt, k, vt, sink):
    b, d, l = qt.shape
    has_sink = sink is not None
    in_specs = [
        pl.BlockSpec((1, d, l), lambda bi: (bi, 0, 0)),
        pl.BlockSpec((1, N_KV_HEADS, l, HEAD_DIM), lambda bi: (bi, 0, 0, 0)),
        pl.BlockSpec((1, N_KV_HEADS, l // V_PIECE, ACC_ROWS, V_PIECE), lambda bi: (bi, 0, 0, 0, 0)),
    ]
    args = [qt, k, vt]
    if has_sink:
        in_specs = [pl.BlockSpec(memory_space=pltpu.SMEM)] + in_specs
        args = [sink] + args
    return pl.pallas_call(
        functools.partial(_ctx_body, has_sink=has_sink),
        grid=(b,),
        in_specs=in_specs,
        out_specs=pl.BlockSpec((1, l, d), lambda bi: (bi, 0, 0)),
        out_shape=jax.ShapeDtypeStruct((b, l, d), _BF16),
        scratch_shapes=_attn_scratch(N_HEADS, l, l),
        compiler_params=_params(1),
        name="attn_ctx_sink" if has_sink else "attn_ctx",
    )(*args)


def _cache_layout(cache_k, cache_v):
    b, depth, p, g, hd = cache_v.shape
    k = cache_k.transpose(1, 0, 3, 2, 4).astype(_BF16)
    vt = cache_v.reshape(b, depth, p // V_PIECE, V_PIECE, g, hd).transpose(1, 0, 4, 2, 5, 3).astype(_BF16)
    extra = jnp.zeros((ACC_ROWS - hd, V_PIECE), _BF16).at[0].set(1.0)
    return k, jnp.concatenate([vt, jnp.broadcast_to(extra, vt.shape[:4] + extra.shape)], axis=4)


def _token_tile(t, cap):
    tm = min(cap, t)
    assert t % tm == 0
    return tm


def kernel(x_prompt, x_sample, cache_k, cache_v, c, c_ctx, w_mod, b_mod, norm_g, w_qkv, w_o,
           q_norm_g, k_norm_g, sink, w_ffn_in, w_ffn_out):
    depth = w_mod.shape[0]
    d = D_MODEL
    bp, lp, _ = x_prompt.shape
    bs, ls, _ = x_sample.shape
    past = cache_k.shape[2]
    assert lp % KV_SUB == 0 and ls % KV_TILE == 0 and past % KV_TILE == 0

    w_in = (w_ffn_in.astype(_BF16).reshape(depth, 2, d, 2 * N_FF_CHUNKS, FF_CHUNK)
            .transpose(0, 1, 3, 2, 4))
    w_out = w_ffn_out.astype(_BF16).reshape(depth, 2, N_FF_CHUNKS, FF_CHUNK, d)
    w_qkv_b = w_qkv.astype(_BF16)
    w_o_b = w_o.astype(_BF16)

    n_rows = -(-(bs + 1) // 8) * 8
    cond = jnp.concatenate([c, c_ctx[None, :], jnp.zeros((n_rows - bs - 1, d), _F32)], axis=0)
    mods = _modulation(cond, w_mod, b_mod)

    tp = bp * lp
    tm_p = _token_tile(tp, FFN_TILE)
    tq_p = _token_tile(lp, QKV_TILE)
    ctx_row = lambda i: bs
    qg_p = jnp.broadcast_to(q_norm_g[:, :, None], (depth, HEAD_DIM, tq_p))
    kg_p = jnp.broadcast_to(k_norm_g[:, :, None], (depth, HEAD_DIM, tq_p))
    h = x_prompt.reshape(tp, d)
    new_ks, new_vs = [], []
    for i in range(depth):
        h = _ffn(h, mods, norm_g, w_in, w_out, i, 0, ctx_row, tm_p)
        qt, ka, vt, k, v = _qkv(h, mods, norm_g, w_qkv_b, qg_p, kg_p, i, ctx_row, tq_p, lp, True)
        new_ks.append(k.reshape(bp, lp, N_KV_HEADS, HEAD_DIM))
        new_vs.append(v.reshape(bp, lp, N_KV_HEADS, HEAD_DIM))
        o = _attend_ctx(qt, ka, vt, sink[i // 2] if i % 2 == 1 else None)
        h = _ffn(h, mods, norm_g, w_in, w_out, i, 1, ctx_row, tm_p, o=o.reshape(tp, d), w_o=w_o_b)
    y_prompt = h.reshape(bp, lp, d)
    new_k = jnp.stack(new_ks, axis=1)
    new_v = jnp.stack(new_vs, axis=1)

    ts = bs * ls
    tm_s = _token_tile(ls, FFN_TILE)
    tq_s = _token_tile(ls, QKV_TILE)
    lat_row = lambda i: i // (ls // tm_s)
    lat_row_q = lambda i: i // (ls // tq_s)
    rope_t = _rope_table_t(ls)
    qg_s = jnp.broadcast_to(q_norm_g[:, :, None], (depth, HEAD_DIM, tq_s))
    kg_s = jnp.broadcast_to(k_norm_g[:, :, None], (depth, HEAD_DIM, tq_s))
    k_ctx, vt_ctx = _cache_layout(cache_k, cache_v)
    blocks_per_step = lambda want: want if ls % (want * Q_TILE) == 0 else 1
    h = x_sample.reshape(ts, d)
    for i in range(depth):
        h = _ffn(h, mods, norm_g, w_in, w_out, i, 0, lat_row, tm_s)
        qt, ka, vt = _qkv(h, mods, norm_g, w_qkv_b, qg_s, kg_s, i, lat_row_q, tq_s, ls, False, rope_t)
        if i % 2 == 0:
            o = _attend_full(qt, k_ctx[i], vt_ctx[i], ka, vt, Q_TILE, KV_TILE, blocks_per_step(FULL_Q_BLOCKS))
        else:
            o = _attend_window(qt, k_ctx[i], vt_ctx[i], ka, vt, sink[i // 2], Q_TILE,
                               blocks_per_step(WINDOW_Q_BLOCKS))
        h = _ffn(h, mods, norm_g, w_in, w_out, i, 1, lat_row, tm_s, o=o.reshape(ts, d), w_o=w_o_b)
    y_sample = h.reshape(bs, ls, d)

    return (y_prompt, y_sample, new_k, new_v)
```

```python
import functools

import jax
import jax.numpy as jnp
from jax import lax
from jax.experimental import pallas as pl
from jax.experimental.pallas import tpu as pltpu

D_MODEL = 1024
N_HEADS = 16
N_KV_HEADS = 4
HEAD_DIM = 64
KV_GROUP = N_HEADS // N_KV_HEADS
GROUP_COLS = KV_GROUP * HEAD_DIM
KV_COLS = N_KV_HEADS * HEAD_DIM
QKV_COLS = (N_HEADS + 2 * N_KV_HEADS) * HEAD_DIM
D_FF = 2816
N_MOD = 9
GRID_W = 64
WINDOW = 128
ROPE_FREQS = HEAD_DIM // 4
ROPE_THETA = 10000.0
ATTN_SCALE = HEAD_DIM ** -0.5
LOG2_E = 1.4426950408889634
Q_SCALE = ATTN_SCALE * LOG2_E
EPS = 1e-6
NEG_INF = -1e30

FF_CHUNK = 256
N_FF_CHUNKS = D_FF // FF_CHUNK
FFN_TILE = 1024
QKV_TILE = 1024
QKV_SUB = 512
Q_TILE = 256
FULL_Q_BLOCKS = 2
WINDOW_Q_BLOCKS = 4
KV_TILE = 512
VMEM_LIMIT_BYTES = 52 * 1024 * 1024

_BF16 = jnp.bfloat16
_F32 = jnp.float32


def _params(n_axes):
    return pltpu.CompilerParams(dimension_semantics=("arbitrary",) * n_axes,
                                vmem_limit_bytes=VMEM_LIMIT_BYTES)


def _resident(block_shape, index_map):
    return pl.BlockSpec(block_shape, index_map, pipeline_mode=pl.Buffered(1))


def _silu(x):
    return x * jax.nn.sigmoid(x)


def _modln(x, g, shift, scale):
    y = x * lax.rsqrt(jnp.mean(x * x, axis=-1, keepdims=True) + EPS)
    return y * (g * (1.0 + scale)) + shift


def _mod_body(cond_ref, w_ref, b_ref, o_ref):
    a = _silu(cond_ref[...]).astype(_BF16)
    w = w_ref[0].astype(_BF16)
    o_ref[0] = jnp.dot(a, w, preferred_element_type=_F32) + b_ref[0]


def _modulation(cond, w_mod, b_mod):
    depth, d, n = w_mod.shape
    r = cond.shape[0]
    tn = n // 8
    out = pl.pallas_call(
        _mod_body,
        grid=(depth, n // tn),
        in_specs=[
            pl.BlockSpec((r, d), lambda i, j: (0, 0)),
            pl.BlockSpec((1, d, tn), lambda i, j: (i, 0, j)),
            pl.BlockSpec((1, 1, tn), lambda i, j: (i, 0, j)),
        ],
        out_specs=pl.BlockSpec((1, r, tn), lambda i, j: (i, 0, j)),
        out_shape=jax.ShapeDtypeStruct((depth, r, n), _F32),
        compiler_params=_params(2),
        name="modulation",
    )(cond, w_mod, b_mod.reshape(depth, 1, n))
    return out.reshape(depth, r, N_MOD, d)


def _ffn_body(*refs, sub, has_proj):
    if has_proj:
        x_ref, mods_ref, ng_ref, o_ref, wo_ref, win_ref, wout_ref, out_ref, xn_ref, acc_ref = refs
    else:
        x_ref, mods_ref, ng_ref, win_ref, wout_ref, out_ref, xn_ref, acc_ref = refs
    m = mods_ref[0, 0]
    x = x_ref[...]
    if has_proj:
        x = x + m[5:6] * jnp.dot(o_ref[...], wo_ref[0], preferred_element_type=_F32)
    k = 3 * sub
    xn_ref[...] = _modln(x, ng_ref[0, sub:sub + 1], m[k:k + 1], m[k + 1:k + 2]).astype(_BF16)
    if has_proj:
        out_ref[...] = x

    def ffn_chunk(c):
        xn = xn_ref[...]
        gate = jnp.dot(xn, win_ref[0, 0, c], preferred_element_type=_F32)
        up = jnp.dot(xn, win_ref[0, 0, c + N_FF_CHUNKS], preferred_element_type=_F32)
        act = (_silu(gate) * up).astype(_BF16)
        return jnp.dot(act, wout_ref[0, 0, c], preferred_element_type=_F32)

    acc_ref[...] = ffn_chunk(0)

    def chunk(c, carry):
        acc_ref[...] += ffn_chunk(c)
        return carry

    lax.fori_loop(1, N_FF_CHUNKS, chunk, 0, unroll=5)
    h = out_ref[...] if has_proj else x_ref[...]
    out_ref[...] = h + (0.5 * m[k + 2:k + 3]) * acc_ref[...]


def _ffn(x, mods, norm_g, w_in, w_out, layer, which, row_of_tile, tm, o=None, w_o=None):
    t, d = x.shape
    sub = 2 * which
    has_proj = o is not None
    in_specs = [
        pl.BlockSpec((tm, d), lambda i: (i, 0)),
        pl.BlockSpec((1, 1, N_MOD, d), lambda i: (layer, row_of_tile(i), 0, 0)),
        pl.BlockSpec((1, 3, d), lambda i: (layer, 0, 0)),
    ]
    args = [x, mods, norm_g]
    if has_proj:
        in_specs += [pl.BlockSpec((tm, d), lambda i: (i, 0)),
                     _resident((1, d, d), lambda i: (layer, 0, 0))]
        args += [o, w_o]
    in_specs += [
        _resident((1, 1, 2 * N_FF_CHUNKS, d, FF_CHUNK), lambda i: (layer, which, 0, 0, 0)),
        _resident((1, 1, N_FF_CHUNKS, FF_CHUNK, d), lambda i: (layer, which, 0, 0, 0)),
    ]
    args += [w_in, w_out]
    return pl.pallas_call(
        functools.partial(_ffn_body, sub=sub, has_proj=has_proj),
        grid=(t // tm,),
        in_specs=in_specs,
        out_specs=pl.BlockSpec((tm, d), lambda i: (i, 0)),
        out_shape=jax.ShapeDtypeStruct((t, d), _F32),
        scratch_shapes=[pltpu.VMEM((tm, d), _BF16), pltpu.VMEM((tm, d), _F32)],
        compiler_params=_params(1),
        name="ffn_proj" if has_proj else "ffn",
    )(*args)


def _head_norm_rope_t(yt, g_t, rope_ref):
    ss = jnp.sum(yt * yt, axis=0, keepdims=True)
    yn = (yt * lax.rsqrt(ss * (1.0 / HEAD_DIM) + EPS)) * g_t
    if rope_ref is None:
        return yn
    f = ROPE_FREQS
    x1r, x2r, x1c, x2c = yn[0:f], yn[f:2 * f], yn[2 * f:3 * f], yn[3 * f:4 * f]
    cr, sr, cc, sc = rope_ref[0], rope_ref[1], rope_ref[2], rope_ref[3]
    return jnp.concatenate([x1r * cr - x2r * sr, x2r * cr + x1r * sr,
                            x1c * cc - x2c * sc, x2c * cc + x1c * sc], axis=0)


def _qkv_body(*refs, rope, keep_f32, sub):
    n_in = 7 if rope else 6
    x_ref, mods_ref, ng_ref, w_ref, qg_ref, kg_ref = refs[:6]
    rope_ref = refs[6] if rope else None
    qt_ref, ka_ref, vt_ref = refs[n_in:n_in + 3]
    m = mods_ref[0, 0]
    tm = x_ref.shape[0]
    k0 = N_HEADS * HEAD_DIM
    row = lax.broadcasted_iota(jnp.int32, (ACC_ROWS - HEAD_DIM, V_PIECE), 0)
    extra = jnp.where(row == 0, 1.0, 0.0).astype(vt_ref.dtype)

    def project(r):
        rows = slice(r * sub, (r + 1) * sub)
        a = _modln(x_ref[rows, :], ng_ref[0, 1:2], m[3:4], m[4:5]).astype(_BF16)
        return jnp.dot(a, w_ref[0], preferred_element_type=_F32)

    def finish(r, proj):
        toks = slice(r * sub, (r + 1) * sub)
        qg, kg = qg_ref[0, :, toks], kg_ref[0, :, toks]
        rope_r = None if rope_ref is None else rope_ref.at[:, :, toks]
        for c in range(k0 // GROUP_COLS):
            yt = proj[:, c * GROUP_COLS:(c + 1) * GROUP_COLS].T
            for h in range(KV_GROUP):
                qn = _head_norm_rope_t(yt[h * HEAD_DIM:(h + 1) * HEAD_DIM], qg, rope_r)
                r0 = c * GROUP_COLS + h * HEAD_DIM
                qt_ref[0, r0:r0 + HEAD_DIM, toks] = (qn * Q_SCALE).astype(qt_ref.dtype)
        kt = proj[:, k0:k0 + KV_COLS].T
        kn = jnp.concatenate([_head_norm_rope_t(kt[g * HEAD_DIM:(g + 1) * HEAD_DIM], kg, rope_r)
                              for g in range(N_KV_HEADS)], axis=0)
        k_tok = kn.T
        v_tok = proj[:, k0 + KV_COLS:]
        vt = v_tok.T
        for g in range(N_KV_HEADS):
            ka_ref[0, g, toks, :] = k_tok[:, g * HEAD_DIM:(g + 1) * HEAD_DIM].astype(ka_ref.dtype)
            for j in range(sub // V_PIECE):
                piece = r * (sub // V_PIECE) + j
                vt_ref[0, g, piece, 0:HEAD_DIM, :] = (
                    vt[g * HEAD_DIM:(g + 1) * HEAD_DIM, j * V_PIECE:(j + 1) * V_PIECE].astype(vt_ref.dtype))
                vt_ref[0, g, piece, HEAD_DIM:ACC_ROWS, :] = extra
        if keep_f32:
            k_ref, v_ref = refs[n_in + 3:n_in + 5]
            k_ref[toks, :] = k_tok
            v_ref[toks, :] = v_tok

    n_sub = tm // sub
    proj = project(0)
    for r in range(n_sub):
        nxt = project(r + 1) if r + 1 < n_sub else None
        finish(r, proj)
        proj = nxt


def _qkv(x, mods, norm_g, w_qkv, qg_t, kg_t, layer, row_of_tile, tm, seq_len, keep_f32, rope_t=None):
    t, d = x.shape
    rope = rope_t is not None
    tiles_per_seq = seq_len // tm
    n_seq = t // seq_len
    seq_tile = lambda i: (i // tiles_per_seq, 0, i % tiles_per_seq)
    out_specs = [pl.BlockSpec((1, N_HEADS * HEAD_DIM, tm), seq_tile),
                 pl.BlockSpec((1, N_KV_HEADS, tm, HEAD_DIM), lambda i: seq_tile(i) + (0,)),
                 pl.BlockSpec((1, N_KV_HEADS, tm // V_PIECE, ACC_ROWS, V_PIECE), lambda i: seq_tile(i) + (0, 0))]
    out_shape = [jax.ShapeDtypeStruct((n_seq, N_HEADS * HEAD_DIM, seq_len), _BF16),
                 jax.ShapeDtypeStruct((n_seq, N_KV_HEADS, seq_len, HEAD_DIM), _BF16),
                 jax.ShapeDtypeStruct((n_seq, N_KV_HEADS, seq_len // V_PIECE, ACC_ROWS, V_PIECE), _BF16)]
    if keep_f32:
        out_specs += [pl.BlockSpec((tm, KV_COLS), lambda i: (i, 0))] * 2
        out_shape += [jax.ShapeDtypeStruct((t, KV_COLS), _F32)] * 2
    in_specs = [
        pl.BlockSpec((tm, d), lambda i: (i, 0)),
        pl.BlockSpec((1, 1, N_MOD, d), lambda i: (layer, row_of_tile(i), 0, 0)),
        pl.BlockSpec((1, 3, d), lambda i: (layer, 0, 0)),
        _resident((1, d, QKV_COLS), lambda i: (layer, 0, 0)),
        pl.BlockSpec((1, HEAD_DIM, tm), lambda i: (layer, 0, 0)),
        pl.BlockSpec((1, HEAD_DIM, tm), lambda i: (layer, 0, 0)),
    ]
    args = [x, mods, norm_g, w_qkv, qg_t, kg_t]
    if rope:
        in_specs += [pl.BlockSpec((4, ROPE_FREQS, tm), lambda i: (0, 0, i % tiles_per_seq))]
        args += [rope_t]
    return pl.pallas_call(
        functools.partial(_qkv_body, rope=rope, keep_f32=keep_f32, sub=min(tm, QKV_SUB)),
        grid=(t // tm,),
        in_specs=in_specs,
        out_specs=out_specs,
        out_shape=out_shape,
        compiler_params=_params(1),
        name="qkv_rope" if rope else "qkv",
    )(*args)


def _rope_table_t(n_tok):
    pos = jnp.arange(n_tok)
    row = (pos // GRID_W).astype(_F32)
    col = (pos % GRID_W).astype(_F32)
    freqs = 1.0 / jnp.power(ROPE_THETA, jnp.arange(ROPE_FREQS, dtype=_F32) / ROPE_FREQS)
    ang_r = row[:, None] * freqs
    ang_c = col[:, None] * freqs
    return jnp.stack([jnp.cos(ang_r).T, jnp.sin(ang_r).T, jnp.cos(ang_c).T, jnp.sin(ang_c).T])


KV_SUB = 256
V_PIECE = 128
SCORE_ROWS = 64
BF16_SUBLANES = 16
ACC_ROWS = HEAD_DIM + BF16_SUBLANES


SCORE_AHEAD = 2
VALUE_BEHIND = 1
S_SLOTS = SCORE_AHEAD + 1
P_SLOTS = VALUE_BEHIND + 1


def _attn_scratch(n_states, tq, tk):
    return [pltpu.VMEM((S_SLOTS, tk, tq), _F32),
            pltpu.VMEM((S_SLOTS + S_SLOTS % 2, 1, tq), _F32),
            pltpu.VMEM((P_SLOTS, tk, tq), _BF16),
            pltpu.VMEM((P_SLOTS, 1, tq), _F32),
            pltpu.VMEM((n_states, 1, tq), _F32),
            pltpu.VMEM((n_states, ACC_ROWS, tq), _F32)]


def _init_states(scr, sinks=None):
    m_ref, acc_ref = scr[4], scr[5]
    tq = m_ref.shape[-1]
    if sinks is None:
        m_ref[...] = jnp.full(m_ref.shape, NEG_INF, _F32)
        acc_ref[...] = jnp.zeros(acc_ref.shape, _F32)
    else:
        for i, sk in enumerate(sinks):
            m_ref[i] = jnp.full((1, tq), sk * LOG2_E, _F32)
        row = lax.broadcasted_iota(jnp.int32, (ACC_ROWS, tq), 0)
        acc_ref[...] = jnp.broadcast_to(jnp.where(row == HEAD_DIM, 1.0, 0.0).astype(_F32), acc_ref.shape)


def _part(j):
    return slice(j * KV_SUB, (j + 1) * KV_SUB)


def _scores(scr, slot, j, item):
    s_ref, mc_ref = scr[0], scr[1]
    _, k, qt, _, bias = item
    kj = k[_part(j)]
    s = jnp.concatenate([jnp.dot(kj[r:r + SCORE_ROWS], qt, preferred_element_type=_F32)
                         for r in range(0, KV_SUB, SCORE_ROWS)], axis=0)
    if bias is not None:
        s = s + bias[_part(j)]
    s_ref[slot, _part(j)] = s
    mx = jnp.max(s, axis=0, keepdims=True)
    mc_ref[slot] = mx if j == 0 else jnp.maximum(mc_ref[slot], mx)


def _new_max(scr, s_slot, p_slot, st):
    mc_ref, al_ref, m_ref = scr[1], scr[3], scr[4]
    m_old = m_ref[st]
    m_new = jnp.maximum(m_old, mc_ref[s_slot])
    al_ref[p_slot] = jnp.exp2(m_old - m_new)
    m_ref[st] = m_new


def _softmax(scr, s_slot, p_slot, j, st):
    s_ref, p_ref, m_ref = scr[0], scr[2], scr[4]
    p_ref[p_slot, _part(j)] = jnp.exp2(s_ref[s_slot, _part(j)] - m_ref[st]).astype(_BF16)


def _values(scr, slot, j, item):
    p_ref, al_ref, acc_ref = scr[2], scr[3], scr[5]
    st, vts = item[0], item[3]
    pv = jnp.dot(vts[j], p_ref[slot, _part(j)], preferred_element_type=_F32)
    acc = acc_ref[st]
    acc_ref[st] = (al_ref[slot] * acc if j == 0 else acc) + pv


def _pipeline(scr, items):
    n = len(items)
    n_parts = len(items[0][3])
    for i0 in range(min(SCORE_AHEAD, n)):
        for j in range(n_parts):
            _scores(scr, i0 % S_SLOTS, j, items[i0])
    for i in range(n + VALUE_BEHIND):
        if i < n:
            _new_max(scr, i % S_SLOTS, i % P_SLOTS, items[i][0])
        for j in range(n_parts):
            if i + SCORE_AHEAD < n:
                _scores(scr, (i + SCORE_AHEAD) % S_SLOTS, j, items[i + SCORE_AHEAD])
            if i < n:
                _softmax(scr, i % S_SLOTS, i % P_SLOTS, j, items[i][0])
            if i >= VALUE_BEHIND:
                _values(scr, (i - VALUE_BEHIND) % P_SLOTS, j, items[i - VALUE_BEHIND])


def _finish(scr, first):
    acc_ref = scr[5]
    ot = jnp.concatenate([acc_ref[first + h, 0:HEAD_DIM] / acc_ref[first + h, HEAD_DIM:HEAD_DIM + 1]
                          for h in range(KV_GROUP)], axis=0)
    return ot.T


def _heads(qt_ref, g=None):
    base = 0 if g is None else g * GROUP_COLS
    return [qt_ref[0, base + h * HEAD_DIM: base + (h + 1) * HEAD_DIM, :] for h in range(KV_GROUP)]


def _vt_part(vt_ref, lead, piece):
    return jnp.concatenate([vt_ref[lead + (piece,)], vt_ref[lead + (piece + 1,)]], axis=1)


def _query_blocks(qt_ref, tq):
    return [[qt_ref[0, h * HEAD_DIM:(h + 1) * HEAD_DIM, b * tq:(b + 1) * tq] for h in range(KV_GROUP)]
            for b in range(qt_ref.shape[2] // tq)]


def _chunk_items(k_ref, vt_ref, blocks, tk):
    items = []
    for c in range(k_ref.shape[2] // tk):
        k = k_ref[0, 0, c * tk:(c + 1) * tk, :]
        vts = [_vt_part(vt_ref, (0, 0), (c * tk + j * KV_SUB) // V_PIECE) for j in range(tk // KV_SUB)]
        for b, heads in enumerate(blocks):
            items += [(b * KV_GROUP + h, k, heads[h], vts, None) for h in range(KV_GROUP)]
    return items


def _store_blocks(scr, o_ref, tq, n_blocks):
    for b in range(n_blocks):
        o_ref[0, b * tq:(b + 1) * tq, :] = _finish(scr, b * KV_GROUP).astype(o_ref.dtype)


def _full_body(qt_ref, kc_ref, vtc_ref, kl_ref, vtl_ref, o_ref, *scr, tq, tk):
    blocks = _query_blocks(qt_ref, tq)
    _init_states(scr)
    _pipeline(scr, _chunk_items(kc_ref, vtc_ref, blocks, tk) + _chunk_items(kl_ref, vtl_ref, blocks, tk))
    _store_blocks(scr, o_ref, tq, len(blocks))


def _kv_specs(lk, index_map):
    return [pl.BlockSpec((1, 1, lk, HEAD_DIM), lambda *i: index_map(*i) + (0, 0)),
            pl.BlockSpec((1, 1, lk // V_PIECE, ACC_ROWS, V_PIECE), lambda *i: index_map(*i) + (0, 0, 0))]


def _attend_full(qt, k_ctx, vt_ctx, k_lat, vt_lat, tq, tk, n_blocks):
    b, d, l = qt.shape
    ts = tq * n_blocks
    by_group = lambda bi, g, qi: (bi, g)
    return pl.pallas_call(
        functools.partial(_full_body, tq=tq, tk=tk),
        grid=(b, N_KV_HEADS, l // ts),
        in_specs=([pl.BlockSpec((1, GROUP_COLS, ts), lambda bi, g, qi: (bi, g, qi))]
                  + _kv_specs(k_ctx.shape[2], by_group) + _kv_specs(l, by_group)),
        out_specs=pl.BlockSpec((1, ts, GROUP_COLS), lambda bi, g, qi: (bi, qi, g)),
        out_shape=jax.ShapeDtypeStruct((b, l, d), _BF16),
        scratch_shapes=_attn_scratch(KV_GROUP * n_blocks, tq, tk),
        compiler_params=_params(3),
        name="attn_full",
    )(qt, k_ctx, vt_ctx, k_lat, vt_lat)


def _band_start(q0, l, tq):
    return jnp.clip(q0 - WINDOW, 0, l - 2 * tq)


def _window_body(sink_ref, qt_ref, kc_ref, vtc_ref, kl_ref, vtl_ref, bias_ref, o_ref, *scr, tq):
    g = pl.program_id(1)
    blocks = _query_blocks(qt_ref, tq)
    _init_states(scr, [sink_ref[g * KV_GROUP + h] for _ in blocks for h in range(KV_GROUP)])
    items = _chunk_items(kc_ref, vtc_ref, blocks, 2 * tq)
    for b, heads in enumerate(blocks):
        q0 = (pl.program_id(2) * len(blocks) + b) * tq
        start = pl.multiple_of(_band_start(q0, kl_ref.shape[2], tq), V_PIECE)
        k_band = kl_ref[0, 0, pl.ds(start, 2 * tq), :]
        vt_band = [_vt_part(vtl_ref, (0, 0), start // V_PIECE + j * (KV_SUB // V_PIECE)) for j in range(2)]
        items += [(b * KV_GROUP + h, k_band, heads[h], vt_band, bias_ref[b]) for h in range(KV_GROUP)]
    _pipeline(scr, items)
    _store_blocks(scr, o_ref, tq, len(blocks))


def _window_bias(l, tq):
    q0 = (jnp.arange(l // tq) * tq)[:, None, None]
    kpos = _band_start(q0, l, tq) + jnp.arange(2 * tq)[None, :, None]
    qpos = q0 + jnp.arange(tq)[None, None, :]
    return jnp.where(jnp.abs(qpos - kpos) <= WINDOW, 0.0, NEG_INF).astype(_F32)


def _attend_window(qt, k_ctx, vt_ctx, k_lat, vt_lat, sink, tq, n_blocks):
    b, d, l = qt.shape
    lc = k_ctx.shape[2]
    ts = tq * n_blocks
    assert tq == KV_SUB and tq >= 2 * WINDOW and lc == 2 * tq and l >= 2 * tq
    by_group = lambda bi, g, qi: (bi, g)
    return pl.pallas_call(
        functools.partial(_window_body, tq=tq),
        grid=(b, N_KV_HEADS, l // ts),
        in_specs=([pl.BlockSpec(memory_space=pltpu.SMEM),
                   pl.BlockSpec((1, GROUP_COLS, ts), lambda bi, g, qi: (bi, g, qi))]
                  + _kv_specs(lc, by_group) + _kv_specs(l, by_group)
                  + [pl.BlockSpec((n_blocks, 2 * tq, tq), lambda bi, g, qi: (qi, 0, 0))]),
        out_specs=pl.BlockSpec((1, ts, GROUP_COLS), lambda bi, g, qi: (bi, qi, g)),
        out_shape=jax.ShapeDtypeStruct((b, l, d), _BF16),
        scratch_shapes=_attn_scratch(KV_GROUP * n_blocks, tq, 2 * tq),
        compiler_params=_params(3),
        name="attn_window",
    )(sink, qt, k_ctx, vt_ctx, k_lat, vt_lat, _window_bias(l, tq))


def _ctx_body(*refs, has_sink):
    if has_sink:
        sink_ref, qt_ref, k_ref, vt_ref, o_ref = refs[:5]
    else:
        qt_ref, k_ref, vt_ref, o_ref = refs[:4]
    scr = refs[5:] if has_sink else refs[4:]
    n_sub = k_ref.shape[2] // KV_SUB
    _init_states(scr, [sink_ref[i] for i in range(N_HEADS)] if has_sink else None)
    items = []
    for g in range(N_KV_HEADS):
        heads = _heads(qt_ref, g)
        k = k_ref[0, g]
        vts = [_vt_part(vt_ref, (0, g), j * (KV_SUB // V_PIECE)) for j in range(n_sub)]
        items += [(g * KV_GROUP + h, k, heads[h], vts, None) for h in range(KV_GROUP)]
    _pipeline(scr, items)
    for g in range(N_KV_HEADS):
        o_ref[0, :, g * GROUP_COLS:(g + 1) * GROUP_COLS] = _finish(scr, g * KV_GROUP).astype(o_ref.dtype)


def _attend_ctx(qt, k, vt, sink):
    b, d, l = qt.shape
    has_sink = sink is not None
    in_specs = [
        pl.BlockSpec((1, d, l), lambda bi: (bi, 0, 0)),
        pl.BlockSpec((1, N_KV_HEADS, l, HEAD_DIM), lambda bi: (bi, 0, 0, 0)),
        pl.BlockSpec((1, N_KV_HEADS, l // V_PIECE, ACC_ROWS, V_PIECE), lambda bi: (bi, 0, 0, 0, 0)),
    ]
    args = [qt, k, vt]
    if has_sink:
        in_specs = [pl.BlockSpec(memory_space=pltpu.SMEM)] + in_specs
        args = [sink] + args
    return pl.pallas_call(
        functools.partial(_ctx_body, has_sink=has_sink),
        grid=(b,),
        in_specs=in_specs,
        out_specs=pl.BlockSpec((1, l, d), lambda bi: (bi, 0, 0)),
        out_shape=jax.ShapeDtypeStruct((b, l, d), _BF16),
        scratch_shapes=_attn_scratch(N_HEADS, l, l),
        compiler_params=_params(1),
        name="attn_ctx_sink" if has_sink else "attn_ctx",
    )(*args)


def _cache_layout(cache_k, cache_v):
    b, depth, p, g, hd = cache_v.shape
    k = cache_k.transpose(1, 0, 3, 2, 4).astype(_BF16)
    vt = cache_v.reshape(b, depth, p // V_PIECE, V_PIECE, g, hd).transpose(1, 0, 4, 2, 5, 3).astype(_BF16)
    extra = jnp.zeros((ACC_ROWS - hd, V_PIECE), _BF16).at[0].set(1.0)
    return k, jnp.concatenate([vt, jnp.broadcast_to(extra, vt.shape[:4] + extra.shape)], axis=4)


def _token_tile(t, cap):
    tm = min(cap, t)
    assert t % tm == 0
    return tm


def kernel(x_prompt, x_sample, cache_k, cache_v, c, c_ctx, w_mod, b_mod, norm_g, w_qkv, w_o,
           q_norm_g, k_norm_g, sink, w_ffn_in, w_ffn_out):
    depth = w_mod.shape[0]
    d = D_MODEL
    bp, lp, _ = x_prompt.shape
    bs, ls, _ = x_sample.shape
    past = cache_k.shape[2]
    assert lp % KV_SUB == 0 and ls % KV_TILE == 0 and past % KV_TILE == 0

    w_in = (w_ffn_in.astype(_BF16).reshape(depth, 2, d, 2 * N_FF_CHUNKS, FF_CHUNK)
            .transpose(0, 1, 3, 2, 4))
    w_out = w_ffn_out.astype(_BF16).reshape(depth, 2, N_FF_CHUNKS, FF_CHUNK, d)
    w_qkv_b = w_qkv.astype(_BF16)
    w_o_b = w_o.astype(_BF16)

    n_rows = -(-(bs + 1) // 8) * 8
    cond = jnp.concatenate([c, c_ctx[None, :], jnp.zeros((n_rows - bs - 1, d), _F32)], axis=0)
    mods = _modulation(cond, w_mod, b_mod)

    tp = bp * lp
    tm_p = _token_tile(tp, FFN_TILE)
    tq_p = _token_tile(lp, QKV_TILE)
    ctx_row = lambda i: bs
    qg_p = jnp.broadcast_to(q_norm_g[:, :, None], (depth, HEAD_DIM, tq_p))
    kg_p = jnp.broadcast_to(k_norm_g[:, :, None], (depth, HEAD_DIM, tq_p))
    h = x_prompt.reshape(tp, d)
    new_ks, new_vs = [], []
    for i in range(depth):
        h = _ffn(h, mods, norm_g, w_in, w_out, i, 0, ctx_row, tm_p)
        qt, ka, vt, k, v = _qkv(h, mods, norm_g, w_qkv_b, qg_p, kg_p, i, ctx_row, tq_p, lp, True)
        new_ks.append(k.reshape(bp, lp, N_KV_HEADS, HEAD_DIM))
        new_vs.append(v.reshape(bp, lp, N_KV_HEADS, HEAD_DIM))
        o = _attend_ctx(qt, ka, vt, sink[i // 2] if i % 2 == 1 else None)
        h = _ffn(h, mods, norm_g, w_in, w_out, i, 1, ctx_row, tm_p, o=o.reshape(tp, d), w_o=w_o_b)
    y_prompt = h.reshape(bp, lp, d)
    new_k = jnp.stack(new_ks, axis=1)
    new_v = jnp.stack(new_vs, axis=1)

    ts = bs * ls
    tm_s = _token_tile(ls, FFN_TILE)
    tq_s = _token_tile(ls, QKV_TILE)
    lat_row = lambda i: i // (ls // tm_s)
    lat_row_q = lambda i: i // (ls // tq_s)
    rope_t = _rope_table_t(ls)
    qg_s = jnp.broadcast_to(q_norm_g[:, :, None], (depth, HEAD_DIM, tq_s))
    kg_s = jnp.broadcast_to(k_norm_g[:, :, None], (depth, HEAD_DIM, tq_s))
    k_ctx, vt_ctx = _cache_layout(cache_k, cache_v)
    blocks_per_step = lambda want: want if ls % (want * Q_TILE) == 0 else 1
    h = x_sample.reshape(ts, d)
    for i in range(depth):
        h = _ffn(h, mods, norm_g, w_in, w_out, i, 0, lat_row, tm_s)
        qt, ka, vt = _qkv(h, mods, norm_g, w_qkv_b, qg_s, kg_s, i, lat_row_q, tq_s, ls, False, rope_t)
        if i % 2 == 0:
            o = _attend_full(qt, k_ctx[i], vt_ctx[i], ka, vt, Q_TILE, KV_TILE, blocks_per_step(FULL_Q_BLOCKS))
        else:
            o = _attend_window(qt, k_ctx[i], vt_ctx[i], ka, vt, sink[i // 2], Q_TILE,
                               blocks_per_step(WINDOW_Q_BLOCKS))
        h = _ffn(h, mods, norm_g, w_in, w_out, i, 1, lat_row, tm_s, o=o.reshape(ts, d), w_o=w_o_b)
    y_sample = h.reshape(bs, ls, d)

    return (y_prompt, y_sample, new_k, new_v)
```

```python
import functools

import jax
import jax.numpy as jnp
from jax import lax
from jax.experimental import pallas as pl
from jax.experimental.pallas import tpu as pltpu

D_MODEL = 1024
N_HEADS = 16
N_KV_HEADS = 4
HEAD_DIM = 64
KV_GROUP = N_HEADS // N_KV_HEADS
GROUP_COLS = KV_GROUP * HEAD_DIM
KV_COLS = N_KV_HEADS * HEAD_DIM
QKV_COLS = (N_HEADS + 2 * N_KV_HEADS) * HEAD_DIM
D_FF = 2816
N_MOD = 9
GRID_W = 64
WINDOW = 128
ROPE_FREQS = HEAD_DIM // 4
ROPE_THETA = 10000.0
ATTN_SCALE = HEAD_DIM ** -0.5
LOG2_E = 1.4426950408889634
Q_SCALE = ATTN_SCALE * LOG2_E
EPS = 1e-6
NEG_INF = -1e30

FF_CHUNK = 256
N_FF_CHUNKS = D_FF // FF_CHUNK
FFN_TILE = 1024
QKV_TILE = 1024
QKV_SUB = 512
Q_TILE = 256
FULL_Q_BLOCKS = 4
WINDOW_Q_BLOCKS = 4
KV_TILE = 512
VMEM_LIMIT_BYTES = 52 * 1024 * 1024

_BF16 = jnp.bfloat16
_F32 = jnp.float32


def _params(n_axes):
    return pltpu.CompilerParams(dimension_semantics=("arbitrary",) * n_axes,
                                vmem_limit_bytes=VMEM_LIMIT_BYTES)


def _resident(block_shape, index_map):
    return pl.BlockSpec(block_shape, index_map, pipeline_mode=pl.Buffered(1))


def _silu(x):
    return x * jax.nn.sigmoid(x)


def _modln(x, g, shift, scale):
    y = x * lax.rsqrt(jnp.mean(x * x, axis=-1, keepdims=True) + EPS)
    return y * (g * (1.0 + scale)) + shift


def _mod_body(cond_ref, w_ref, b_ref, o_ref):
    a = _silu(cond_ref[...]).astype(_BF16)
    w = w_ref[0].astype(_BF16)
    o_ref[0] = jnp.dot(a, w, preferred_element_type=_F32) + b_ref[0]


def _modulation(cond, w_mod, b_mod):
    depth, d, n = w_mod.shape
    r = cond.shape[0]
    tn = n // 8
    out = pl.pallas_call(
        _mod_body,
        grid=(depth, n // tn),
        in_specs=[
            pl.BlockSpec((r, d), lambda i, j: (0, 0)),
            pl.BlockSpec((1, d, tn), lambda i, j: (i, 0, j)),
            pl.BlockSpec((1, 1, tn), lambda i, j: (i, 0, j)),
        ],
        out_specs=pl.BlockSpec((1, r, tn), lambda i, j: (i, 0, j)),
        out_shape=jax.ShapeDtypeStruct((depth, r, n), _F32),
        compiler_params=_params(2),
        name="modulation",
    )(cond, w_mod, b_mod.reshape(depth, 1, n))
    return out.reshape(depth, r, N_MOD, d)


def _ffn_body(*refs, sub, has_proj):
    if has_proj:
        x_ref, mods_ref, ng_ref, o_ref, wo_ref, win_ref, wout_ref, out_ref, xn_ref, acc_ref = refs
    else:
        x_ref, mods_ref, ng_ref, win_ref, wout_ref, out_ref, xn_ref, acc_ref = refs
    m = mods_ref[0, 0]
    x = x_ref[...]
    if has_proj:
        x = x + m[5:6] * jnp.dot(o_ref[...], wo_ref[0], preferred_element_type=_F32)
    k = 3 * sub
    xn_ref[...] = _modln(x, ng_ref[0, sub:sub + 1], m[k:k + 1], m[k + 1:k + 2]).astype(_BF16)
    if has_proj:
        out_ref[...] = x

    def ffn_chunk(c):
        xn = xn_ref[...]
        gate = jnp.dot(xn, win_ref[0, 0, c], preferred_element_type=_F32)
        up = jnp.dot(xn, win_ref[0, 0, c + N_FF_CHUNKS], preferred_element_type=_F32)
        act = (_silu(gate) * up).astype(_BF16)
        return jnp.dot(act, wout_ref[0, 0, c], preferred_element_type=_F32)

    acc_ref[...] = ffn_chunk(0)

    def chunk(c, carry):
        acc_ref[...] += ffn_chunk(c)
        return carry

    lax.fori_loop(1, N_FF_CHUNKS, chunk, 0, unroll=5)
    h = out_ref[...] if has_proj else x_ref[...]
    out_ref[...] = h + (0.5 * m[k + 2:k + 3]) * acc_ref[...]


def _ffn(x, mods, norm_g, w_in, w_out, layer, which, row_of_tile, tm, o=None, w_o=None):
    t, d = x.shape
    sub = 2 * which
    has_proj = o is not None
    in_specs = [
        pl.BlockSpec((tm, d), lambda i: (i, 0)),
        pl.BlockSpec((1, 1, N_MOD, d), lambda i: (layer, row_of_tile(i), 0, 0)),
        pl.BlockSpec((1, 3, d), lambda i: (layer, 0, 0)),
    ]
    args = [x, mods, norm_g]
    if has_proj:
        in_specs += [pl.BlockSpec((tm, d), lambda i: (i, 0)),
                     _resident((1, d, d), lambda i: (layer, 0, 0))]
        args += [o, w_o]
    in_specs += [
        _resident((1, 1, 2 * N_FF_CHUNKS, d, FF_CHUNK), lambda i: (layer, which, 0, 0, 0)),
        _resident((1, 1, N_FF_CHUNKS, FF_CHUNK, d), lambda i: (layer, which, 0, 0, 0)),
    ]
    args += [w_in, w_out]
    return pl.pallas_call(
        functools.partial(_ffn_body, sub=sub, has_proj=has_proj),
        grid=(t // tm,),
        in_specs=in_specs,
        out_specs=pl.BlockSpec((tm, d), lambda i: (i, 0)),
        out_shape=jax.ShapeDtypeStruct((t, d), _F32),
        scratch_shapes=[pltpu.VMEM((tm, d), _BF16), pltpu.VMEM((tm, d), _F32)],
        compiler_params=_params(1),
        name="ffn_proj" if has_proj else "ffn",
    )(*args)


def _head_norm_rope_t(yt, g_t, rope_ref):
    ss = jnp.sum(yt * yt, axis=0, keepdims=True)
    yn = (yt * lax.rsqrt(ss * (1.0 / HEAD_DIM) + EPS)) * g_t
    if rope_ref is None:
        return yn
    f = ROPE_FREQS
    x1r, x2r, x1c, x2c = yn[0:f], yn[f:2 * f], yn[2 * f:3 * f], yn[3 * f:4 * f]
    cr, sr, cc, sc = rope_ref[0], rope_ref[1], rope_ref[2], rope_ref[3]
    return jnp.concatenate([x1r * cr - x2r * sr, x2r * cr + x1r * sr,
                            x1c * cc - x2c * sc, x2c * cc + x1c * sc], axis=0)


def _qkv_body(*refs, rope, keep_f32, sub):
    n_in = 7 if rope else 6
    x_ref, mods_ref, ng_ref, w_ref, qg_ref, kg_ref = refs[:6]
    rope_ref = refs[6] if rope else None
    qt_ref, ka_ref, vt_ref = refs[n_in:n_in + 3]
    m = mods_ref[0, 0]
    tm = x_ref.shape[0]
    k0 = N_HEADS * HEAD_DIM
    row = lax.broadcasted_iota(jnp.int32, (ACC_ROWS - HEAD_DIM, V_PIECE), 0)
    extra = jnp.where(row == 0, 1.0, 0.0).astype(vt_ref.dtype)

    def project(r):
        rows = slice(r * sub, (r + 1) * sub)
        a = _modln(x_ref[rows, :], ng_ref[0, 1:2], m[3:4], m[4:5]).astype(_BF16)
        return jnp.dot(a, w_ref[0], preferred_element_type=_F32)

    def finish(r, proj):
        toks = slice(r * sub, (r + 1) * sub)
        qg, kg = qg_ref[0, :, toks], kg_ref[0, :, toks]
        rope_r = None if rope_ref is None else rope_ref.at[:, :, toks]
        for c in range(k0 // GROUP_COLS):
            yt = proj[:, c * GROUP_COLS:(c + 1) * GROUP_COLS].T
            for h in range(KV_GROUP):
                qn = _head_norm_rope_t(yt[h * HEAD_DIM:(h + 1) * HEAD_DIM], qg, rope_r)
                r0 = c * GROUP_COLS + h * HEAD_DIM
                qt_ref[0, r0:r0 + HEAD_DIM, toks] = (qn * Q_SCALE).astype(qt_ref.dtype)
        kt = proj[:, k0:k0 + KV_COLS].T
        kn = jnp.concatenate([_head_norm_rope_t(kt[g * HEAD_DIM:(g + 1) * HEAD_DIM], kg, rope_r)
                              for g in range(N_KV_HEADS)], axis=0)
        k_tok = kn.T
        v_tok = proj[:, k0 + KV_COLS:]
        vt = v_tok.T
        for g in range(N_KV_HEADS):
            ka_ref[0, g, toks, :] = k_tok[:, g * HEAD_DIM:(g + 1) * HEAD_DIM].astype(ka_ref.dtype)
            for j in range(sub // V_PIECE):
                piece = r * (sub // V_PIECE) + j
                vt_ref[0, g, piece, 0:HEAD_DIM, :] = (
                    vt[g * HEAD_DIM:(g + 1) * HEAD_DIM, j * V_PIECE:(j + 1) * V_PIECE].astype(vt_ref.dtype))
                vt_ref[0, g, piece, HEAD_DIM:ACC_ROWS, :] = extra
        if keep_f32:
            k_ref, v_ref = refs[n_in + 3:n_in + 5]
            k_ref[toks, :] = k_tok
            v_ref[toks, :] = v_tok

    n_sub = tm // sub
    proj = project(0)
    for r in range(n_sub):
        nxt = project(r + 1) if r + 1 < n_sub else None
        finish(r, proj)
        proj = nxt


def _qkv(x, mods, norm_g, w_qkv, qg_t, kg_t, layer, row_of_tile, tm, seq_len, keep_f32, rope_t=None):
    t, d = x.shape
    rope = rope_t is not None
    tiles_per_seq = seq_len // tm
    n_seq = t // seq_len
    seq_tile = lambda i: (i // tiles_per_seq, 0, i % tiles_per_seq)
    out_specs = [pl.BlockSpec((1, N_HEADS * HEAD_DIM, tm), seq_tile),
                 pl.BlockSpec((1, N_KV_HEADS, tm, HEAD_DIM), lambda i: seq_tile(i) + (0,)),
                 pl.BlockSpec((1, N_KV_HEADS, tm // V_PIECE, ACC_ROWS, V_PIECE), lambda i: seq_tile(i) + (0, 0))]
    out_shape = [jax.ShapeDtypeStruct((n_seq, N_HEADS * HEAD_DIM, seq_len), _BF16),
                 jax.ShapeDtypeStruct((n_seq, N_KV_HEADS, seq_len, HEAD_DIM), _BF16),
                 jax.ShapeDtypeStruct((n_seq, N_KV_HEADS, seq_len // V_PIECE, ACC_ROWS, V_PIECE), _BF16)]
    if keep_f32:
        out_specs += [pl.BlockSpec((tm, KV_COLS), lambda i: (i, 0))] * 2
        out_shape += [jax.ShapeDtypeStruct((t, KV_COLS), _F32)] * 2
    in_specs = [
        pl.BlockSpec((tm, d), lambda i: (i, 0)),
        pl.BlockSpec((1, 1, N_MOD, d), lambda i: (layer, row_of_tile(i), 0, 0)),
        pl.BlockSpec((1, 3, d), lambda i: (layer, 0, 0)),
        _resident((1, d, QKV_COLS), lambda i: (layer, 0, 0)),
        pl.BlockSpec((1, HEAD_DIM, tm), lambda i: (layer, 0, 0)),
        pl.BlockSpec((1, HEAD_DIM, tm), lambda i: (layer, 0, 0)),
    ]
    args = [x, mods, norm_g, w_qkv, qg_t, kg_t]
    if rope:
        in_specs += [pl.BlockSpec((4, ROPE_FREQS, tm), lambda i: (0, 0, i % tiles_per_seq))]
        args += [rope_t]
    return pl.pallas_call(
        functools.partial(_qkv_body, rope=rope, keep_f32=keep_f32, sub=min(tm, QKV_SUB)),
        grid=(t // tm,),
        in_specs=in_specs,
        out_specs=out_specs,
        out_shape=out_shape,
        compiler_params=_params(1),
        name="qkv_rope" if rope else "qkv",
    )(*args)


def _rope_table_t(n_tok):
    pos = jnp.arange(n_tok)
    row = (pos // GRID_W).astype(_F32)
    col = (pos % GRID_W).astype(_F32)
    freqs = 1.0 / jnp.power(ROPE_THETA, jnp.arange(ROPE_FREQS, dtype=_F32) / ROPE_FREQS)
    ang_r = row[:, None] * freqs
    ang_c = col[:, None] * freqs
    return jnp.stack([jnp.cos(ang_r).T, jnp.sin(ang_r).T, jnp.cos(ang_c).T, jnp.sin(ang_c).T])


KV_SUB = 256
V_PIECE = 128
SCORE_ROWS = 64
BF16_SUBLANES = 16
ACC_ROWS = HEAD_DIM + BF16_SUBLANES


SCORE_AHEAD = 2
VALUE_BEHIND = 1
S_SLOTS = SCORE_AHEAD + 1
P_SLOTS = VALUE_BEHIND + 1


def _attn_scratch(n_states, tq, tk):
    return [pltpu.VMEM((S_SLOTS, tk, tq), _F32),
            pltpu.VMEM((S_SLOTS, 1, tq), _F32),
            pltpu.VMEM((P_SLOTS, tk, tq), _BF16),
            pltpu.VMEM((P_SLOTS, 1, tq), _F32),
            pltpu.VMEM((n_states, 1, tq), _F32),
            pltpu.VMEM((n_states, ACC_ROWS, tq), _F32)]


def _init_states(scr, sinks=None):
    m_ref, acc_ref = scr[4], scr[5]
    tq = m_ref.shape[-1]
    if sinks is None:
        m_ref[...] = jnp.full(m_ref.shape, NEG_INF, _F32)
        acc_ref[...] = jnp.zeros(acc_ref.shape, _F32)
    else:
        for i, sk in enumerate(sinks):
            m_ref[i] = jnp.full((1, tq), sk * LOG2_E, _F32)
        row = lax.broadcasted_iota(jnp.int32, (ACC_ROWS, tq), 0)
        acc_ref[...] = jnp.broadcast_to(jnp.where(row == HEAD_DIM, 1.0, 0.0).astype(_F32), acc_ref.shape)


def _part(j):
    return slice(j * KV_SUB, (j + 1) * KV_SUB)


def _scores(scr, slot, j, item):
    s_ref, mc_ref = scr[0], scr[1]
    _, k, qt, _, bias = item
    kj = k[_part(j)]
    s = jnp.concatenate([jnp.dot(kj[r:r + SCORE_ROWS], qt, preferred_element_type=_F32)
                         for r in range(0, KV_SUB, SCORE_ROWS)], axis=0)
    if bias is not None:
        s = s + bias[_part(j)]
    s_ref[slot, _part(j)] = s
    mx = jnp.max(s, axis=0, keepdims=True)
    mc_ref[slot] = mx if j == 0 else jnp.maximum(mc_ref[slot], mx)


def _new_max(scr, s_slot, p_slot, st):
    mc_ref, al_ref, m_ref = scr[1], scr[3], scr[4]
    m_old = m_ref[st]
    m_new = jnp.maximum(m_old, mc_ref[s_slot])
    al_ref[p_slot] = jnp.exp2(m_old - m_new)
    m_ref[st] = m_new


def _softmax(scr, s_slot, p_slot, j, st):
    s_ref, p_ref, m_ref = scr[0], scr[2], scr[4]
    p_ref[p_slot, _part(j)] = jnp.exp2(s_ref[s_slot, _part(j)] - m_ref[st]).astype(_BF16)


def _values(scr, slot, j, item):
    p_ref, al_ref, acc_ref = scr[2], scr[3], scr[5]
    st, vts = item[0], item[3]
    pv = jnp.dot(vts[j], p_ref[slot, _part(j)], preferred_element_type=_F32)
    acc = acc_ref[st]
    acc_ref[st] = (al_ref[slot] * acc if j == 0 else acc) + pv


def _pipeline(scr, items):
    n = len(items)
    n_parts = len(items[0][3])
    for i0 in range(min(SCORE_AHEAD, n)):
        for j in range(n_parts):
            _scores(scr, i0 % S_SLOTS, j, items[i0])
    for i in range(n + VALUE_BEHIND):
        if i < n:
            _new_max(scr, i % S_SLOTS, i % P_SLOTS, items[i][0])
        for j in range(n_parts):
            if i + SCORE_AHEAD < n:
                _scores(scr, (i + SCORE_AHEAD) % S_SLOTS, j, items[i + SCORE_AHEAD])
            if i < n:
                _softmax(scr, i % S_SLOTS, i % P_SLOTS, j, items[i][0])
            if i >= VALUE_BEHIND:
                _values(scr, (i - VALUE_BEHIND) % P_SLOTS, j, items[i - VALUE_BEHIND])


def _finish(scr, first):
    acc_ref = scr[5]
    ot = jnp.concatenate([acc_ref[first + h, 0:HEAD_DIM] / acc_ref[first + h, HEAD_DIM:HEAD_DIM + 1]
                          for h in range(KV_GROUP)], axis=0)
    return ot.T


def _heads(qt_ref, g=None):
    base = 0 if g is None else g * GROUP_COLS
    return [qt_ref[0, base + h * HEAD_DIM: base + (h + 1) * HEAD_DIM, :] for h in range(KV_GROUP)]


def _vt_part(vt_ref, lead, piece):
    return jnp.concatenate([vt_ref[lead + (piece,)], vt_ref[lead + (piece + 1,)]], axis=1)


def _query_blocks(qt_ref, tq):
    return [[qt_ref[0, h * HEAD_DIM:(h + 1) * HEAD_DIM, b * tq:(b + 1) * tq] for h in range(KV_GROUP)]
            for b in range(qt_ref.shape[2] // tq)]


def _chunk_items(k_ref, vt_ref, blocks, tk):
    items = []
    for c in range(k_ref.shape[2] // tk):
        k = k_ref[0, 0, c * tk:(c + 1) * tk, :]
        vts = [_vt_part(vt_ref, (0, 0), (c * tk + j * KV_SUB) // V_PIECE) for j in range(tk // KV_SUB)]
        for b, heads in enumerate(blocks):
            items += [(b * KV_GROUP + h, k, heads[h], vts, None) for h in range(KV_GROUP)]
    return items


def _store_blocks(scr, o_ref, tq, n_blocks):
    for b in range(n_blocks):
        o_ref[0, b * tq:(b + 1) * tq, :] = _finish(scr, b * KV_GROUP).astype(o_ref.dtype)


def _full_body(qt_ref, kc_ref, vtc_ref, kl_ref, vtl_ref, o_ref, *scr, tq, tk):
    blocks = _query_blocks(qt_ref, tq)
    _init_states(scr)
    _pipeline(scr, _chunk_items(kc_ref, vtc_ref, blocks, tk) + _chunk_items(kl_ref, vtl_ref, blocks, tk))
    _store_blocks(scr, o_ref, tq, len(blocks))


def _kv_specs(lk, index_map):
    return [pl.BlockSpec((1, 1, lk, HEAD_DIM), lambda *i: index_map(*i) + (0, 0)),
            pl.BlockSpec((1, 1, lk // V_PIECE, ACC_ROWS, V_PIECE), lambda *i: index_map(*i) + (0, 0, 0))]


def _attend_full(qt, k_ctx, vt_ctx, k_lat, vt_lat, tq, tk, n_blocks):
    b, d, l = qt.shape
    ts = tq * n_blocks
    by_group = lambda bi, g, qi: (bi, g)
    return pl.pallas_call(
        functools.partial(_full_body, tq=tq, tk=tk),
        grid=(b, N_KV_HEADS, l // ts),
        in_specs=([pl.BlockSpec((1, GROUP_COLS, ts), lambda bi, g, qi: (bi, g, qi))]
                  + _kv_specs(k_ctx.shape[2], by_group) + _kv_specs(l, by_group)),
        out_specs=pl.BlockSpec((1, ts, GROUP_COLS), lambda bi, g, qi: (bi, qi, g)),
        out_shape=jax.ShapeDtypeStruct((b, l, d), _BF16),
        scratch_shapes=_attn_scratch(KV_GROUP * n_blocks, tq, tk),
        compiler_params=_params(3),
        name="attn_full",
    )(qt, k_ctx, vt_ctx, k_lat, vt_lat)


def _band_start(q0, l, tq):
    return jnp.clip(q0 - WINDOW, 0, l - 2 * tq)


def _window_body(sink_ref, qt_ref, kc_ref, vtc_ref, kl_ref, vtl_ref, bias_ref, o_ref, *scr, tq):
    g = pl.program_id(1)
    blocks = _query_blocks(qt_ref, tq)
    _init_states(scr, [sink_ref[g * KV_GROUP + h] for _ in blocks for h in range(KV_GROUP)])
    items = _chunk_items(kc_ref, vtc_ref, blocks, 2 * tq)
    for b, heads in enumerate(blocks):
        q0 = (pl.program_id(2) * len(blocks) + b) * tq
        start = pl.multiple_of(_band_start(q0, kl_ref.shape[2], tq), V_PIECE)
        k_band = kl_ref[0, 0, pl.ds(start, 2 * tq), :]
        vt_band = [_vt_part(vtl_ref, (0, 0), start // V_PIECE + j * (KV_SUB // V_PIECE)) for j in range(2)]
        items += [(b * KV_GROUP + h, k_band, heads[h], vt_band, bias_ref[b]) for h in range(KV_GROUP)]
    _pipeline(scr, items)
    _store_blocks(scr, o_ref, tq, len(blocks))


def _window_bias(l, tq):
    q0 = (jnp.arange(l // tq) * tq)[:, None, None]
    kpos = _band_start(q0, l, tq) + jnp.arange(2 * tq)[None, :, None]
    qpos = q0 + jnp.arange(tq)[None, None, :]
    return jnp.where(jnp.abs(qpos - kpos) <= WINDOW, 0.0, NEG_INF).astype(_F32)


def _attend_window(qt, k_ctx, vt_ctx, k_lat, vt_lat, sink, tq, n_blocks):
    b, d, l = qt.shape
    lc = k_ctx.shape[2]
    ts = tq * n_blocks
    assert tq == KV_SUB and tq >= 2 * WINDOW and lc == 2 * tq and l >= 2 * tq
    by_group = lambda bi, g, qi: (bi, g)
    return pl.pallas_call(
        functools.partial(_window_body, tq=tq),
        grid=(b, N_KV_HEADS, l // ts),
        in_specs=([pl.BlockSpec(memory_space=pltpu.SMEM),
                   pl.BlockSpec((1, GROUP_COLS, ts), lambda bi, g, qi: (bi, g, qi))]
                  + _kv_specs(lc, by_group) + _kv_specs(l, by_group)
                  + [pl.BlockSpec((n_blocks, 2 * tq, tq), lambda bi, g, qi: (qi, 0, 0))]),
        out_specs=pl.BlockSpec((1, ts, GROUP_COLS), lambda bi, g, qi: (bi, qi, g)),
        out_shape=jax.ShapeDtypeStruct((b, l, d), _BF16),
        scratch_shapes=_attn_scratch(KV_GROUP * n_blocks, tq, 2 * tq),
        compiler_params=_params(3),
        name="attn_window",
    )(sink, qt, k_ctx, vt_ctx, k_lat, vt_lat, _window_bias(l, tq))


def _ctx_body(*refs, has_sink):
    if has_sink:
        sink_ref, qt_ref, k_ref, vt_ref, o_ref = refs[:5]
    else:
        qt_ref, k_ref, vt_ref, o_ref = refs[:4]
    scr = refs[5:] if has_sink else refs[4:]
    n_sub = k_ref.shape[2] // KV_SUB
    _init_states(scr, [sink_ref[i] for i in range(N_HEADS)] if has_sink else None)
    items = []
    for g in range(N_KV_HEADS):
        heads = _heads(qt_ref, g)
        k = k_ref[0, g]
        vts = [_vt_part(vt_ref, (0, g), j * (KV_SUB // V_PIECE)) for j in range(n_sub)]
        items += [(g * KV_GROUP + h, k, heads[h], vts, None) for h in range(KV_GROUP)]
    _pipeline(scr, items)
    for g in range(N_KV_HEADS):
        o_ref[0, :, g * GROUP_COLS:(g + 1) * GROUP_COLS] = _finish(scr, g * KV_GROUP).astype(o_ref.dtype)


def _attend_ctx(qt, k, vt, sink):
    b, d, l = qt.shape
    has_sink = sink is not None
    in_specs = [
        pl.BlockSpec((1, d, l), lambda bi: (bi, 0, 0)),
        pl.BlockSpec((1, N_KV_HEADS, l, HEAD_DIM), lambda bi: (bi, 0, 0, 0)),
        pl.BlockSpec((1, N_KV_HEADS, l // V_PIECE, ACC_ROWS, V_PIECE), lambda bi: (bi, 0, 0, 0, 0)),
    ]
    args = [qt, k, vt]
    if has_sink:
        in_specs = [pl.BlockSpec(memory_space=pltpu.SMEM)] + in_specs
        args = [sink] + args
    return pl.pallas_call(
        functools.partial(_ctx_body, has_sink=has_sink),
        grid=(b,),
        in_specs=in_specs,
        out_specs=pl.BlockSpec((1, l, d), lambda bi: (bi, 0, 0)),
        out_shape=jax.ShapeDtypeStruct((b, l, d), _BF16),
        scratch_shapes=_attn_scratch(N_HEADS, l, l),
        compiler_params=_params(1),
        name="attn_ctx_sink" if has_sink else "attn_ctx",
    )(*args)


def _cache_layout(cache_k, cache_v):
    b, depth, p, g, hd = cache_v.shape
    k = cache_k.transpose(1, 0, 3, 2, 4).astype(_BF16)
    vt = cache_v.reshape(b, depth, p // V_PIECE, V_PIECE, g, hd).transpose(1, 0, 4, 2, 5, 3).astype(_BF16)
    extra = jnp.zeros((ACC_ROWS - hd, V_PIECE), _BF16).at[0].set(1.0)
    return k, jnp.concatenate([vt, jnp.broadcast_to(extra, vt.shape[:4] + extra.shape)], axis=4)


def _token_tile(t, cap):
    tm = min(cap, t)
    assert t % tm == 0
    return tm


def kernel(x_prompt, x_sample, cache_k, cache_v, c, c_ctx, w_mod, b_mod, norm_g, w_qkv, w_o,
           q_norm_g, k_norm_g, sink, w_ffn_in, w_ffn_out):
    depth = w_mod.shape[0]
    d = D_MODEL
    bp, lp, _ = x_prompt.shape
    bs, ls, _ = x_sample.shape
    past = cache_k.shape[2]
    assert lp % KV_SUB == 0 and ls % KV_TILE == 0 and past % KV_TILE == 0

    w_in = (w_ffn_in.astype(_BF16).reshape(depth, 2, d, 2 * N_FF_CHUNKS, FF_CHUNK)
            .transpose(0, 1, 3, 2, 4))
    w_out = w_ffn_out.astype(_BF16).reshape(depth, 2, N_FF_CHUNKS, FF_CHUNK, d)
    w_qkv_b = w_qkv.astype(_BF16)
    w_o_b = w_o.astype(_BF16)

    n_rows = -(-(bs + 1) // 8) * 8
    cond = jnp.concatenate([c, c_ctx[None, :], jnp.zeros((n_rows - bs - 1, d), _F32)], axis=0)
    mods = _modulation(cond, w_mod, b_mod)

    tp = bp * lp
    tm_p = _token_tile(tp, FFN_TILE)
    tq_p = _token_tile(lp, QKV_TILE)
    ctx_row = lambda i: bs
    qg_p = jnp.broadcast_to(q_norm_g[:, :, None], (depth, HEAD_DIM, tq_p))
    kg_p = jnp.broadcast_to(k_norm_g[:, :, None], (depth, HEAD_DIM, tq_p))
    h = x_prompt.reshape(tp, d)
    new_ks, new_vs = [], []
    for i in range(depth):
        h = _ffn(h, mods, norm_g, w_in, w_out, i, 0, ctx_row, tm_p)
        qt, ka, vt, k, v = _qkv(h, mods, norm_g, w_qkv_b, qg_p, kg_p, i, ctx_row, tq_p, lp, True)
        new_ks.append(k.reshape(bp, lp, N_KV_HEADS, HEAD_DIM))
        new_vs.append(v.reshape(bp, lp, N_KV_HEADS, HEAD_DIM))
        o = _attend_ctx(qt, ka, vt, sink[i // 2] if i % 2 == 1 else None)
        h = _ffn(h, mods, norm_g, w_in, w_out, i, 1, ctx_row, tm_p, o=o.reshape(tp, d), w_o=w_o_b)
    y_prompt = h.reshape(bp, lp, d)
    new_k = jnp.stack(new_ks, axis=1)
    new_v = jnp.stack(new_vs, axis=1)

    ts = bs * ls
    tm_s = _token_tile(ls, FFN_TILE)
    tq_s = _token_tile(ls, QKV_TILE)
    lat_row = lambda i: i // (ls // tm_s)
    lat_row_q = lambda i: i // (ls // tq_s)
    rope_t = _rope_table_t(ls)
    qg_s = jnp.broadcast_to(q_norm_g[:, :, None], (depth, HEAD_DIM, tq_s))
    kg_s = jnp.broadcast_to(k_norm_g[:, :, None], (depth, HEAD_DIM, tq_s))
    k_ctx, vt_ctx = _cache_layout(cache_k, cache_v)
    blocks_per_step = lambda want: want if ls % (want * Q_TILE) == 0 else 1
    h = x_sample.reshape(ts, d)
    for i in range(depth):
        h = _ffn(h, mods, norm_g, w_in, w_out, i, 0, lat_row, tm_s)
        qt, ka, vt = _qkv(h, mods, norm_g, w_qkv_b, qg_s, kg_s, i, lat_row_q, tq_s, ls, False, rope_t)
        if i % 2 == 0:
            o = _attend_full(qt, k_ctx[i], vt_ctx[i], ka, vt, Q_TILE, KV_TILE, blocks_per_step(FULL_Q_BLOCKS))
        else:
            o = _attend_window(qt, k_ctx[i], vt_ctx[i], ka, vt, sink[i // 2], Q_TILE,
                               blocks_per_step(WINDOW_Q_BLOCKS))
        h = _ffn(h, mods, norm_g, w_in, w_out, i, 1, lat_row, tm_s, o=o.reshape(ts, d), w_o=w_o_b)
    y_sample = h.reshape(bs, ls, d)

    return (y_prompt, y_sample, new_k, new_v)
```

```python
import functools

import jax
import jax.numpy as jnp
from jax import lax
from jax.experimental import pallas as pl
from jax.experimental.pallas import tpu as pltpu

D_MODEL = 1024
N_HEADS = 16
N_KV_HEADS = 4
HEAD_DIM = 64
KV_GROUP = N_HEADS // N_KV_HEADS
GROUP_COLS = KV_GROUP * HEAD_DIM
KV_COLS = N_KV_HEADS * HEAD_DIM
QKV_COLS = (N_HEADS + 2 * N_KV_HEADS) * HEAD_DIM
D_FF = 2816
N_MOD = 9
GRID_W = 64
WINDOW = 128
ROPE_FREQS = HEAD_DIM // 4
ROPE_THETA = 10000.0
ATTN_SCALE = HEAD_DIM ** -0.5
LOG2_E = 1.4426950408889634
Q_SCALE = ATTN_SCALE * LOG2_E
EPS = 1e-6
NEG_INF = -1e30

FF_CHUNK = 256
N_FF_CHUNKS = D_FF // FF_CHUNK
FFN_TILE = 1024
QKV_TILE = 1024
QKV_SUB = 512
Q_TILE = 256
FULL_Q_BLOCKS = 4
WINDOW_Q_BLOCKS = 4
KV_TILE = 512
VMEM_LIMIT_BYTES = 52 * 1024 * 1024

_BF16 = jnp.bfloat16
_F32 = jnp.float32


def _params(n_axes):
    return pltpu.CompilerParams(dimension_semantics=("arbitrary",) * n_axes,
                                vmem_limit_bytes=VMEM_LIMIT_BYTES)


def _resident(block_shape, index_map):
    return pl.BlockSpec(block_shape, index_map, pipeline_mode=pl.Buffered(1))


def _silu(x):
    return x * jax.nn.sigmoid(x)


def _modln(x, g, shift, scale):
    y = x * lax.rsqrt(jnp.mean(x * x, axis=-1, keepdims=True) + EPS)
    return y * (g * (1.0 + scale)) + shift


def _mod_body(cond_ref, w_ref, b_ref, o_ref):
    a = _silu(cond_ref[...]).astype(_BF16)
    w = w_ref[0].astype(_BF16)
    o_ref[0] = jnp.dot(a, w, preferred_element_type=_F32) + b_ref[0]


def _modulation(cond, w_mod, b_mod):
    depth, d, n = w_mod.shape
    r = cond.shape[0]
    tn = n // 8
    out = pl.pallas_call(
        _mod_body,
        grid=(depth, n // tn),
        in_specs=[
            pl.BlockSpec((r, d), lambda i, j: (0, 0)),
            pl.BlockSpec((1, d, tn), lambda i, j: (i, 0, j)),
            pl.BlockSpec((1, 1, tn), lambda i, j: (i, 0, j)),
        ],
        out_specs=pl.BlockSpec((1, r, tn), lambda i, j: (i, 0, j)),
        out_shape=jax.ShapeDtypeStruct((depth, r, n), _F32),
        compiler_params=_params(2),
        name="modulation",
    )(cond, w_mod, b_mod.reshape(depth, 1, n))
    return out.reshape(depth, r, N_MOD, d)


def _ffn_body(*refs, sub, has_proj):
    if has_proj:
        x_ref, mods_ref, ng_ref, o_ref, wo_ref, win_ref, wout_ref, out_ref, xn_ref, acc_ref = refs
    else:
        x_ref, mods_ref, ng_ref, win_ref, wout_ref, out_ref, xn_ref, acc_ref = refs
    m = mods_ref[0, 0]
    x = x_ref[...]
    if has_proj:
        x = x + m[5:6] * jnp.dot(o_ref[...], wo_ref[0], preferred_element_type=_F32)
    k = 3 * sub
    xn_ref[...] = _modln(x, ng_ref[0, sub:sub + 1], m[k:k + 1], m[k + 1:k + 2]).astype(_BF16)
    if has_proj:
        out_ref[...] = x

    def ffn_chunk(c):
        xn = xn_ref[...]
        gate = jnp.dot(xn, win_ref[0, 0, c], preferred_element_type=_F32)
        up = jnp.dot(xn, win_ref[0, 0, c + N_FF_CHUNKS], preferred_element_type=_F32)
        act = (_silu(gate) * up).astype(_BF16)
        return jnp.dot(act, wout_ref[0, 0, c], preferred_element_type=_F32)

    acc_ref[...] = ffn_chunk(0)

    def chunk(c, carry):
        acc_ref[...] += ffn_chunk(c)
        return carry

    lax.fori_loop(1, N_FF_CHUNKS, chunk, 0, unroll=5)
    h = out_ref[...] if has_proj else x_ref[...]
    out_ref[...] = h + (0.5 * m[k + 2:k + 3]) * acc_ref[...]


def _ffn(x, mods, norm_g, w_in, w_out, layer, which, row_of_tile, tm, o=None, w_o=None):
    t, d = x.shape
    sub = 2 * which
    has_proj = o is not None
    in_specs = [
        pl.BlockSpec((tm, d), lambda i: (i, 0)),
        pl.BlockSpec((1, 1, N_MOD, d), lambda i: (layer, row_of_tile(i), 0, 0)),
        pl.BlockSpec((1, 3, d), lambda i: (layer, 0, 0)),
    ]
    args = [x, mods, norm_g]
    if has_proj:
        in_specs += [pl.BlockSpec((tm, d), lambda i: (i, 0)),
                     _resident((1, d, d), lambda i: (layer, 0, 0))]
        args += [o, w_o]
    in_specs += [
        _resident((1, 1, 2 * N_FF_CHUNKS, d, FF_CHUNK), lambda i: (layer, which, 0, 0, 0)),
        _resident((1, 1, N_FF_CHUNKS, FF_CHUNK, d), lambda i: (layer, which, 0, 0, 0)),
    ]
    args += [w_in, w_out]
    return pl.pallas_call(
        functools.partial(_ffn_body, sub=sub, has_proj=has_proj),
        grid=(t // tm,),
        in_specs=in_specs,
        out_specs=pl.BlockSpec((tm, d), lambda i: (i, 0)),
        out_shape=jax.ShapeDtypeStruct((t, d), _F32),
        scratch_shapes=[pltpu.VMEM((tm, d), _BF16), pltpu.VMEM((tm, d), _F32)],
        compiler_params=_params(1),
        name="ffn_proj" if has_proj else "ffn",
    )(*args)


def _head_norm_rope_t(yt, g_t, rope_ref):
    ss = jnp.sum(yt * yt, axis=0, keepdims=True)
    yn = (yt * lax.rsqrt(ss * (1.0 / HEAD_DIM) + EPS)) * g_t
    if rope_ref is None:
        return yn
    f = ROPE_FREQS
    x1r, x2r, x1c, x2c = yn[0:f], yn[f:2 * f], yn[2 * f:3 * f], yn[3 * f:4 * f]
    cr, sr, cc, sc = rope_ref[0], rope_ref[1], rope_ref[2], rope_ref[3]
    return jnp.concatenate([x1r * cr - x2r * sr, x2r * cr + x1r * sr,
                            x1c * cc - x2c * sc, x2c * cc + x1c * sc], axis=0)


def _qkv_body(*refs, rope, keep_f32, sub):
    n_in = 7 if rope else 6
    x_ref, mods_ref, ng_ref, w_ref, qg_ref, kg_ref = refs[:6]
    rope_ref = refs[6] if rope else None
    qt_ref, ka_ref, vt_ref = refs[n_in:n_in + 3]
    m = mods_ref[0, 0]
    tm = x_ref.shape[0]
    k0 = N_HEADS * HEAD_DIM
    row = lax.broadcasted_iota(jnp.int32, (ACC_ROWS - HEAD_DIM, V_PIECE), 0)
    extra = jnp.where(row == 0, 1.0, 0.0).astype(vt_ref.dtype)

    def project(r):
        rows = slice(r * sub, (r + 1) * sub)
        a = _modln(x_ref[rows, :], ng_ref[0, 1:2], m[3:4], m[4:5]).astype(_BF16)
        return jnp.dot(a, w_ref[0], preferred_element_type=_F32)

    def finish(r, proj):
        toks = slice(r * sub, (r + 1) * sub)
        qg, kg = qg_ref[0, :, toks], kg_ref[0, :, toks]
        rope_r = None if rope_ref is None else rope_ref.at[:, :, toks]
        for c in range(k0 // GROUP_COLS):
            yt = proj[:, c * GROUP_COLS:(c + 1) * GROUP_COLS].T
            for h in range(KV_GROUP):
                qn = _head_norm_rope_t(yt[h * HEAD_DIM:(h + 1) * HEAD_DIM], qg, rope_r)
                r0 = c * GROUP_COLS + h * HEAD_DIM
                qt_ref[0, r0:r0 + HEAD_DIM, toks] = (qn * Q_SCALE).astype(qt_ref.dtype)
        kt = proj[:, k0:k0 + KV_COLS].T
        kn = jnp.concatenate([_head_norm_rope_t(kt[g * HEAD_DIM:(g + 1) * HEAD_DIM], kg, rope_r)
                              for g in range(N_KV_HEADS)], axis=0)
        k_tok = kn.T
        v_tok = proj[:, k0 + KV_COLS:]
        vt = v_tok.T
        for g in range(N_KV_HEADS):
            ka_ref[0, g, toks, :] = k_tok[:, g * HEAD_DIM:(g + 1) * HEAD_DIM].astype(ka_ref.dtype)
            for j in range(sub // V_PIECE):
                piece = r * (sub // V_PIECE) + j
                vt_ref[0, g, piece, 0:HEAD_DIM, :] = (
                    vt[g * HEAD_DIM:(g + 1) * HEAD_DIM, j * V_PIECE:(j + 1) * V_PIECE].astype(vt_ref.dtype))
                vt_ref[0, g, piece, HEAD_DIM:ACC_ROWS, :] = extra
        if keep_f32:
            k_ref, v_ref = refs[n_in + 3:n_in + 5]
            k_ref[toks, :] = k_tok
            v_ref[toks, :] = v_tok

    n_sub = tm // sub
    proj = project(0)
    for r in range(n_sub):
        nxt = project(r + 1) if r + 1 < n_sub else None
        finish(r, proj)
        proj = nxt


def _qkv(x, mods, norm_g, w_qkv, qg_t, kg_t, layer, row_of_tile, tm, seq_len, keep_f32, rope_t=None):
    t, d = x.shape
    rope = rope_t is not None
    tiles_per_seq = seq_len // tm
    n_seq = t // seq_len
    seq_tile = lambda i: (i // tiles_per_seq, 0, i % tiles_per_seq)
    out_specs = [pl.BlockSpec((1, N_HEADS * HEAD_DIM, tm), seq_tile),
                 pl.BlockSpec((1, N_KV_HEADS, tm, HEAD_DIM), lambda i: seq_tile(i) + (0,)),
                 pl.BlockSpec((1, N_KV_HEADS, tm // V_PIECE, ACC_ROWS, V_PIECE), lambda i: seq_tile(i) + (0, 0))]
    out_shape = [jax.ShapeDtypeStruct((n_seq, N_HEADS * HEAD_DIM, seq_len), _BF16),
                 jax.ShapeDtypeStruct((n_seq, N_KV_HEADS, seq_len, HEAD_DIM), _BF16),
                 jax.ShapeDtypeStruct((n_seq, N_KV_HEADS, seq_len // V_PIECE, ACC_ROWS, V_PIECE), _BF16)]
    if keep_f32:
        out_specs += [pl.BlockSpec((tm, KV_COLS), lambda i: (i, 0))] * 2
        out_shape += [jax.ShapeDtypeStruct((t, KV_COLS), _F32)] * 2
    in_specs = [
        pl.BlockSpec((tm, d), lambda i: (i, 0)),
        pl.BlockSpec((1, 1, N_MOD, d), lambda i: (layer, row_of_tile(i), 0, 0)),
        pl.BlockSpec((1, 3, d), lambda i: (layer, 0, 0)),
        _resident((1, d, QKV_COLS), lambda i: (layer, 0, 0)),
        pl.BlockSpec((1, HEAD_DIM, tm), lambda i: (layer, 0, 0)),
        pl.BlockSpec((1, HEAD_DIM, tm), lambda i: (layer, 0, 0)),
    ]
    args = [x, mods, norm_g, w_qkv, qg_t, kg_t]
    if rope:
        in_specs += [pl.BlockSpec((4, ROPE_FREQS, tm), lambda i: (0, 0, i % tiles_per_seq))]
        args += [rope_t]
    return pl.pallas_call(
        functools.partial(_qkv_body, rope=rope, keep_f32=keep_f32, sub=min(tm, QKV_SUB)),
        grid=(t // tm,),
        in_specs=in_specs,
        out_specs=out_specs,
        out_shape=out_shape,
        compiler_params=_params(1),
        name="qkv_rope" if rope else "qkv",
    )(*args)


def _rope_table_t(n_tok):
    pos = jnp.arange(n_tok)
    row = (pos // GRID_W).astype(_F32)
    col = (pos % GRID_W).astype(_F32)
    freqs = 1.0 / jnp.power(ROPE_THETA, jnp.arange(ROPE_FREQS, dtype=_F32) / ROPE_FREQS)
    ang_r = row[:, None] * freqs
    ang_c = col[:, None] * freqs
    return jnp.stack([jnp.cos(ang_r).T, jnp.sin(ang_r).T, jnp.cos(ang_c).T, jnp.sin(ang_c).T])


KV_SUB = 256
V_PIECE = 128
SCORE_ROWS = 256
BF16_SUBLANES = 16
ACC_ROWS = HEAD_DIM + BF16_SUBLANES


SCORE_AHEAD = 2
VALUE_BEHIND = 1
S_SLOTS = SCORE_AHEAD + 1
P_SLOTS = VALUE_BEHIND + 1


def _attn_scratch(n_states, tq, tk):
    return [pltpu.VMEM((S_SLOTS, tk, tq), _F32),
            pltpu.VMEM((S_SLOTS, 1, tq), _F32),
            pltpu.VMEM((P_SLOTS, tk, tq), _BF16),
            pltpu.VMEM((P_SLOTS, 1, tq), _F32),
            pltpu.VMEM((n_states, 1, tq), _F32),
            pltpu.VMEM((n_states, ACC_ROWS, tq), _F32)]


def _init_states(scr, sinks=None):
    m_ref, acc_ref = scr[4], scr[5]
    tq = m_ref.shape[-1]
    if sinks is None:
        m_ref[...] = jnp.full(m_ref.shape, NEG_INF, _F32)
        acc_ref[...] = jnp.zeros(acc_ref.shape, _F32)
    else:
        for i, sk in enumerate(sinks):
            m_ref[i] = jnp.full((1, tq), sk * LOG2_E, _F32)
        row = lax.broadcasted_iota(jnp.int32, (ACC_ROWS, tq), 0)
        acc_ref[...] = jnp.broadcast_to(jnp.where(row == HEAD_DIM, 1.0, 0.0).astype(_F32), acc_ref.shape)


def _part(j):
    return slice(j * KV_SUB, (j + 1) * KV_SUB)


def _scores(scr, slot, j, item):
    s_ref, mc_ref = scr[0], scr[1]
    _, k, qt, _, bias = item
    kj = k[_part(j)]
    s = jnp.concatenate([jnp.dot(kj[r:r + SCORE_ROWS], qt, preferred_element_type=_F32)
                         for r in range(0, KV_SUB, SCORE_ROWS)], axis=0)
    if bias is not None:
        s = s + bias[_part(j)]
    s_ref[slot, _part(j)] = s
    mx = jnp.max(s, axis=0, keepdims=True)
    mc_ref[slot] = mx if j == 0 else jnp.maximum(mc_ref[slot], mx)


def _new_max(scr, s_slot, p_slot, st):
    mc_ref, al_ref, m_ref = scr[1], scr[3], scr[4]
    m_old = m_ref[st]
    m_new = jnp.maximum(m_old, mc_ref[s_slot])
    al_ref[p_slot] = jnp.exp2(m_old - m_new)
    m_ref[st] = m_new


def _softmax(scr, s_slot, p_slot, j, st):
    s_ref, p_ref, m_ref = scr[0], scr[2], scr[4]
    p_ref[p_slot, _part(j)] = jnp.exp2(s_ref[s_slot, _part(j)] - m_ref[st]).astype(_BF16)


def _values(scr, slot, j, item):
    p_ref, al_ref, acc_ref = scr[2], scr[3], scr[5]
    st, vts = item[0], item[3]
    pv = jnp.dot(vts[j], p_ref[slot, _part(j)], preferred_element_type=_F32)
    acc = acc_ref[st]
    acc_ref[st] = (al_ref[slot] * acc if j == 0 else acc) + pv


def _pipeline(scr, items):
    n = len(items)
    n_parts = len(items[0][3])
    for i0 in range(min(SCORE_AHEAD, n)):
        for j in range(n_parts):
            _scores(scr, i0 % S_SLOTS, j, items[i0])
    for i in range(n + VALUE_BEHIND):
        if i < n:
            _new_max(scr, i % S_SLOTS, i % P_SLOTS, items[i][0])
        for j in range(n_parts):
            if i + SCORE_AHEAD < n:
                _scores(scr, (i + SCORE_AHEAD) % S_SLOTS, j, items[i + SCORE_AHEAD])
            if i < n:
                _softmax(scr, i % S_SLOTS, i % P_SLOTS, j, items[i][0])
            if i >= VALUE_BEHIND:
                _values(scr, (i - VALUE_BEHIND) % P_SLOTS, j, items[i - VALUE_BEHIND])


def _finish(scr, first):
    acc_ref = scr[5]
    ot = jnp.concatenate([acc_ref[first + h, 0:HEAD_DIM] / acc_ref[first + h, HEAD_DIM:HEAD_DIM + 1]
                          for h in range(KV_GROUP)], axis=0)
    return ot.T


def _heads(qt_ref, g=None):
    base = 0 if g is None else g * GROUP_COLS
    return [qt_ref[0, base + h * HEAD_DIM: base + (h + 1) * HEAD_DIM, :] for h in range(KV_GROUP)]


def _vt_part(vt_ref, lead, piece):
    return jnp.concatenate([vt_ref[lead + (piece,)], vt_ref[lead + (piece + 1,)]], axis=1)


def _query_blocks(qt_ref, tq):
    return [[qt_ref[0, h * HEAD_DIM:(h + 1) * HEAD_DIM, b * tq:(b + 1) * tq] for h in range(KV_GROUP)]
            for b in range(qt_ref.shape[2] // tq)]


def _chunk_items(k_ref, vt_ref, blocks, tk):
    items = []
    for c in range(k_ref.shape[2] // tk):
        k = k_ref[0, 0, c * tk:(c + 1) * tk, :]
        vts = [_vt_part(vt_ref, (0, 0), (c * tk + j * KV_SUB) // V_PIECE) for j in range(tk // KV_SUB)]
        for b, heads in enumerate(blocks):
            items += [(b * KV_GROUP + h, k, heads[h], vts, None) for h in range(KV_GROUP)]
    return items


def _store_blocks(scr, o_ref, tq, n_blocks):
    for b in range(n_blocks):
        o_ref[0, b * tq:(b + 1) * tq, :] = _finish(scr, b * KV_GROUP).astype(o_ref.dtype)


def _full_body(qt_ref, kc_ref, vtc_ref, kl_ref, vtl_ref, o_ref, *scr, tq, tk):
    blocks = _query_blocks(qt_ref, tq)
    _init_states(scr)
    _pipeline(scr, _chunk_items(kc_ref, vtc_ref, blocks, tk) + _chunk_items(kl_ref, vtl_ref, blocks, tk))
    _store_blocks(scr, o_ref, tq, len(blocks))


def _kv_specs(lk, index_map):
    return [pl.BlockSpec((1, 1, lk, HEAD_DIM), lambda *i: index_map(*i) + (0, 0)),
            pl.BlockSpec((1, 1, lk // V_PIECE, ACC_ROWS, V_PIECE), lambda *i: index_map(*i) + (0, 0, 0))]


def _attend_full(qt, k_ctx, vt_ctx, k_lat, vt_lat, tq, tk, n_blocks):
    b, d, l = qt.shape
    ts = tq * n_blocks
    by_group = lambda bi, g, qi: (bi, g)
    return pl.pallas_call(
        functools.partial(_full_body, tq=tq, tk=tk),
        grid=(b, N_KV_HEADS, l // ts),
        in_specs=([pl.BlockSpec((1, GROUP_COLS, ts), lambda bi, g, qi: (bi, g, qi))]
                  + _kv_specs(k_ctx.shape[2], by_group) + _kv_specs(l, by_group)),
        out_specs=pl.BlockSpec((1, ts, GROUP_COLS), lambda bi, g, qi: (bi, qi, g)),
        out_shape=jax.ShapeDtypeStruct((b, l, d), _BF16),
        scratch_shapes=_attn_scratch(KV_GROUP * n_blocks, tq, tk),
        compiler_params=_params(3),
        name="attn_full",
    )(qt, k_ctx, vt_ctx, k_lat, vt_lat)


def _band_start(q0, l, tq):
    return jnp.clip(q0 - WINDOW, 0, l - 2 * tq)


def _window_body(sink_ref, qt_ref, kc_ref, vtc_ref, kl_ref, vtl_ref, bias_ref, o_ref, *scr, tq):
    g = pl.program_id(1)
    blocks = _query_blocks(qt_ref, tq)
    _init_states(scr, [sink_ref[g * KV_GROUP + h] for _ in blocks for h in range(KV_GROUP)])
    items = _chunk_items(kc_ref, vtc_ref, blocks, 2 * tq)
    for b, heads in enumerate(blocks):
        q0 = (pl.program_id(2) * len(blocks) + b) * tq
        start = pl.multiple_of(_band_start(q0, kl_ref.shape[2], tq), V_PIECE)
        k_band = kl_ref[0, 0, pl.ds(start, 2 * tq), :]
        vt_band = [_vt_part(vtl_ref, (0, 0), start // V_PIECE + j * (KV_SUB // V_PIECE)) for j in range(2)]
        items += [(b * KV_GROUP + h, k_band, heads[h], vt_band, bias_ref[b]) for h in range(KV_GROUP)]
    _pipeline(scr, items)
    _store_blocks(scr, o_ref, tq, len(blocks))


def _window_bias(l, tq):
    q0 = (jnp.arange(l // tq) * tq)[:, None, None]
    kpos = _band_start(q0, l, tq) + jnp.arange(2 * tq)[None, :, None]
    qpos = q0 + jnp.arange(tq)[None, None, :]
    return jnp.where(jnp.abs(qpos - kpos) <= WINDOW, 0.0, NEG_INF).astype(_F32)


def _attend_window(qt, k_ctx, vt_ctx, k_lat, vt_lat, sink, tq, n_blocks):
    b, d, l = qt.shape
    lc = k_ctx.shape[2]
    ts = tq * n_blocks
    assert tq == KV_SUB and tq >= 2 * WINDOW and lc == 2 * tq and l >= 2 * tq
    by_group = lambda bi, g, qi: (bi, g)
    return pl.pallas_call(
        functools.partial(_window_body, tq=tq),
        grid=(b, N_KV_HEADS, l // ts),
        in_specs=([pl.BlockSpec(memory_space=pltpu.SMEM),
                   pl.BlockSpec((1, GROUP_COLS, ts), lambda bi, g, qi: (bi, g, qi))]
                  + _kv_specs(lc, by_group) + _kv_specs(l, by_group)
                  + [pl.BlockSpec((n_blocks, 2 * tq, tq), lambda bi, g, qi: (qi, 0, 0))]),
        out_specs=pl.BlockSpec((1, ts, GROUP_COLS), lambda bi, g, qi: (bi, qi, g)),
        out_shape=jax.ShapeDtypeStruct((b, l, d), _BF16),
        scratch_shapes=_attn_scratch(KV_GROUP * n_blocks, tq, 2 * tq),
        compiler_params=_params(3),
        name="attn_window",
    )(sink, qt, k_ctx, vt_ctx, k_lat, vt_lat, _window_bias(l, tq))


def _ctx_body(*refs, has_sink):
    if has_sink:
        sink_ref, qt_ref, k_ref, vt_ref, o_ref = refs[:5]
    else:
        qt_ref, k_ref, vt_ref, o_ref = refs[:4]
    scr = refs[5:] if has_sink else refs[4:]
    n_sub = k_ref.shape[2] // KV_SUB
    _init_states(scr, [sink_ref[i] for i in range(N_HEADS)] if has_sink else None)
    items = []
    for g in range(N_KV_HEADS):
        heads = _heads(qt_ref, g)
        k = k_ref[0, g]
        vts = [_vt_part(vt_ref, (0, g), j * (KV_SUB // V_PIECE)) for j in range(n_sub)]
        items += [(g * KV_GROUP + h, k, heads[h], vts, None) for h in range(KV_GROUP)]
    _pipeline(scr, items)
    for g in range(N_KV_HEADS):
        o_ref[0, :, g * GROUP_COLS:(g + 1) * GROUP_COLS] = _finish(scr, g * KV_GROUP).astype(o_ref.dtype)


def _attend_ctx(qt, k, vt, sink):
    b, d, l = qt.shape
    has_sink = sink is not None
    in_specs = [
        pl.BlockSpec((1, d, l), lambda bi: (bi, 0, 0)),
        pl.BlockSpec((1, N_KV_HEADS, l, HEAD_DIM), lambda bi: (bi, 0, 0, 0)),
        pl.BlockSpec((1, N_KV_HEADS, l // V_PIECE, ACC_ROWS, V_PIECE), lambda bi: (bi, 0, 0, 0, 0)),
    ]
    args = [qt, k, vt]
    if has_sink:
        in_specs = [pl.BlockSpec(memory_space=pltpu.SMEM)] + in_specs
        args = [sink] + args
    return pl.pallas_call(
        functools.partial(_ctx_body, has_sink=has_sink),
        grid=(b,),
        in_specs=in_specs,
        out_specs=pl.BlockSpec((1, l, d), lambda bi: (bi, 0, 0)),
        out_shape=jax.ShapeDtypeStruct((b, l, d), _BF16),
        scratch_shapes=_attn_scratch(N_HEADS, l, l),
        compiler_params=_params(1),
        name="attn_ctx_sink" if has_sink else "attn_ctx",
    )(*args)


def _cache_layout(cache_k, cache_v):
    b, depth, p, g, hd = cache_v.shape
    k = cache_k.transpose(1, 0, 3, 2, 4).astype(_BF16)
    vt = cache_v.reshape(b, depth, p // V_PIECE, V_PIECE, g, hd).transpose(1, 0, 4, 2, 5, 3).astype(_BF16)
    extra = jnp.zeros((ACC_ROWS - hd, V_PIECE), _BF16).at[0].set(1.0)
    return k, jnp.concatenate([vt, jnp.broadcast_to(extra, vt.shape[:4] + extra.shape)], axis=4)


def _token_tile(t, cap):
    tm = min(cap, t)
    assert t % tm == 0
    return tm


def kernel(x_prompt, x_sample, cache_k, cache_v, c, c_ctx, w_mod, b_mod, norm_g, w_qkv, w_o,
           q_norm_g, k_norm_g, sink, w_ffn_in, w_ffn_out):
    depth = w_mod.shape[0]
    d = D_MODEL
    bp, lp, _ = x_prompt.shape
    bs, ls, _ = x_sample.shape
    past = cache_k.shape[2]
    assert lp % KV_SUB == 0 and ls % KV_TILE == 0 and past % KV_TILE == 0

    w_in = (w_ffn_in.astype(_BF16).reshape(depth, 2, d, 2 * N_FF_CHUNKS, FF_CHUNK)
            .transpose(0, 1, 3, 2, 4))
    w_out = w_ffn_out.astype(_BF16).reshape(depth, 2, N_FF_CHUNKS, FF_CHUNK, d)
    w_qkv_b = w_qkv.astype(_BF16)
    w_o_b = w_o.astype(_BF16)

    n_rows = -(-(bs + 1) // 8) * 8
    cond = jnp.concatenate([c, c_ctx[None, :], jnp.zeros((n_rows - bs - 1, d), _F32)], axis=0)
    mods = _modulation(cond, w_mod, b_mod)

    tp = bp * lp
    tm_p = _token_tile(tp, FFN_TILE)
    tq_p = _token_tile(lp, QKV_TILE)
    ctx_row = lambda i: bs
    qg_p = jnp.broadcast_to(q_norm_g[:, :, None], (depth, HEAD_DIM, tq_p))
    kg_p = jnp.broadcast_to(k_norm_g[:, :, None], (depth, HEAD_DIM, tq_p))
    h = x_prompt.reshape(tp, d)
    new_ks, new_vs = [], []
    for i in range(depth):
        h = _ffn(h, mods, norm_g, w_in, w_out, i, 0, ctx_row, tm_p)
        qt, ka, vt, k, v = _qkv(h, mods, norm_g, w_qkv_b, qg_p, kg_p, i, ctx_row, tq_p, lp, True)
        new_ks.append(k.reshape(bp, lp, N_KV_HEADS, HEAD_DIM))
        new_vs.append(v.reshape(bp, lp, N_KV_HEADS, HEAD_DIM))
        o = _attend_ctx(qt, ka, vt, sink[i // 2] if i % 2 == 1 else None)
        h = _ffn(h, mods, norm_g, w_in, w_out, i, 1, ctx_row, tm_p, o=o.reshape(tp, d), w_o=w_o_b)
    y_prompt = h.reshape(bp, lp, d)
    new_k = jnp.stack(new_ks, axis=1)
    new_v = jnp.stack(new_vs, axis=1)

    ts = bs * ls
    tm_s = _token_tile(ls, FFN_TILE)
    tq_s = _token_tile(ls, QKV_TILE)
    lat_row = lambda i: i // (ls // tm_s)
    lat_row_q = lambda i: i // (ls // tq_s)
    rope_t = _rope_table_t(ls)
    qg_s = jnp.broadcast_to(q_norm_g[:, :, None], (depth, HEAD_DIM, tq_s))
    kg_s = jnp.broadcast_to(k_norm_g[:, :, None], (depth, HEAD_DIM, tq_s))
    k_ctx, vt_ctx = _cache_layout(cache_k, cache_v)
    blocks_per_step = lambda want: want if ls % (want * Q_TILE) == 0 else 1
    h = x_sample.reshape(ts, d)
    for i in range(depth):
        h = _ffn(h, mods, norm_g, w_in, w_out, i, 0, lat_row, tm_s)
        qt, ka, vt = _qkv(h, mods, norm_g, w_qkv_b, qg_s, kg_s, i, lat_row_q, tq_s, ls, False, rope_t)
        if i % 2 == 0:
            o = _attend_full(qt, k_ctx[i], vt_ctx[i], ka, vt, Q_TILE, KV_TILE, blocks_per_step(FULL_Q_BLOCKS))
        else:
            o = _attend_window(qt, k_ctx[i], vt_ctx[i], ka, vt, sink[i // 2], Q_TILE,
                               blocks_per_step(WINDOW_Q_BLOCKS))
        h = _ffn(h, mods, norm_g, w_in, w_out, i, 1, lat_row, tm_s, o=o.reshape(ts, d), w_o=w_o_b)
    y_sample = h.reshape(bs, ls, d)

    return (y_prompt, y_sample, new_k, new_v)
```

```python
import functools

import jax
import jax.numpy as jnp
from jax import lax
from jax.experimental import pallas as pl
from jax.experimental.pallas import tpu as pltpu

D_MODEL = 1024
N_HEADS = 16
N_KV_HEADS = 4
HEAD_DIM = 64
KV_GROUP = N_HEADS // N_KV_HEADS
GROUP_COLS = KV_GROUP * HEAD_DIM
KV_COLS = N_KV_HEADS * HEAD_DIM
QKV_COLS = (N_HEADS + 2 * N_KV_HEADS) * HEAD_DIM
D_FF = 2816
N_MOD = 9
GRID_W = 64
WINDOW = 128
ROPE_FREQS = HEAD_DIM // 4
ROPE_THETA = 10000.0
ATTN_SCALE = HEAD_DIM ** -0.5
LOG2_E = 1.4426950408889634
Q_SCALE = ATTN_SCALE * LOG2_E
EPS = 1e-6
NEG_INF = -1e30

FF_CHUNK = 256
N_FF_CHUNKS = D_FF // FF_CHUNK
FFN_TILE = 1024
QKV_TILE = 1024
QKV_SUB = 512
Q_TILE = 256
FULL_Q_BLOCKS = 4
WINDOW_Q_BLOCKS = 4
KV_TILE = 512
VMEM_LIMIT_BYTES = 52 * 1024 * 1024

_BF16 = jnp.bfloat16
_F32 = jnp.float32


def _params(n_axes):
    return pltpu.CompilerParams(dimension_semantics=("arbitrary",) * n_axes,
                                vmem_limit_bytes=VMEM_LIMIT_BYTES)


def _resident(block_shape, index_map):
    return pl.BlockSpec(block_shape, index_map, pipeline_mode=pl.Buffered(1))


def _silu(x):
    return x * jax.nn.sigmoid(x)


def _modln(x, g, shift, scale):
    y = x * lax.rsqrt(jnp.mean(x * x, axis=-1, keepdims=True) + EPS)
    return y * (g * (1.0 + scale)) + shift


def _mod_body(cond_ref, w_ref, b_ref, o_ref):
    a = _silu(cond_ref[...]).astype(_BF16)
    w = w_ref[0].astype(_BF16)
    o_ref[0] = jnp.dot(a, w, preferred_element_type=_F32) + b_ref[0]


def _modulation(cond, w_mod, b_mod):
    depth, d, n = w_mod.shape
    r = cond.shape[0]
    tn = n // 8
    out = pl.pallas_call(
        _mod_body,
        grid=(depth, n // tn),
        in_specs=[
            pl.BlockSpec((r, d), lambda i, j: (0, 0)),
            pl.BlockSpec((1, d, tn), lambda i, j: (i, 0, j)),
            pl.BlockSpec((1, 1, tn), lambda i, j: (i, 0, j)),
        ],
        out_specs=pl.BlockSpec((1, r, tn), lambda i, j: (i, 0, j)),
        out_shape=jax.ShapeDtypeStruct((depth, r, n), _F32),
        compiler_params=_params(2),
        name="modulation",
    )(cond, w_mod, b_mod.reshape(depth, 1, n))
    return out.reshape(depth, r, N_MOD, d)


def _ffn_body(*refs, sub, has_proj):
    if has_proj:
        x_ref, mods_ref, ng_ref, o_ref, wo_ref, win_ref, wout_ref, out_ref, xn_ref, acc_ref = refs
    else:
        x_ref, mods_ref, ng_ref, win_ref, wout_ref, out_ref, xn_ref, acc_ref = refs
    m = mods_ref[0, 0]
    x = x_ref[...]
    if has_proj:
        x = x + m[5:6] * jnp.dot(o_ref[...], wo_ref[0], preferred_element_type=_F32)
    k = 3 * sub
    xn_ref[...] = _modln(x, ng_ref[0, sub:sub + 1], m[k:k + 1], m[k + 1:k + 2]).astype(_BF16)
    if has_proj:
        out_ref[...] = x

    def ffn_chunk(c):
        xn = xn_ref[...]
        gate = jnp.dot(xn, win_ref[0, 0, c], preferred_element_type=_F32)
        up = jnp.dot(xn, win_ref[0, 0, c + N_FF_CHUNKS], preferred_element_type=_F32)
        act = (_silu(gate) * up).astype(_BF16)
        return jnp.dot(act, wout_ref[0, 0, c], preferred_element_type=_F32)

    acc_ref[...] = ffn_chunk(0)

    assert N_FF_CHUNKS % 2 == 1
    for c in range(1, N_FF_CHUNKS, 2):
        acc_ref[...] += ffn_chunk(c) + ffn_chunk(c + 1)
    h = out_ref[...] if has_proj else x_ref[...]
    out_ref[...] = h + (0.5 * m[k + 2:k + 3]) * acc_ref[...]


def _ffn(x, mods, norm_g, w_in, w_out, layer, which, row_of_tile, tm, o=None, w_o=None):
    t, d = x.shape
    sub = 2 * which
    has_proj = o is not None
    in_specs = [
        pl.BlockSpec((tm, d), lambda i: (i, 0)),
        pl.BlockSpec((1, 1, N_MOD, d), lambda i: (layer, row_of_tile(i), 0, 0)),
        pl.BlockSpec((1, 3, d), lambda i: (layer, 0, 0)),
    ]
    args = [x, mods, norm_g]
    if has_proj:
        in_specs += [pl.BlockSpec((tm, d), lambda i: (i, 0)),
                     _resident((1, d, d), lambda i: (layer, 0, 0))]
        args += [o, w_o]
    in_specs += [
        _resident((1, 1, 2 * N_FF_CHUNKS, d, FF_CHUNK), lambda i: (layer, which, 0, 0, 0)),
        _resident((1, 1, N_FF_CHUNKS, FF_CHUNK, d), lambda i: (layer, which, 0, 0, 0)),
    ]
    args += [w_in, w_out]
    return pl.pallas_call(
        functools.partial(_ffn_body, sub=sub, has_proj=has_proj),
        grid=(t // tm,),
        in_specs=in_specs,
        out_specs=pl.BlockSpec((tm, d), lambda i: (i, 0)),
        out_shape=jax.ShapeDtypeStruct((t, d), _F32),
        scratch_shapes=[pltpu.VMEM((tm, d), _BF16), pltpu.VMEM((tm, d), _F32)],
        compiler_params=_params(1),
        name="ffn_proj" if has_proj else "ffn",
    )(*args)


def _head_norm_rope_t(yt, g_t, rope_ref):
    ss = jnp.sum(yt * yt, axis=0, keepdims=True)
    yn = (yt * lax.rsqrt(ss * (1.0 / HEAD_DIM) + EPS)) * g_t
    if rope_ref is None:
        return yn
    f = ROPE_FREQS
    x1r, x2r, x1c, x2c = yn[0:f], yn[f:2 * f], yn[2 * f:3 * f], yn[3 * f:4 * f]
    cr, sr, cc, sc = rope_ref[0], rope_ref[1], rope_ref[2], rope_ref[3]
    return jnp.concatenate([x1r * cr - x2r * sr, x2r * cr + x1r * sr,
                            x1c * cc - x2c * sc, x2c * cc + x1c * sc], axis=0)


def _qkv_body(*refs, rope, keep_f32, sub):
    n_in = 7 if rope else 6
    x_ref, mods_ref, ng_ref, w_ref, qg_ref, kg_ref = refs[:6]
    rope_ref = refs[6] if rope else None
    qt_ref, ka_ref, vt_ref = refs[n_in:n_in + 3]
    m = mods_ref[0, 0]
    tm = x_ref.shape[0]
    k0 = N_HEADS * HEAD_DIM
    row = lax.broadcasted_iota(jnp.int32, (ACC_ROWS - HEAD_DIM, V_PIECE), 0)
    extra = jnp.where(row == 0, 1.0, 0.0).astype(vt_ref.dtype)

    def project(r):
        rows = slice(r * sub, (r + 1) * sub)
        a = _modln(x_ref[rows, :], ng_ref[0, 1:2], m[3:4], m[4:5]).astype(_BF16)
        return jnp.dot(a, w_ref[0], preferred_element_type=_F32)

    def finish(r, proj):
        toks = slice(r * sub, (r + 1) * sub)
        qg, kg = qg_ref[0, :, toks], kg_ref[0, :, toks]
        rope_r = None if rope_ref is None else rope_ref.at[:, :, toks]
        for c in range(k0 // GROUP_COLS):
            yt = proj[:, c * GROUP_COLS:(c + 1) * GROUP_COLS].T
            for h in range(KV_GROUP):
                qn = _head_norm_rope_t(yt[h * HEAD_DIM:(h + 1) * HEAD_DIM], qg, rope_r)
                r0 = c * GROUP_COLS + h * HEAD_DIM
                qt_ref[0, r0:r0 + HEAD_DIM, toks] = (qn * Q_SCALE).astype(qt_ref.dtype)
        kt = proj[:, k0:k0 + KV_COLS].T
        kn = jnp.concatenate([_head_norm_rope_t(kt[g * HEAD_DIM:(g + 1) * HEAD_DIM], kg, rope_r)
                              for g in range(N_KV_HEADS)], axis=0)
        k_tok = kn.T
        v_tok = proj[:, k0 + KV_COLS:]
        vt = v_tok.T
        for g in range(N_KV_HEADS):
            ka_ref[0, g, toks, :] = k_tok[:, g * HEAD_DIM:(g + 1) * HEAD_DIM].astype(ka_ref.dtype)
            for j in range(sub // V_PIECE):
                piece = r * (sub // V_PIECE) + j
                vt_ref[0, g, piece, 0:HEAD_DIM, :] = (
                    vt[g * HEAD_DIM:(g + 1) * HEAD_DIM, j * V_PIECE:(j + 1) * V_PIECE].astype(vt_ref.dtype))
                vt_ref[0, g, piece, HEAD_DIM:ACC_ROWS, :] = extra
        if keep_f32:
            k_ref, v_ref = refs[n_in + 3:n_in + 5]
            k_ref[toks, :] = k_tok
            v_ref[toks, :] = v_tok

    n_sub = tm // sub
    proj = project(0)
    for r in range(n_sub):
        nxt = project(r + 1) if r + 1 < n_sub else None
        finish(r, proj)
        proj = nxt


def _qkv(x, mods, norm_g, w_qkv, qg_t, kg_t, layer, row_of_tile, tm, seq_len, keep_f32, rope_t=None):
    t, d = x.shape
    rope = rope_t is not None
    tiles_per_seq = seq_len // tm
    n_seq = t // seq_len
    seq_tile = lambda i: (i // tiles_per_seq, 0, i % tiles_per_seq)
    out_specs = [pl.BlockSpec((1, N_HEADS * HEAD_DIM, tm), seq_tile),
                 pl.BlockSpec((1, N_KV_HEADS, tm, HEAD_DIM), lambda i: seq_tile(i) + (0,)),
                 pl.BlockSpec((1, N_KV_HEADS, tm // V_PIECE, ACC_ROWS, V_PIECE), lambda i: seq_tile(i) + (0, 0))]
    out_shape = [jax.ShapeDtypeStruct((n_seq, N_HEADS * HEAD_DIM, seq_len), _BF16),
                 jax.ShapeDtypeStruct((n_seq, N_KV_HEADS, seq_len, HEAD_DIM), _BF16),
                 jax.ShapeDtypeStruct((n_seq, N_KV_HEADS, seq_len // V_PIECE, ACC_ROWS, V_PIECE), _BF16)]
    if keep_f32:
        out_specs += [pl.BlockSpec((tm, KV_COLS), lambda i: (i, 0))] * 2
        out_shape += [jax.ShapeDtypeStruct((t, KV_COLS), _F32)] * 2
    in_specs = [
        pl.BlockSpec((tm, d), lambda i: (i, 0)),
        pl.BlockSpec((1, 1, N_MOD, d), lambda i: (layer, row_of_tile(i), 0, 0)),
        pl.BlockSpec((1, 3, d), lambda i: (layer, 0, 0)),
        _resident((1, d, QKV_COLS), lambda i: (layer, 0, 0)),
        pl.BlockSpec((1, HEAD_DIM, tm), lambda i: (layer, 0, 0)),
        pl.BlockSpec((1, HEAD_DIM, tm), lambda i: (layer, 0, 0)),
    ]
    args = [x, mods, norm_g, w_qkv, qg_t, kg_t]
    if rope:
        in_specs += [pl.BlockSpec((4, ROPE_FREQS, tm), lambda i: (0, 0, i % tiles_per_seq))]
        args += [rope_t]
    return pl.pallas_call(
        functools.partial(_qkv_body, rope=rope, keep_f32=keep_f32, sub=min(tm, QKV_SUB)),
        grid=(t // tm,),
        in_specs=in_specs,
        out_specs=out_specs,
        out_shape=out_shape,
        compiler_params=_params(1),
        name="qkv_rope" if rope else "qkv",
    )(*args)


def _rope_table_t(n_tok):
    pos = jnp.arange(n_tok)
    row = (pos // GRID_W).astype(_F32)
    col = (pos % GRID_W).astype(_F32)
    freqs = 1.0 / jnp.power(ROPE_THETA, jnp.arange(ROPE_FREQS, dtype=_F32) / ROPE_FREQS)
    ang_r = row[:, None] * freqs
    ang_c = col[:, None] * freqs
    return jnp.stack([jnp.cos(ang_r).T, jnp.sin(ang_r).T, jnp.cos(ang_c).T, jnp.sin(ang_c).T])


KV_SUB = 256
V_PIECE = 128
SCORE_ROWS = 128
BF16_SUBLANES = 16
ACC_ROWS = HEAD_DIM + BF16_SUBLANES


SCORE_AHEAD = 2
VALUE_BEHIND = 1
S_SLOTS = SCORE_AHEAD + 1
P_SLOTS = VALUE_BEHIND + 1


def _attn_scratch(n_states, tq, tk):
    return [pltpu.VMEM((S_SLOTS, tk, tq), _F32),
            pltpu.VMEM((S_SLOTS, 1, tq), _F32),
            pltpu.VMEM((P_SLOTS, tk, tq), _BF16),
            pltpu.VMEM((P_SLOTS, 1, tq), _F32),
            pltpu.VMEM((n_states, 1, tq), _F32),
            pltpu.VMEM((n_states, ACC_ROWS, tq), _F32)]


def _init_states(scr, sinks=None):
    m_ref, acc_ref = scr[4], scr[5]
    tq = m_ref.shape[-1]
    if sinks is None:
        m_ref[...] = jnp.full(m_ref.shape, NEG_INF, _F32)
        acc_ref[...] = jnp.zeros(acc_ref.shape, _F32)
    else:
        for i, sk in enumerate(sinks):
            m_ref[i] = jnp.full((1, tq), sk * LOG2_E, _F32)
        row = lax.broadcasted_iota(jnp.int32, (ACC_ROWS, tq), 0)
        acc_ref[...] = jnp.broadcast_to(jnp.where(row == HEAD_DIM, 1.0, 0.0).astype(_F32), acc_ref.shape)


def _part(j):
    return slice(j * KV_SUB, (j + 1) * KV_SUB)


def _scores(scr, slot, j, item):
    s_ref, mc_ref = scr[0], scr[1]
    _, k, qt, _, bias = item
    kj = k[_part(j)]
    s = jnp.concatenate([jnp.dot(kj[r:r + SCORE_ROWS], qt, preferred_element_type=_F32)
                         for r in range(0, KV_SUB, SCORE_ROWS)], axis=0)
    if bias is not None:
        s = s + bias[_part(j)]
    s_ref[slot, _part(j)] = s
    mx = jnp.max(s, axis=0, keepdims=True)
    mc_ref[slot] = mx if j == 0 else jnp.maximum(mc_ref[slot], mx)


def _new_max(scr, s_slot, p_slot, st):
    mc_ref, al_ref, m_ref = scr[1], scr[3], scr[4]
    m_old = m_ref[st]
    m_new = jnp.maximum(m_old, mc_ref[s_slot])
    al_ref[p_slot] = jnp.exp2(m_old - m_new)
    m_ref[st] = m_new


def _softmax(scr, s_slot, p_slot, j, st):
    s_ref, p_ref, m_ref = scr[0], scr[2], scr[4]
    p_ref[p_slot, _part(j)] = jnp.exp2(s_ref[s_slot, _part(j)] - m_ref[st]).astype(_BF16)


def _values(scr, slot, j, item):
    p_ref, al_ref, acc_ref = scr[2], scr[3], scr[5]
    st, vts = item[0], item[3]
    pv = jnp.dot(vts[j], p_ref[slot, _part(j)], preferred_element_type=_F32)
    acc = acc_ref[st]
    acc_ref[st] = (al_ref[slot] * acc if j == 0 else acc) + pv


def _pipeline(scr, items):
    n = len(items)
    n_parts = len(items[0][3])
    for i0 in range(min(SCORE_AHEAD, n)):
        for j in range(n_parts):
            _scores(scr, i0 % S_SLOTS, j, items[i0])
    for i in range(n + VALUE_BEHIND):
        if i < n:
            _new_max(scr, i % S_SLOTS, i % P_SLOTS, items[i][0])
        for j in range(n_parts):
            if i + SCORE_AHEAD < n:
                _scores(scr, (i + SCORE_AHEAD) % S_SLOTS, j, items[i + SCORE_AHEAD])
            if i < n:
                _softmax(scr, i % S_SLOTS, i % P_SLOTS, j, items[i][0])
            if i >= VALUE_BEHIND:
                _values(scr, (i - VALUE_BEHIND) % P_SLOTS, j, items[i - VALUE_BEHIND])


def _finish(scr, first):
    acc_ref = scr[5]
    ot = jnp.concatenate([acc_ref[first + h, 0:HEAD_DIM] / acc_ref[first + h, HEAD_DIM:HEAD_DIM + 1]
                          for h in range(KV_GROUP)], axis=0)
    return ot.T


def _heads(qt_ref, g=None):
    base = 0 if g is None else g * GROUP_COLS
    return [qt_ref[0, base + h * HEAD_DIM: base + (h + 1) * HEAD_DIM, :] for h in range(KV_GROUP)]


def _vt_part(vt_ref, lead, piece):
    return jnp.concatenate([vt_ref[lead + (piece,)], vt_ref[lead + (piece + 1,)]], axis=1)


def _query_blocks(qt_ref, tq):
    return [[qt_ref[0, h * HEAD_DIM:(h + 1) * HEAD_DIM, b * tq:(b + 1) * tq] for h in range(KV_GROUP)]
            for b in range(qt_ref.shape[2] // tq)]


def _chunk_items(k_ref, vt_ref, blocks, tk):
    items = []
    for c in range(k_ref.shape[2] // tk):
        k = k_ref[0, 0, c * tk:(c + 1) * tk, :]
        vts = [_vt_part(vt_ref, (0, 0), (c * tk + j * KV_SUB) // V_PIECE) for j in range(tk // KV_SUB)]
        for b, heads in enumerate(blocks):
            items += [(b * KV_GROUP + h, k, heads[h], vts, None) for h in range(KV_GROUP)]
    return items


def _store_blocks(scr, o_ref, tq, n_blocks):
    for b in range(n_blocks):
        o_ref[0, b * tq:(b + 1) * tq, :] = _finish(scr, b * KV_GROUP).astype(o_ref.dtype)


def _full_body(qt_ref, kc_ref, vtc_ref, kl_ref, vtl_ref, o_ref, *scr, tq, tk):
    blocks = _query_blocks(qt_ref, tq)
    _init_states(scr)
    _pipeline(scr, _chunk_items(kc_ref, vtc_ref, blocks, tk) + _chunk_items(kl_ref, vtl_ref, blocks, tk))
    _store_blocks(scr, o_ref, tq, len(blocks))


def _kv_specs(lk, index_map):
    return [pl.BlockSpec((1, 1, lk, HEAD_DIM), lambda *i: index_map(*i) + (0, 0)),
            pl.BlockSpec((1, 1, lk // V_PIECE, ACC_ROWS, V_PIECE), lambda *i: index_map(*i) + (0, 0, 0))]


def _attend_full(qt, k_ctx, vt_ctx, k_lat, vt_lat, tq, tk, n_blocks):
    b, d, l = qt.shape
    ts = tq * n_blocks
    by_group = lambda bi, g, qi: (bi, g)
    return pl.pallas_call(
        functools.partial(_full_body, tq=tq, tk=tk),
        grid=(b, N_KV_HEADS, l // ts),
        in_specs=([pl.BlockSpec((1, GROUP_COLS, ts), lambda bi, g, qi: (bi, g, qi))]
                  + _kv_specs(k_ctx.shape[2], by_group) + _kv_specs(l, by_group)),
        out_specs=pl.BlockSpec((1, ts, GROUP_COLS), lambda bi, g, qi: (bi, qi, g)),
        out_shape=jax.ShapeDtypeStruct((b, l, d), _BF16),
        scratch_shapes=_attn_scratch(KV_GROUP * n_blocks, tq, tk),
        compiler_params=_params(3),
        name="attn_full",
    )(qt, k_ctx, vt_ctx, k_lat, vt_lat)


def _band_start(q0, l, tq):
    return jnp.clip(q0 - WINDOW, 0, l - 2 * tq)


def _window_body(sink_ref, qt_ref, kc_ref, vtc_ref, kl_ref, vtl_ref, bias_ref, o_ref, *scr, tq):
    g = pl.program_id(1)
    blocks = _query_blocks(qt_ref, tq)
    _init_states(scr, [sink_ref[g * KV_GROUP + h] for _ in blocks for h in range(KV_GROUP)])
    items = _chunk_items(kc_ref, vtc_ref, blocks, 2 * tq)
    for b, heads in enumerate(blocks):
        q0 = (pl.program_id(2) * len(blocks) + b) * tq
        start = pl.multiple_of(_band_start(q0, kl_ref.shape[2], tq), V_PIECE)
        k_band = kl_ref[0, 0, pl.ds(start, 2 * tq), :]
        vt_band = [_vt_part(vtl_ref, (0, 0), start // V_PIECE + j * (KV_SUB // V_PIECE)) for j in range(2)]
        items += [(b * KV_GROUP + h, k_band, heads[h], vt_band, bias_ref[b]) for h in range(KV_GROUP)]
    _pipeline(scr, items)
    _store_blocks(scr, o_ref, tq, len(blocks))


def _window_bias(l, tq):
    q0 = (jnp.arange(l // tq) * tq)[:, None, None]
    kpos = _band_start(q0, l, tq) + jnp.arange(2 * tq)[None, :, None]
    qpos = q0 + jnp.arange(tq)[None, None, :]
    return jnp.where(jnp.abs(qpos - kpos) <= WINDOW, 0.0, NEG_INF).astype(_F32)


def _attend_window(qt, k_ctx, vt_ctx, k_lat, vt_lat, sink, tq, n_blocks):
    b, d, l = qt.shape
    lc = k_ctx.shape[2]
    ts = tq * n_blocks
    assert tq == KV_SUB and tq >= 2 * WINDOW and lc == 2 * tq and l >= 2 * tq
    by_group = lambda bi, g, qi: (bi, g)
    return pl.pallas_call(
        functools.partial(_window_body, tq=tq),
        grid=(b, N_KV_HEADS, l // ts),
        in_specs=([pl.BlockSpec(memory_space=pltpu.SMEM),
                   pl.BlockSpec((1, GROUP_COLS, ts), lambda bi, g, qi: (bi, g, qi))]
                  + _kv_specs(lc, by_group) + _kv_specs(l, by_group)
                  + [pl.BlockSpec((n_blocks, 2 * tq, tq), lambda bi, g, qi: (qi, 0, 0))]),
        out_specs=pl.BlockSpec((1, ts, GROUP_COLS), lambda bi, g, qi: (bi, qi, g)),
        out_shape=jax.ShapeDtypeStruct((b, l, d), _BF16),
        scratch_shapes=_attn_scratch(KV_GROUP * n_blocks, tq, 2 * tq),
        compiler_params=_params(3),
        name="attn_window",
    )(sink, qt, k_ctx, vt_ctx, k_lat, vt_lat, _window_bias(l, tq))


def _ctx_body(*refs, has_sink):
    if has_sink:
        sink_ref, qt_ref, k_ref, vt_ref, o_ref = refs[:5]
    else:
        qt_ref, k_ref, vt_ref, o_ref = refs[:4]
    scr = refs[5:] if has_sink else refs[4:]
    n_sub = k_ref.shape[2] // KV_SUB
    _init_states(scr, [sink_ref[i] for i in range(N_HEADS)] if has_sink else None)
    items = []
    for g in range(N_KV_HEADS):
        heads = _heads(qt_ref, g)
        k = k_ref[0, g]
        vts = [_vt_part(vt_ref, (0, g), j * (KV_SUB // V_PIECE)) for j in range(n_sub)]
        items += [(g * KV_GROUP + h, k, heads[h], vts, None) for h in range(KV_GROUP)]
    _pipeline(scr, items)
    for g in range(N_KV_HEADS):
        o_ref[0, :, g * GROUP_COLS:(g + 1) * GROUP_COLS] = _finish(scr, g * KV_GROUP).astype(o_ref.dtype)


def _attend_ctx(qt, k, vt, sink):
    b, d, l = qt.shape
    has_sink = sink is not None
    in_specs = [
        pl.BlockSpec((1, d, l), lambda bi: (bi, 0, 0)),
        pl.BlockSpec((1, N_KV_HEADS, l, HEAD_DIM), lambda bi: (bi, 0, 0, 0)),
        pl.BlockSpec((1, N_KV_HEADS, l // V_PIECE, ACC_ROWS, V_PIECE), lambda bi: (bi, 0, 0, 0, 0)),
    ]
    args = [qt, k, vt]
    if has_sink:
        in_specs = [pl.BlockSpec(memory_space=pltpu.SMEM)] + in_specs
        args = [sink] + args
    return pl.pallas_call(
        functools.partial(_ctx_body, has_sink=has_sink),
        grid=(b,),
        in_specs=in_specs,
        out_specs=pl.BlockSpec((1, l, d), lambda bi: (bi, 0, 0)),
        out_shape=jax.ShapeDtypeStruct((b, l, d), _BF16),
        scratch_shapes=_attn_scratch(N_HEADS, l, l),
        compiler_params=_params(1),
        name="attn_ctx_sink" if has_sink else "attn_ctx",
    )(*args)


def _cache_layout(cache_k, cache_v):
    b, depth, p, g, hd = cache_v.shape
    k = cache_k.transpose(1, 0, 3, 2, 4).astype(_BF16)
    vt = cache_v.reshape(b, depth, p // V_PIECE, V_PIECE, g, hd).transpose(1, 0, 4, 2, 5, 3).astype(_BF16)
    extra = jnp.zeros((ACC_ROWS - hd, V_PIECE), _BF16).at[0].set(1.0)
    return k, jnp.concatenate([vt, jnp.broadcast_to(extra, vt.shape[:4] + extra.shape)], axis=4)


def _token_tile(t, cap):
    tm = min(cap, t)
    assert t % tm == 0
    return tm


def kernel(x_prompt, x_sample, cache_k, cache_v, c, c_ctx, w_mod, b_mod, norm_g, w_qkv, w_o,
           q_norm_g, k_norm_g, sink, w_ffn_in, w_ffn_out):
    depth = w_mod.shape[0]
    d = D_MODEL
    bp, lp, _ = x_prompt.shape
    bs, ls, _ = x_sample.shape
    past = cache_k.shape[2]
    assert lp % KV_SUB == 0 and ls % KV_TILE == 0 and past % KV_TILE == 0

    w_in = (w_ffn_in.astype(_BF16).reshape(depth, 2, d, 2 * N_FF_CHUNKS, FF_CHUNK)
            .transpose(0, 1, 3, 2, 4))
    w_out = w_ffn_out.astype(_BF16).reshape(depth, 2, N_FF_CHUNKS, FF_CHUNK, d)
    w_qkv_b = w_qkv.astype(_BF16)
    w_o_b = w_o.astype(_BF16)

    n_rows = -(-(bs + 1) // 8) * 8
    cond = jnp.concatenate([c, c_ctx[None, :], jnp.zeros((n_rows - bs - 1, d), _F32)], axis=0)
    mods = _modulation(cond, w_mod, b_mod)

    tp = bp * lp
    tm_p = _token_tile(tp, FFN_TILE)
    tq_p = _token_tile(lp, QKV_TILE)
    ctx_row = lambda i: bs
    qg_p = jnp.broadcast_to(q_norm_g[:, :, None], (depth, HEAD_DIM, tq_p))
    kg_p = jnp.broadcast_to(k_norm_g[:, :, None], (depth, HEAD_DIM, tq_p))
    h = x_prompt.reshape(tp, d)
    new_ks, new_vs = [], []
    for i in range(depth):
        h = _ffn(h, mods, norm_g, w_in, w_out, i, 0, ctx_row, tm_p)
        qt, ka, vt, k, v = _qkv(h, mods, norm_g, w_qkv_b, qg_p, kg_p, i, ctx_row, tq_p, lp, True)
        new_ks.append(k.reshape(bp, lp, N_KV_HEADS, HEAD_DIM))
        new_vs.append(v.reshape(bp, lp, N_KV_HEADS, HEAD_DIM))
        o = _attend_ctx(qt, ka, vt, sink[i // 2] if i % 2 == 1 else None)
        h = _ffn(h, mods, norm_g, w_in, w_out, i, 1, ctx_row, tm_p, o=o.reshape(tp, d), w_o=w_o_b)
    y_prompt = h.reshape(bp, lp, d)
    new_k = jnp.stack(new_ks, axis=1)
    new_v = jnp.stack(new_vs, axis=1)

    ts = bs * ls
    tm_s = _token_tile(ls, FFN_TILE)
    tq_s = _token_tile(ls, QKV_TILE)
    lat_row = lambda i: i // (ls // tm_s)
    lat_row_q = lambda i: i // (ls // tq_s)
    rope_t = _rope_table_t(ls)
    qg_s = jnp.broadcast_to(q_norm_g[:, :, None], (depth, HEAD_DIM, tq_s))
    kg_s = jnp.broadcast_to(k_norm_g[:, :, None], (depth, HEAD_DIM, tq_s))
    k_ctx, vt_ctx = _cache_layout(cache_k, cache_v)
    blocks_per_step = lambda want: want if ls % (want * Q_TILE) == 0 else 1
    h = x_sample.reshape(ts, d)
    for i in range(depth):
        h = _ffn(h, mods, norm_g, w_in, w_out, i, 0, lat_row, tm_s)
        qt, ka, vt = _qkv(h, mods, norm_g, w_qkv_b, qg_s, kg_s, i, lat_row_q, tq_s, ls, False, rope_t)
        if i % 2 == 0:
            o = _attend_full(qt, k_ctx[i], vt_ctx[i], ka, vt, Q_TILE, KV_TILE, blocks_per_step(FULL_Q_BLOCKS))
        else:
            o = _attend_window(qt, k_ctx[i], vt_ctx[i], ka, vt, sink[i // 2], Q_TILE,
                               blocks_per_step(WINDOW_Q_BLOCKS))
        h = _ffn(h, mods, norm_g, w_in, w_out, i, 1, lat_row, tm_s, o=o.reshape(ts, d), w_o=w_o_b)
    y_sample = h.reshape(bs, ls, d)

    return (y_prompt, y_sample, new_k, new_v)
```

```python
import functools

import jax
import jax.numpy as jnp
from jax import lax
from jax.experimental import pallas as pl
from jax.experimental.pallas import tpu as pltpu

D_MODEL = 1024
N_HEADS = 16
N_KV_HEADS = 4
HEAD_DIM = 64
KV_GROUP = N_HEADS // N_KV_HEADS
GROUP_COLS = KV_GROUP * HEAD_DIM
KV_COLS = N_KV_HEADS * HEAD_DIM
QKV_COLS = (N_HEADS + 2 * N_KV_HEADS) * HEAD_DIM
D_FF = 2816
N_MOD = 9
GRID_W = 64
WINDOW = 128
ROPE_FREQS = HEAD_DIM // 4
ROPE_THETA = 10000.0
ATTN_SCALE = HEAD_DIM ** -0.5
LOG2_E = 1.4426950408889634
Q_SCALE = ATTN_SCALE * LOG2_E
EPS = 1e-6
NEG_INF = -1e30

FF_CHUNK = 256
N_FF_CHUNKS = D_FF // FF_CHUNK
FF_GROUP = 4
FFN_TILE = 1024
QKV_TILE = 1024
QKV_SUB = 512
Q_TILE = 256
FULL_Q_BLOCKS = 4
WINDOW_Q_BLOCKS = 4
KV_TILE = 512
VMEM_LIMIT_BYTES = 52 * 1024 * 1024

_BF16 = jnp.bfloat16
_F32 = jnp.float32


def _params(n_axes):
    return pltpu.CompilerParams(dimension_semantics=("arbitrary",) * n_axes,
                                vmem_limit_bytes=VMEM_LIMIT_BYTES)


def _resident(block_shape, index_map):
    return pl.BlockSpec(block_shape, index_map, pipeline_mode=pl.Buffered(1))


def _silu(x):
    return x * jax.nn.sigmoid(x)


def _modln(x, g, shift, scale):
    y = x * lax.rsqrt(jnp.mean(x * x, axis=-1, keepdims=True) + EPS)
    return y * (g * (1.0 + scale)) + shift


def _mod_body(cond_ref, w_ref, b_ref, o_ref):
    a = _silu(cond_ref[...]).astype(_BF16)
    w = w_ref[0].astype(_BF16)
    o_ref[0] = jnp.dot(a, w, preferred_element_type=_F32) + b_ref[0]


def _modulation(cond, w_mod, b_mod):
    depth, d, n = w_mod.shape
    r = cond.shape[0]
    tn = n // 8
    out = pl.pallas_call(
        _mod_body,
        grid=(depth, n // tn),
        in_specs=[
            pl.BlockSpec((r, d), lambda i, j: (0, 0)),
            pl.BlockSpec((1, d, tn), lambda i, j: (i, 0, j)),
            pl.BlockSpec((1, 1, tn), lambda i, j: (i, 0, j)),
        ],
        out_specs=pl.BlockSpec((1, r, tn), lambda i, j: (i, 0, j)),
        out_shape=jax.ShapeDtypeStruct((depth, r, n), _F32),
        compiler_params=_params(2),
        name="modulation",
    )(cond, w_mod, b_mod.reshape(depth, 1, n))
    return out.reshape(depth, r, N_MOD, d)


def _ffn_body(*refs, sub, has_proj):
    if has_proj:
        x_ref, mods_ref, ng_ref, o_ref, wo_ref, win_ref, wout_ref, out_ref, xn_ref, acc_ref = refs
    else:
        x_ref, mods_ref, ng_ref, win_ref, wout_ref, out_ref, xn_ref, acc_ref = refs
    m = mods_ref[0, 0]
    x = x_ref[...]
    if has_proj:
        x = x + m[5:6] * jnp.dot(o_ref[...], wo_ref[0], preferred_element_type=_F32)
    k = 3 * sub
    xn_ref[...] = _modln(x, ng_ref[0, sub:sub + 1], m[k:k + 1], m[k + 1:k + 2]).astype(_BF16)
    if has_proj:
        out_ref[...] = x

    def ffn_chunk(c):
        xn = xn_ref[...]
        gate = jnp.dot(xn, win_ref[0, 0, c], preferred_element_type=_F32)
        up = jnp.dot(xn, win_ref[0, 0, c + N_FF_CHUNKS], preferred_element_type=_F32)
        act = (_silu(gate) * up).astype(_BF16)
        return jnp.dot(act, wout_ref[0, 0, c], preferred_element_type=_F32)

    def ffn_chunks(lo, hi):
        total = ffn_chunk(lo)
        for c in range(lo + 1, hi):
            total = total + ffn_chunk(c)
        return total

    bounds = list(range(0, N_FF_CHUNKS, FF_GROUP)) + [N_FF_CHUNKS]
    acc_ref[...] = ffn_chunks(bounds[0], bounds[1])
    for lo, hi in zip(bounds[1:-1], bounds[2:]):
        acc_ref[...] += ffn_chunks(lo, hi)
    h = out_ref[...] if has_proj else x_ref[...]
    out_ref[...] = h + (0.5 * m[k + 2:k + 3]) * acc_ref[...]


def _ffn(x, mods, norm_g, w_in, w_out, layer, which, row_of_tile, tm, o=None, w_o=None):
    t, d = x.shape
    sub = 2 * which
    has_proj = o is not None
    in_specs = [
        pl.BlockSpec((tm, d), lambda i: (i, 0)),
        pl.BlockSpec((1, 1, N_MOD, d), lambda i: (layer, row_of_tile(i), 0, 0)),
        pl.BlockSpec((1, 3, d), lambda i: (layer, 0, 0)),
    ]
    args = [x, mods, norm_g]
    if has_proj:
        in_specs += [pl.BlockSpec((tm, d), lambda i: (i, 0)),
                     _resident((1, d, d), lambda i: (layer, 0, 0))]
        args += [o, w_o]
    in_specs += [
        _resident((1, 1, 2 * N_FF_CHUNKS, d, FF_CHUNK), lambda i: (layer, which, 0, 0, 0)),
        _resident((1, 1, N_FF_CHUNKS, FF_CHUNK, d), lambda i: (layer, which, 0, 0, 0)),
    ]
    args += [w_in, w_out]
    return pl.pallas_call(
        functools.partial(_ffn_body, sub=sub, has_proj=has_proj),
        grid=(t // tm,),
        in_specs=in_specs,
        out_specs=pl.BlockSpec((tm, d), lambda i: (i, 0)),
        out_shape=jax.ShapeDtypeStruct((t, d), _F32),
        scratch_shapes=[pltpu.VMEM((tm, d), _BF16), pltpu.VMEM((tm, d), _F32)],
        compiler_params=_params(1),
        name="ffn_proj" if has_proj else "ffn",
    )(*args)


def _head_norm_rope_t(yt, g_t, rope_ref):
    ss = jnp.sum(yt * yt, axis=0, keepdims=True)
    yn = (yt * lax.rsqrt(ss * (1.0 / HEAD_DIM) + EPS)) * g_t
    if rope_ref is None:
        return yn
    f = ROPE_FREQS
    x1r, x2r, x1c, x2c = yn[0:f], yn[f:2 * f], yn[2 * f:3 * f], yn[3 * f:4 * f]
    cr, sr, cc, sc = rope_ref[0], rope_ref[1], rope_ref[2], rope_ref[3]
    return jnp.concatenate([x1r * cr - x2r * sr, x2r * cr + x1r * sr,
                            x1c * cc - x2c * sc, x2c * cc + x1c * sc], axis=0)


def _qkv_body(*refs, rope, keep_f32, sub):
    n_in = 7 if rope else 6
    x_ref, mods_ref, ng_ref, w_ref, qg_ref, kg_ref = refs[:6]
    rope_ref = refs[6] if rope else None
    qt_ref, ka_ref, vt_ref = refs[n_in:n_in + 3]
    m = mods_ref[0, 0]
    tm = x_ref.shape[0]
    k0 = N_HEADS * HEAD_DIM
    row = lax.broadcasted_iota(jnp.int32, (ACC_ROWS - HEAD_DIM, V_PIECE), 0)
    extra = jnp.where(row == 0, 1.0, 0.0).astype(vt_ref.dtype)

    def project(r):
        rows = slice(r * sub, (r + 1) * sub)
        a = _modln(x_ref[rows, :], ng_ref[0, 1:2], m[3:4], m[4:5]).astype(_BF16)
        return jnp.dot(a, w_ref[0], preferred_element_type=_F32)

    def finish(r, proj):
        toks = slice(r * sub, (r + 1) * sub)
        qg, kg = qg_ref[0, :, toks], kg_ref[0, :, toks]
        rope_r = None if rope_ref is None else rope_ref.at[:, :, toks]
        for c in range(k0 // GROUP_COLS):
            yt = proj[:, c * GROUP_COLS:(c + 1) * GROUP_COLS].T
            for h in range(KV_GROUP):
                qn = _head_norm_rope_t(yt[h * HEAD_DIM:(h + 1) * HEAD_DIM], qg, rope_r)
                r0 = c * GROUP_COLS + h * HEAD_DIM
                qt_ref[0, r0:r0 + HEAD_DIM, toks] = (qn * Q_SCALE).astype(qt_ref.dtype)
        kt = proj[:, k0:k0 + KV_COLS].T
        kn = jnp.concatenate([_head_norm_rope_t(kt[g * HEAD_DIM:(g + 1) * HEAD_DIM], kg, rope_r)
                              for g in range(N_KV_HEADS)], axis=0)
        k_tok = kn.T
        v_tok = proj[:, k0 + KV_COLS:]
        vt = v_tok.T
        for g in range(N_KV_HEADS):
            ka_ref[0, g, toks, :] = k_tok[:, g * HEAD_DIM:(g + 1) * HEAD_DIM].astype(ka_ref.dtype)
            for j in range(sub // V_PIECE):
                piece = r * (sub // V_PIECE) + j
                vt_ref[0, g, piece, 0:HEAD_DIM, :] = (
                    vt[g * HEAD_DIM:(g + 1) * HEAD_DIM, j * V_PIECE:(j + 1) * V_PIECE].astype(vt_ref.dtype))
                vt_ref[0, g, piece, HEAD_DIM:ACC_ROWS, :] = extra
        if keep_f32:
            k_ref, v_ref = refs[n_in + 3:n_in + 5]
            k_ref[toks, :] = k_tok
            v_ref[toks, :] = v_tok

    n_sub = tm // sub
    proj = project(0)
    for r in range(n_sub):
        nxt = project(r + 1) if r + 1 < n_sub else None
        finish(r, proj)
        proj = nxt


def _qkv(x, mods, norm_g, w_qkv, qg_t, kg_t, layer, row_of_tile, tm, seq_len, keep_f32, rope_t=None):
    t, d = x.shape
    rope = rope_t is not None
    tiles_per_seq = seq_len // tm
    n_seq = t // seq_len
    seq_tile = lambda i: (i // tiles_per_seq, 0, i % tiles_per_seq)
    out_specs = [pl.BlockSpec((1, N_HEADS * HEAD_DIM, tm), seq_tile),
                 pl.BlockSpec((1, N_KV_HEADS, tm, HEAD_DIM), lambda i: seq_tile(i) + (0,)),
                 pl.BlockSpec((1, N_KV_HEADS, tm // V_PIECE, ACC_ROWS, V_PIECE), lambda i: seq_tile(i) + (0, 0))]
    out_shape = [jax.ShapeDtypeStruct((n_seq, N_HEADS * HEAD_DIM, seq_len), _BF16),
                 jax.ShapeDtypeStruct((n_seq, N_KV_HEADS, seq_len, HEAD_DIM), _BF16),
                 jax.ShapeDtypeStruct((n_seq, N_KV_HEADS, seq_len // V_PIECE, ACC_ROWS, V_PIECE), _BF16)]
    if keep_f32:
        out_specs += [pl.BlockSpec((tm, KV_COLS), lambda i: (i, 0))] * 2
        out_shape += [jax.ShapeDtypeStruct((t, KV_COLS), _F32)] * 2
    in_specs = [
        pl.BlockSpec((tm, d), lambda i: (i, 0)),
        pl.BlockSpec((1, 1, N_MOD, d), lambda i: (layer, row_of_tile(i), 0, 0)),
        pl.BlockSpec((1, 3, d), lambda i: (layer, 0, 0)),
        _resident((1, d, QKV_COLS), lambda i: (layer, 0, 0)),
        pl.BlockSpec((1, HEAD_DIM, tm), lambda i: (layer, 0, 0)),
        pl.BlockSpec((1, HEAD_DIM, tm), lambda i: (layer, 0, 0)),
    ]
    args = [x, mods, norm_g, w_qkv, qg_t, kg_t]
    if rope:
        in_specs += [pl.BlockSpec((4, ROPE_FREQS, tm), lambda i: (0, 0, i % tiles_per_seq))]
        args += [rope_t]
    return pl.pallas_call(
        functools.partial(_qkv_body, rope=rope, keep_f32=keep_f32, sub=min(tm, QKV_SUB)),
        grid=(t // tm,),
        in_specs=in_specs,
        out_specs=out_specs,
        out_shape=out_shape,
        compiler_params=_params(1),
        name="qkv_rope" if rope else "qkv",
    )(*args)


def _rope_table_t(n_tok):
    pos = jnp.arange(n_tok)
    row = (pos // GRID_W).astype(_F32)
    col = (pos % GRID_W).astype(_F32)
    freqs = 1.0 / jnp.power(ROPE_THETA, jnp.arange(ROPE_FREQS, dtype=_F32) / ROPE_FREQS)
    ang_r = row[:, None] * freqs
    ang_c = col[:, None] * freqs
    return jnp.stack([jnp.cos(ang_r).T, jnp.sin(ang_r).T, jnp.cos(ang_c).T, jnp.sin(ang_c).T])


KV_SUB = 256
V_PIECE = 128
SCORE_ROWS = 128
BF16_SUBLANES = 16
ACC_ROWS = HEAD_DIM + BF16_SUBLANES


SCORE_AHEAD = 2
VALUE_BEHIND = 1
S_SLOTS = SCORE_AHEAD + 1
P_SLOTS = VALUE_BEHIND + 1


def _attn_scratch(n_states, tq, tk):
    return [pltpu.VMEM((S_SLOTS, tk, tq), _F32),
            pltpu.VMEM((S_SLOTS, 1, tq), _F32),
            pltpu.VMEM((P_SLOTS, tk, tq), _BF16),
            pltpu.VMEM((P_SLOTS, 1, tq), _F32),
            pltpu.VMEM((n_states, 1, tq), _F32),
            pltpu.VMEM((n_states, ACC_ROWS, tq), _F32)]


def _init_states(scr, sinks=None):
    m_ref, acc_ref = scr[4], scr[5]
    tq = m_ref.shape[-1]
    if sinks is None:
        m_ref[...] = jnp.full(m_ref.shape, NEG_INF, _F32)
        acc_ref[...] = jnp.zeros(acc_ref.shape, _F32)
    else:
        for i, sk in enumerate(sinks):
            m_ref[i] = jnp.full((1, tq), sk * LOG2_E, _F32)
        row = lax.broadcasted_iota(jnp.int32, (ACC_ROWS, tq), 0)
        acc_ref[...] = jnp.broadcast_to(jnp.where(row == HEAD_DIM, 1.0, 0.0).astype(_F32), acc_ref.shape)


def _part(j):
    return slice(j * KV_SUB, (j + 1) * KV_SUB)


def _scores(scr, slot, j, item):
    s_ref, mc_ref = scr[0], scr[1]
    _, k, qt, _, bias = item
    kj = k[_part(j)]
    s = jnp.concatenate([jnp.dot(kj[r:r + SCORE_ROWS], qt, preferred_element_type=_F32)
                         for r in range(0, KV_SUB, SCORE_ROWS)], axis=0)
    if bias is not None:
        s = s + bias[_part(j)]
    s_ref[slot, _part(j)] = s
    mx = jnp.max(s, axis=0, keepdims=True)
    mc_ref[slot] = mx if j == 0 else jnp.maximum(mc_ref[slot], mx)


def _new_max(scr, s_slot, p_slot, st):
    mc_ref, al_ref, m_ref = scr[1], scr[3], scr[4]
    m_old = m_ref[st]
    m_new = jnp.maximum(m_old, mc_ref[s_slot])
    al_ref[p_slot] = jnp.exp2(m_old - m_new)
    m_ref[st] = m_new


def _softmax(scr, s_slot, p_slot, j, st):
    s_ref, p_ref, m_ref = scr[0], scr[2], scr[4]
    p_ref[p_slot, _part(j)] = jnp.exp2(s_ref[s_slot, _part(j)] - m_ref[st]).astype(_BF16)


def _values(scr, slot, j, item):
    p_ref, al_ref, acc_ref = scr[2], scr[3], scr[5]
    st, vts = item[0], item[3]
    pv = jnp.dot(vts[j], p_ref[slot, _part(j)], preferred_element_type=_F32)
    acc = acc_ref[st]
    acc_ref[st] = (al_ref[slot] * acc if j == 0 else acc) + pv


def _pipeline(scr, items):
    n = len(items)
    n_parts = len(items[0][3])
    for i0 in range(min(SCORE_AHEAD, n)):
        for j in range(n_parts):
            _scores(scr, i0 % S_SLOTS, j, items[i0])
    for i in range(n + VALUE_BEHIND):
        if i < n:
            _new_max(scr, i % S_SLOTS, i % P_SLOTS, items[i][0])
        for j in range(n_parts):
            if i + SCORE_AHEAD < n:
                _scores(scr, (i + SCORE_AHEAD) % S_SLOTS, j, items[i + SCORE_AHEAD])
            if i < n:
                _softmax(scr, i % S_SLOTS, i % P_SLOTS, j, items[i][0])
            if i >= VALUE_BEHIND:
                _values(scr, (i - VALUE_BEHIND) % P_SLOTS, j, items[i - VALUE_BEHIND])


def _finish(scr, first):
    acc_ref = scr[5]
    ot = jnp.concatenate([acc_ref[first + h, 0:HEAD_DIM] / acc_ref[first + h, HEAD_DIM:HEAD_DIM + 1]
                          for h in range(KV_GROUP)], axis=0)
    return ot.T


def _heads(qt_ref, g=None):
    base = 0 if g is None else g * GROUP_COLS
    return [qt_ref[0, base + h * HEAD_DIM: base + (h + 1) * HEAD_DIM, :] for h in range(KV_GROUP)]


def _vt_part(vt_ref, lead, piece):
    return jnp.concatenate([vt_ref[lead + (piece,)], vt_ref[lead + (piece + 1,)]], axis=1)


def _query_blocks(qt_ref, tq):
    return [[qt_ref[0, h * HEAD_DIM:(h + 1) * HEAD_DIM, b * tq:(b + 1) * tq] for h in range(KV_GROUP)]
            for b in range(qt_ref.shape[2] // tq)]


def _chunk_items(k_ref, vt_ref, blocks, tk):
    items = []
    for c in range(k_ref.shape[2] // tk):
        k = k_ref[0, 0, c * tk:(c + 1) * tk, :]
        vts = [_vt_part(vt_ref, (0, 0), (c * tk + j * KV_SUB) // V_PIECE) for j in range(tk // KV_SUB)]
        for b, heads in enumerate(blocks):
            items += [(b * KV_GROUP + h, k, heads[h], vts, None) for h in range(KV_GROUP)]
    return items


def _store_blocks(scr, o_ref, tq, n_blocks):
    for b in range(n_blocks):
        o_ref[0, b * tq:(b + 1) * tq, :] = _finish(scr, b * KV_GROUP).astype(o_ref.dtype)


def _full_body(qt_ref, kc_ref, vtc_ref, kl_ref, vtl_ref, o_ref, *scr, tq, tk):
    blocks = _query_blocks(qt_ref, tq)
    _init_states(scr)
    _pipeline(scr, _chunk_items(kc_ref, vtc_ref, blocks, tk) + _chunk_items(kl_ref, vtl_ref, blocks, tk))
    _store_blocks(scr, o_ref, tq, len(blocks))


def _kv_specs(lk, index_map):
    return [pl.BlockSpec((1, 1, lk, HEAD_DIM), lambda *i: index_map(*i) + (0, 0)),
            pl.BlockSpec((1, 1, lk // V_PIECE, ACC_ROWS, V_PIECE), lambda *i: index_map(*i) + (0, 0, 0))]


def _attend_full(qt, k_ctx, vt_ctx, k_lat, vt_lat, tq, tk, n_blocks):
    b, d, l = qt.shape
    ts = tq * n_blocks
    by_group = lambda bi, g, qi: (bi, g)
    return pl.pallas_call(
        functools.partial(_full_body, tq=tq, tk=tk),
        grid=(b, N_KV_HEADS, l // ts),
        in_specs=([pl.BlockSpec((1, GROUP_COLS, ts), lambda bi, g, qi: (bi, g, qi))]
                  + _kv_specs(k_ctx.shape[2], by_group) + _kv_specs(l, by_group)),
        out_specs=pl.BlockSpec((1, ts, GROUP_COLS), lambda bi, g, qi: (bi, qi, g)),
        out_shape=jax.ShapeDtypeStruct((b, l, d), _BF16),
        scratch_shapes=_attn_scratch(KV_GROUP * n_blocks, tq, tk),
        compiler_params=_params(3),
        name="attn_full",
    )(qt, k_ctx, vt_ctx, k_lat, vt_lat)


def _band_start(q0, l, tq):
    return jnp.clip(q0 - WINDOW, 0, l - 2 * tq)


def _window_body(sink_ref, qt_ref, kc_ref, vtc_ref, kl_ref, vtl_ref, bias_ref, o_ref, *scr, tq):
    g = pl.program_id(1)
    blocks = _query_blocks(qt_ref, tq)
    _init_states(scr, [sink_ref[g * KV_GROUP + h] for _ in blocks for h in range(KV_GROUP)])
    items = _chunk_items(kc_ref, vtc_ref, blocks, 2 * tq)
    for b, heads in enumerate(blocks):
        q0 = (pl.program_id(2) * len(blocks) + b) * tq
        start = pl.multiple_of(_band_start(q0, kl_ref.shape[2], tq), V_PIECE)
        k_band = kl_ref[0, 0, pl.ds(start, 2 * tq), :]
        vt_band = [_vt_part(vtl_ref, (0, 0), start // V_PIECE + j * (KV_SUB // V_PIECE)) for j in range(2)]
        items += [(b * KV_GROUP + h, k_band, heads[h], vt_band, bias_ref[b]) for h in range(KV_GROUP)]
    _pipeline(scr, items)
    _store_blocks(scr, o_ref, tq, len(blocks))


def _window_bias(l, tq):
    q0 = (jnp.arange(l // tq) * tq)[:, None, None]
    kpos = _band_start(q0, l, tq) + jnp.arange(2 * tq)[None, :, None]
    qpos = q0 + jnp.arange(tq)[None, None, :]
    return jnp.where(jnp.abs(qpos - kpos) <= WINDOW, 0.0, NEG_INF).astype(_F32)


def _attend_window(qt, k_ctx, vt_ctx, k_lat, vt_lat, sink, tq, n_blocks):
    b, d, l = qt.shape
    lc = k_ctx.shape[2]
    ts = tq * n_blocks
    assert tq == KV_SUB and tq >= 2 * WINDOW and lc == 2 * tq and l >= 2 * tq
    by_group = lambda bi, g, qi: (bi, g)
    return pl.pallas_call(
        functools.partial(_window_body, tq=tq),
        grid=(b, N_KV_HEADS, l // ts),
        in_specs=([pl.BlockSpec(memory_space=pltpu.SMEM),
                   pl.BlockSpec((1, GROUP_COLS, ts), lambda bi, g, qi: (bi, g, qi))]
                  + _kv_specs(lc, by_group) + _kv_specs(l, by_group)
                  + [pl.BlockSpec((n_blocks, 2 * tq, tq), lambda bi, g, qi: (qi, 0, 0))]),
        out_specs=pl.BlockSpec((1, ts, GROUP_COLS), lambda bi, g, qi: (bi, qi, g)),
        out_shape=jax.ShapeDtypeStruct((b, l, d), _BF16),
        scratch_shapes=_attn_scratch(KV_GROUP * n_blocks, tq, 2 * tq),
        compiler_params=_params(3),
        name="attn_window",
    )(sink, qt, k_ctx, vt_ctx, k_lat, vt_lat, _window_bias(l, tq))


def _ctx_body(*refs, has_sink):
    if has_sink:
        sink_ref, qt_ref, k_ref, vt_ref, o_ref = refs[:5]
    else:
        qt_ref, k_ref, vt_ref, o_ref = refs[:4]
    scr = refs[5:] if has_sink else refs[4:]
    n_sub = k_ref.shape[2] // KV_SUB
    _init_states(scr, [sink_ref[i] for i in range(N_HEADS)] if has_sink else None)
    items = []
    for g in range(N_KV_HEADS):
        heads = _heads(qt_ref, g)
        k = k_ref[0, g]
        vts = [_vt_part(vt_ref, (0, g), j * (KV_SUB // V_PIECE)) for j in range(n_sub)]
        items += [(g * KV_GROUP + h, k, heads[h], vts, None) for h in range(KV_GROUP)]
    _pipeline(scr, items)
    for g in range(N_KV_HEADS):
        o_ref[0, :, g * GROUP_COLS:(g + 1) * GROUP_COLS] = _finish(scr, g * KV_GROUP).astype(o_ref.dtype)


def _attend_ctx(qt, k, vt, sink):
    b, d, l = qt.shape
    has_sink = sink is not None
    in_specs = [
        pl.BlockSpec((1, d, l), lambda bi: (bi, 0, 0)),
        pl.BlockSpec((1, N_KV_HEADS, l, HEAD_DIM), lambda bi: (bi, 0, 0, 0)),
        pl.BlockSpec((1, N_KV_HEADS, l // V_PIECE, ACC_ROWS, V_PIECE), lambda bi: (bi, 0, 0, 0, 0)),
    ]
    args = [qt, k, vt]
    if has_sink:
        in_specs = [pl.BlockSpec(memory_space=pltpu.SMEM)] + in_specs
        args = [sink] + args
    return pl.pallas_call(
        functools.partial(_ctx_body, has_sink=has_sink),
        grid=(b,),
        in_specs=in_specs,
        out_specs=pl.BlockSpec((1, l, d), lambda bi: (bi, 0, 0)),
        out_shape=jax.ShapeDtypeStruct((b, l, d), _BF16),
        scratch_shapes=_attn_scratch(N_HEADS, l, l),
        compiler_params=_params(1),
        name="attn_ctx_sink" if has_sink else "attn_ctx",
    )(*args)


def _cache_layout(cache_k, cache_v):
    b, depth, p, g, hd = cache_v.shape
    k = cache_k.transpose(1, 0, 3, 2, 4).astype(_BF16)
    vt = cache_v.reshape(b, depth, p // V_PIECE, V_PIECE, g, hd).transpose(1, 0, 4, 2, 5, 3).astype(_BF16)
    extra = jnp.zeros((ACC_ROWS - hd, V_PIECE), _BF16).at[0].set(1.0)
    return k, jnp.concatenate([vt, jnp.broadcast_to(extra, vt.shape[:4] + extra.shape)], axis=4)


def _token_tile(t, cap):
    tm = min(cap, t)
    assert t % tm == 0
    return tm


def kernel(x_prompt, x_sample, cache_k, cache_v, c, c_ctx, w_mod, b_mod, norm_g, w_qkv, w_o,
           q_norm_g, k_norm_g, sink, w_ffn_in, w_ffn_out):
    depth = w_mod.shape[0]
    d = D_MODEL
    bp, lp, _ = x_prompt.shape
    bs, ls, _ = x_sample.shape
    past = cache_k.shape[2]
    assert lp % KV_SUB == 0 and ls % KV_TILE == 0 and past % KV_TILE == 0

    w_in = (w_ffn_in.astype(_BF16).reshape(depth, 2, d, 2 * N_FF_CHUNKS, FF_CHUNK)
            .transpose(0, 1, 3, 2, 4))
    w_out = w_ffn_out.astype(_BF16).reshape(depth, 2, N_FF_CHUNKS, FF_CHUNK, d)
    w_qkv_b = w_qkv.astype(_BF16)
    w_o_b = w_o.astype(_BF16)

    n_rows = -(-(bs + 1) // 8) * 8
    cond = jnp.concatenate([c, c_ctx[None, :], jnp.zeros((n_rows - bs - 1, d), _F32)], axis=0)
    mods = _modulation(cond, w_mod, b_mod)

    tp = bp * lp
    tm_p = _token_tile(tp, FFN_TILE)
    tq_p = _token_tile(lp, QKV_TILE)
    ctx_row = lambda i: bs
    qg_p = jnp.broadcast_to(q_norm_g[:, :, None], (depth, HEAD_DIM, tq_p))
    kg_p = jnp.broadcast_to(k_norm_g[:, :, None], (depth, HEAD_DIM, tq_p))
    h = x_prompt.reshape(tp, d)
    new_ks, new_vs = [], []
    for i in range(depth):
        h = _ffn(h, mods, norm_g, w_in, w_out, i, 0, ctx_row, tm_p)
        qt, ka, vt, k, v = _qkv(h, mods, norm_g, w_qkv_b, qg_p, kg_p, i, ctx_row, tq_p, lp, True)
        new_ks.append(k.reshape(bp, lp, N_KV_HEADS, HEAD_DIM))
        new_vs.append(v.reshape(bp, lp, N_KV_HEADS, HEAD_DIM))
        o = _attend_ctx(qt, ka, vt, sink[i // 2] if i % 2 == 1 else None)
        h = _ffn(h, mods, norm_g, w_in, w_out, i, 1, ctx_row, tm_p, o=o.reshape(tp, d), w_o=w_o_b)
    y_prompt = h.reshape(bp, lp, d)
    new_k = jnp.stack(new_ks, axis=1)
    new_v = jnp.stack(new_vs, axis=1)

    ts = bs * ls
    tm_s = _token_tile(ls, FFN_TILE)
    tq_s = _token_tile(ls, QKV_TILE)
    lat_row = lambda i: i // (ls // tm_s)
    lat_row_q = lambda i: i // (ls // tq_s)
    rope_t = _rope_table_t(ls)
    qg_s = jnp.broadcast_to(q_norm_g[:, :, None], (depth, HEAD_DIM, tq_s))
    kg_s = jnp.broadcast_to(k_norm_g[:, :, None], (depth, HEAD_DIM, tq_s))
    k_ctx, vt_ctx = _cache_layout(cache_k, cache_v)
    blocks_per_step = lambda want: want if ls % (want * Q_TILE) == 0 else 1
    h = x_sample.reshape(ts, d)
    for i in range(depth):
        h = _ffn(h, mods, norm_g, w_in, w_out, i, 0, lat_row, tm_s)
        qt, ka, vt = _qkv(h, mods, norm_g, w_qkv_b, qg_s, kg_s, i, lat_row_q, tq_s, ls, False, rope_t)
        if i % 2 == 0:
            o = _attend_full(qt, k_ctx[i], vt_ctx[i], ka, vt, Q_TILE, KV_TILE, blocks_per_step(FULL_Q_BLOCKS))
        else:
            o = _attend_window(qt, k_ctx[i], vt_ctx[i], ka, vt, sink[i // 2], Q_TILE,
                               blocks_per_step(WINDOW_Q_BLOCKS))
        h = _ffn(h, mods, norm_g, w_in, w_out, i, 1, lat_row, tm_s, o=o.reshape(ts, d), w_o=w_o_b)
    y_sample = h.reshape(bs, ls, d)

    return (y_prompt, y_sample, new_k, new_v)
```

```python
import functools

import jax
import jax.numpy as jnp
from jax import lax
from jax.experimental import pallas as pl
from jax.experimental.pallas import tpu as pltpu

D_MODEL = 1024
N_HEADS = 16
N_KV_HEADS = 4
HEAD_DIM = 64
KV_GROUP = N_HEADS // N_KV_HEADS
GROUP_COLS = KV_GROUP * HEAD_DIM
KV_COLS = N_KV_HEADS * HEAD_DIM
QKV_COLS = (N_HEADS + 2 * N_KV_HEADS) * HEAD_DIM
D_FF = 2816
N_MOD = 9
GRID_W = 64
WINDOW = 128
ROPE_FREQS = HEAD_DIM // 4
ROPE_THETA = 10000.0
ATTN_SCALE = HEAD_DIM ** -0.5
LOG2_E = 1.4426950408889634
Q_SCALE = ATTN_SCALE * LOG2_E
EPS = 1e-6
NEG_INF = -1e30

FF_CHUNK = 256
N_FF_CHUNKS = D_FF // FF_CHUNK
FFN_TILE = 1024
QKV_TILE = 1024
QKV_SUB = 512
Q_TILE = 256
FULL_Q_BLOCKS = 4
WINDOW_Q_BLOCKS = 4
KV_TILE = 512
VMEM_LIMIT_BYTES = 52 * 1024 * 1024

_BF16 = jnp.bfloat16
_F32 = jnp.float32


def _params(n_axes):
    return pltpu.CompilerParams(dimension_semantics=("arbitrary",) * n_axes,
                                vmem_limit_bytes=VMEM_LIMIT_BYTES)


def _resident(block_shape, index_map):
    return pl.BlockSpec(block_shape, index_map, pipeline_mode=pl.Buffered(1))


def _silu(x):
    return x * jax.nn.sigmoid(x)


def _modln(x, g, shift, scale):
    y = x * lax.rsqrt(jnp.mean(x * x, axis=-1, keepdims=True) + EPS)
    return y * (g * (1.0 + scale)) + shift


def _mod_body(cond_ref, w_ref, b_ref, o_ref):
    a = _silu(cond_ref[...]).astype(_BF16)
    w = w_ref[0].astype(_BF16)
    o_ref[0] = jnp.dot(a, w, preferred_element_type=_F32) + b_ref[0]


def _modulation(cond, w_mod, b_mod):
    depth, d, n = w_mod.shape
    r = cond.shape[0]
    tn = n // 8
    out = pl.pallas_call(
        _mod_body,
        grid=(depth, n // tn),
        in_specs=[
            pl.BlockSpec((r, d), lambda i, j: (0, 0)),
            pl.BlockSpec((1, d, tn), lambda i, j: (i, 0, j)),
            pl.BlockSpec((1, 1, tn), lambda i, j: (i, 0, j)),
        ],
        out_specs=pl.BlockSpec((1, r, tn), lambda i, j: (i, 0, j)),
        out_shape=jax.ShapeDtypeStruct((depth, r, n), _F32),
        compiler_params=_params(2),
        name="modulation",
    )(cond, w_mod, b_mod.reshape(depth, 1, n))
    return out.reshape(depth, r, N_MOD, d)


def _ffn_body(*refs, sub, has_proj):
    if has_proj:
        x_ref, mods_ref, ng_ref, o_ref, wo_ref, win_ref, wout_ref, out_ref, xn_ref, acc_ref = refs
    else:
        x_ref, mods_ref, ng_ref, win_ref, wout_ref, out_ref, xn_ref, acc_ref = refs
    m = mods_ref[0, 0]
    x = x_ref[...]
    if has_proj:
        x = x + m[5:6] * jnp.dot(o_ref[...], wo_ref[0], preferred_element_type=_F32)
    k = 3 * sub
    xn_ref[...] = _modln(x, ng_ref[0, sub:sub + 1], m[k:k + 1], m[k + 1:k + 2]).astype(_BF16)
    if has_proj:
        out_ref[...] = x

    def ffn_chunk(c):
        xn = xn_ref[...]
        gate = jnp.dot(xn, win_ref[0, 0, c], preferred_element_type=_F32)
        up = jnp.dot(xn, win_ref[0, 0, c + N_FF_CHUNKS], preferred_element_type=_F32)
        act = (_silu(gate) * up).astype(_BF16)
        return jnp.dot(act, wout_ref[0, 0, c], preferred_element_type=_F32)

    acc_ref[...] = ffn_chunk(0)

    assert N_FF_CHUNKS % 2 == 1
    for c in range(1, N_FF_CHUNKS, 2):
        acc_ref[...] += ffn_chunk(c) + ffn_chunk(c + 1)
    h = out_ref[...] if has_proj else x_ref[...]
    out_ref[...] = h + (0.5 * m[k + 2:k + 3]) * acc_ref[...]


def _ffn(x, mods, norm_g, w_in, w_out, layer, which, row_of_tile, tm, o=None, w_o=None):
    t, d = x.shape
    sub = 2 * which
    has_proj = o is not None
    in_specs = [
        pl.BlockSpec((tm, d), lambda i: (i, 0)),
        pl.BlockSpec((1, 1, N_MOD, d), lambda i: (layer, row_of_tile(i), 0, 0)),
        pl.BlockSpec((1, 3, d), lambda i: (layer, 0, 0)),
    ]
    args = [x, mods, norm_g]
    if has_proj:
        in_specs += [pl.BlockSpec((tm, d), lambda i: (i, 0)),
                     _resident((1, d, d), lambda i: (layer, 0, 0))]
        args += [o, w_o]
    in_specs += [
        _resident((1, 1, 2 * N_FF_CHUNKS, d, FF_CHUNK), lambda i: (layer, which, 0, 0, 0)),
        _resident((1, 1, N_FF_CHUNKS, FF_CHUNK, d), lambda i: (layer, which, 0, 0, 0)),
    ]
    args += [w_in, w_out]
    return pl.pallas_call(
        functools.partial(_ffn_body, sub=sub, has_proj=has_proj),
        grid=(t // tm,),
        in_specs=in_specs,
        out_specs=pl.BlockSpec((tm, d), lambda i: (i, 0)),
        out_shape=jax.ShapeDtypeStruct((t, d), _F32),
        scratch_shapes=[pltpu.VMEM((tm, d), _BF16), pltpu.VMEM((tm, d), _F32)],
        compiler_params=_params(1),
        name="ffn_proj" if has_proj else "ffn",
    )(*args)


def _head_norm_rope_t(yt, g_t, rope_ref):
    ss = jnp.sum(yt * yt, axis=0, keepdims=True)
    yn = (yt * lax.rsqrt(ss * (1.0 / HEAD_DIM) + EPS)) * g_t
    if rope_ref is None:
        return yn
    f = ROPE_FREQS
    x1r, x2r, x1c, x2c = yn[0:f], yn[f:2 * f], yn[2 * f:3 * f], yn[3 * f:4 * f]
    cr, sr, cc, sc = rope_ref[0], rope_ref[1], rope_ref[2], rope_ref[3]
    return jnp.concatenate([x1r * cr - x2r * sr, x2r * cr + x1r * sr,
                            x1c * cc - x2c * sc, x2c * cc + x1c * sc], axis=0)


def _qkv_body(*refs, rope, keep_f32, sub):
    n_in = 7 if rope else 6
    x_ref, mods_ref, ng_ref, w_ref, qg_ref, kg_ref = refs[:6]
    rope_ref = refs[6] if rope else None
    qt_ref, ka_ref, vt_ref = refs[n_in:n_in + 3]
    m = mods_ref[0, 0]
    tm = x_ref.shape[0]
    k0 = N_HEADS * HEAD_DIM
    row = lax.broadcasted_iota(jnp.int32, (ACC_ROWS - HEAD_DIM, V_PIECE), 0)
    extra = jnp.where(row == 0, 1.0, 0.0).astype(vt_ref.dtype)

    def project(r):
        rows = slice(r * sub, (r + 1) * sub)
        a = _modln(x_ref[rows, :], ng_ref[0, 1:2], m[3:4], m[4:5]).astype(_BF16)
        return jnp.dot(a, w_ref[0], preferred_element_type=_F32)

    def finish(r, proj):
        toks = slice(r * sub, (r + 1) * sub)
        qg, kg = qg_ref[0, :, toks], kg_ref[0, :, toks]
        rope_r = None if rope_ref is None else rope_ref.at[:, :, toks]
        for c in range(k0 // GROUP_COLS):
            yt = proj[:, c * GROUP_COLS:(c + 1) * GROUP_COLS].T
            for h in range(KV_GROUP):
                qn = _head_norm_rope_t(yt[h * HEAD_DIM:(h + 1) * HEAD_DIM], qg, rope_r)
                r0 = c * GROUP_COLS + h * HEAD_DIM
                qt_ref[0, r0:r0 + HEAD_DIM, toks] = (qn * Q_SCALE).astype(qt_ref.dtype)
        kt = proj[:, k0:k0 + KV_COLS].T
        kn = jnp.concatenate([_head_norm_rope_t(kt[g * HEAD_DIM:(g + 1) * HEAD_DIM], kg, rope_r)
                              for g in range(N_KV_HEADS)], axis=0)
        k_tok = kn.T
        v_tok = proj[:, k0 + KV_COLS:]
        vt = v_tok.T
        for g in range(N_KV_HEADS):
            ka_ref[0, g, toks, :] = k_tok[:, g * HEAD_DIM:(g + 1) * HEAD_DIM].astype(ka_ref.dtype)
            for j in range(sub // V_PIECE):
                piece = r * (sub // V_PIECE) + j
                vt_ref[0, g, piece, 0:HEAD_DIM, :] = (
                    vt[g * HEAD_DIM:(g + 1) * HEAD_DIM, j * V_PIECE:(j + 1) * V_PIECE].astype(vt_ref.dtype))
                vt_ref[0, g, piece, HEAD_DIM:ACC_ROWS, :] = extra
        if keep_f32:
            k_ref, v_ref = refs[n_in + 3:n_in + 5]
            k_ref[toks, :] = k_tok
            v_ref[toks, :] = v_tok

    n_sub = tm // sub
    proj = project(0)
    for r in range(n_sub):
        nxt = project(r + 1) if r + 1 < n_sub else None
        finish(r, proj)
        proj = nxt


def _qkv(x, mods, norm_g, w_qkv, qg_t, kg_t, layer, row_of_tile, tm, seq_len, keep_f32, rope_t=None):
    t, d = x.shape
    rope = rope_t is not None
    tiles_per_seq = seq_len // tm
    n_seq = t // seq_len
    seq_tile = lambda i: (i // tiles_per_seq, 0, i % tiles_per_seq)
    out_specs = [pl.BlockSpec((1, N_HEADS * HEAD_DIM, tm), seq_tile),
                 pl.BlockSpec((1, N_KV_HEADS, tm, HEAD_DIM), lambda i: seq_tile(i) + (0,)),
                 pl.BlockSpec((1, N_KV_HEADS, tm // V_PIECE, ACC_ROWS, V_PIECE), lambda i: seq_tile(i) + (0, 0))]
    out_shape = [jax.ShapeDtypeStruct((n_seq, N_HEADS * HEAD_DIM, seq_len), _BF16),
                 jax.ShapeDtypeStruct((n_seq, N_KV_HEADS, seq_len, HEAD_DIM), _BF16),
                 jax.ShapeDtypeStruct((n_seq, N_KV_HEADS, seq_len // V_PIECE, ACC_ROWS, V_PIECE), _BF16)]
    if keep_f32:
        out_specs += [pl.BlockSpec((tm, KV_COLS), lambda i: (i, 0))] * 2
        out_shape += [jax.ShapeDtypeStruct((t, KV_COLS), _F32)] * 2
    in_specs = [
        pl.BlockSpec((tm, d), lambda i: (i, 0)),
        pl.BlockSpec((1, 1, N_MOD, d), lambda i: (layer, row_of_tile(i), 0, 0)),
        pl.BlockSpec((1, 3, d), lambda i: (layer, 0, 0)),
        _resident((1, d, QKV_COLS), lambda i: (layer, 0, 0)),
        pl.BlockSpec((1, HEAD_DIM, tm), lambda i: (layer, 0, 0)),
        pl.BlockSpec((1, HEAD_DIM, tm), lambda i: (layer, 0, 0)),
    ]
    args = [x, mods, norm_g, w_qkv, qg_t, kg_t]
    if rope:
        in_specs += [pl.BlockSpec((4, ROPE_FREQS, tm), lambda i: (0, 0, i % tiles_per_seq))]
        args += [rope_t]
    return pl.pallas_call(
        functools.partial(_qkv_body, rope=rope, keep_f32=keep_f32, sub=min(tm, QKV_SUB)),
        grid=(t // tm,),
        in_specs=in_specs,
        out_specs=out_specs,
        out_shape=out_shape,
        compiler_params=_params(1),
        name="qkv_rope" if rope else "qkv",
    )(*args)


def _rope_table_t(n_tok):
    pos = jnp.arange(n_tok)
    row = (pos // GRID_W).astype(_F32)
    col = (pos % GRID_W).astype(_F32)
    freqs = 1.0 / jnp.power(ROPE_THETA, jnp.arange(ROPE_FREQS, dtype=_F32) / ROPE_FREQS)
    ang_r = row[:, None] * freqs
    ang_c = col[:, None] * freqs
    return jnp.stack([jnp.cos(ang_r).T, jnp.sin(ang_r).T, jnp.cos(ang_c).T, jnp.sin(ang_c).T])


KV_SUB = 256
V_PIECE = 128
SCORE_ROWS = 128
BF16_SUBLANES = 16
ACC_ROWS = HEAD_DIM + BF16_SUBLANES


SCORE_AHEAD = 2
VALUE_BEHIND = 1
S_SLOTS = SCORE_AHEAD + 1
P_SLOTS = VALUE_BEHIND + 1


def _attn_scratch(n_states, tq, tk):
    return [pltpu.VMEM((S_SLOTS, tk, tq), _F32),
            pltpu.VMEM((S_SLOTS, 1, tq), _F32),
            pltpu.VMEM((P_SLOTS, tk, tq), _BF16),
            pltpu.VMEM((P_SLOTS, 1, tq), _F32),
            pltpu.VMEM((n_states, 1, tq), _F32),
            pltpu.VMEM((n_states, ACC_ROWS, tq), _F32)]


def _init_states(scr, sinks=None):
    m_ref, acc_ref = scr[4], scr[5]
    tq = m_ref.shape[-1]
    if sinks is None:
        m_ref[...] = jnp.full(m_ref.shape, NEG_INF, _F32)
        acc_ref[...] = jnp.zeros(acc_ref.shape, _F32)
    else:
        for i, sk in enumerate(sinks):
            m_ref[i] = jnp.full((1, tq), sk * LOG2_E, _F32)
        row = lax.broadcasted_iota(jnp.int32, (ACC_ROWS, tq), 0)
        acc_ref[...] = jnp.broadcast_to(jnp.where(row == HEAD_DIM, 1.0, 0.0).astype(_F32), acc_ref.shape)


def _part(j, item):
    part = item[3][0].shape[1]
    return slice(j * part, (j + 1) * part)


def _scores(scr, slot, j, item):
    s_ref, mc_ref = scr[0], scr[1]
    _, k, qt, _, bias = item
    kj = k[_part(j, item)]
    s = jnp.concatenate([jnp.dot(kj[r:r + SCORE_ROWS], qt, preferred_element_type=_F32)
                         for r in range(0, kj.shape[0], SCORE_ROWS)], axis=0)
    if bias is not None:
        s = s + bias[_part(j, item)]
    s_ref[slot, _part(j, item)] = s
    mx = jnp.max(s, axis=0, keepdims=True)
    mc_ref[slot] = mx if j == 0 else jnp.maximum(mc_ref[slot], mx)


def _new_max(scr, s_slot, p_slot, st):
    mc_ref, al_ref, m_ref = scr[1], scr[3], scr[4]
    m_old = m_ref[st]
    m_new = jnp.maximum(m_old, mc_ref[s_slot])
    al_ref[p_slot] = jnp.exp2(m_old - m_new)
    m_ref[st] = m_new


def _softmax(scr, s_slot, p_slot, j, item):
    s_ref, p_ref, m_ref = scr[0], scr[2], scr[4]
    rows = _part(j, item)
    p_ref[p_slot, rows] = jnp.exp2(s_ref[s_slot, rows] - m_ref[item[0]]).astype(_BF16)


def _values(scr, slot, j, item):
    p_ref, al_ref, acc_ref = scr[2], scr[3], scr[5]
    st, vts = item[0], item[3]
    pv = jnp.dot(vts[j], p_ref[slot, _part(j, item)], preferred_element_type=_F32)
    acc = acc_ref[st]
    acc_ref[st] = (al_ref[slot] * acc if j == 0 else acc) + pv


def _pipeline(scr, items):
    n = len(items)
    n_parts = len(items[0][3])
    for i0 in range(min(SCORE_AHEAD, n)):
        for j in range(n_parts):
            _scores(scr, i0 % S_SLOTS, j, items[i0])
    for i in range(n + VALUE_BEHIND):
        if i < n:
            _new_max(scr, i % S_SLOTS, i % P_SLOTS, items[i][0])
        for j in range(n_parts):
            if i + SCORE_AHEAD < n:
                _scores(scr, (i + SCORE_AHEAD) % S_SLOTS, j, items[i + SCORE_AHEAD])
            if i < n:
                _softmax(scr, i % S_SLOTS, i % P_SLOTS, j, items[i])
            if i >= VALUE_BEHIND:
                _values(scr, (i - VALUE_BEHIND) % P_SLOTS, j, items[i - VALUE_BEHIND])


def _finish(scr, first):
    acc_ref = scr[5]
    ot = jnp.concatenate([acc_ref[first + h, 0:HEAD_DIM] / acc_ref[first + h, HEAD_DIM:HEAD_DIM + 1]
                          for h in range(KV_GROUP)], axis=0)
    return ot.T


def _heads(qt_ref, g=None):
    base = 0 if g is None else g * GROUP_COLS
    return [qt_ref[0, base + h * HEAD_DIM: base + (h + 1) * HEAD_DIM, :] for h in range(KV_GROUP)]


def _vt_part(vt_ref, lead, piece, part=KV_SUB):
    return jnp.concatenate([vt_ref[lead + (piece + i,)] for i in range(part // V_PIECE)], axis=1)


def _query_blocks(qt_ref, tq):
    return [[qt_ref[0, h * HEAD_DIM:(h + 1) * HEAD_DIM, b * tq:(b + 1) * tq] for h in range(KV_GROUP)]
            for b in range(qt_ref.shape[2] // tq)]


def _chunk_items(k_ref, vt_ref, blocks, tk, part=KV_SUB):
    items = []
    for c in range(k_ref.shape[2] // tk):
        k = k_ref[0, 0, c * tk:(c + 1) * tk, :]
        vts = [_vt_part(vt_ref, (0, 0), (c * tk + j * part) // V_PIECE, part) for j in range(tk // part)]
        for b, heads in enumerate(blocks):
            items += [(b * KV_GROUP + h, k, heads[h], vts, None) for h in range(KV_GROUP)]
    return items


def _store_blocks(scr, o_ref, tq, n_blocks):
    for b in range(n_blocks):
        o_ref[0, b * tq:(b + 1) * tq, :] = _finish(scr, b * KV_GROUP).astype(o_ref.dtype)


def _full_body(qt_ref, kc_ref, vtc_ref, kl_ref, vtl_ref, o_ref, *scr, tq, tk):
    blocks = _query_blocks(qt_ref, tq)
    _init_states(scr)
    _pipeline(scr, _chunk_items(kc_ref, vtc_ref, blocks, tk, tk) + _chunk_items(kl_ref, vtl_ref, blocks, tk, tk))
    _store_blocks(scr, o_ref, tq, len(blocks))


def _kv_specs(lk, index_map):
    return [pl.BlockSpec((1, 1, lk, HEAD_DIM), lambda *i: index_map(*i) + (0, 0)),
            pl.BlockSpec((1, 1, lk // V_PIECE, ACC_ROWS, V_PIECE), lambda *i: index_map(*i) + (0, 0, 0))]


def _attend_full(qt, k_ctx, vt_ctx, k_lat, vt_lat, tq, tk, n_blocks):
    b, d, l = qt.shape
    ts = tq * n_blocks
    by_group = lambda bi, g, qi: (bi, g)
    return pl.pallas_call(
        functools.partial(_full_body, tq=tq, tk=tk),
        grid=(b, N_KV_HEADS, l // ts),
        in_specs=([pl.BlockSpec((1, GROUP_COLS, ts), lambda bi, g, qi: (bi, g, qi))]
                  + _kv_specs(k_ctx.shape[2], by_group) + _kv_specs(l, by_group)),
        out_specs=pl.BlockSpec((1, ts, GROUP_COLS), lambda bi, g, qi: (bi, qi, g)),
        out_shape=jax.ShapeDtypeStruct((b, l, d), _BF16),
        scratch_shapes=_attn_scratch(KV_GROUP * n_blocks, tq, tk),
        compiler_params=_params(3),
        name="attn_full",
    )(qt, k_ctx, vt_ctx, k_lat, vt_lat)


def _band_start(q0, l, tq):
    return jnp.clip(q0 - WINDOW, 0, l - 2 * tq)


def _window_body(sink_ref, qt_ref, kc_ref, vtc_ref, kl_ref, vtl_ref, bias_ref, o_ref, *scr, tq):
    g = pl.program_id(1)
    blocks = _query_blocks(qt_ref, tq)
    _init_states(scr, [sink_ref[g * KV_GROUP + h] for _ in blocks for h in range(KV_GROUP)])
    items = _chunk_items(kc_ref, vtc_ref, blocks, 2 * tq)
    for b, heads in enumerate(blocks):
        q0 = (pl.program_id(2) * len(blocks) + b) * tq
        start = pl.multiple_of(_band_start(q0, kl_ref.shape[2], tq), V_PIECE)
        k_band = kl_ref[0, 0, pl.ds(start, 2 * tq), :]
        vt_band = [_vt_part(vtl_ref, (0, 0), start // V_PIECE + j * (KV_SUB // V_PIECE)) for j in range(2)]
        items += [(b * KV_GROUP + h, k_band, heads[h], vt_band, bias_ref[b]) for h in range(KV_GROUP)]
    _pipeline(scr, items)
    _store_blocks(scr, o_ref, tq, len(blocks))


def _window_bias(l, tq):
    q0 = (jnp.arange(l // tq) * tq)[:, None, None]
    kpos = _band_start(q0, l, tq) + jnp.arange(2 * tq)[None, :, None]
    qpos = q0 + jnp.arange(tq)[None, None, :]
    return jnp.where(jnp.abs(qpos - kpos) <= WINDOW, 0.0, NEG_INF).astype(_F32)


def _attend_window(qt, k_ctx, vt_ctx, k_lat, vt_lat, sink, tq, n_blocks):
    b, d, l = qt.shape
    lc = k_ctx.shape[2]
    ts = tq * n_blocks
    assert tq == KV_SUB and tq >= 2 * WINDOW and lc == 2 * tq and l >= 2 * tq
    by_group = lambda bi, g, qi: (bi, g)
    return pl.pallas_call(
        functools.partial(_window_body, tq=tq),
        grid=(b, N_KV_HEADS, l // ts),
        in_specs=([pl.BlockSpec(memory_space=pltpu.SMEM),
                   pl.BlockSpec((1, GROUP_COLS, ts), lambda bi, g, qi: (bi, g, qi))]
                  + _kv_specs(lc, by_group) + _kv_specs(l, by_group)
                  + [pl.BlockSpec((n_blocks, 2 * tq, tq), lambda bi, g, qi: (qi, 0, 0))]),
        out_specs=pl.BlockSpec((1, ts, GROUP_COLS), lambda bi, g, qi: (bi, qi, g)),
        out_shape=jax.ShapeDtypeStruct((b, l, d), _BF16),
        scratch_shapes=_attn_scratch(KV_GROUP * n_blocks, tq, 2 * tq),
        compiler_params=_params(3),
        name="attn_window",
    )(sink, qt, k_ctx, vt_ctx, k_lat, vt_lat, _window_bias(l, tq))


def _ctx_body(*refs, has_sink):
    if has_sink:
        sink_ref, qt_ref, k_ref, vt_ref, o_ref = refs[:5]
    else:
        qt_ref, k_ref, vt_ref, o_ref = refs[:4]
    scr = refs[5:] if has_sink else refs[4:]
    n_sub = k_ref.shape[2] // KV_SUB
    _init_states(scr, [sink_ref[i] for i in range(N_HEADS)] if has_sink else None)
    items = []
    for g in range(N_KV_HEADS):
        heads = _heads(qt_ref, g)
        k = k_ref[0, g]
        vts = [_vt_part(vt_ref, (0, g), j * (KV_SUB // V_PIECE)) for j in range(n_sub)]
        items += [(g * KV_GROUP + h, k, heads[h], vts, None) for h in range(KV_GROUP)]
    _pipeline(scr, items)
    for g in range(N_KV_HEADS):
        o_ref[0, :, g * GROUP_COLS:(g + 1) * GROUP_COLS] = _finish(scr, g * KV_GROUP).astype(o_ref.dtype)


def _attend_ctx(qt, k, vt, sink):
    b, d, l = qt.shape
    has_sink = sink is not None
    in_specs = [
        pl.BlockSpec((1, d, l), lambda bi: (bi, 0, 0)),
        pl.BlockSpec((1, N_KV_HEADS, l, HEAD_DIM), lambda bi: (bi, 0, 0, 0)),
        pl.BlockSpec((1, N_KV_HEADS, l // V_PIECE, ACC_ROWS, V_PIECE), lambda bi: (bi, 0, 0, 0, 0)),
    ]
    args = [qt, k, vt]
    if has_sink:
        in_specs = [pl.BlockSpec(memory_space=pltpu.SMEM)] + in_specs
        args = [sink] + args
    return pl.pallas_call(
        functools.partial(_ctx_body, has_sink=has_sink),
        grid=(b,),
        in_specs=in_specs,
        out_specs=pl.BlockSpec((1, l, d), lambda bi: (bi, 0, 0)),
        out_shape=jax.ShapeDtypeStruct((b, l, d), _BF16),
        scratch_shapes=_attn_scratch(N_HEADS, l, l),
        compiler_params=_params(1),
        name="attn_ctx_sink" if has_sink else "attn_ctx",
    )(*args)


def _cache_layout(cache_k, cache_v):
    b, depth, p, g, hd = cache_v.shape
    k = cache_k.transpose(1, 0, 3, 2, 4).astype(_BF16)
    vt = cache_v.reshape(b, depth, p // V_PIECE, V_PIECE, g, hd).transpose(1, 0, 4, 2, 5, 3).astype(_BF16)
    extra = jnp.zeros((ACC_ROWS - hd, V_PIECE), _BF16).at[0].set(1.0)
    return k, jnp.concatenate([vt, jnp.broadcast_to(extra, vt.shape[:4] + extra.shape)], axis=4)


def _token_tile(t, cap):
    tm = min(cap, t)
    assert t % tm == 0
    return tm


def kernel(x_prompt, x_sample, cache_k, cache_v, c, c_ctx, w_mod, b_mod, norm_g, w_qkv, w_o,
           q_norm_g, k_norm_g, sink, w_ffn_in, w_ffn_out):
    depth = w_mod.shape[0]
    d = D_MODEL
    bp, lp, _ = x_prompt.shape
    bs, ls, _ = x_sample.shape
    past = cache_k.shape[2]
    assert lp % KV_SUB == 0 and ls % KV_TILE == 0 and past % KV_TILE == 0

    w_in = (w_ffn_in.astype(_BF16).reshape(depth, 2, d, 2 * N_FF_CHUNKS, FF_CHUNK)
            .transpose(0, 1, 3, 2, 4))
    w_out = w_ffn_out.astype(_BF16).reshape(depth, 2, N_FF_CHUNKS, FF_CHUNK, d)
    w_qkv_b = w_qkv.astype(_BF16)
    w_o_b = w_o.astype(_BF16)

    n_rows = -(-(bs + 1) // 8) * 8
    cond = jnp.concatenate([c, c_ctx[None, :], jnp.zeros((n_rows - bs - 1, d), _F32)], axis=0)
    mods = _modulation(cond, w_mod, b_mod)

    tp = bp * lp
    tm_p = _token_tile(tp, FFN_TILE)
    tq_p = _token_tile(lp, QKV_TILE)
    ctx_row = lambda i: bs
    qg_p = jnp.broadcast_to(q_norm_g[:, :, None], (depth, HEAD_DIM, tq_p))
    kg_p = jnp.broadcast_to(k_norm_g[:, :, None], (depth, HEAD_DIM, tq_p))
    h = x_prompt.reshape(tp, d)
    new_ks, new_vs = [], []
    for i in range(depth):
        h = _ffn(h, mods, norm_g, w_in, w_out, i, 0, ctx_row, tm_p)
        qt, ka, vt, k, v = _qkv(h, mods, norm_g, w_qkv_b, qg_p, kg_p, i, ctx_row, tq_p, lp, True)
        new_ks.append(k.reshape(bp, lp, N_KV_HEADS, HEAD_DIM))
        new_vs.append(v.reshape(bp, lp, N_KV_HEADS, HEAD_DIM))
        o = _attend_ctx(qt, ka, vt, sink[i // 2] if i % 2 == 1 else None)
        h = _ffn(h, mods, norm_g, w_in, w_out, i, 1, ctx_row, tm_p, o=o.reshape(tp, d), w_o=w_o_b)
    y_prompt = h.reshape(bp, lp, d)
    new_k = jnp.stack(new_ks, axis=1)
    new_v = jnp.stack(new_vs, axis=1)

    ts = bs * ls
    tm_s = _token_tile(ls, FFN_TILE)
    tq_s = _token_tile(ls, QKV_TILE)
    lat_row = lambda i: i // (ls // tm_s)
    lat_row_q = lambda i: i // (ls // tq_s)
    rope_t = _rope_table_t(ls)
    qg_s = jnp.broadcast_to(q_norm_g[:, :, None], (depth, HEAD_DIM, tq_s))
    kg_s = jnp.broadcast_to(k_norm_g[:, :, None], (depth, HEAD_DIM, tq_s))
    k_ctx, vt_ctx = _cache_layout(cache_k, cache_v)
    blocks_per_step = lambda want: want if ls % (want * Q_TILE) == 0 else 1
    h = x_sample.reshape(ts, d)
    for i in range(depth):
        h = _ffn(h, mods, norm_g, w_in, w_out, i, 0, lat_row, tm_s)
        qt, ka, vt = _qkv(h, mods, norm_g, w_qkv_b, qg_s, kg_s, i, lat_row_q, tq_s, ls, False, rope_t)
        if i % 2 == 0:
            o = _attend_full(qt, k_ctx[i], vt_ctx[i], ka, vt, Q_TILE, KV_TILE, blocks_per_step(FULL_Q_BLOCKS))
        else:
            o = _attend_window(qt, k_ctx[i], vt_ctx[i], ka, vt, sink[i // 2], Q_TILE,
                               blocks_per_step(WINDOW_Q_BLOCKS))
        h = _ffn(h, mods, norm_g, w_in, w_out, i, 1, lat_row, tm_s, o=o.reshape(ts, d), w_o=w_o_b)
    y_sample = h.reshape(bs, ls, d)

    return (y_prompt, y_sample, new_k, new_v)
```

```python
import functools

import jax
import jax.numpy as jnp
from jax import lax
from jax.experimental import pallas as pl
from jax.experimental.pallas import tpu as pltpu

D_MODEL = 1024
N_HEADS = 16
N_KV_HEADS = 4
HEAD_DIM = 64
KV_GROUP = N_HEADS // N_KV_HEADS
GROUP_COLS = KV_GROUP * HEAD_DIM
KV_COLS = N_KV_HEADS * HEAD_DIM
QKV_COLS = (N_HEADS + 2 * N_KV_HEADS) * HEAD_DIM
D_FF = 2816
N_MOD = 9
GRID_W = 64
WINDOW = 128
ROPE_FREQS = HEAD_DIM // 4
ROPE_THETA = 10000.0
ATTN_SCALE = HEAD_DIM ** -0.5
LOG2_E = 1.4426950408889634
Q_SCALE = ATTN_SCALE * LOG2_E
EPS = 1e-6
NEG_INF = -1e30

FF_CHUNK = 256
N_FF_CHUNKS = D_FF // FF_CHUNK
FFN_TILE = 1024
QKV_TILE = 1024
QKV_SUB = 512
Q_TILE = 256
FULL_Q_BLOCKS = 4
WINDOW_Q_BLOCKS = 4
KV_TILE = 512
VMEM_LIMIT_BYTES = 52 * 1024 * 1024

_BF16 = jnp.bfloat16
_F32 = jnp.float32


def _params(n_axes):
    return pltpu.CompilerParams(dimension_semantics=("arbitrary",) * n_axes,
                                vmem_limit_bytes=VMEM_LIMIT_BYTES)


def _resident(block_shape, index_map):
    return pl.BlockSpec(block_shape, index_map, pipeline_mode=pl.Buffered(1))


def _silu(x):
    return x * jax.nn.sigmoid(x)


def _modln(x, g, shift, scale):
    y = x * lax.rsqrt(jnp.mean(x * x, axis=-1, keepdims=True) + EPS)
    return y * (g * (1.0 + scale)) + shift


def _mod_body(cond_ref, w_ref, b_ref, o_ref):
    a = _silu(cond_ref[...]).astype(_BF16)
    w = w_ref[0].astype(_BF16)
    o_ref[0] = jnp.dot(a, w, preferred_element_type=_F32) + b_ref[0]


def _modulation(cond, w_mod, b_mod):
    depth, d, n = w_mod.shape
    r = cond.shape[0]
    tn = n // 8
    out = pl.pallas_call(
        _mod_body,
        grid=(depth, n // tn),
        in_specs=[
            pl.BlockSpec((r, d), lambda i, j: (0, 0)),
            pl.BlockSpec((1, d, tn), lambda i, j: (i, 0, j)),
            pl.BlockSpec((1, 1, tn), lambda i, j: (i, 0, j)),
        ],
        out_specs=pl.BlockSpec((1, r, tn), lambda i, j: (i, 0, j)),
        out_shape=jax.ShapeDtypeStruct((depth, r, n), _F32),
        compiler_params=_params(2),
        name="modulation",
    )(cond, w_mod, b_mod.reshape(depth, 1, n))
    return out.reshape(depth, r, N_MOD, d)


def _ffn_body(*refs, sub, has_proj):
    if has_proj:
        x_ref, mods_ref, ng_ref, o_ref, wo_ref, win_ref, wout_ref, out_ref, xn_ref, acc_ref = refs
    else:
        x_ref, mods_ref, ng_ref, win_ref, wout_ref, out_ref, xn_ref, acc_ref = refs
    m = mods_ref[0, 0]
    x = x_ref[...]
    if has_proj:
        x = x + m[5:6] * jnp.dot(o_ref[...], wo_ref[0], preferred_element_type=_F32)
    k = 3 * sub
    xn_ref[...] = _modln(x, ng_ref[0, sub:sub + 1], m[k:k + 1], m[k + 1:k + 2]).astype(_BF16)
    if has_proj:
        out_ref[...] = x

    def ffn_chunk(c):
        xn = xn_ref[...]
        gate = jnp.dot(xn, win_ref[0, 0, c], preferred_element_type=_F32)
        up = jnp.dot(xn, win_ref[0, 0, c + N_FF_CHUNKS], preferred_element_type=_F32)
        act = (_silu(gate) * up).astype(_BF16)
        return jnp.dot(act, wout_ref[0, 0, c], preferred_element_type=_F32)

    acc_ref[...] = ffn_chunk(0)

    assert N_FF_CHUNKS % 2 == 1
    for c in range(1, N_FF_CHUNKS, 2):
        acc_ref[...] += ffn_chunk(c) + ffn_chunk(c + 1)
    h = out_ref[...] if has_proj else x_ref[...]
    out_ref[...] = h + (0.5 * m[k + 2:k + 3]) * acc_ref[...]


def _ffn(x, mods, norm_g, w_in, w_out, layer, which, row_of_tile, tm, o=None, w_o=None):
    t, d = x.shape
    sub = 2 * which
    has_proj = o is not None
    in_specs = [
        pl.BlockSpec((tm, d), lambda i: (i, 0)),
        pl.BlockSpec((1, 1, N_MOD, d), lambda i: (layer, row_of_tile(i), 0, 0)),
        pl.BlockSpec((1, 3, d), lambda i: (layer, 0, 0)),
    ]
    args = [x, mods, norm_g]
    if has_proj:
        in_specs += [pl.BlockSpec((tm, d), lambda i: (i, 0)),
                     _resident((1, d, d), lambda i: (layer, 0, 0))]
        args += [o, w_o]
    in_specs += [
        _resident((1, 1, 2 * N_FF_CHUNKS, d, FF_CHUNK), lambda i: (layer, which, 0, 0, 0)),
        _resident((1, 1, N_FF_CHUNKS, FF_CHUNK, d), lambda i: (layer, which, 0, 0, 0)),
    ]
    args += [w_in, w_out]
    return pl.pallas_call(
        functools.partial(_ffn_body, sub=sub, has_proj=has_proj),
        grid=(t // tm,),
        in_specs=in_specs,
        out_specs=pl.BlockSpec((tm, d), lambda i: (i, 0)),
        out_shape=jax.ShapeDtypeStruct((t, d), _F32),
        scratch_shapes=[pltpu.VMEM((tm, d), _BF16), pltpu.VMEM((tm, d), _F32)],
        compiler_params=_params(1),
        name="ffn_proj" if has_proj else "ffn",
    )(*args)


def _head_norm_rope_t(yt, g_t, rope_ref):
    ss = jnp.sum(yt * yt, axis=0, keepdims=True)
    yn = (yt * lax.rsqrt(ss * (1.0 / HEAD_DIM) + EPS)) * g_t
    if rope_ref is None:
        return yn
    f = ROPE_FREQS
    x1r, x2r, x1c, x2c = yn[0:f], yn[f:2 * f], yn[2 * f:3 * f], yn[3 * f:4 * f]
    cr, sr, cc, sc = rope_ref[0], rope_ref[1], rope_ref[2], rope_ref[3]
    return jnp.concatenate([x1r * cr - x2r * sr, x2r * cr + x1r * sr,
                            x1c * cc - x2c * sc, x2c * cc + x1c * sc], axis=0)


def _qkv_body(*refs, rope, keep_f32, sub):
    n_in = 7 if rope else 6
    x_ref, mods_ref, ng_ref, w_ref, qg_ref, kg_ref = refs[:6]
    rope_ref = refs[6] if rope else None
    qt_ref, ka_ref, vt_ref = refs[n_in:n_in + 3]
    m = mods_ref[0, 0]
    tm = x_ref.shape[0]
    k0 = N_HEADS * HEAD_DIM
    row = lax.broadcasted_iota(jnp.int32, (ACC_ROWS - HEAD_DIM, V_PIECE), 0)
    extra = jnp.where(row == 0, 1.0, 0.0).astype(vt_ref.dtype)

    def project(r):
        rows = slice(r * sub, (r + 1) * sub)
        a = _modln(x_ref[rows, :], ng_ref[0, 1:2], m[3:4], m[4:5]).astype(_BF16)
        return jnp.dot(a, w_ref[0], preferred_element_type=_F32)

    def finish(r, proj):
        toks = slice(r * sub, (r + 1) * sub)
        qg, kg = qg_ref[0, :, toks], kg_ref[0, :, toks]
        rope_r = None if rope_ref is None else rope_ref.at[:, :, toks]
        for c in range(k0 // GROUP_COLS):
            yt = proj[:, c * GROUP_COLS:(c + 1) * GROUP_COLS].T
            for h in range(KV_GROUP):
                qn = _head_norm_rope_t(yt[h * HEAD_DIM:(h + 1) * HEAD_DIM], qg, rope_r)
                r0 = c * GROUP_COLS + h * HEAD_DIM
                qt_ref[0, r0:r0 + HEAD_DIM, toks] = (qn * Q_SCALE).astype(qt_ref.dtype)
        kt = proj[:, k0:k0 + KV_COLS].T
        kn = jnp.concatenate([_head_norm_rope_t(kt[g * HEAD_DIM:(g + 1) * HEAD_DIM], kg, rope_r)
                              for g in range(N_KV_HEADS)], axis=0)
        k_tok = kn.T
        v_tok = proj[:, k0 + KV_COLS:]
        vt = v_tok.T
        for g in range(N_KV_HEADS):
            ka_ref[0, g, toks, :] = k_tok[:, g * HEAD_DIM:(g + 1) * HEAD_DIM].astype(ka_ref.dtype)
            for j in range(sub // V_PIECE):
                piece = r * (sub // V_PIECE) + j
                vt_ref[0, g, piece, 0:HEAD_DIM, :] = (
                    vt[g * HEAD_DIM:(g + 1) * HEAD_DIM, j * V_PIECE:(j + 1) * V_PIECE].astype(vt_ref.dtype))
                vt_ref[0, g, piece, HEAD_DIM:ACC_ROWS, :] = extra
        if keep_f32:
            k_ref, v_ref = refs[n_in + 3:n_in + 5]
            k_ref[toks, :] = k_tok
            v_ref[toks, :] = v_tok

    n_sub = tm // sub
    proj = project(0)
    for r in range(n_sub):
        nxt = project(r + 1) if r + 1 < n_sub else None
        finish(r, proj)
        proj = nxt


def _qkv(x, mods, norm_g, w_qkv, qg_t, kg_t, layer, row_of_tile, tm, seq_len, keep_f32, rope_t=None):
    t, d = x.shape
    rope = rope_t is not None
    tiles_per_seq = seq_len // tm
    n_seq = t // seq_len
    seq_tile = lambda i: (i // tiles_per_seq, 0, i % tiles_per_seq)
    out_specs = [pl.BlockSpec((1, N_HEADS * HEAD_DIM, tm), seq_tile),
                 pl.BlockSpec((1, N_KV_HEADS, tm, HEAD_DIM), lambda i: seq_tile(i) + (0,)),
                 pl.BlockSpec((1, N_KV_HEADS, tm // V_PIECE, ACC_ROWS, V_PIECE), lambda i: seq_tile(i) + (0, 0))]
    out_shape = [jax.ShapeDtypeStruct((n_seq, N_HEADS * HEAD_DIM, seq_len), _BF16),
                 jax.ShapeDtypeStruct((n_seq, N_KV_HEADS, seq_len, HEAD_DIM), _BF16),
                 jax.ShapeDtypeStruct((n_seq, N_KV_HEADS, seq_len // V_PIECE, ACC_ROWS, V_PIECE), _BF16)]
    if keep_f32:
        out_specs += [pl.BlockSpec((tm, KV_COLS), lambda i: (i, 0))] * 2
        out_shape += [jax.ShapeDtypeStruct((t, KV_COLS), _F32)] * 2
    in_specs = [
        pl.BlockSpec((tm, d), lambda i: (i, 0)),
        pl.BlockSpec((1, 1, N_MOD, d), lambda i: (layer, row_of_tile(i), 0, 0)),
        pl.BlockSpec((1, 3, d), lambda i: (layer, 0, 0)),
        _resident((1, d, QKV_COLS), lambda i: (layer, 0, 0)),
        pl.BlockSpec((1, HEAD_DIM, tm), lambda i: (layer, 0, 0)),
        pl.BlockSpec((1, HEAD_DIM, tm), lambda i: (layer, 0, 0)),
    ]
    args = [x, mods, norm_g, w_qkv, qg_t, kg_t]
    if rope:
        in_specs += [pl.BlockSpec((4, ROPE_FREQS, tm), lambda i: (0, 0, i % tiles_per_seq))]
        args += [rope_t]
    return pl.pallas_call(
        functools.partial(_qkv_body, rope=rope, keep_f32=keep_f32, sub=min(tm, QKV_SUB)),
        grid=(t // tm,),
        in_specs=in_specs,
        out_specs=out_specs,
        out_shape=out_shape,
        compiler_params=_params(1),
        name="qkv_rope" if rope else "qkv",
    )(*args)


def _rope_table_t(n_tok):
    pos = jnp.arange(n_tok)
    row = (pos // GRID_W).astype(_F32)
    col = (pos % GRID_W).astype(_F32)
    freqs = 1.0 / jnp.power(ROPE_THETA, jnp.arange(ROPE_FREQS, dtype=_F32) / ROPE_FREQS)
    ang_r = row[:, None] * freqs
    ang_c = col[:, None] * freqs
    return jnp.stack([jnp.cos(ang_r).T, jnp.sin(ang_r).T, jnp.cos(ang_c).T, jnp.sin(ang_c).T])


KV_SUB = 256
V_PIECE = 128
SCORE_ROWS = 128
BF16_SUBLANES = 16
ACC_ROWS = HEAD_DIM + BF16_SUBLANES


SCORE_AHEAD = 3
VALUE_BEHIND = 1
S_SLOTS = SCORE_AHEAD + 1
P_SLOTS = VALUE_BEHIND + 1


def _attn_scratch(n_states, tq, tk):
    return [pltpu.VMEM((S_SLOTS, tk, tq), _F32),
            pltpu.VMEM((S_SLOTS, 1, tq), _F32),
            pltpu.VMEM((P_SLOTS, tk, tq), _BF16),
            pltpu.VMEM((P_SLOTS, 1, tq), _F32),
            pltpu.VMEM((n_states, 1, tq), _F32),
            pltpu.VMEM((n_states, ACC_ROWS, tq), _F32)]


def _init_states(scr, sinks=None):
    m_ref, acc_ref = scr[4], scr[5]
    tq = m_ref.shape[-1]
    if sinks is None:
        m_ref[...] = jnp.full(m_ref.shape, NEG_INF, _F32)
        acc_ref[...] = jnp.zeros(acc_ref.shape, _F32)
    else:
        for i, sk in enumerate(sinks):
            m_ref[i] = jnp.full((1, tq), sk * LOG2_E, _F32)
        row = lax.broadcasted_iota(jnp.int32, (ACC_ROWS, tq), 0)
        acc_ref[...] = jnp.broadcast_to(jnp.where(row == HEAD_DIM, 1.0, 0.0).astype(_F32), acc_ref.shape)


def _part(j, item):
    part = item[3][0].shape[1]
    return slice(j * part, (j + 1) * part)


def _scores(scr, slot, j, item):
    s_ref, mc_ref = scr[0], scr[1]
    _, k, qt, _, bias = item
    kj = k[_part(j, item)]
    s = jnp.concatenate([jnp.dot(kj[r:r + SCORE_ROWS], qt, preferred_element_type=_F32)
                         for r in range(0, kj.shape[0], SCORE_ROWS)], axis=0)
    if bias is not None:
        s = s + bias[_part(j, item)]
    s_ref[slot, _part(j, item)] = s
    mx = jnp.max(s, axis=0, keepdims=True)
    mc_ref[slot] = mx if j == 0 else jnp.maximum(mc_ref[slot], mx)


def _new_max(scr, s_slot, p_slot, st):
    mc_ref, al_ref, m_ref = scr[1], scr[3], scr[4]
    m_old = m_ref[st]
    m_new = jnp.maximum(m_old, mc_ref[s_slot])
    al_ref[p_slot] = jnp.exp2(m_old - m_new)
    m_ref[st] = m_new


def _softmax(scr, s_slot, p_slot, j, item):
    s_ref, p_ref, m_ref = scr[0], scr[2], scr[4]
    rows = _part(j, item)
    p_ref[p_slot, rows] = jnp.exp2(s_ref[s_slot, rows] - m_ref[item[0]]).astype(_BF16)


def _values(scr, slot, j, item):
    p_ref, al_ref, acc_ref = scr[2], scr[3], scr[5]
    st, vts = item[0], item[3]
    pv = jnp.dot(vts[j], p_ref[slot, _part(j, item)], preferred_element_type=_F32)
    acc = acc_ref[st]
    acc_ref[st] = (al_ref[slot] * acc if j == 0 else acc) + pv


def _pipeline(scr, items):
    n = len(items)
    n_parts = len(items[0][3])
    for i0 in range(min(SCORE_AHEAD, n)):
        for j in range(n_parts):
            _scores(scr, i0 % S_SLOTS, j, items[i0])
    for i in range(n + VALUE_BEHIND):
        if i < n:
            _new_max(scr, i % S_SLOTS, i % P_SLOTS, items[i][0])
        for j in range(n_parts):
            if i + SCORE_AHEAD < n:
                _scores(scr, (i + SCORE_AHEAD) % S_SLOTS, j, items[i + SCORE_AHEAD])
            if i < n:
                _softmax(scr, i % S_SLOTS, i % P_SLOTS, j, items[i])
            if i >= VALUE_BEHIND:
                _values(scr, (i - VALUE_BEHIND) % P_SLOTS, j, items[i - VALUE_BEHIND])


def _finish(scr, first):
    acc_ref = scr[5]
    ot = jnp.concatenate([acc_ref[first + h, 0:HEAD_DIM] / acc_ref[first + h, HEAD_DIM:HEAD_DIM + 1]
                          for h in range(KV_GROUP)], axis=0)
    return ot.T


def _heads(qt_ref, g=None):
    base = 0 if g is None else g * GROUP_COLS
    return [qt_ref[0, base + h * HEAD_DIM: base + (h + 1) * HEAD_DIM, :] for h in range(KV_GROUP)]


def _vt_part(vt_ref, lead, piece, part=KV_SUB):
    return jnp.concatenate([vt_ref[lead + (piece + i,)] for i in range(part // V_PIECE)], axis=1)


def _query_blocks(qt_ref, tq):
    return [[qt_ref[0, h * HEAD_DIM:(h + 1) * HEAD_DIM, b * tq:(b + 1) * tq] for h in range(KV_GROUP)]
            for b in range(qt_ref.shape[2] // tq)]


def _chunk_items(k_ref, vt_ref, blocks, tk, part=KV_SUB):
    items = []
    for c in range(k_ref.shape[2] // tk):
        k = k_ref[0, 0, c * tk:(c + 1) * tk, :]
        vts = [_vt_part(vt_ref, (0, 0), (c * tk + j * part) // V_PIECE, part) for j in range(tk // part)]
        for b, heads in enumerate(blocks):
            items += [(b * KV_GROUP + h, k, heads[h], vts, None) for h in range(KV_GROUP)]
    return items


def _store_blocks(scr, o_ref, tq, n_blocks):
    for b in range(n_blocks):
        o_ref[0, b * tq:(b + 1) * tq, :] = _finish(scr, b * KV_GROUP).astype(o_ref.dtype)


def _full_body(qt_ref, kc_ref, vtc_ref, kl_ref, vtl_ref, o_ref, *scr, tq, tk):
    blocks = _query_blocks(qt_ref, tq)
    _init_states(scr)
    _pipeline(scr, _chunk_items(kc_ref, vtc_ref, blocks, tk) + _chunk_items(kl_ref, vtl_ref, blocks, tk))
    _store_blocks(scr, o_ref, tq, len(blocks))


def _kv_specs(lk, index_map):
    return [pl.BlockSpec((1, 1, lk, HEAD_DIM), lambda *i: index_map(*i) + (0, 0)),
            pl.BlockSpec((1, 1, lk // V_PIECE, ACC_ROWS, V_PIECE), lambda *i: index_map(*i) + (0, 0, 0))]


def _attend_full(qt, k_ctx, vt_ctx, k_lat, vt_lat, tq, tk, n_blocks):
    b, d, l = qt.shape
    ts = tq * n_blocks
    by_group = lambda bi, g, qi: (bi, g)
    return pl.pallas_call(
        functools.partial(_full_body, tq=tq, tk=tk),
        grid=(b, N_KV_HEADS, l // ts),
        in_specs=([pl.BlockSpec((1, GROUP_COLS, ts), lambda bi, g, qi: (bi, g, qi))]
                  + _kv_specs(k_ctx.shape[2], by_group) + _kv_specs(l, by_group)),
        out_specs=pl.BlockSpec((1, ts, GROUP_COLS), lambda bi, g, qi: (bi, qi, g)),
        out_shape=jax.ShapeDtypeStruct((b, l, d), _BF16),
        scratch_shapes=_attn_scratch(KV_GROUP * n_blocks, tq, tk),
        compiler_params=_params(3),
        name="attn_full",
    )(qt, k_ctx, vt_ctx, k_lat, vt_lat)


def _band_start(q0, l, tq):
    return jnp.clip(q0 - WINDOW, 0, l - 2 * tq)


def _window_body(sink_ref, qt_ref, kc_ref, vtc_ref, kl_ref, vtl_ref, bias_ref, o_ref, *scr, tq):
    g = pl.program_id(1)
    blocks = _query_blocks(qt_ref, tq)
    _init_states(scr, [sink_ref[g * KV_GROUP + h] for _ in blocks for h in range(KV_GROUP)])
    items = _chunk_items(kc_ref, vtc_ref, blocks, 2 * tq)
    for b, heads in enumerate(blocks):
        q0 = (pl.program_id(2) * len(blocks) + b) * tq
        start = pl.multiple_of(_band_start(q0, kl_ref.shape[2], tq), V_PIECE)
        k_band = kl_ref[0, 0, pl.ds(start, 2 * tq), :]
        vt_band = [_vt_part(vtl_ref, (0, 0), start // V_PIECE + j * (KV_SUB // V_PIECE)) for j in range(2)]
        items += [(b * KV_GROUP + h, k_band, heads[h], vt_band, bias_ref[b]) for h in range(KV_GROUP)]
    _pipeline(scr, items)
    _store_blocks(scr, o_ref, tq, len(blocks))


def _window_bias(l, tq):
    q0 = (jnp.arange(l // tq) * tq)[:, None, None]
    kpos = _band_start(q0, l, tq) + jnp.arange(2 * tq)[None, :, None]
    qpos = q0 + jnp.arange(tq)[None, None, :]
    return jnp.where(jnp.abs(qpos - kpos) <= WINDOW, 0.0, NEG_INF).astype(_F32)


def _attend_window(qt, k_ctx, vt_ctx, k_lat, vt_lat, sink, tq, n_blocks):
    b, d, l = qt.shape
    lc = k_ctx.shape[2]
    ts = tq * n_blocks
    assert tq == KV_SUB and tq >= 2 * WINDOW and lc == 2 * tq and l >= 2 * tq
    by_group = lambda bi, g, qi: (bi, g)
    return pl.pallas_call(
        functools.partial(_window_body, tq=tq),
        grid=(b, N_KV_HEADS, l // ts),
        in_specs=([pl.BlockSpec(memory_space=pltpu.SMEM),
                   pl.BlockSpec((1, GROUP_COLS, ts), lambda bi, g, qi: (bi, g, qi))]
                  + _kv_specs(lc, by_group) + _kv_specs(l, by_group)
                  + [pl.BlockSpec((n_blocks, 2 * tq, tq), lambda bi, g, qi: (qi, 0, 0))]),
        out_specs=pl.BlockSpec((1, ts, GROUP_COLS), lambda bi, g, qi: (bi, qi, g)),
        out_shape=jax.ShapeDtypeStruct((b, l, d), _BF16),
        scratch_shapes=_attn_scratch(KV_GROUP * n_blocks, tq, 2 * tq),
        compiler_params=_params(3),
        name="attn_window",
    )(sink, qt, k_ctx, vt_ctx, k_lat, vt_lat, _window_bias(l, tq))


def _ctx_body(*refs, has_sink):
    if has_sink:
        sink_ref, qt_ref, k_ref, vt_ref, o_ref = refs[:5]
    else:
        qt_ref, k_ref, vt_ref, o_ref = refs[:4]
    scr = refs[5:] if has_sink else refs[4:]
    n_sub = k_ref.shape[2] // KV_SUB
    _init_states(scr, [sink_ref[i] for i in range(N_HEADS)] if has_sink else None)
    items = []
    for g in range(N_KV_HEADS):
        heads = _heads(qt_ref, g)
        k = k_ref[0, g]
        vts = [_vt_part(vt_ref, (0, g), j * (KV_SUB // V_PIECE)) for j in range(n_sub)]
        items += [(g * KV_GROUP + h, k, heads[h], vts, None) for h in range(KV_GROUP)]
    _pipeline(scr, items)
    for g in range(N_KV_HEADS):
        o_ref[0, :, g * GROUP_COLS:(g + 1) * GROUP_COLS] = _finish(scr, g * KV_GROUP).astype(o_ref.dtype)


def _attend_ctx(qt, k, vt, sink):
    b, d, l = qt.shape
    has_sink = sink is not None
    in_specs = [
        pl.BlockSpec((1, d, l), lambda bi: (bi, 0, 0)),
        pl.BlockSpec((1, N_KV_HEADS, l, HEAD_DIM), lambda bi: (bi, 0, 0, 0)),
        pl.BlockSpec((1, N_KV_HEADS, l // V_PIECE, ACC_ROWS, V_PIECE), lambda bi: (bi, 0, 0, 0, 0)),
    ]
    args = [qt, k, vt]
    if has_sink:
        in_specs = [pl.BlockSpec(memory_space=pltpu.SMEM)] + in_specs
        args = [sink] + args
    return pl.pallas_call(
        functools.partial(_ctx_body, has_sink=has_sink),
        grid=(b,),
        in_specs=in_specs,
        out_specs=pl.BlockSpec((1, l, d), lambda bi: (bi, 0, 0)),
        out_shape=jax.ShapeDtypeStruct((b, l, d), _BF16),
        scratch_shapes=_attn_scratch(N_HEADS, l, l),
        compiler_params=_params(1),
        name="attn_ctx_sink" if has_sink else "attn_ctx",
    )(*args)


def _cache_layout(cache_k, cache_v):
    b, depth, p, g, hd = cache_v.shape
    k = cache_k.transpose(1, 0, 3, 2, 4).astype(_BF16)
    vt = cache_v.reshape(b, depth, p // V_PIECE, V_PIECE, g, hd).transpose(1, 0, 4, 2, 5, 3).astype(_BF16)
    extra = jnp.zeros((ACC_ROWS - hd, V_PIECE), _BF16).at[0].set(1.0)
    return k, jnp.concatenate([vt, jnp.broadcast_to(extra, vt.shape[:4] + extra.shape)], axis=4)


def _token_tile(t, cap):
    tm = min(cap, t)
    assert t % tm == 0
    return tm


def kernel(x_prompt, x_sample, cache_k, cache_v, c, c_ctx, w_mod, b_mod, norm_g, w_qkv, w_o,
           q_norm_g, k_norm_g, sink, w_ffn_in, w_ffn_out):
    depth = w_mod.shape[0]
    d = D_MODEL
    bp, lp, _ = x_prompt.shape
    bs, ls, _ = x_sample.shape
    past = cache_k.shape[2]
    assert lp % KV_SUB == 0 and ls % KV_TILE == 0 and past % KV_TILE == 0

    w_in = (w_ffn_in.astype(_BF16).reshape(depth, 2, d, 2 * N_FF_CHUNKS, FF_CHUNK)
            .transpose(0, 1, 3, 2, 4))
    w_out = w_ffn_out.astype(_BF16).reshape(depth, 2, N_FF_CHUNKS, FF_CHUNK, d)
    w_qkv_b = w_qkv.astype(_BF16)
    w_o_b = w_o.astype(_BF16)

    n_rows = -(-(bs + 1) // 8) * 8
    cond = jnp.concatenate([c, c_ctx[None, :], jnp.zeros((n_rows - bs - 1, d), _F32)], axis=0)
    mods = _modulation(cond, w_mod, b_mod)

    tp = bp * lp
    tm_p = _token_tile(tp, FFN_TILE)
    tq_p = _token_tile(lp, QKV_TILE)
    ctx_row = lambda i: bs
    qg_p = jnp.broadcast_to(q_norm_g[:, :, None], (depth, HEAD_DIM, tq_p))
    kg_p = jnp.broadcast_to(k_norm_g[:, :, None], (depth, HEAD_DIM, tq_p))
    h = x_prompt.reshape(tp, d)
    new_ks, new_vs = [], []
    for i in range(depth):
        h = _ffn(h, mods, norm_g, w_in, w_out, i, 0, ctx_row, tm_p)
        qt, ka, vt, k, v = _qkv(h, mods, norm_g, w_qkv_b, qg_p, kg_p, i, ctx_row, tq_p, lp, True)
        new_ks.append(k.reshape(bp, lp, N_KV_HEADS, HEAD_DIM))
        new_vs.append(v.reshape(bp, lp, N_KV_HEADS, HEAD_DIM))
        o = _attend_ctx(qt, ka, vt, sink[i // 2] if i % 2 == 1 else None)
        h = _ffn(h, mods, norm_g, w_in, w_out, i, 1, ctx_row, tm_p, o=o.reshape(tp, d), w_o=w_o_b)
    y_prompt = h.reshape(bp, lp, d)
    new_k = jnp.stack(new_ks, axis=1)
    new_v = jnp.stack(new_vs, axis=1)

    ts = bs * ls
    tm_s = _token_tile(ls, FFN_TILE)
    tq_s = _token_tile(ls, QKV_TILE)
    lat_row = lambda i: i // (ls // tm_s)
    lat_row_q = lambda i: i // (ls // tq_s)
    rope_t = _rope_table_t(ls)
    qg_s = jnp.broadcast_to(q_norm_g[:, :, None], (depth, HEAD_DIM, tq_s))
    kg_s = jnp.broadcast_to(k_norm_g[:, :, None], (depth, HEAD_DIM, tq_s))
    k_ctx, vt_ctx = _cache_layout(cache_k, cache_v)
    blocks_per_step = lambda want: want if ls % (want * Q_TILE) == 0 else 1
    h = x_sample.reshape(ts, d)
    for i in range(depth):
        h = _ffn(h, mods, norm_g, w_in, w_out, i, 0, lat_row, tm_s)
        qt, ka, vt = _qkv(h, mods, norm_g, w_qkv_b, qg_s, kg_s, i, lat_row_q, tq_s, ls, False, rope_t)
        if i % 2 == 0:
            o = _attend_full(qt, k_ctx[i], vt_ctx[i], ka, vt, Q_TILE, KV_TILE, blocks_per_step(FULL_Q_BLOCKS))
        else:
            o = _attend_window(qt, k_ctx[i], vt_ctx[i], ka, vt, sink[i // 2], Q_TILE,
                               blocks_per_step(WINDOW_Q_BLOCKS))
        h = _ffn(h, mods, norm_g, w_in, w_out, i, 1, lat_row, tm_s, o=o.reshape(ts, d), w_o=w_o_b)
    y_sample = h.reshape(bs, ls, d)

    return (y_prompt, y_sample, new_k, new_v)
```

```python
import functools

import jax
import jax.numpy as jnp
from jax import lax
from jax.experimental import pallas as pl
from jax.experimental.pallas import tpu as pltpu

D_MODEL = 1024
N_HEADS = 16
N_KV_HEADS = 4
HEAD_DIM = 64
KV_GROUP = N_HEADS // N_KV_HEADS
GROUP_COLS = KV_GROUP * HEAD_DIM
KV_COLS = N_KV_HEADS * HEAD_DIM
QKV_COLS = (N_HEADS + 2 * N_KV_HEADS) * HEAD_DIM
D_FF = 2816
N_MOD = 9
GRID_W = 64
WINDOW = 128
ROPE_FREQS = HEAD_DIM // 4
ROPE_THETA = 10000.0
ATTN_SCALE = HEAD_DIM ** -0.5
LOG2_E = 1.4426950408889634
Q_SCALE = ATTN_SCALE * LOG2_E
EPS = 1e-6
NEG_INF = -1e30

FF_CHUNK = 256
N_FF_CHUNKS = D_FF // FF_CHUNK
FFN_TILE = 1024
QKV_TILE = 1024
QKV_SUB = 512
Q_TILE = 256
FULL_Q_BLOCKS = 4
WINDOW_Q_BLOCKS = 4
KV_TILE = 512
VMEM_LIMIT_BYTES = 52 * 1024 * 1024

_BF16 = jnp.bfloat16
_F32 = jnp.float32


def _params(n_axes):
    return pltpu.CompilerParams(dimension_semantics=("arbitrary",) * n_axes,
                                vmem_limit_bytes=VMEM_LIMIT_BYTES)


def _resident(block_shape, index_map):
    return pl.BlockSpec(block_shape, index_map, pipeline_mode=pl.Buffered(1))


def _silu(x):
    return x * jax.nn.sigmoid(x)


def _modln(x, g, shift, scale):
    y = x * lax.rsqrt(jnp.mean(x * x, axis=-1, keepdims=True) + EPS)
    return y * (g * (1.0 + scale)) + shift


def _mod_body(cond_ref, w_ref, b_ref, o_ref):
    a = _silu(cond_ref[...]).astype(_BF16)
    w = w_ref[0].astype(_BF16)
    o_ref[0] = jnp.dot(a, w, preferred_element_type=_F32) + b_ref[0]


def _modulation(cond, w_mod, b_mod):
    depth, d, n = w_mod.shape
    r = cond.shape[0]
    tn = n // 8
    out = pl.pallas_call(
        _mod_body,
        grid=(depth, n // tn),
        in_specs=[
            pl.BlockSpec((r, d), lambda i, j: (0, 0)),
            pl.BlockSpec((1, d, tn), lambda i, j: (i, 0, j)),
            pl.BlockSpec((1, 1, tn), lambda i, j: (i, 0, j)),
        ],
        out_specs=pl.BlockSpec((1, r, tn), lambda i, j: (i, 0, j)),
        out_shape=jax.ShapeDtypeStruct((depth, r, n), _F32),
        compiler_params=_params(2),
        name="modulation",
    )(cond, w_mod, b_mod.reshape(depth, 1, n))
    return out.reshape(depth, r, N_MOD, d)


def _ffn_body(*refs, sub, has_proj):
    if has_proj:
        x_ref, mods_ref, ng_ref, o_ref, wo_ref, win_ref, wout_ref, out_ref, xn_ref, acc_ref = refs
    else:
        x_ref, mods_ref, ng_ref, win_ref, wout_ref, out_ref, xn_ref, acc_ref = refs
    m = mods_ref[0, 0]
    x = x_ref[...]
    if has_proj:
        x = x + m[5:6] * jnp.dot(o_ref[...], wo_ref[0], preferred_element_type=_F32)
    k = 3 * sub
    xn_ref[...] = _modln(x, ng_ref[0, sub:sub + 1], m[k:k + 1], m[k + 1:k + 2]).astype(_BF16)
    if has_proj:
        out_ref[...] = x

    def ffn_chunk(c):
        xn = xn_ref[...]
        cols = slice(c * FF_CHUNK, (c + 1) * FF_CHUNK)
        gate = jnp.dot(xn, win_ref[0, 0, :, cols], preferred_element_type=_F32)
        up = jnp.dot(xn, win_ref[0, 0, :, D_FF + c * FF_CHUNK:D_FF + (c + 1) * FF_CHUNK],
                     preferred_element_type=_F32)
        act = (_silu(gate) * up).astype(_BF16)
        return jnp.dot(act, wout_ref[0, 0, c], preferred_element_type=_F32)

    acc_ref[...] = ffn_chunk(0)

    assert N_FF_CHUNKS % 2 == 1
    for c in range(1, N_FF_CHUNKS, 2):
        acc_ref[...] += ffn_chunk(c) + ffn_chunk(c + 1)
    h = out_ref[...] if has_proj else x_ref[...]
    out_ref[...] = h + (0.5 * m[k + 2:k + 3]) * acc_ref[...]


def _ffn(x, mods, norm_g, w_in, w_out, layer, which, row_of_tile, tm, o=None, w_o=None):
    t, d = x.shape
    sub = 2 * which
    has_proj = o is not None
    in_specs = [
        pl.BlockSpec((tm, d), lambda i: (i, 0)),
        pl.BlockSpec((1, 1, N_MOD, d), lambda i: (layer, row_of_tile(i), 0, 0)),
        pl.BlockSpec((1, 3, d), lambda i: (layer, 0, 0)),
    ]
    args = [x, mods, norm_g]
    if has_proj:
        in_specs += [pl.BlockSpec((tm, d), lambda i: (i, 0)),
                     _resident((1, d, d), lambda i: (layer, 0, 0))]
        args += [o, w_o]
    in_specs += [
        _resident((1, 1, d, 2 * D_FF), lambda i: (layer, which, 0, 0)),
        _resident((1, 1, N_FF_CHUNKS, FF_CHUNK, d), lambda i: (layer, which, 0, 0, 0)),
    ]
    args += [w_in, w_out]
    return pl.pallas_call(
        functools.partial(_ffn_body, sub=sub, has_proj=has_proj),
        grid=(t // tm,),
        in_specs=in_specs,
        out_specs=pl.BlockSpec((tm, d), lambda i: (i, 0)),
        out_shape=jax.ShapeDtypeStruct((t, d), _F32),
        scratch_shapes=[pltpu.VMEM((tm, d), _BF16), pltpu.VMEM((tm, d), _F32)],
        compiler_params=_params(1),
        name="ffn_proj" if has_proj else "ffn",
    )(*args)


def _head_norm_rope_t(yt, g_t, rope_ref):
    ss = jnp.sum(yt * yt, axis=0, keepdims=True)
    yn = (yt * lax.rsqrt(ss * (1.0 / HEAD_DIM) + EPS)) * g_t
    if rope_ref is None:
        return yn
    f = ROPE_FREQS
    x1r, x2r, x1c, x2c = yn[0:f], yn[f:2 * f], yn[2 * f:3 * f], yn[3 * f:4 * f]
    cr, sr, cc, sc = rope_ref[0], rope_ref[1], rope_ref[2], rope_ref[3]
    return jnp.concatenate([x1r * cr - x2r * sr, x2r * cr + x1r * sr,
                            x1c * cc - x2c * sc, x2c * cc + x1c * sc], axis=0)


def _qkv_body(*refs, rope, keep_f32, sub):
    n_in = 7 if rope else 6
    x_ref, mods_ref, ng_ref, w_ref, qg_ref, kg_ref = refs[:6]
    rope_ref = refs[6] if rope else None
    qt_ref, ka_ref, vt_ref = refs[n_in:n_in + 3]
    m = mods_ref[0, 0]
    tm = x_ref.shape[0]
    k0 = N_HEADS * HEAD_DIM
    row = lax.broadcasted_iota(jnp.int32, (ACC_ROWS - HEAD_DIM, V_PIECE), 0)
    extra = jnp.where(row == 0, 1.0, 0.0).astype(vt_ref.dtype)

    def project(r):
        rows = slice(r * sub, (r + 1) * sub)
        a = _modln(x_ref[rows, :], ng_ref[0, 1:2], m[3:4], m[4:5]).astype(_BF16)
        return jnp.dot(a, w_ref[0], preferred_element_type=_F32)

    def finish(r, proj):
        toks = slice(r * sub, (r + 1) * sub)
        qg, kg = qg_ref[0, :, toks], kg_ref[0, :, toks]
        rope_r = None if rope_ref is None else rope_ref.at[:, :, toks]
        for c in range(k0 // GROUP_COLS):
            yt = proj[:, c * GROUP_COLS:(c + 1) * GROUP_COLS].T
            for h in range(KV_GROUP):
                qn = _head_norm_rope_t(yt[h * HEAD_DIM:(h + 1) * HEAD_DIM], qg, rope_r)
                r0 = c * GROUP_COLS + h * HEAD_DIM
                qt_ref[0, r0:r0 + HEAD_DIM, toks] = (qn * Q_SCALE).astype(qt_ref.dtype)
        kt = proj[:, k0:k0 + KV_COLS].T
        kn = jnp.concatenate([_head_norm_rope_t(kt[g * HEAD_DIM:(g + 1) * HEAD_DIM], kg, rope_r)
                              for g in range(N_KV_HEADS)], axis=0)
        k_tok = kn.T
        v_tok = proj[:, k0 + KV_COLS:]
        vt = v_tok.T
        for g in range(N_KV_HEADS):
            ka_ref[0, g, toks, :] = k_tok[:, g * HEAD_DIM:(g + 1) * HEAD_DIM].astype(ka_ref.dtype)
            for j in range(sub // V_PIECE):
                piece = r * (sub // V_PIECE) + j
                vt_ref[0, g, piece, 0:HEAD_DIM, :] = (
                    vt[g * HEAD_DIM:(g + 1) * HEAD_DIM, j * V_PIECE:(j + 1) * V_PIECE].astype(vt_ref.dtype))
                vt_ref[0, g, piece, HEAD_DIM:ACC_ROWS, :] = extra
        if keep_f32:
            k_ref, v_ref = refs[n_in + 3:n_in + 5]
            k_ref[toks, :] = k_tok
            v_ref[toks, :] = v_tok

    n_sub = tm // sub
    proj = project(0)
    for r in range(n_sub):
        nxt = project(r + 1) if r + 1 < n_sub else None
        finish(r, proj)
        proj = nxt


def _qkv(x, mods, norm_g, w_qkv, qg_t, kg_t, layer, row_of_tile, tm, seq_len, keep_f32, rope_t=None):
    t, d = x.shape
    rope = rope_t is not None
    tiles_per_seq = seq_len // tm
    n_seq = t // seq_len
    seq_tile = lambda i: (i // tiles_per_seq, 0, i % tiles_per_seq)
    out_specs = [pl.BlockSpec((1, N_HEADS * HEAD_DIM, tm), seq_tile),
                 pl.BlockSpec((1, N_KV_HEADS, tm, HEAD_DIM), lambda i: seq_tile(i) + (0,)),
                 pl.BlockSpec((1, N_KV_HEADS, tm // V_PIECE, ACC_ROWS, V_PIECE), lambda i: seq_tile(i) + (0, 0))]
    out_shape = [jax.ShapeDtypeStruct((n_seq, N_HEADS * HEAD_DIM, seq_len), _BF16),
                 jax.ShapeDtypeStruct((n_seq, N_KV_HEADS, seq_len, HEAD_DIM), _BF16),
                 jax.ShapeDtypeStruct((n_seq, N_KV_HEADS, seq_len // V_PIECE, ACC_ROWS, V_PIECE), _BF16)]
    if keep_f32:
        out_specs += [pl.BlockSpec((tm, KV_COLS), lambda i: (i, 0))] * 2
        out_shape += [jax.ShapeDtypeStruct((t, KV_COLS), _F32)] * 2
    in_specs = [
        pl.BlockSpec((tm, d), lambda i: (i, 0)),
        pl.BlockSpec((1, 1, N_MOD, d), lambda i: (layer, row_of_tile(i), 0, 0)),
        pl.BlockSpec((1, 3, d), lambda i: (layer, 0, 0)),
        _resident((1, d, QKV_COLS), lambda i: (layer, 0, 0)),
        pl.BlockSpec((1, HEAD_DIM, tm), lambda i: (layer, 0, 0)),
        pl.BlockSpec((1, HEAD_DIM, tm), lambda i: (layer, 0, 0)),
    ]
    args = [x, mods, norm_g, w_qkv, qg_t, kg_t]
    if rope:
        in_specs += [pl.BlockSpec((4, ROPE_FREQS, tm), lambda i: (0, 0, i % tiles_per_seq))]
        args += [rope_t]
    return pl.pallas_call(
        functools.partial(_qkv_body, rope=rope, keep_f32=keep_f32, sub=min(tm, QKV_SUB)),
        grid=(t // tm,),
        in_specs=in_specs,
        out_specs=out_specs,
        out_shape=out_shape,
        compiler_params=_params(1),
        name="qkv_rope" if rope else "qkv",
    )(*args)


def _rope_table_t(n_tok):
    pos = jnp.arange(n_tok)
    row = (pos // GRID_W).astype(_F32)
    col = (pos % GRID_W).astype(_F32)
    freqs = 1.0 / jnp.power(ROPE_THETA, jnp.arange(ROPE_FREQS, dtype=_F32) / ROPE_FREQS)
    ang_r = row[:, None] * freqs
    ang_c = col[:, None] * freqs
    return jnp.stack([jnp.cos(ang_r).T, jnp.sin(ang_r).T, jnp.cos(ang_c).T, jnp.sin(ang_c).T])


KV_SUB = 256
V_PIECE = 128
SCORE_ROWS = 128
BF16_SUBLANES = 16
ACC_ROWS = HEAD_DIM + BF16_SUBLANES


SCORE_AHEAD = 2
VALUE_BEHIND = 1
S_SLOTS = SCORE_AHEAD + 1
P_SLOTS = VALUE_BEHIND + 1


def _attn_scratch(n_states, tq, tk):
    return [pltpu.VMEM((S_SLOTS, tk, tq), _F32),
            pltpu.VMEM((S_SLOTS, 1, tq), _F32),
            pltpu.VMEM((P_SLOTS, tk, tq), _BF16),
            pltpu.VMEM((P_SLOTS, 1, tq), _F32),
            pltpu.VMEM((n_states, 1, tq), _F32),
            pltpu.VMEM((n_states, ACC_ROWS, tq), _F32)]


def _init_states(scr, sinks=None):
    m_ref, acc_ref = scr[4], scr[5]
    tq = m_ref.shape[-1]
    if sinks is None:
        m_ref[...] = jnp.full(m_ref.shape, NEG_INF, _F32)
        acc_ref[...] = jnp.zeros(acc_ref.shape, _F32)
    else:
        for i, sk in enumerate(sinks):
            m_ref[i] = jnp.full((1, tq), sk * LOG2_E, _F32)
        row = lax.broadcasted_iota(jnp.int32, (ACC_ROWS, tq), 0)
        acc_ref[...] = jnp.broadcast_to(jnp.where(row == HEAD_DIM, 1.0, 0.0).astype(_F32), acc_ref.shape)


def _part(j, item):
    part = item[3][0].shape[1]
    return slice(j * part, (j + 1) * part)


def _scores(scr, slot, j, item):
    s_ref, mc_ref = scr[0], scr[1]
    _, k, qt, _, bias = item
    kj = k[_part(j, item)]
    s = jnp.concatenate([jnp.dot(kj[r:r + SCORE_ROWS], qt, preferred_element_type=_F32)
                         for r in range(0, kj.shape[0], SCORE_ROWS)], axis=0)
    if bias is not None:
        s = s + bias[_part(j, item)]
    s_ref[slot, _part(j, item)] = s
    mx = jnp.max(s, axis=0, keepdims=True)
    mc_ref[slot] = mx if j == 0 else jnp.maximum(mc_ref[slot], mx)


def _new_max(scr, s_slot, p_slot, st):
    mc_ref, al_ref, m_ref = scr[1], scr[3], scr[4]
    m_old = m_ref[st]
    m_new = jnp.maximum(m_old, mc_ref[s_slot])
    al_ref[p_slot] = jnp.exp2(m_old - m_new)
    m_ref[st] = m_new


def _softmax(scr, s_slot, p_slot, j, item):
    s_ref, p_ref, m_ref = scr[0], scr[2], scr[4]
    rows = _part(j, item)
    p_ref[p_slot, rows] = jnp.exp2(s_ref[s_slot, rows] - m_ref[item[0]]).astype(_BF16)


def _values(scr, slot, j, item):
    p_ref, al_ref, acc_ref = scr[2], scr[3], scr[5]
    st, vts = item[0], item[3]
    pv = jnp.dot(vts[j], p_ref[slot, _part(j, item)], preferred_element_type=_F32)
    acc = acc_ref[st]
    acc_ref[st] = (al_ref[slot] * acc if j == 0 else acc) + pv


def _pipeline(scr, items):
    n = len(items)
    n_parts = len(items[0][3])
    for i0 in range(min(SCORE_AHEAD, n)):
        for j in range(n_parts):
            _scores(scr, i0 % S_SLOTS, j, items[i0])
    for i in range(n + VALUE_BEHIND):
        if i < n:
            _new_max(scr, i % S_SLOTS, i % P_SLOTS, items[i][0])
        for j in range(n_parts):
            if i + SCORE_AHEAD < n:
                _scores(scr, (i + SCORE_AHEAD) % S_SLOTS, j, items[i + SCORE_AHEAD])
            if i < n:
                _softmax(scr, i % S_SLOTS, i % P_SLOTS, j, items[i])
            if i >= VALUE_BEHIND:
                _values(scr, (i - VALUE_BEHIND) % P_SLOTS, j, items[i - VALUE_BEHIND])


def _finish(scr, first):
    acc_ref = scr[5]
    ot = jnp.concatenate([acc_ref[first + h, 0:HEAD_DIM] / acc_ref[first + h, HEAD_DIM:HEAD_DIM + 1]
                          for h in range(KV_GROUP)], axis=0)
    return ot.T


def _heads(qt_ref, g=None):
    base = 0 if g is None else g * GROUP_COLS
    return [qt_ref[0, base + h * HEAD_DIM: base + (h + 1) * HEAD_DIM, :] for h in range(KV_GROUP)]


def _vt_part(vt_ref, lead, piece, part=KV_SUB):
    return jnp.concatenate([vt_ref[lead + (piece + i,)] for i in range(part // V_PIECE)], axis=1)


def _query_blocks(qt_ref, tq):
    return [[qt_ref[0, h * HEAD_DIM:(h + 1) * HEAD_DIM, b * tq:(b + 1) * tq] for h in range(KV_GROUP)]
            for b in range(qt_ref.shape[2] // tq)]


def _chunk_items(k_ref, vt_ref, blocks, tk, part=KV_SUB):
    items = []
    for c in range(k_ref.shape[2] // tk):
        k = k_ref[0, 0, c * tk:(c + 1) * tk, :]
        vts = [_vt_part(vt_ref, (0, 0), (c * tk + j * part) // V_PIECE, part) for j in range(tk // part)]
        for b, heads in enumerate(blocks):
            items += [(b * KV_GROUP + h, k, heads[h], vts, None) for h in range(KV_GROUP)]
    return items


def _store_blocks(scr, o_ref, tq, n_blocks):
    for b in range(n_blocks):
        o_ref[0, b * tq:(b + 1) * tq, :] = _finish(scr, b * KV_GROUP).astype(o_ref.dtype)


def _full_body(qt_ref, kc_ref, vtc_ref, kl_ref, vtl_ref, o_ref, *scr, tq, tk):
    blocks = _query_blocks(qt_ref, tq)
    _init_states(scr)
    _pipeline(scr, _chunk_items(kc_ref, vtc_ref, blocks, tk) + _chunk_items(kl_ref, vtl_ref, blocks, tk))
    _store_blocks(scr, o_ref, tq, len(blocks))


def _kv_specs(lk, index_map):
    return [pl.BlockSpec((1, 1, lk, HEAD_DIM), lambda *i: index_map(*i) + (0, 0)),
            pl.BlockSpec((1, 1, lk // V_PIECE, ACC_ROWS, V_PIECE), lambda *i: index_map(*i) + (0, 0, 0))]


def _attend_full(qt, k_ctx, vt_ctx, k_lat, vt_lat, tq, tk, n_blocks):
    b, d, l = qt.shape
    ts = tq * n_blocks
    by_group = lambda bi, g, qi: (bi, g)
    return pl.pallas_call(
        functools.partial(_full_body, tq=tq, tk=tk),
        grid=(b, N_KV_HEADS, l // ts),
        in_specs=([pl.BlockSpec((1, GROUP_COLS, ts), lambda bi, g, qi: (bi, g, qi))]
                  + _kv_specs(k_ctx.shape[2], by_group) + _kv_specs(l, by_group)),
        out_specs=pl.BlockSpec((1, ts, GROUP_COLS), lambda bi, g, qi: (bi, qi, g)),
        out_shape=jax.ShapeDtypeStruct((b, l, d), _BF16),
        scratch_shapes=_attn_scratch(KV_GROUP * n_blocks, tq, tk),
        compiler_params=_params(3),
        name="attn_full",
    )(qt, k_ctx, vt_ctx, k_lat, vt_lat)


def _band_start(q0, l, tq):
    return jnp.clip(q0 - WINDOW, 0, l - 2 * tq)


def _window_body(sink_ref, qt_ref, kc_ref, vtc_ref, kl_ref, vtl_ref, bias_ref, o_ref, *scr, tq):
    g = pl.program_id(1)
    blocks = _query_blocks(qt_ref, tq)
    _init_states(scr, [sink_ref[g * KV_GROUP + h] for _ in blocks for h in range(KV_GROUP)])
    items = _chunk_items(kc_ref, vtc_ref, blocks, 2 * tq)
    for b, heads in enumerate(blocks):
        q0 = (pl.program_id(2) * len(blocks) + b) * tq
        start = pl.multiple_of(_band_start(q0, kl_ref.shape[2], tq), V_PIECE)
        k_band = kl_ref[0, 0, pl.ds(start, 2 * tq), :]
        vt_band = [_vt_part(vtl_ref, (0, 0), start // V_PIECE + j * (KV_SUB // V_PIECE)) for j in range(2)]
        items += [(b * KV_GROUP + h, k_band, heads[h], vt_band, bias_ref[b]) for h in range(KV_GROUP)]
    _pipeline(scr, items)
    _store_blocks(scr, o_ref, tq, len(blocks))


def _window_bias(l, tq):
    q0 = (jnp.arange(l // tq) * tq)[:, None, None]
    kpos = _band_start(q0, l, tq) + jnp.arange(2 * tq)[None, :, None]
    qpos = q0 + jnp.arange(tq)[None, None, :]
    return jnp.where(jnp.abs(qpos - kpos) <= WINDOW, 0.0, NEG_INF).astype(_F32)


def _attend_window(qt, k_ctx, vt_ctx, k_lat, vt_lat, sink, tq, n_blocks):
    b, d, l = qt.shape
    lc = k_ctx.shape[2]
    ts = tq * n_blocks
    assert tq == KV_SUB and tq >= 2 * WINDOW and lc == 2 * tq and l >= 2 * tq
    by_group = lambda bi, g, qi: (bi, g)
    return pl.pallas_call(
        functools.partial(_window_body, tq=tq),
        grid=(b, N_KV_HEADS, l // ts),
        in_specs=([pl.BlockSpec(memory_space=pltpu.SMEM),
                   pl.BlockSpec((1, GROUP_COLS, ts), lambda bi, g, qi: (bi, g, qi))]
                  + _kv_specs(lc, by_group) + _kv_specs(l, by_group)
                  + [pl.BlockSpec((n_blocks, 2 * tq, tq), lambda bi, g, qi: (qi, 0, 0))]),
        out_specs=pl.BlockSpec((1, ts, GROUP_COLS), lambda bi, g, qi: (bi, qi, g)),
        out_shape=jax.ShapeDtypeStruct((b, l, d), _BF16),
        scratch_shapes=_attn_scratch(KV_GROUP * n_blocks, tq, 2 * tq),
        compiler_params=_params(3),
        name="attn_window",
    )(sink, qt, k_ctx, vt_ctx, k_lat, vt_lat, _window_bias(l, tq))


def _ctx_body(*refs, has_sink):
    if has_sink:
        sink_ref, qt_ref, k_ref, vt_ref, o_ref = refs[:5]
    else:
        qt_ref, k_ref, vt_ref, o_ref = refs[:4]
    scr = refs[5:] if has_sink else refs[4:]
    n_sub = k_ref.shape[2] // KV_SUB
    _init_states(scr, [sink_ref[i] for i in range(N_HEADS)] if has_sink else None)
    items = []
    for g in range(N_KV_HEADS):
        heads = _heads(qt_ref, g)
        k = k_ref[0, g]
        vts = [_vt_part(vt_ref, (0, g), j * (KV_SUB // V_PIECE)) for j in range(n_sub)]
        items += [(g * KV_GROUP + h, k, heads[h], vts, None) for h in range(KV_GROUP)]
    _pipeline(scr, items)
    for g in range(N_KV_HEADS):
        o_ref[0, :, g * GROUP_COLS:(g + 1) * GROUP_COLS] = _finish(scr, g * KV_GROUP).astype(o_ref.dtype)


def _attend_ctx(qt, k, vt, sink):
    b, d, l = qt.shape
    has_sink = sink is not None
    in_specs = [
        pl.BlockSpec((1, d, l), lambda bi: (bi, 0, 0)),
        pl.BlockSpec((1, N_KV_HEADS, l, HEAD_DIM), lambda bi: (bi, 0, 0, 0)),
        pl.BlockSpec((1, N_KV_HEADS, l // V_PIECE, ACC_ROWS, V_PIECE), lambda bi: (bi, 0, 0, 0, 0)),
    ]
    args = [qt, k, vt]
    if has_sink:
        in_specs = [pl.BlockSpec(memory_space=pltpu.SMEM)] + in_specs
        args = [sink] + args
    return pl.pallas_call(
        functools.partial(_ctx_body, has_sink=has_sink),
        grid=(b,),
        in_specs=in_specs,
        out_specs=pl.BlockSpec((1, l, d), lambda bi: (bi, 0, 0)),
        out_shape=jax.ShapeDtypeStruct((b, l, d), _BF16),
        scratch_shapes=_attn_scratch(N_HEADS, l, l),
        compiler_params=_params(1),
        name="attn_ctx_sink" if has_sink else "attn_ctx",
    )(*args)


def _cache_layout(cache_k, cache_v):
    b, depth, p, g, hd = cache_v.shape
    k = cache_k.transpose(1, 0, 3, 2, 4).astype(_BF16)
    vt = cache_v.reshape(b, depth, p // V_PIECE, V_PIECE, g, hd).transpose(1, 0, 4, 2, 5, 3).astype(_BF16)
    extra = jnp.zeros((ACC_ROWS - hd, V_PIECE), _BF16).at[0].set(1.0)
    return k, jnp.concatenate([vt, jnp.broadcast_to(extra, vt.shape[:4] + extra.shape)], axis=4)


def _token_tile(t, cap):
    tm = min(cap, t)
    assert t % tm == 0
    return tm


def kernel(x_prompt, x_sample, cache_k, cache_v, c, c_ctx, w_mod, b_mod, norm_g, w_qkv, w_o,
           q_norm_g, k_norm_g, sink, w_ffn_in, w_ffn_out):
    depth = w_mod.shape[0]
    d = D_MODEL
    bp, lp, _ = x_prompt.shape
    bs, ls, _ = x_sample.shape
    past = cache_k.shape[2]
    assert lp % KV_SUB == 0 and ls % KV_TILE == 0 and past % KV_TILE == 0

    w_in = w_ffn_in.astype(_BF16)
    w_out = w_ffn_out.astype(_BF16).reshape(depth, 2, N_FF_CHUNKS, FF_CHUNK, d)
    w_qkv_b = w_qkv.astype(_BF16)
    w_o_b = w_o.astype(_BF16)

    n_rows = -(-(bs + 1) // 8) * 8
    cond = jnp.concatenate([c, c_ctx[None, :], jnp.zeros((n_rows - bs - 1, d), _F32)], axis=0)
    mods = _modulation(cond, w_mod, b_mod)

    tp = bp * lp
    tm_p = _token_tile(tp, FFN_TILE)
    tq_p = _token_tile(lp, QKV_TILE)
    ctx_row = lambda i: bs
    qg_p = jnp.broadcast_to(q_norm_g[:, :, None], (depth, HEAD_DIM, tq_p))
    kg_p = jnp.broadcast_to(k_norm_g[:, :, None], (depth, HEAD_DIM, tq_p))
    h = x_prompt.reshape(tp, d)
    new_ks, new_vs = [], []
    for i in range(depth):
        h = _ffn(h, mods, norm_g, w_in, w_out, i, 0, ctx_row, tm_p)
        qt, ka, vt, k, v = _qkv(h, mods, norm_g, w_qkv_b, qg_p, kg_p, i, ctx_row, tq_p, lp, True)
        new_ks.append(k.reshape(bp, lp, N_KV_HEADS, HEAD_DIM))
        new_vs.append(v.reshape(bp, lp, N_KV_HEADS, HEAD_DIM))
        o = _attend_ctx(qt, ka, vt, sink[i // 2] if i % 2 == 1 else None)
        h = _ffn(h, mods, norm_g, w_in, w_out, i, 1, ctx_row, tm_p, o=o.reshape(tp, d), w_o=w_o_b)
    y_prompt = h.reshape(bp, lp, d)
    new_k = jnp.stack(new_ks, axis=1)
    new_v = jnp.stack(new_vs, axis=1)

    ts = bs * ls
    tm_s = _token_tile(ls, FFN_TILE)
    tq_s = _token_tile(ls, QKV_TILE)
    lat_row = lambda i: i // (ls // tm_s)
    lat_row_q = lambda i: i // (ls // tq_s)
    rope_t = _rope_table_t(ls)
    qg_s = jnp.broadcast_to(q_norm_g[:, :, None], (depth, HEAD_DIM, tq_s))
    kg_s = jnp.broadcast_to(k_norm_g[:, :, None], (depth, HEAD_DIM, tq_s))
    k_ctx, vt_ctx = _cache_layout(cache_k, cache_v)
    blocks_per_step = lambda want: want if ls % (want * Q_TILE) == 0 else 1
    h = x_sample.reshape(ts, d)
    for i in range(depth):
        h = _ffn(h, mods, norm_g, w_in, w_out, i, 0, lat_row, tm_s)
        qt, ka, vt = _qkv(h, mods, norm_g, w_qkv_b, qg_s, kg_s, i, lat_row_q, tq_s, ls, False, rope_t)
        if i % 2 == 0:
            o = _attend_full(qt, k_ctx[i], vt_ctx[i], ka, vt, Q_TILE, KV_TILE, blocks_per_step(FULL_Q_BLOCKS))
        else:
            o = _attend_window(qt, k_ctx[i], vt_ctx[i], ka, vt, sink[i // 2], Q_TILE,
                               blocks_per_step(WINDOW_Q_BLOCKS))
        h = _ffn(h, mods, norm_g, w_in, w_out, i, 1, lat_row, tm_s, o=o.reshape(ts, d), w_o=w_o_b)
    y_sample = h.reshape(bs, ls, d)

    return (y_prompt, y_sample, new_k, new_v)
```

```python
import functools

import jax
import jax.numpy as jnp
from jax import lax
from jax.experimental import pallas as pl
from jax.experimental.pallas import tpu as pltpu

D_MODEL = 1024
N_HEADS = 16
N_KV_HEADS = 4
HEAD_DIM = 64
KV_GROUP = N_HEADS // N_KV_HEADS
GROUP_COLS = KV_GROUP * HEAD_DIM
KV_COLS = N_KV_HEADS * HEAD_DIM
QKV_COLS = (N_HEADS + 2 * N_KV_HEADS) * HEAD_DIM
D_FF = 2816
N_MOD = 9
GRID_W = 64
WINDOW = 128
ROPE_FREQS = HEAD_DIM // 4
ROPE_THETA = 10000.0
ATTN_SCALE = HEAD_DIM ** -0.5
LOG2_E = 1.4426950408889634
Q_SCALE = ATTN_SCALE * LOG2_E
EPS = 1e-6
NEG_INF = -1e30

FF_CHUNK = 256
N_FF_CHUNKS = D_FF // FF_CHUNK
FFN_TILE = 1024
QKV_TILE = 1024
QKV_SUB = 512
Q_TILE = 256
FULL_Q_BLOCKS = 4
WINDOW_Q_BLOCKS = 4
KV_TILE = 256
VMEM_LIMIT_BYTES = 52 * 1024 * 1024

_BF16 = jnp.bfloat16
_F32 = jnp.float32


def _params(n_axes):
    return pltpu.CompilerParams(dimension_semantics=("arbitrary",) * n_axes,
                                vmem_limit_bytes=VMEM_LIMIT_BYTES)


def _resident(block_shape, index_map):
    return pl.BlockSpec(block_shape, index_map, pipeline_mode=pl.Buffered(1))


def _silu(x):
    return x * jax.nn.sigmoid(x)


def _modln(x, g, shift, scale):
    y = x * lax.rsqrt(jnp.mean(x * x, axis=-1, keepdims=True) + EPS)
    return y * (g * (1.0 + scale)) + shift


def _mod_body(cond_ref, w_ref, b_ref, o_ref):
    a = _silu(cond_ref[...]).astype(_BF16)
    w = w_ref[0].astype(_BF16)
    o_ref[0] = jnp.dot(a, w, preferred_element_type=_F32) + b_ref[0]


def _modulation(cond, w_mod, b_mod):
    depth, d, n = w_mod.shape
    r = cond.shape[0]
    tn = n // 8
    out = pl.pallas_call(
        _mod_body,
        grid=(depth, n // tn),
        in_specs=[
            pl.BlockSpec((r, d), lambda i, j: (0, 0)),
            pl.BlockSpec((1, d, tn), lambda i, j: (i, 0, j)),
            pl.BlockSpec((1, 1, tn), lambda i, j: (i, 0, j)),
        ],
        out_specs=pl.BlockSpec((1, r, tn), lambda i, j: (i, 0, j)),
        out_shape=jax.ShapeDtypeStruct((depth, r, n), _F32),
        compiler_params=_params(2),
        name="modulation",
    )(cond, w_mod, b_mod.reshape(depth, 1, n))
    return out.reshape(depth, r, N_MOD, d)


def _ffn_body(*refs, sub, has_proj):
    if has_proj:
        x_ref, mods_ref, ng_ref, o_ref, wo_ref, win_ref, wout_ref, out_ref, xn_ref, acc_ref = refs
    else:
        x_ref, mods_ref, ng_ref, win_ref, wout_ref, out_ref, xn_ref, acc_ref = refs
    m = mods_ref[0, 0]
    x = x_ref[...]
    if has_proj:
        x = x + m[5:6] * jnp.dot(o_ref[...], wo_ref[0], preferred_element_type=_F32)
    k = 3 * sub
    xn_ref[...] = _modln(x, ng_ref[0, sub:sub + 1], m[k:k + 1], m[k + 1:k + 2]).astype(_BF16)
    if has_proj:
        out_ref[...] = x

    def ffn_chunk(c):
        xn = xn_ref[...]
        cols = slice(c * FF_CHUNK, (c + 1) * FF_CHUNK)
        gate = jnp.dot(xn, win_ref[0, 0, :, cols], preferred_element_type=_F32)
        up = jnp.dot(xn, win_ref[0, 0, :, D_FF + c * FF_CHUNK:D_FF + (c + 1) * FF_CHUNK],
                     preferred_element_type=_F32)
        act = (_silu(gate) * up).astype(_BF16)
        return jnp.dot(act, wout_ref[0, 0, c], preferred_element_type=_F32)

    acc_ref[...] = ffn_chunk(0)

    assert N_FF_CHUNKS % 2 == 1
    for c in range(1, N_FF_CHUNKS, 2):
        acc_ref[...] += ffn_chunk(c) + ffn_chunk(c + 1)
    h = out_ref[...] if has_proj else x_ref[...]
    out_ref[...] = h + (0.5 * m[k + 2:k + 3]) * acc_ref[...]


def _ffn(x, mods, norm_g, w_in, w_out, layer, which, row_of_tile, tm, o=None, w_o=None):
    t, d = x.shape
    sub = 2 * which
    has_proj = o is not None
    in_specs = [
        pl.BlockSpec((tm, d), lambda i: (i, 0)),
        pl.BlockSpec((1, 1, N_MOD, d), lambda i: (layer, row_of_tile(i), 0, 0)),
        pl.BlockSpec((1, 3, d), lambda i: (layer, 0, 0)),
    ]
    args = [x, mods, norm_g]
    if has_proj:
        in_specs += [pl.BlockSpec((tm, d), lambda i: (i, 0)),
                     _resident((1, d, d), lambda i: (layer, 0, 0))]
        args += [o, w_o]
    in_specs += [
        _resident((1, 1, d, 2 * D_FF), lambda i: (layer, which, 0, 0)),
        _resident((1, 1, N_FF_CHUNKS, FF_CHUNK, d), lambda i: (layer, which, 0, 0, 0)),
    ]
    args += [w_in, w_out]
    return pl.pallas_call(
        functools.partial(_ffn_body, sub=sub, has_proj=has_proj),
        grid=(t // tm,),
        in_specs=in_specs,
        out_specs=pl.BlockSpec((tm, d), lambda i: (i, 0)),
        out_shape=jax.ShapeDtypeStruct((t, d), _F32),
        scratch_shapes=[pltpu.VMEM((tm, d), _BF16), pltpu.VMEM((tm, d), _F32)],
        compiler_params=_params(1),
        name="ffn_proj" if has_proj else "ffn",
    )(*args)


def _head_norm_rope_t(yt, g_t, rope_ref):
    ss = jnp.sum(yt * yt, axis=0, keepdims=True)
    yn = (yt * lax.rsqrt(ss * (1.0 / HEAD_DIM) + EPS)) * g_t
    if rope_ref is None:
        return yn
    f = ROPE_FREQS
    x1r, x2r, x1c, x2c = yn[0:f], yn[f:2 * f], yn[2 * f:3 * f], yn[3 * f:4 * f]
    cr, sr, cc, sc = rope_ref[0], rope_ref[1], rope_ref[2], rope_ref[3]
    return jnp.concatenate([x1r * cr - x2r * sr, x2r * cr + x1r * sr,
                            x1c * cc - x2c * sc, x2c * cc + x1c * sc], axis=0)


def _qkv_body(*refs, rope, keep_f32, sub):
    n_in = 7 if rope else 6
    x_ref, mods_ref, ng_ref, w_ref, qg_ref, kg_ref = refs[:6]
    rope_ref = refs[6] if rope else None
    qt_ref, ka_ref, vt_ref = refs[n_in:n_in + 3]
    m = mods_ref[0, 0]
    tm = x_ref.shape[0]
    k0 = N_HEADS * HEAD_DIM
    row = lax.broadcasted_iota(jnp.int32, (ACC_ROWS - HEAD_DIM, V_PIECE), 0)
    extra = jnp.where(row == 0, 1.0, 0.0).astype(vt_ref.dtype)

    def project(r):
        rows = slice(r * sub, (r + 1) * sub)
        a = _modln(x_ref[rows, :], ng_ref[0, 1:2], m[3:4], m[4:5]).astype(_BF16)
        return jnp.dot(a, w_ref[0], preferred_element_type=_F32)

    def finish(r, proj):
        toks = slice(r * sub, (r + 1) * sub)
        qg, kg = qg_ref[0, :, toks], kg_ref[0, :, toks]
        rope_r = None if rope_ref is None else rope_ref.at[:, :, toks]
        for c in range(k0 // GROUP_COLS):
            yt = proj[:, c * GROUP_COLS:(c + 1) * GROUP_COLS].T
            for h in range(KV_GROUP):
                qn = _head_norm_rope_t(yt[h * HEAD_DIM:(h + 1) * HEAD_DIM], qg, rope_r)
                r0 = c * GROUP_COLS + h * HEAD_DIM
                qt_ref[0, r0:r0 + HEAD_DIM, toks] = (qn * Q_SCALE).astype(qt_ref.dtype)
        kt = proj[:, k0:k0 + KV_COLS].T
        kn = jnp.concatenate([_head_norm_rope_t(kt[g * HEAD_DIM:(g + 1) * HEAD_DIM], kg, rope_r)
                              for g in range(N_KV_HEADS)], axis=0)
        k_tok = kn.T
        v_tok = proj[:, k0 + KV_COLS:]
        vt = v_tok.T
        for g in range(N_KV_HEADS):
            ka_ref[0, g, toks, :] = k_tok[:, g * HEAD_DIM:(g + 1) * HEAD_DIM].astype(ka_ref.dtype)
            for j in range(sub // V_PIECE):
                piece = r * (sub // V_PIECE) + j
                vt_ref[0, g, piece, 0:HEAD_DIM, :] = (
                    vt[g * HEAD_DIM:(g + 1) * HEAD_DIM, j * V_PIECE:(j + 1) * V_PIECE].astype(vt_ref.dtype))
                vt_ref[0, g, piece, HEAD_DIM:ACC_ROWS, :] = extra
        if keep_f32:
            k_ref, v_ref = refs[n_in + 3:n_in + 5]
            k_ref[toks, :] = k_tok
            v_ref[toks, :] = v_tok

    n_sub = tm // sub
    proj = project(0)
    for r in range(n_sub):
        nxt = project(r + 1) if r + 1 < n_sub else None
        finish(r, proj)
        proj = nxt


def _qkv(x, mods, norm_g, w_qkv, qg_t, kg_t, layer, row_of_tile, tm, seq_len, keep_f32, rope_t=None):
    t, d = x.shape
    rope = rope_t is not None
    tiles_per_seq = seq_len // tm
    n_seq = t // seq_len
    seq_tile = lambda i: (i // tiles_per_seq, 0, i % tiles_per_seq)
    out_specs = [pl.BlockSpec((1, N_HEADS * HEAD_DIM, tm), seq_tile),
                 pl.BlockSpec((1, N_KV_HEADS, tm, HEAD_DIM), lambda i: seq_tile(i) + (0,)),
                 pl.BlockSpec((1, N_KV_HEADS, tm // V_PIECE, ACC_ROWS, V_PIECE), lambda i: seq_tile(i) + (0, 0))]
    out_shape = [jax.ShapeDtypeStruct((n_seq, N_HEADS * HEAD_DIM, seq_len), _BF16),
                 jax.ShapeDtypeStruct((n_seq, N_KV_HEADS, seq_len, HEAD_DIM), _BF16),
                 jax.ShapeDtypeStruct((n_seq, N_KV_HEADS, seq_len // V_PIECE, ACC_ROWS, V_PIECE), _BF16)]
    if keep_f32:
        out_specs += [pl.BlockSpec((tm, KV_COLS), lambda i: (i, 0))] * 2
        out_shape += [jax.ShapeDtypeStruct((t, KV_COLS), _F32)] * 2
    in_specs = [
        pl.BlockSpec((tm, d), lambda i: (i, 0)),
        pl.BlockSpec((1, 1, N_MOD, d), lambda i: (layer, row_of_tile(i), 0, 0)),
        pl.BlockSpec((1, 3, d), lambda i: (layer, 0, 0)),
        _resident((1, d, QKV_COLS), lambda i: (layer, 0, 0)),
        pl.BlockSpec((1, HEAD_DIM, tm), lambda i: (layer, 0, 0)),
        pl.BlockSpec((1, HEAD_DIM, tm), lambda i: (layer, 0, 0)),
    ]
    args = [x, mods, norm_g, w_qkv, qg_t, kg_t]
    if rope:
        in_specs += [pl.BlockSpec((4, ROPE_FREQS, tm), lambda i: (0, 0, i % tiles_per_seq))]
        args += [rope_t]
    return pl.pallas_call(
        functools.partial(_qkv_body, rope=rope, keep_f32=keep_f32, sub=min(tm, QKV_SUB)),
        grid=(t // tm,),
        in_specs=in_specs,
        out_specs=out_specs,
        out_shape=out_shape,
        compiler_params=_params(1),
        name="qkv_rope" if rope else "qkv",
    )(*args)


def _rope_table_t(n_tok):
    pos = jnp.arange(n_tok)
    row = (pos // GRID_W).astype(_F32)
    col = (pos % GRID_W).astype(_F32)
    freqs = 1.0 / jnp.power(ROPE_THETA, jnp.arange(ROPE_FREQS, dtype=_F32) / ROPE_FREQS)
    ang_r = row[:, None] * freqs
    ang_c = col[:, None] * freqs
    return jnp.stack([jnp.cos(ang_r).T, jnp.sin(ang_r).T, jnp.cos(ang_c).T, jnp.sin(ang_c).T])


KV_SUB = 256
V_PIECE = 128
SCORE_ROWS = 128
BF16_SUBLANES = 16
ACC_ROWS = HEAD_DIM + BF16_SUBLANES


SCORE_AHEAD = 2
VALUE_BEHIND = 1
S_SLOTS = SCORE_AHEAD + 1
P_SLOTS = VALUE_BEHIND + 1


def _attn_scratch(n_states, tq, tk):
    return [pltpu.VMEM((S_SLOTS, tk, tq), _F32),
            pltpu.VMEM((S_SLOTS, 1, tq), _F32),
            pltpu.VMEM((P_SLOTS, tk, tq), _BF16),
            pltpu.VMEM((P_SLOTS, 1, tq), _F32),
            pltpu.VMEM((n_states, 1, tq), _F32),
            pltpu.VMEM((n_states, ACC_ROWS, tq), _F32)]


def _init_states(scr, sinks=None):
    m_ref, acc_ref = scr[4], scr[5]
    tq = m_ref.shape[-1]
    if sinks is None:
        m_ref[...] = jnp.full(m_ref.shape, NEG_INF, _F32)
        acc_ref[...] = jnp.zeros(acc_ref.shape, _F32)
    else:
        for i, sk in enumerate(sinks):
            m_ref[i] = jnp.full((1, tq), sk * LOG2_E, _F32)
        row = lax.broadcasted_iota(jnp.int32, (ACC_ROWS, tq), 0)
        acc_ref[...] = jnp.broadcast_to(jnp.where(row == HEAD_DIM, 1.0, 0.0).astype(_F32), acc_ref.shape)


def _part(j, item):
    part = item[3][0].shape[1]
    return slice(j * part, (j + 1) * part)


def _scores(scr, slot, j, item):
    s_ref, mc_ref = scr[0], scr[1]
    _, k, qt, _, bias = item
    kj = k[_part(j, item)]
    s = jnp.concatenate([jnp.dot(kj[r:r + SCORE_ROWS], qt, preferred_element_type=_F32)
                         for r in range(0, kj.shape[0], SCORE_ROWS)], axis=0)
    if bias is not None:
        s = s + bias[_part(j, item)]
    s_ref[slot, _part(j, item)] = s
    mx = jnp.max(s, axis=0, keepdims=True)
    mc_ref[slot] = mx if j == 0 else jnp.maximum(mc_ref[slot], mx)


def _new_max(scr, s_slot, p_slot, st):
    mc_ref, al_ref, m_ref = scr[1], scr[3], scr[4]
    m_old = m_ref[st]
    m_new = jnp.maximum(m_old, mc_ref[s_slot])
    al_ref[p_slot] = jnp.exp2(m_old - m_new)
    m_ref[st] = m_new


def _softmax(scr, s_slot, p_slot, j, item):
    s_ref, p_ref, m_ref = scr[0], scr[2], scr[4]
    rows = _part(j, item)
    p_ref[p_slot, rows] = jnp.exp2(s_ref[s_slot, rows] - m_ref[item[0]]).astype(_BF16)


def _values(scr, slot, j, item):
    p_ref, al_ref, acc_ref = scr[2], scr[3], scr[5]
    st, vts = item[0], item[3]
    pv = jnp.dot(vts[j], p_ref[slot, _part(j, item)], preferred_element_type=_F32)
    acc = acc_ref[st]
    acc_ref[st] = (al_ref[slot] * acc if j == 0 else acc) + pv


def _pipeline(scr, items):
    n = len(items)
    n_parts = len(items[0][3])
    for i0 in range(min(SCORE_AHEAD, n)):
        for j in range(n_parts):
            _scores(scr, i0 % S_SLOTS, j, items[i0])
    for i in range(n + VALUE_BEHIND):
        if i < n:
            _new_max(scr, i % S_SLOTS, i % P_SLOTS, items[i][0])
        for j in range(n_parts):
            if i + SCORE_AHEAD < n:
                _scores(scr, (i + SCORE_AHEAD) % S_SLOTS, j, items[i + SCORE_AHEAD])
            if i < n:
                _softmax(scr, i % S_SLOTS, i % P_SLOTS, j, items[i])
            if i >= VALUE_BEHIND:
                _values(scr, (i - VALUE_BEHIND) % P_SLOTS, j, items[i - VALUE_BEHIND])


def _finish(scr, first):
    acc_ref = scr[5]
    ot = jnp.concatenate([acc_ref[first + h, 0:HEAD_DIM] / acc_ref[first + h, HEAD_DIM:HEAD_DIM + 1]
                          for h in range(KV_GROUP)], axis=0)
    return ot.T


def _heads(qt_ref, g=None):
    base = 0 if g is None else g * GROUP_COLS
    return [qt_ref[0, base + h * HEAD_DIM: base + (h + 1) * HEAD_DIM, :] for h in range(KV_GROUP)]


def _vt_part(vt_ref, lead, piece, part=KV_SUB):
    return jnp.concatenate([vt_ref[lead + (piece + i,)] for i in range(part // V_PIECE)], axis=1)


def _query_blocks(qt_ref, tq):
    return [[qt_ref[0, h * HEAD_DIM:(h + 1) * HEAD_DIM, b * tq:(b + 1) * tq] for h in range(KV_GROUP)]
            for b in range(qt_ref.shape[2] // tq)]


def _chunk_items(k_ref, vt_ref, blocks, tk, part=KV_SUB):
    items = []
    for c in range(k_ref.shape[2] // tk):
        k = k_ref[0, 0, c * tk:(c + 1) * tk, :]
        vts = [_vt_part(vt_ref, (0, 0), (c * tk + j * part) // V_PIECE, part) for j in range(tk // part)]
        for b, heads in enumerate(blocks):
            items += [(b * KV_GROUP + h, k, heads[h], vts, None) for h in range(KV_GROUP)]
    return items


def _store_blocks(scr, o_ref, tq, n_blocks):
    for b in range(n_blocks):
        o_ref[0, b * tq:(b + 1) * tq, :] = _finish(scr, b * KV_GROUP).astype(o_ref.dtype)


def _full_body(qt_ref, kc_ref, vtc_ref, kl_ref, vtl_ref, o_ref, *scr, tq, tk):
    blocks = _query_blocks(qt_ref, tq)
    _init_states(scr)
    _pipeline(scr, _chunk_items(kc_ref, vtc_ref, blocks, tk) + _chunk_items(kl_ref, vtl_ref, blocks, tk))
    _store_blocks(scr, o_ref, tq, len(blocks))


def _kv_specs(lk, index_map):
    return [pl.BlockSpec((1, 1, lk, HEAD_DIM), lambda *i: index_map(*i) + (0, 0)),
            pl.BlockSpec((1, 1, lk // V_PIECE, ACC_ROWS, V_PIECE), lambda *i: index_map(*i) + (0, 0, 0))]


def _attend_full(qt, k_ctx, vt_ctx, k_lat, vt_lat, tq, tk, n_blocks):
    b, d, l = qt.shape
    ts = tq * n_blocks
    by_group = lambda bi, g, qi: (bi, g)
    return pl.pallas_call(
        functools.partial(_full_body, tq=tq, tk=tk),
        grid=(b, N_KV_HEADS, l // ts),
        in_specs=([pl.BlockSpec((1, GROUP_COLS, ts), lambda bi, g, qi: (bi, g, qi))]
                  + _kv_specs(k_ctx.shape[2], by_group) + _kv_specs(l, by_group)),
        out_specs=pl.BlockSpec((1, ts, GROUP_COLS), lambda bi, g, qi: (bi, qi, g)),
        out_shape=jax.ShapeDtypeStruct((b, l, d), _BF16),
        scratch_shapes=_attn_scratch(KV_GROUP * n_blocks, tq, tk),
        compiler_params=_params(3),
        name="attn_full",
    )(qt, k_ctx, vt_ctx, k_lat, vt_lat)


def _band_start(q0, l, tq):
    return jnp.clip(q0 - WINDOW, 0, l - 2 * tq)


def _window_body(sink_ref, qt_ref, kc_ref, vtc_ref, kl_ref, vtl_ref, bias_ref, o_ref, *scr, tq):
    g = pl.program_id(1)
    blocks = _query_blocks(qt_ref, tq)
    _init_states(scr, [sink_ref[g * KV_GROUP + h] for _ in blocks for h in range(KV_GROUP)])
    items = _chunk_items(kc_ref, vtc_ref, blocks, 2 * tq)
    for b, heads in enumerate(blocks):
        q0 = (pl.program_id(2) * len(blocks) + b) * tq
        start = pl.multiple_of(_band_start(q0, kl_ref.shape[2], tq), V_PIECE)
        k_band = kl_ref[0, 0, pl.ds(start, 2 * tq), :]
        vt_band = [_vt_part(vtl_ref, (0, 0), start // V_PIECE + j * (KV_SUB // V_PIECE)) for j in range(2)]
        items += [(b * KV_GROUP + h, k_band, heads[h], vt_band, bias_ref[b]) for h in range(KV_GROUP)]
    _pipeline(scr, items)
    _store_blocks(scr, o_ref, tq, len(blocks))


def _window_bias(l, tq):
    q0 = (jnp.arange(l // tq) * tq)[:, None, None]
    kpos = _band_start(q0, l, tq) + jnp.arange(2 * tq)[None, :, None]
    qpos = q0 + jnp.arange(tq)[None, None, :]
    return jnp.where(jnp.abs(qpos - kpos) <= WINDOW, 0.0, NEG_INF).astype(_F32)


def _attend_window(qt, k_ctx, vt_ctx, k_lat, vt_lat, sink, tq, n_blocks):
    b, d, l = qt.shape
    lc = k_ctx.shape[2]
    ts = tq * n_blocks
    assert tq == KV_SUB and tq >= 2 * WINDOW and lc == 2 * tq and l >= 2 * tq
    by_group = lambda bi, g, qi: (bi, g)
    return pl.pallas_call(
        functools.partial(_window_body, tq=tq),
        grid=(b, N_KV_HEADS, l // ts),
        in_specs=([pl.BlockSpec(memory_space=pltpu.SMEM),
                   pl.BlockSpec((1, GROUP_COLS, ts), lambda bi, g, qi: (bi, g, qi))]
                  + _kv_specs(lc, by_group) + _kv_specs(l, by_group)
                  + [pl.BlockSpec((n_blocks, 2 * tq, tq), lambda bi, g, qi: (qi, 0, 0))]),
        out_specs=pl.BlockSpec((1, ts, GROUP_COLS), lambda bi, g, qi: (bi, qi, g)),
        out_shape=jax.ShapeDtypeStruct((b, l, d), _BF16),
        scratch_shapes=_attn_scratch(KV_GROUP * n_blocks, tq, 2 * tq),
        compiler_params=_params(3),
        name="attn_window",
    )(sink, qt, k_ctx, vt_ctx, k_lat, vt_lat, _window_bias(l, tq))


def _ctx_body(*refs, has_sink):
    if has_sink:
        sink_ref, qt_ref, k_ref, vt_ref, o_ref = refs[:5]
    else:
        qt_ref, k_ref, vt_ref, o_ref = refs[:4]
    scr = refs[5:] if has_sink else refs[4:]
    n_sub = k_ref.shape[2] // KV_SUB
    _init_states(scr, [sink_ref[i] for i in range(N_HEADS)] if has_sink else None)
    items = []
    for g in range(N_KV_HEADS):
        heads = _heads(qt_ref, g)
        k = k_ref[0, g]
        vts = [_vt_part(vt_ref, (0, g), j * (KV_SUB // V_PIECE)) for j in range(n_sub)]
        items += [(g * KV_GROUP + h, k, heads[h], vts, None) for h in range(KV_GROUP)]
    _pipeline(scr, items)
    for g in range(N_KV_HEADS):
        o_ref[0, :, g * GROUP_COLS:(g + 1) * GROUP_COLS] = _finish(scr, g * KV_GROUP).astype(o_ref.dtype)


def _attend_ctx(qt, k, vt, sink):
    b, d, l = qt.shape
    has_sink = sink is not None
    in_specs = [
        pl.BlockSpec((1, d, l), lambda bi: (bi, 0, 0)),
        pl.BlockSpec((1, N_KV_HEADS, l, HEAD_DIM), lambda bi: (bi, 0, 0, 0)),
        pl.BlockSpec((1, N_KV_HEADS, l // V_PIECE, ACC_ROWS, V_PIECE), lambda bi: (bi, 0, 0, 0, 0)),
    ]
    args = [qt, k, vt]
    if has_sink:
        in_specs = [pl.BlockSpec(memory_space=pltpu.SMEM)] + in_specs
        args = [sink] + args
    return pl.pallas_call(
        functools.partial(_ctx_body, has_sink=has_sink),
        grid=(b,),
        in_specs=in_specs,
        out_specs=pl.BlockSpec((1, l, d), lambda bi: (bi, 0, 0)),
        out_shape=jax.ShapeDtypeStruct((b, l, d), _BF16),
        scratch_shapes=_attn_scratch(N_HEADS, l, l),
        compiler_params=_params(1),
        name="attn_ctx_sink" if has_sink else "attn_ctx",
    )(*args)


def _cache_layout(cache_k, cache_v):
    b, depth, p, g, hd = cache_v.shape
    k = cache_k.transpose(1, 0, 3, 2, 4).astype(_BF16)
    vt = cache_v.reshape(b, depth, p // V_PIECE, V_PIECE, g, hd).transpose(1, 0, 4, 2, 5, 3).astype(_BF16)
    extra = jnp.zeros((ACC_ROWS - hd, V_PIECE), _BF16).at[0].set(1.0)
    return k, jnp.concatenate([vt, jnp.broadcast_to(extra, vt.shape[:4] + extra.shape)], axis=4)


def _token_tile(t, cap):
    tm = min(cap, t)
    assert t % tm == 0
    return tm


def kernel(x_prompt, x_sample, cache_k, cache_v, c, c_ctx, w_mod, b_mod, norm_g, w_qkv, w_o,
           q_norm_g, k_norm_g, sink, w_ffn_in, w_ffn_out):
    depth = w_mod.shape[0]
    d = D_MODEL
    bp, lp, _ = x_prompt.shape
    bs, ls, _ = x_sample.shape
    past = cache_k.shape[2]
    assert lp % KV_SUB == 0 and ls % KV_TILE == 0 and past % KV_TILE == 0

    w_in = w_ffn_in.astype(_BF16)
    w_out = w_ffn_out.astype(_BF16).reshape(depth, 2, N_FF_CHUNKS, FF_CHUNK, d)
    w_qkv_b = w_qkv.astype(_BF16)
    w_o_b = w_o.astype(_BF16)

    n_rows = -(-(bs + 1) // 8) * 8
    cond = jnp.concatenate([c, c_ctx[None, :], jnp.zeros((n_rows - bs - 1, d), _F32)], axis=0)
    mods = _modulation(cond, w_mod, b_mod)

    tp = bp * lp
    tm_p = _token_tile(tp, FFN_TILE)
    tq_p = _token_tile(lp, QKV_TILE)
    ctx_row = lambda i: bs
    qg_p = jnp.broadcast_to(q_norm_g[:, :, None], (depth, HEAD_DIM, tq_p))
    kg_p = jnp.broadcast_to(k_norm_g[:, :, None], (depth, HEAD_DIM, tq_p))
    h = x_prompt.reshape(tp, d)
    new_ks, new_vs = [], []
    for i in range(depth):
        h = _ffn(h, mods, norm_g, w_in, w_out, i, 0, ctx_row, tm_p)
        qt, ka, vt, k, v = _qkv(h, mods, norm_g, w_qkv_b, qg_p, kg_p, i, ctx_row, tq_p, lp, True)
        new_ks.append(k.reshape(bp, lp, N_KV_HEADS, HEAD_DIM))
        new_vs.append(v.reshape(bp, lp, N_KV_HEADS, HEAD_DIM))
        o = _attend_ctx(qt, ka, vt, sink[i // 2] if i % 2 == 1 else None)
        h = _ffn(h, mods, norm_g, w_in, w_out, i, 1, ctx_row, tm_p, o=o.reshape(tp, d), w_o=w_o_b)
    y_prompt = h.reshape(bp, lp, d)
    new_k = jnp.stack(new_ks, axis=1)
    new_v = jnp.stack(new_vs, axis=1)

    ts = bs * ls
    tm_s = _token_tile(ls, FFN_TILE)
    tq_s = _token_tile(ls, QKV_TILE)
    lat_row = lambda i: i // (ls // tm_s)
    lat_row_q = lambda i: i // (ls // tq_s)
    rope_t = _rope_table_t(ls)
    qg_s = jnp.broadcast_to(q_norm_g[:, :, None], (depth, HEAD_DIM, tq_s))
    kg_s = jnp.broadcast_to(k_norm_g[:, :, None], (depth, HEAD_DIM, tq_s))
    k_ctx, vt_ctx = _cache_layout(cache_k, cache_v)
    blocks_per_step = lambda want: want if ls % (want * Q_TILE) == 0 else 1
    h = x_sample.reshape(ts, d)
    for i in range(depth):
        h = _ffn(h, mods, norm_g, w_in, w_out, i, 0, lat_row, tm_s)
        qt, ka, vt = _qkv(h, mods, norm_g, w_qkv_b, qg_s, kg_s, i, lat_row_q, tq_s, ls, False, rope_t)
        if i % 2 == 0:
            o = _attend_full(qt, k_ctx[i], vt_ctx[i], ka, vt, Q_TILE, KV_TILE, blocks_per_step(FULL_Q_BLOCKS))
        else:
            o = _attend_window(qt, k_ctx[i], vt_ctx[i], ka, vt, sink[i // 2], Q_TILE,
                               blocks_per_step(WINDOW_Q_BLOCKS))
        h = _ffn(h, mods, norm_g, w_in, w_out, i, 1, lat_row, tm_s, o=o.reshape(ts, d), w_o=w_o_b)
    y_sample = h.reshape(bs, ls, d)

    return (y_prompt, y_sample, new_k, new_v)
```

```python
import functools

import jax
import jax.numpy as jnp
from jax import lax
from jax.experimental import pallas as pl
from jax.experimental.pallas import tpu as pltpu

D_MODEL = 1024
N_HEADS = 16
N_KV_HEADS = 4
HEAD_DIM = 64
KV_GROUP = N_HEADS // N_KV_HEADS
GROUP_COLS = KV_GROUP * HEAD_DIM
KV_COLS = N_KV_HEADS * HEAD_DIM
QKV_COLS = (N_HEADS + 2 * N_KV_HEADS) * HEAD_DIM
D_FF = 2816
N_MOD = 9
GRID_W = 64
WINDOW = 128
ROPE_FREQS = HEAD_DIM // 4
ROPE_THETA = 10000.0
ATTN_SCALE = HEAD_DIM ** -0.5
LOG2_E = 1.4426950408889634
Q_SCALE = ATTN_SCALE * LOG2_E
EPS = 1e-6
NEG_INF = -1e30

FF_CHUNK = 256
N_FF_CHUNKS = D_FF // FF_CHUNK
FFN_TILE = 1024
QKV_TILE = 1024
QKV_SUB = 512
Q_TILE = 256
FULL_Q_BLOCKS = 4
WINDOW_Q_BLOCKS = 4
KV_TILE = 512
VMEM_LIMIT_BYTES = 52 * 1024 * 1024

_BF16 = jnp.bfloat16
_F32 = jnp.float32


def _params(n_axes):
    return pltpu.CompilerParams(dimension_semantics=("arbitrary",) * n_axes,
                                vmem_limit_bytes=VMEM_LIMIT_BYTES)


def _resident(block_shape, index_map):
    return pl.BlockSpec(block_shape, index_map, pipeline_mode=pl.Buffered(1))


def _silu(x):
    return x * jax.nn.sigmoid(x)


def _modln(x, g, shift, scale):
    y = x * lax.rsqrt(jnp.mean(x * x, axis=-1, keepdims=True) + EPS)
    return y * (g * (1.0 + scale)) + shift


def _mod_body(cond_ref, w_ref, b_ref, o_ref):
    a = _silu(cond_ref[...]).astype(_BF16)
    w = w_ref[0].astype(_BF16)
    o_ref[0] = jnp.dot(a, w, preferred_element_type=_F32) + b_ref[0]


def _modulation(cond, w_mod, b_mod):
    depth, d, n = w_mod.shape
    r = cond.shape[0]
    tn = n // 8
    out = pl.pallas_call(
        _mod_body,
        grid=(depth, n // tn),
        in_specs=[
            pl.BlockSpec((r, d), lambda i, j: (0, 0)),
            pl.BlockSpec((1, d, tn), lambda i, j: (i, 0, j)),
            pl.BlockSpec((1, 1, tn), lambda i, j: (i, 0, j)),
        ],
        out_specs=pl.BlockSpec((1, r, tn), lambda i, j: (i, 0, j)),
        out_shape=jax.ShapeDtypeStruct((depth, r, n), _F32),
        compiler_params=_params(2),
        name="modulation",
    )(cond, w_mod, b_mod.reshape(depth, 1, n))
    return out.reshape(depth, r, N_MOD, d)


def _ffn_body(*refs, sub, has_proj):
    if has_proj:
        x_ref, mods_ref, ng_ref, o_ref, wo_ref, win_ref, wout_ref, out_ref, xn_ref, acc_ref = refs
    else:
        x_ref, mods_ref, ng_ref, win_ref, wout_ref, out_ref, xn_ref, acc_ref = refs
    m = mods_ref[0, 0]
    x = x_ref[...]
    if has_proj:
        x = x + m[5:6] * jnp.dot(o_ref[...], wo_ref[0], preferred_element_type=_F32)
    k = 3 * sub
    xn_ref[...] = _modln(x, ng_ref[0, sub:sub + 1], m[k:k + 1], m[k + 1:k + 2]).astype(_BF16)
    if has_proj:
        out_ref[...] = x

    def ffn_chunk(c):
        xn = xn_ref[...]
        cols = slice(c * FF_CHUNK, (c + 1) * FF_CHUNK)
        gate = jnp.dot(xn, win_ref[0, 0, :, cols], preferred_element_type=_F32)
        up = jnp.dot(xn, win_ref[0, 0, :, D_FF + c * FF_CHUNK:D_FF + (c + 1) * FF_CHUNK],
                     preferred_element_type=_F32)
        act = (_silu(gate) * up).astype(_BF16)
        return jnp.dot(act, wout_ref[0, 0, c], preferred_element_type=_F32)

    acc_ref[...] = ffn_chunk(0)

    assert N_FF_CHUNKS % 2 == 1
    for c in range(1, N_FF_CHUNKS, 2):
        acc_ref[...] += ffn_chunk(c) + ffn_chunk(c + 1)
    h = out_ref[...] if has_proj else x_ref[...]
    out_ref[...] = h + (0.5 * m[k + 2:k + 3]) * acc_ref[...]


def _ffn(x, mods, norm_g, w_in, w_out, layer, which, row_of_tile, tm, o=None, w_o=None):
    t, d = x.shape
    sub = 2 * which
    has_proj = o is not None
    in_specs = [
        pl.BlockSpec((tm, d), lambda i: (i, 0)),
        pl.BlockSpec((1, 1, N_MOD, d), lambda i: (layer, row_of_tile(i), 0, 0)),
        pl.BlockSpec((1, 3, d), lambda i: (layer, 0, 0)),
    ]
    args = [x, mods, norm_g]
    if has_proj:
        in_specs += [pl.BlockSpec((tm, d), lambda i: (i, 0)),
                     _resident((1, d, d), lambda i: (layer, 0, 0))]
        args += [o, w_o]
    in_specs += [
        _resident((1, 1, d, 2 * D_FF), lambda i: (layer, which, 0, 0)),
        _resident((1, 1, N_FF_CHUNKS, FF_CHUNK, d), lambda i: (layer, which, 0, 0, 0)),
    ]
    args += [w_in, w_out]
    return pl.pallas_call(
        functools.partial(_ffn_body, sub=sub, has_proj=has_proj),
        grid=(t // tm,),
        in_specs=in_specs,
        out_specs=pl.BlockSpec((tm, d), lambda i: (i, 0)),
        out_shape=jax.ShapeDtypeStruct((t, d), _F32),
        scratch_shapes=[pltpu.VMEM((tm, d), _BF16), pltpu.VMEM((tm, d), _F32)],
        compiler_params=_params(1),
        name="ffn_proj" if has_proj else "ffn",
    )(*args)


def _head_norm_rope_t(yt, g_t, rope_ref):
    ss = jnp.sum(yt * yt, axis=0, keepdims=True)
    yn = (yt * lax.rsqrt(ss * (1.0 / HEAD_DIM) + EPS)) * g_t
    if rope_ref is None:
        return yn
    f = ROPE_FREQS
    x1r, x2r, x1c, x2c = yn[0:f], yn[f:2 * f], yn[2 * f:3 * f], yn[3 * f:4 * f]
    cr, sr, cc, sc = rope_ref[0], rope_ref[1], rope_ref[2], rope_ref[3]
    return jnp.concatenate([x1r * cr - x2r * sr, x2r * cr + x1r * sr,
                            x1c * cc - x2c * sc, x2c * cc + x1c * sc], axis=0)


def _qkv_body(*refs, rope, keep_f32, sub):
    n_in = 7 if rope else 6
    x_ref, mods_ref, ng_ref, w_ref, qg_ref, kg_ref = refs[:6]
    rope_ref = refs[6] if rope else None
    qt_ref, ka_ref, vt_ref = refs[n_in:n_in + 3]
    m = mods_ref[0, 0]
    tm = x_ref.shape[0]
    k0 = N_HEADS * HEAD_DIM
    row = lax.broadcasted_iota(jnp.int32, (ACC_ROWS - HEAD_DIM, V_PIECE), 0)
    extra = jnp.where(row == 0, 1.0, 0.0).astype(vt_ref.dtype)

    def project(r):
        rows = slice(r * sub, (r + 1) * sub)
        a = _modln(x_ref[rows, :], ng_ref[0, 1:2], m[3:4], m[4:5]).astype(_BF16)
        return jnp.dot(a, w_ref[0], preferred_element_type=_F32)

    def finish(r, proj):
        toks = slice(r * sub, (r + 1) * sub)
        qg, kg = qg_ref[0, :, toks], kg_ref[0, :, toks]
        rope_r = None if rope_ref is None else rope_ref.at[:, :, toks]
        for c in range(k0 // GROUP_COLS):
            yt = proj[:, c * GROUP_COLS:(c + 1) * GROUP_COLS].T
            for h in range(KV_GROUP):
                qn = _head_norm_rope_t(yt[h * HEAD_DIM:(h + 1) * HEAD_DIM], qg, rope_r)
                r0 = c * GROUP_COLS + h * HEAD_DIM
                qt_ref[0, r0:r0 + HEAD_DIM, toks] = (qn * Q_SCALE).astype(qt_ref.dtype)
        kt = proj[:, k0:k0 + KV_COLS].T
        kn = jnp.concatenate([_head_norm_rope_t(kt[g * HEAD_DIM:(g + 1) * HEAD_DIM], kg, rope_r)
                              for g in range(N_KV_HEADS)], axis=0)
        k_tok = kn.T
        v_tok = proj[:, k0 + KV_COLS:]
        vt = v_tok.T
        for g in range(N_KV_HEADS):
            ka_ref[0, g, toks, :] = k_tok[:, g * HEAD_DIM:(g + 1) * HEAD_DIM].astype(ka_ref.dtype)
            for j in range(sub // V_PIECE):
                piece = r * (sub // V_PIECE) + j
                vt_ref[0, g, piece, 0:HEAD_DIM, :] = (
                    vt[g * HEAD_DIM:(g + 1) * HEAD_DIM, j * V_PIECE:(j + 1) * V_PIECE].astype(vt_ref.dtype))
                vt_ref[0, g, piece, HEAD_DIM:ACC_ROWS, :] = extra
        if keep_f32:
            k_ref, v_ref = refs[n_in + 3:n_in + 5]
            k_ref[toks, :] = k_tok
            v_ref[toks, :] = v_tok

    n_sub = tm // sub
    proj = project(0)
    for r in range(n_sub):
        nxt = project(r + 1) if r + 1 < n_sub else None
        finish(r, proj)
        proj = nxt


def _qkv(x, mods, norm_g, w_qkv, qg_t, kg_t, layer, row_of_tile, tm, seq_len, keep_f32, rope_t=None):
    t, d = x.shape
    rope = rope_t is not None
    tiles_per_seq = seq_len // tm
    n_seq = t // seq_len
    seq_tile = lambda i: (i // tiles_per_seq, 0, i % tiles_per_seq)
    out_specs = [pl.BlockSpec((1, N_HEADS * HEAD_DIM, tm), seq_tile),
                 pl.BlockSpec((1, N_KV_HEADS, tm, HEAD_DIM), lambda i: seq_tile(i) + (0,)),
                 pl.BlockSpec((1, N_KV_HEADS, tm // V_PIECE, ACC_ROWS, V_PIECE), lambda i: seq_tile(i) + (0, 0))]
    out_shape = [jax.ShapeDtypeStruct((n_seq, N_HEADS * HEAD_DIM, seq_len), _BF16),
                 jax.ShapeDtypeStruct((n_seq, N_KV_HEADS, seq_len, HEAD_DIM), _BF16),
                 jax.ShapeDtypeStruct((n_seq, N_KV_HEADS, seq_len // V_PIECE, ACC_ROWS, V_PIECE), _BF16)]
    if keep_f32:
        out_specs += [pl.BlockSpec((tm, KV_COLS), lambda i: (i, 0))] * 2
        out_shape += [jax.ShapeDtypeStruct((t, KV_COLS), _F32)] * 2
    in_specs = [
        pl.BlockSpec((tm, d), lambda i: (i, 0)),
        pl.BlockSpec((1, 1, N_MOD, d), lambda i: (layer, row_of_tile(i), 0, 0)),
        pl.BlockSpec((1, 3, d), lambda i: (layer, 0, 0)),
        _resident((1, d, QKV_COLS), lambda i: (layer, 0, 0)),
        pl.BlockSpec((1, HEAD_DIM, tm), lambda i: (layer, 0, 0)),
        pl.BlockSpec((1, HEAD_DIM, tm), lambda i: (layer, 0, 0)),
    ]
    args = [x, mods, norm_g, w_qkv, qg_t, kg_t]
    if rope:
        in_specs += [pl.BlockSpec((4, ROPE_FREQS, tm), lambda i: (0, 0, i % tiles_per_seq))]
        args += [rope_t]
    return pl.pallas_call(
        functools.partial(_qkv_body, rope=rope, keep_f32=keep_f32, sub=min(tm, QKV_SUB)),
        grid=(t // tm,),
        in_specs=in_specs,
        out_specs=out_specs,
        out_shape=out_shape,
        compiler_params=_params(1),
        name="qkv_rope" if rope else "qkv",
    )(*args)


def _rope_table_t(n_tok):
    pos = jnp.arange(n_tok)
    row = (pos // GRID_W).astype(_F32)
    col = (pos % GRID_W).astype(_F32)
    freqs = 1.0 / jnp.power(ROPE_THETA, jnp.arange(ROPE_FREQS, dtype=_F32) / ROPE_FREQS)
    ang_r = row[:, None] * freqs
    ang_c = col[:, None] * freqs
    return jnp.stack([jnp.cos(ang_r).T, jnp.sin(ang_r).T, jnp.cos(ang_c).T, jnp.sin(ang_c).T])


KV_SUB = 256
V_PIECE = 128
SCORE_ROWS = 128
BF16_SUBLANES = 16
ACC_ROWS = HEAD_DIM + BF16_SUBLANES


SCORE_AHEAD = 2
VALUE_BEHIND = 1
S_SLOTS = SCORE_AHEAD + 1
P_SLOTS = VALUE_BEHIND + 1


def _attn_scratch(n_states, tq, tk):
    return [pltpu.VMEM((S_SLOTS, tk, tq), _F32),
            pltpu.VMEM((S_SLOTS, 1, tq), _F32),
            pltpu.VMEM((P_SLOTS, tk, tq), _BF16),
            pltpu.VMEM((P_SLOTS, 1, tq), _F32),
            pltpu.VMEM((n_states, 1, tq), _F32),
            pltpu.VMEM((n_states, ACC_ROWS, tq), _F32)]


def _init_states(scr, sinks=None):
    m_ref, acc_ref = scr[4], scr[5]
    tq = m_ref.shape[-1]
    if sinks is None:
        m_ref[...] = jnp.full(m_ref.shape, NEG_INF, _F32)
        acc_ref[...] = jnp.zeros(acc_ref.shape, _F32)
    else:
        for i, sk in enumerate(sinks):
            m_ref[i] = jnp.full((1, tq), sk * LOG2_E, _F32)
        row = lax.broadcasted_iota(jnp.int32, (ACC_ROWS, tq), 0)
        acc_ref[...] = jnp.broadcast_to(jnp.where(row == HEAD_DIM, 1.0, 0.0).astype(_F32), acc_ref.shape)


def _part(j, item):
    part = item[3][0].shape[1]
    return slice(j * part, (j + 1) * part)


def _sub(j, u, item):
    r0 = _part(j, item).start + u * SCORE_ROWS
    return slice(r0, r0 + SCORE_ROWS)


def _scores(scr, slot, j, u, item):
    s_ref, mc_ref = scr[0], scr[1]
    _, k, qt, _, bias = item
    rows = _sub(j, u, item)
    s = jnp.dot(k[rows], qt, preferred_element_type=_F32)
    if bias is not None:
        s = s + bias[rows]
    s_ref[slot, rows] = s
    mx = jnp.max(s, axis=0, keepdims=True)
    mc_ref[slot] = mx if rows.start == 0 else jnp.maximum(mc_ref[slot], mx)


def _new_max(scr, s_slot, p_slot, st):
    mc_ref, al_ref, m_ref = scr[1], scr[3], scr[4]
    m_old = m_ref[st]
    m_new = jnp.maximum(m_old, mc_ref[s_slot])
    al_ref[p_slot] = jnp.exp2(m_old - m_new)
    m_ref[st] = m_new


def _softmax(scr, s_slot, p_slot, j, u, item):
    s_ref, p_ref, m_ref = scr[0], scr[2], scr[4]
    rows = _sub(j, u, item)
    p_ref[p_slot, rows] = jnp.exp2(s_ref[s_slot, rows] - m_ref[item[0]]).astype(_BF16)


def _values(scr, slot, j, item):
    p_ref, al_ref, acc_ref = scr[2], scr[3], scr[5]
    st, vts = item[0], item[3]
    pv = jnp.dot(vts[j], p_ref[slot, _part(j, item)], preferred_element_type=_F32)
    acc = acc_ref[st]
    acc_ref[st] = (al_ref[slot] * acc if j == 0 else acc) + pv


def _pipeline(scr, items):
    n = len(items)
    n_parts = len(items[0][3])
    n_sub = items[0][3][0].shape[1] // SCORE_ROWS
    for i0 in range(min(SCORE_AHEAD, n)):
        for j in range(n_parts):
            for u in range(n_sub):
                _scores(scr, i0 % S_SLOTS, j, u, items[i0])
    for i in range(n + VALUE_BEHIND):
        if i < n:
            _new_max(scr, i % S_SLOTS, i % P_SLOTS, items[i][0])
        for j in range(n_parts):
            for u in range(n_sub):
                if i + SCORE_AHEAD < n:
                    _scores(scr, (i + SCORE_AHEAD) % S_SLOTS, j, u, items[i + SCORE_AHEAD])
                if i < n:
                    _softmax(scr, i % S_SLOTS, i % P_SLOTS, j, u, items[i])
            if i >= VALUE_BEHIND:
                _values(scr, (i - VALUE_BEHIND) % P_SLOTS, j, items[i - VALUE_BEHIND])


def _finish(scr, first):
    acc_ref = scr[5]
    ot = jnp.concatenate([acc_ref[first + h, 0:HEAD_DIM] / acc_ref[first + h, HEAD_DIM:HEAD_DIM + 1]
                          for h in range(KV_GROUP)], axis=0)
    return ot.T


def _heads(qt_ref, g=None):
    base = 0 if g is None else g * GROUP_COLS
    return [qt_ref[0, base + h * HEAD_DIM: base + (h + 1) * HEAD_DIM, :] for h in range(KV_GROUP)]


def _vt_part(vt_ref, lead, piece, part=KV_SUB):
    return jnp.concatenate([vt_ref[lead + (piece + i,)] for i in range(part // V_PIECE)], axis=1)


def _query_blocks(qt_ref, tq):
    return [[qt_ref[0, h * HEAD_DIM:(h + 1) * HEAD_DIM, b * tq:(b + 1) * tq] for h in range(KV_GROUP)]
            for b in range(qt_ref.shape[2] // tq)]


def _chunk_items(k_ref, vt_ref, blocks, tk, part=KV_SUB):
    items = []
    for c in range(k_ref.shape[2] // tk):
        k = k_ref[0, 0, c * tk:(c + 1) * tk, :]
        vts = [_vt_part(vt_ref, (0, 0), (c * tk + j * part) // V_PIECE, part) for j in range(tk // part)]
        for b, heads in enumerate(blocks):
            items += [(b * KV_GROUP + h, k, heads[h], vts, None) for h in range(KV_GROUP)]
    return items


def _store_blocks(scr, o_ref, tq, n_blocks):
    for b in range(n_blocks):
        o_ref[0, b * tq:(b + 1) * tq, :] = _finish(scr, b * KV_GROUP).astype(o_ref.dtype)


def _full_body(qt_ref, kc_ref, vtc_ref, kl_ref, vtl_ref, o_ref, *scr, tq, tk):
    blocks = _query_blocks(qt_ref, tq)
    _init_states(scr)
    _pipeline(scr, _chunk_items(kc_ref, vtc_ref, blocks, tk) + _chunk_items(kl_ref, vtl_ref, blocks, tk))
    _store_blocks(scr, o_ref, tq, len(blocks))


def _kv_specs(lk, index_map):
    return [pl.BlockSpec((1, 1, lk, HEAD_DIM), lambda *i: index_map(*i) + (0, 0)),
            pl.BlockSpec((1, 1, lk // V_PIECE, ACC_ROWS, V_PIECE), lambda *i: index_map(*i) + (0, 0, 0))]


def _attend_full(qt, k_ctx, vt_ctx, k_lat, vt_lat, tq, tk, n_blocks):
    b, d, l = qt.shape
    ts = tq * n_blocks
    by_group = lambda bi, g, qi: (bi, g)
    return pl.pallas_call(
        functools.partial(_full_body, tq=tq, tk=tk),
        grid=(b, N_KV_HEADS, l // ts),
        in_specs=([pl.BlockSpec((1, GROUP_COLS, ts), lambda bi, g, qi: (bi, g, qi))]
                  + _kv_specs(k_ctx.shape[2], by_group) + _kv_specs(l, by_group)),
        out_specs=pl.BlockSpec((1, ts, GROUP_COLS), lambda bi, g, qi: (bi, qi, g)),
        out_shape=jax.ShapeDtypeStruct((b, l, d), _BF16),
        scratch_shapes=_attn_scratch(KV_GROUP * n_blocks, tq, tk),
        compiler_params=_params(3),
        name="attn_full",
    )(qt, k_ctx, vt_ctx, k_lat, vt_lat)


def _band_start(q0, l, tq):
    return jnp.clip(q0 - WINDOW, 0, l - 2 * tq)


def _window_body(sink_ref, qt_ref, kc_ref, vtc_ref, kl_ref, vtl_ref, bias_ref, o_ref, *scr, tq):
    g = pl.program_id(1)
    blocks = _query_blocks(qt_ref, tq)
    _init_states(scr, [sink_ref[g * KV_GROUP + h] for _ in blocks for h in range(KV_GROUP)])
    items = _chunk_items(kc_ref, vtc_ref, blocks, 2 * tq)
    for b, heads in enumerate(blocks):
        q0 = (pl.program_id(2) * len(blocks) + b) * tq
        start = pl.multiple_of(_band_start(q0, kl_ref.shape[2], tq), V_PIECE)
        k_band = kl_ref[0, 0, pl.ds(start, 2 * tq), :]
        vt_band = [_vt_part(vtl_ref, (0, 0), start // V_PIECE + j * (KV_SUB // V_PIECE)) for j in range(2)]
        items += [(b * KV_GROUP + h, k_band, heads[h], vt_band, bias_ref[b]) for h in range(KV_GROUP)]
    _pipeline(scr, items)
    _store_blocks(scr, o_ref, tq, len(blocks))


def _window_bias(l, tq):
    q0 = (jnp.arange(l // tq) * tq)[:, None, None]
    kpos = _band_start(q0, l, tq) + jnp.arange(2 * tq)[None, :, None]
    qpos = q0 + jnp.arange(tq)[None, None, :]
    return jnp.where(jnp.abs(qpos - kpos) <= WINDOW, 0.0, NEG_INF).astype(_F32)


def _attend_window(qt, k_ctx, vt_ctx, k_lat, vt_lat, sink, tq, n_blocks):
    b, d, l = qt.shape
    lc = k_ctx.shape[2]
    ts = tq * n_blocks
    assert tq == KV_SUB and tq >= 2 * WINDOW and lc == 2 * tq and l >= 2 * tq
    by_group = lambda bi, g, qi: (bi, g)
    return pl.pallas_call(
        functools.partial(_window_body, tq=tq),
        grid=(b, N_KV_HEADS, l // ts),
        in_specs=([pl.BlockSpec(memory_space=pltpu.SMEM),
                   pl.BlockSpec((1, GROUP_COLS, ts), lambda bi, g, qi: (bi, g, qi))]
                  + _kv_specs(lc, by_group) + _kv_specs(l, by_group)
                  + [pl.BlockSpec((n_blocks, 2 * tq, tq), lambda bi, g, qi: (qi, 0, 0))]),
        out_specs=pl.BlockSpec((1, ts, GROUP_COLS), lambda bi, g, qi: (bi, qi, g)),
        out_shape=jax.ShapeDtypeStruct((b, l, d), _BF16),
        scratch_shapes=_attn_scratch(KV_GROUP * n_blocks, tq, 2 * tq),
        compiler_params=_params(3),
        name="attn_window",
    )(sink, qt, k_ctx, vt_ctx, k_lat, vt_lat, _window_bias(l, tq))


def _ctx_body(*refs, has_sink):
    if has_sink:
        sink_ref, qt_ref, k_ref, vt_ref, o_ref = refs[:5]
    else:
        qt_ref, k_ref, vt_ref, o_ref = refs[:4]
    scr = refs[5:] if has_sink else refs[4:]
    n_sub = k_ref.shape[2] // KV_SUB
    _init_states(scr, [sink_ref[i] for i in range(N_HEADS)] if has_sink else None)
    items = []
    for g in range(N_KV_HEADS):
        heads = _heads(qt_ref, g)
        k = k_ref[0, g]
        vts = [_vt_part(vt_ref, (0, g), j * (KV_SUB // V_PIECE)) for j in range(n_sub)]
        items += [(g * KV_GROUP + h, k, heads[h], vts, None) for h in range(KV_GROUP)]
    _pipeline(scr, items)
    for g in range(N_KV_HEADS):
        o_ref[0, :, g * GROUP_COLS:(g + 1) * GROUP_COLS] = _finish(scr, g * KV_GROUP).astype(o_ref.dtype)


def _attend_ctx(qt, k, vt, sink):
    b, d, l = qt.shape
    has_sink = sink is not None
    in_specs = [
        pl.BlockSpec((1, d, l), lambda bi: (bi, 0, 0)),
        pl.BlockSpec((1, N_KV_HEADS, l, HEAD_DIM), lambda bi: (bi, 0, 0, 0)),
        pl.BlockSpec((1, N_KV_HEADS, l // V_PIECE, ACC_ROWS, V_PIECE), lambda bi: (bi, 0, 0, 0, 0)),
    ]
    args = [qt, k, vt]
    if has_sink:
        in_specs = [pl.BlockSpec(memory_space=pltpu.SMEM)] + in_specs
        args = [sink] + args
    return pl.pallas_call(
        functools.partial(_ctx_body, has_sink=has_sink),
        grid=(b,),
        in_specs=in_specs,
        out_specs=pl.BlockSpec((1, l, d), lambda bi: (bi, 0, 0)),
        out_shape=jax.ShapeDtypeStruct((b, l, d), _BF16),
        scratch_shapes=_attn_scratch(N_HEADS, l, l),
        compiler_params=_params(1),
        name="attn_ctx_sink" if has_sink else "attn_ctx",
    )(*args)


def _cache_layout(cache_k, cache_v):
    b, depth, p, g, hd = cache_v.shape
    k = cache_k.transpose(1, 0, 3, 2, 4).astype(_BF16)
    vt = cache_v.reshape(b, depth, p // V_PIECE, V_PIECE, g, hd).transpose(1, 0, 4, 2, 5, 3).astype(_BF16)
    extra = jnp.zeros((ACC_ROWS - hd, V_PIECE), _BF16).at[0].set(1.0)
    return k, jnp.concatenate([vt, jnp.broadcast_to(extra, vt.shape[:4] + extra.shape)], axis=4)


def _token_tile(t, cap):
    tm = min(cap, t)
    assert t % tm == 0
    return tm


def kernel(x_prompt, x_sample, cache_k, cache_v, c, c_ctx, w_mod, b_mod, norm_g, w_qkv, w_o,
           q_norm_g, k_norm_g, sink, w_ffn_in, w_ffn_out):
    depth = w_mod.shape[0]
    d = D_MODEL
    bp, lp, _ = x_prompt.shape
    bs, ls, _ = x_sample.shape
    past = cache_k.shape[2]
    assert lp % KV_SUB == 0 and ls % KV_TILE == 0 and past % KV_TILE == 0

    w_in = w_ffn_in.astype(_BF16)
    w_out = w_ffn_out.astype(_BF16).reshape(depth, 2, N_FF_CHUNKS, FF_CHUNK, d)
    w_qkv_b = w_qkv.astype(_BF16)
    w_o_b = w_o.astype(_BF16)

    n_rows = -(-(bs + 1) // 8) * 8
    cond = jnp.concatenate([c, c_ctx[None, :], jnp.zeros((n_rows - bs - 1, d), _F32)], axis=0)
    mods = _modulation(cond, w_mod, b_mod)

    tp = bp * lp
    tm_p = _token_tile(tp, FFN_TILE)
    tq_p = _token_tile(lp, QKV_TILE)
    ctx_row = lambda i: bs
    qg_p = jnp.broadcast_to(q_norm_g[:, :, None], (depth, HEAD_DIM, tq_p))
    kg_p = jnp.broadcast_to(k_norm_g[:, :, None], (depth, HEAD_DIM, tq_p))
    h = x_prompt.reshape(tp, d)
    new_ks, new_vs = [], []
    for i in range(depth):
        h = _ffn(h, mods, norm_g, w_in, w_out, i, 0, ctx_row, tm_p)
        qt, ka, vt, k, v = _qkv(h, mods, norm_g, w_qkv_b, qg_p, kg_p, i, ctx_row, tq_p, lp, True)
        new_ks.append(k.reshape(bp, lp, N_KV_HEADS, HEAD_DIM))
        new_vs.append(v.reshape(bp, lp, N_KV_HEADS, HEAD_DIM))
        o = _attend_ctx(qt, ka, vt, sink[i // 2] if i % 2 == 1 else None)
        h = _ffn(h, mods, norm_g, w_in, w_out, i, 1, ctx_row, tm_p, o=o.reshape(tp, d), w_o=w_o_b)
    y_prompt = h.reshape(bp, lp, d)
    new_k = jnp.stack(new_ks, axis=1)
    new_v = jnp.stack(new_vs, axis=1)

    ts = bs * ls
    tm_s = _token_tile(ls, FFN_TILE)
    tq_s = _token_tile(ls, QKV_TILE)
    lat_row = lambda i: i // (ls // tm_s)
    lat_row_q = lambda i: i // (ls // tq_s)
    rope_t = _rope_table_t(ls)
    qg_s = jnp.broadcast_to(q_norm_g[:, :, None], (depth, HEAD_DIM, tq_s))
    kg_s = jnp.broadcast_to(k_norm_g[:, :, None], (depth, HEAD_DIM, tq_s))
    k_ctx, vt_ctx = _cache_layout(cache_k, cache_v)
    blocks_per_step = lambda want: want if ls % (want * Q_TILE) == 0 else 1
    h = x_sample.reshape(ts, d)
    for i in range(depth):
        h = _ffn(h, mods, norm_g, w_in, w_out, i, 0, lat_row, tm_s)
        qt, ka, vt = _qkv(h, mods, norm_g, w_qkv_b, qg_s, kg_s, i, lat_row_q, tq_s, ls, False, rope_t)
        if i % 2 == 0:
            o = _attend_full(qt, k_ctx[i], vt_ctx[i], ka, vt, Q_TILE, KV_TILE, blocks_per_step(FULL_Q_BLOCKS))
        else:
            o = _attend_window(qt, k_ctx[i], vt_ctx[i], ka, vt, sink[i // 2], Q_TILE,
                               blocks_per_step(WINDOW_Q_BLOCKS))
        h = _ffn(h, mods, norm_g, w_in, w_out, i, 1, lat_row, tm_s, o=o.reshape(ts, d), w_o=w_o_b)
    y_sample = h.reshape(bs, ls, d)

    return (y_prompt, y_sample, new_k, new_v)
```

```python
import functools

import jax
import jax.numpy as jnp
from jax import lax
from jax.experimental import pallas as pl
from jax.experimental.pallas import tpu as pltpu

D_MODEL = 1024
N_HEADS = 16
N_KV_HEADS = 4
HEAD_DIM = 64
KV_GROUP = N_HEADS // N_KV_HEADS
GROUP_COLS = KV_GROUP * HEAD_DIM
KV_COLS = N_KV_HEADS * HEAD_DIM
QKV_COLS = (N_HEADS + 2 * N_KV_HEADS) * HEAD_DIM
D_FF = 2816
N_MOD = 9
GRID_W = 64
WINDOW = 128
ROPE_FREQS = HEAD_DIM // 4
ROPE_THETA = 10000.0
ATTN_SCALE = HEAD_DIM ** -0.5
LOG2_E = 1.4426950408889634
Q_SCALE = ATTN_SCALE * LOG2_E
EPS = 1e-6
NEG_INF = -1e30

FF_CHUNK = 256
N_FF_CHUNKS = D_FF // FF_CHUNK
FFN_TILE = 1024
QKV_TILE = 1024
QKV_SUB = 512
Q_TILE = 256
FULL_Q_BLOCKS = 4
WINDOW_Q_BLOCKS = 4
KV_TILE = 512
VMEM_LIMIT_BYTES = 52 * 1024 * 1024

_BF16 = jnp.bfloat16
_F32 = jnp.float32


def _params(n_axes):
    return pltpu.CompilerParams(dimension_semantics=("arbitrary",) * n_axes,
                                vmem_limit_bytes=VMEM_LIMIT_BYTES)


def _resident(block_shape, index_map):
    return pl.BlockSpec(block_shape, index_map, pipeline_mode=pl.Buffered(1))


def _silu(x):
    return x * jax.nn.sigmoid(x)


def _modln(x, g, shift, scale):
    y = x * lax.rsqrt(jnp.mean(x * x, axis=-1, keepdims=True) + EPS)
    return y * (g * (1.0 + scale)) + shift


def _mod_body(cond_ref, w_ref, b_ref, o_ref):
    a = _silu(cond_ref[...]).astype(_BF16)
    w = w_ref[0].astype(_BF16)
    o_ref[0] = jnp.dot(a, w, preferred_element_type=_F32) + b_ref[0]


def _modulation(cond, w_mod, b_mod):
    depth, d, n = w_mod.shape
    r = cond.shape[0]
    tn = n // 8
    out = pl.pallas_call(
        _mod_body,
        grid=(depth, n // tn),
        in_specs=[
            pl.BlockSpec((r, d), lambda i, j: (0, 0)),
            pl.BlockSpec((1, d, tn), lambda i, j: (i, 0, j)),
            pl.BlockSpec((1, 1, tn), lambda i, j: (i, 0, j)),
        ],
        out_specs=pl.BlockSpec((1, r, tn), lambda i, j: (i, 0, j)),
        out_shape=jax.ShapeDtypeStruct((depth, r, n), _F32),
        compiler_params=_params(2),
        name="modulation",
    )(cond, w_mod, b_mod.reshape(depth, 1, n))
    return out.reshape(depth, r, N_MOD, d)


def _ffn_body(*refs, sub, has_proj):
    if has_proj:
        x_ref, mods_ref, ng_ref, o_ref, wo_ref, win_ref, wout_ref, out_ref, xn_ref, acc_ref = refs
    else:
        x_ref, mods_ref, ng_ref, win_ref, wout_ref, out_ref, xn_ref, acc_ref = refs
    m = mods_ref[0, 0]
    x = x_ref[...]
    if has_proj:
        x = x + m[5:6] * jnp.dot(o_ref[...], wo_ref[0], preferred_element_type=_F32)
    k = 3 * sub
    xn_ref[...] = _modln(x, ng_ref[0, sub:sub + 1], m[k:k + 1], m[k + 1:k + 2]).astype(_BF16)
    if has_proj:
        out_ref[...] = x

    def ffn_chunk(c):
        xn = xn_ref[...]
        cols = slice(c * FF_CHUNK, (c + 1) * FF_CHUNK)
        gate = jnp.dot(xn, win_ref[0, 0, :, cols], preferred_element_type=_F32)
        up = jnp.dot(xn, win_ref[0, 0, :, D_FF + c * FF_CHUNK:D_FF + (c + 1) * FF_CHUNK],
                     preferred_element_type=_F32)
        act = (_silu(gate) * up).astype(_BF16)
        return jnp.dot(act, wout_ref[0, 0, c], preferred_element_type=_F32)

    acc_ref[...] = ffn_chunk(0)

    assert N_FF_CHUNKS % 2 == 1
    for c in range(1, N_FF_CHUNKS, 2):
        acc_ref[...] += ffn_chunk(c) + ffn_chunk(c + 1)
    h = out_ref[...] if has_proj else x_ref[...]
    out_ref[...] = h + (0.5 * m[k + 2:k + 3]) * acc_ref[...]


def _ffn(x, mods, norm_g, w_in, w_out, layer, which, row_of_tile, tm, o=None, w_o=None):
    t, d = x.shape
    sub = 2 * which
    has_proj = o is not None
    in_specs = [
        pl.BlockSpec((tm, d), lambda i: (i, 0)),
        pl.BlockSpec((1, 1, N_MOD, d), lambda i: (layer, row_of_tile(i), 0, 0)),
        pl.BlockSpec((1, 3, d), lambda i: (layer, 0, 0)),
    ]
    args = [x, mods, norm_g]
    if has_proj:
        in_specs += [pl.BlockSpec((tm, d), lambda i: (i, 0)),
                     _resident((1, d, d), lambda i: (layer, 0, 0))]
        args += [o, w_o]
    in_specs += [
        _resident((1, 1, d, 2 * D_FF), lambda i: (layer, which, 0, 0)),
        _resident((1, 1, N_FF_CHUNKS, FF_CHUNK, d), lambda i: (layer, which, 0, 0, 0)),
    ]
    args += [w_in, w_out]
    return pl.pallas_call(
        functools.partial(_ffn_body, sub=sub, has_proj=has_proj),
        grid=(t // tm,),
        in_specs=in_specs,
        out_specs=pl.BlockSpec((tm, d), lambda i: (i, 0)),
        out_shape=jax.ShapeDtypeStruct((t, d), _F32),
        scratch_shapes=[pltpu.VMEM((tm, d), _BF16), pltpu.VMEM((tm, d), _F32)],
        compiler_params=_params(1),
        name="ffn_proj" if has_proj else "ffn",
    )(*args)


def _head_norm_rope_t(yt, g_t, rope_ref):
    ss = jnp.sum(yt * yt, axis=0, keepdims=True)
    yn = (yt * lax.rsqrt(ss * (1.0 / HEAD_DIM) + EPS)) * g_t
    if rope_ref is None:
        return yn
    f = ROPE_FREQS
    x1r, x2r, x1c, x2c = yn[0:f], yn[f:2 * f], yn[2 * f:3 * f], yn[3 * f:4 * f]
    cr, sr, cc, sc = rope_ref[0], rope_ref[1], rope_ref[2], rope_ref[3]
    return jnp.concatenate([x1r * cr - x2r * sr, x2r * cr + x1r * sr,
                            x1c * cc - x2c * sc, x2c * cc + x1c * sc], axis=0)


def _qkv_body(*refs, rope, keep_f32, sub):
    n_in = 7 if rope else 6
    x_ref, mods_ref, ng_ref, w_ref, qg_ref, kg_ref = refs[:6]
    rope_ref = refs[6] if rope else None
    qt_ref, ka_ref, vt_ref = refs[n_in:n_in + 3]
    m = mods_ref[0, 0]
    tm = x_ref.shape[0]
    k0 = N_HEADS * HEAD_DIM
    row = lax.broadcasted_iota(jnp.int32, (ACC_ROWS - HEAD_DIM, V_PIECE), 0)
    extra = jnp.where(row == 0, 1.0, 0.0).astype(vt_ref.dtype)

    def project(r):
        rows = slice(r * sub, (r + 1) * sub)
        a = _modln(x_ref[rows, :], ng_ref[0, 1:2], m[3:4], m[4:5]).astype(_BF16)
        return jnp.dot(a, w_ref[0], preferred_element_type=_F32)

    def finish(r, proj):
        toks = slice(r * sub, (r + 1) * sub)
        qg, kg = qg_ref[0, :, toks], kg_ref[0, :, toks]
        rope_r = None if rope_ref is None else rope_ref.at[:, :, toks]
        for c in range(k0 // GROUP_COLS):
            yt = proj[:, c * GROUP_COLS:(c + 1) * GROUP_COLS].T
            for h in range(KV_GROUP):
                qn = _head_norm_rope_t(yt[h * HEAD_DIM:(h + 1) * HEAD_DIM], qg, rope_r)
                r0 = c * GROUP_COLS + h * HEAD_DIM
                qt_ref[0, r0:r0 + HEAD_DIM, toks] = (qn * Q_SCALE).astype(qt_ref.dtype)
        kt = proj[:, k0:k0 + KV_COLS].T
        kn = jnp.concatenate([_head_norm_rope_t(kt[g * HEAD_DIM:(g + 1) * HEAD_DIM], kg, rope_r)
                              for g in range(N_KV_HEADS)], axis=0)
        k_tok = kn.T
        v_tok = proj[:, k0 + KV_COLS:]
        vt = v_tok.T
        for g in range(N_KV_HEADS):
            ka_ref[0, g, toks, :] = k_tok[:, g * HEAD_DIM:(g + 1) * HEAD_DIM].astype(ka_ref.dtype)
            for j in range(sub // V_PIECE):
                piece = r * (sub // V_PIECE) + j
                vt_ref[0, g, piece, 0:HEAD_DIM, :] = (
                    vt[g * HEAD_DIM:(g + 1) * HEAD_DIM, j * V_PIECE:(j + 1) * V_PIECE].astype(vt_ref.dtype))
                vt_ref[0, g, piece, HEAD_DIM:ACC_ROWS, :] = extra
        if keep_f32:
            k_ref, v_ref = refs[n_in + 3:n_in + 5]
            k_ref[toks, :] = k_tok
            v_ref[toks, :] = v_tok

    n_sub = tm // sub
    proj = project(0)
    for r in range(n_sub):
        nxt = project(r + 1) if r + 1 < n_sub else None
        finish(r, proj)
        proj = nxt


def _qkv(x, mods, norm_g, w_qkv, qg_t, kg_t, layer, row_of_tile, tm, seq_len, keep_f32, rope_t=None):
    t, d = x.shape
    rope = rope_t is not None
    tiles_per_seq = seq_len // tm
    n_seq = t // seq_len
    seq_tile = lambda i: (i // tiles_per_seq, 0, i % tiles_per_seq)
    out_specs = [pl.BlockSpec((1, N_HEADS * HEAD_DIM, tm), seq_tile),
                 pl.BlockSpec((1, N_KV_HEADS, tm, HEAD_DIM), lambda i: seq_tile(i) + (0,)),
                 pl.BlockSpec((1, N_KV_HEADS, tm // V_PIECE, ACC_ROWS, V_PIECE), lambda i: seq_tile(i) + (0, 0))]
    out_shape = [jax.ShapeDtypeStruct((n_seq, N_HEADS * HEAD_DIM, seq_len), _BF16),
                 jax.ShapeDtypeStruct((n_seq, N_KV_HEADS, seq_len, HEAD_DIM), _BF16),
                 jax.ShapeDtypeStruct((n_seq, N_KV_HEADS, seq_len // V_PIECE, ACC_ROWS, V_PIECE), _BF16)]
    if keep_f32:
        out_specs += [pl.BlockSpec((tm, KV_COLS), lambda i: (i, 0))] * 2
        out_shape += [jax.ShapeDtypeStruct((t, KV_COLS), _F32)] * 2
    in_specs = [
        pl.BlockSpec((tm, d), lambda i: (i, 0)),
        pl.BlockSpec((1, 1, N_MOD, d), lambda i: (layer, row_of_tile(i), 0, 0)),
        pl.BlockSpec((1, 3, d), lambda i: (layer, 0, 0)),
        _resident((1, d, QKV_COLS), lambda i: (layer, 0, 0)),
        pl.BlockSpec((1, HEAD_DIM, tm), lambda i: (layer, 0, 0)),
        pl.BlockSpec((1, HEAD_DIM, tm), lambda i: (layer, 0, 0)),
    ]
    args = [x, mods, norm_g, w_qkv, qg_t, kg_t]
    if rope:
        in_specs += [pl.BlockSpec((4, ROPE_FREQS, tm), lambda i: (0, 0, i % tiles_per_seq))]
        args += [rope_t]
    return pl.pallas_call(
        functools.partial(_qkv_body, rope=rope, keep_f32=keep_f32, sub=min(tm, QKV_SUB)),
        grid=(t // tm,),
        in_specs=in_specs,
        out_specs=out_specs,
        out_shape=out_shape,
        compiler_params=_params(1),
        name="qkv_rope" if rope else "qkv",
    )(*args)


def _rope_table_t(n_tok):
    pos = jnp.arange(n_tok)
    row = (pos // GRID_W).astype(_F32)
    col = (pos % GRID_W).astype(_F32)
    freqs = 1.0 / jnp.power(ROPE_THETA, jnp.arange(ROPE_FREQS, dtype=_F32) / ROPE_FREQS)
    ang_r = row[:, None] * freqs
    ang_c = col[:, None] * freqs
    return jnp.stack([jnp.cos(ang_r).T, jnp.sin(ang_r).T, jnp.cos(ang_c).T, jnp.sin(ang_c).T])


KV_SUB = 256
V_PIECE = 128
SCORE_ROWS = 128
BF16_SUBLANES = 16
ACC_ROWS = HEAD_DIM + BF16_SUBLANES


SCORE_AHEAD = 2
VALUE_BEHIND = 1
S_SLOTS = SCORE_AHEAD + 1
P_SLOTS = VALUE_BEHIND + 1


def _attn_scratch(n_states, tq, tk):
    return [pltpu.VMEM((S_SLOTS, tk, tq), _F32),
            pltpu.VMEM((S_SLOTS, 1, tq), _F32),
            pltpu.VMEM((P_SLOTS, tk, tq), _BF16),
            pltpu.VMEM((P_SLOTS, 1, tq), _F32),
            pltpu.VMEM((n_states, 1, tq), _F32),
            pltpu.VMEM((n_states, ACC_ROWS, tq), _F32)]


def _init_states(scr, sinks=None):
    m_ref, acc_ref = scr[4], scr[5]
    tq = m_ref.shape[-1]
    if sinks is None:
        m_ref[...] = jnp.full(m_ref.shape, NEG_INF, _F32)
        acc_ref[...] = jnp.zeros(acc_ref.shape, _F32)
    else:
        for i, sk in enumerate(sinks):
            m_ref[i] = jnp.full((1, tq), sk * LOG2_E, _F32)
        row = lax.broadcasted_iota(jnp.int32, (ACC_ROWS, tq), 0)
        acc_ref[...] = jnp.broadcast_to(jnp.where(row == HEAD_DIM, 1.0, 0.0).astype(_F32), acc_ref.shape)


def _part(j, item):
    part = item[3][0].shape[1]
    return slice(j * part, (j + 1) * part)


def _sub(j, u, item):
    r0 = _part(j, item).start + u * SCORE_ROWS
    return slice(r0, r0 + SCORE_ROWS)


def _scores(scr, slot, j, u, item):
    s_ref, mc_ref = scr[0], scr[1]
    _, k, qt, _, bias = item
    rows = _sub(j, u, item)
    s = jnp.dot(k[rows], qt, preferred_element_type=_F32)
    if bias is not None:
        s = s + bias[rows]
    s_ref[slot, rows] = s
    mx = jnp.max(s, axis=0, keepdims=True)
    mc_ref[slot] = mx if rows.start == 0 else jnp.maximum(mc_ref[slot], mx)


def _new_max(scr, s_slot, p_slot, st):
    mc_ref, al_ref, m_ref = scr[1], scr[3], scr[4]
    m_old = m_ref[st]
    m_new = jnp.maximum(m_old, mc_ref[s_slot])
    al_ref[p_slot] = jnp.exp2(m_old - m_new)
    m_ref[st] = m_new


def _softmax(scr, s_slot, p_slot, j, u, item):
    s_ref, p_ref, m_ref = scr[0], scr[2], scr[4]
    rows = _sub(j, u, item)
    p_ref[p_slot, rows] = jnp.exp2(s_ref[s_slot, rows] - m_ref[item[0]]).astype(_BF16)


def _values(scr, slot, j, item):
    p_ref, al_ref, acc_ref = scr[2], scr[3], scr[5]
    st, vts = item[0], item[3]
    pv = jnp.dot(vts[j], p_ref[slot, _part(j, item)], preferred_element_type=_F32)
    acc = acc_ref[st]
    acc_ref[st] = (al_ref[slot] * acc if j == 0 else acc) + pv


def _pipeline(scr, items, fine):
    n = len(items)
    n_parts = len(items[0][3])
    subs = range(items[0][3][0].shape[1] // SCORE_ROWS)
    for i0 in range(min(SCORE_AHEAD, n)):
        for j in range(n_parts):
            for u in subs:
                _scores(scr, i0 % S_SLOTS, j, u, items[i0])
    for i in range(n + VALUE_BEHIND):
        ahead = i + SCORE_AHEAD
        if i < n:
            _new_max(scr, i % S_SLOTS, i % P_SLOTS, items[i][0])
        for j in range(n_parts):
            for group in ([(u,) for u in subs] if fine else [tuple(subs)]):
                for u in group:
                    if ahead < n:
                        _scores(scr, ahead % S_SLOTS, j, u, items[ahead])
                for u in group:
                    if i < n:
                        _softmax(scr, i % S_SLOTS, i % P_SLOTS, j, u, items[i])
            if i >= VALUE_BEHIND:
                _values(scr, (i - VALUE_BEHIND) % P_SLOTS, j, items[i - VALUE_BEHIND])


def _finish(scr, first):
    acc_ref = scr[5]
    ot = jnp.concatenate([acc_ref[first + h, 0:HEAD_DIM] / acc_ref[first + h, HEAD_DIM:HEAD_DIM + 1]
                          for h in range(KV_GROUP)], axis=0)
    return ot.T


def _heads(qt_ref, g=None):
    base = 0 if g is None else g * GROUP_COLS
    return [qt_ref[0, base + h * HEAD_DIM: base + (h + 1) * HEAD_DIM, :] for h in range(KV_GROUP)]


def _vt_part(vt_ref, lead, piece, part=KV_SUB):
    return jnp.concatenate([vt_ref[lead + (piece + i,)] for i in range(part // V_PIECE)], axis=1)


def _query_blocks(qt_ref, tq):
    return [[qt_ref[0, h * HEAD_DIM:(h + 1) * HEAD_DIM, b * tq:(b + 1) * tq] for h in range(KV_GROUP)]
            for b in range(qt_ref.shape[2] // tq)]


def _chunk_items(k_ref, vt_ref, blocks, tk, part=KV_SUB):
    items = []
    for c in range(k_ref.shape[2] // tk):
        k = k_ref[0, 0, c * tk:(c + 1) * tk, :]
        vts = [_vt_part(vt_ref, (0, 0), (c * tk + j * part) // V_PIECE, part) for j in range(tk // part)]
        for b, heads in enumerate(blocks):
            items += [(b * KV_GROUP + h, k, heads[h], vts, None) for h in range(KV_GROUP)]
    return items


def _store_blocks(scr, o_ref, tq, n_blocks):
    for b in range(n_blocks):
        o_ref[0, b * tq:(b + 1) * tq, :] = _finish(scr, b * KV_GROUP).astype(o_ref.dtype)


def _full_body(qt_ref, kc_ref, vtc_ref, kl_ref, vtl_ref, o_ref, *scr, tq, tk):
    blocks = _query_blocks(qt_ref, tq)
    _init_states(scr)
    _pipeline(scr, _chunk_items(kc_ref, vtc_ref, blocks, tk) + _chunk_items(kl_ref, vtl_ref, blocks, tk),
              fine=False)
    _store_blocks(scr, o_ref, tq, len(blocks))


def _kv_specs(lk, index_map):
    return [pl.BlockSpec((1, 1, lk, HEAD_DIM), lambda *i: index_map(*i) + (0, 0)),
            pl.BlockSpec((1, 1, lk // V_PIECE, ACC_ROWS, V_PIECE), lambda *i: index_map(*i) + (0, 0, 0))]


def _attend_full(qt, k_ctx, vt_ctx, k_lat, vt_lat, tq, tk, n_blocks):
    b, d, l = qt.shape
    ts = tq * n_blocks
    by_group = lambda bi, g, qi: (bi, g)
    return pl.pallas_call(
        functools.partial(_full_body, tq=tq, tk=tk),
        grid=(b, N_KV_HEADS, l // ts),
        in_specs=([pl.BlockSpec((1, GROUP_COLS, ts), lambda bi, g, qi: (bi, g, qi))]
                  + _kv_specs(k_ctx.shape[2], by_group) + _kv_specs(l, by_group)),
        out_specs=pl.BlockSpec((1, ts, GROUP_COLS), lambda bi, g, qi: (bi, qi, g)),
        out_shape=jax.ShapeDtypeStruct((b, l, d), _BF16),
        scratch_shapes=_attn_scratch(KV_GROUP * n_blocks, tq, tk),
        compiler_params=_params(3),
        name="attn_full",
    )(qt, k_ctx, vt_ctx, k_lat, vt_lat)


def _band_start(q0, l, tq):
    return jnp.clip(q0 - WINDOW, 0, l - 2 * tq)


def _window_body(sink_ref, qt_ref, kc_ref, vtc_ref, kl_ref, vtl_ref, bias_ref, o_ref, *scr, tq):
    g = pl.program_id(1)
    blocks = _query_blocks(qt_ref, tq)
    _init_states(scr, [sink_ref[g * KV_GROUP + h] for _ in blocks for h in range(KV_GROUP)])
    items = _chunk_items(kc_ref, vtc_ref, blocks, 2 * tq)
    for b, heads in enumerate(blocks):
        q0 = (pl.program_id(2) * len(blocks) + b) * tq
        start = pl.multiple_of(_band_start(q0, kl_ref.shape[2], tq), V_PIECE)
        k_band = kl_ref[0, 0, pl.ds(start, 2 * tq), :]
        vt_band = [_vt_part(vtl_ref, (0, 0), start // V_PIECE + j * (KV_SUB // V_PIECE)) for j in range(2)]
        items += [(b * KV_GROUP + h, k_band, heads[h], vt_band, bias_ref[b]) for h in range(KV_GROUP)]
    _pipeline(scr, items, fine=True)
    _store_blocks(scr, o_ref, tq, len(blocks))


def _window_bias(l, tq):
    q0 = (jnp.arange(l // tq) * tq)[:, None, None]
    kpos = _band_start(q0, l, tq) + jnp.arange(2 * tq)[None, :, None]
    qpos = q0 + jnp.arange(tq)[None, None, :]
    return jnp.where(jnp.abs(qpos - kpos) <= WINDOW, 0.0, NEG_INF).astype(_F32)


def _attend_window(qt, k_ctx, vt_ctx, k_lat, vt_lat, sink, tq, n_blocks):
    b, d, l = qt.shape
    lc = k_ctx.shape[2]
    ts = tq * n_blocks
    assert tq == KV_SUB and tq >= 2 * WINDOW and lc == 2 * tq and l >= 2 * tq
    by_group = lambda bi, g, qi: (bi, g)
    return pl.pallas_call(
        functools.partial(_window_body, tq=tq),
        grid=(b, N_KV_HEADS, l // ts),
        in_specs=([pl.BlockSpec(memory_space=pltpu.SMEM),
                   pl.BlockSpec((1, GROUP_COLS, ts), lambda bi, g, qi: (bi, g, qi))]
                  + _kv_specs(lc, by_group) + _kv_specs(l, by_group)
                  + [pl.BlockSpec((n_blocks, 2 * tq, tq), lambda bi, g, qi: (qi, 0, 0))]),
        out_specs=pl.BlockSpec((1, ts, GROUP_COLS), lambda bi, g, qi: (bi, qi, g)),
        out_shape=jax.ShapeDtypeStruct((b, l, d), _BF16),
        scratch_shapes=_attn_scratch(KV_GROUP * n_blocks, tq, 2 * tq),
        compiler_params=_params(3),
        name="attn_window",
    )(sink, qt, k_ctx, vt_ctx, k_lat, vt_lat, _window_bias(l, tq))


def _ctx_body(*refs, has_sink):
    if has_sink:
        sink_ref, qt_ref, k_ref, vt_ref, o_ref = refs[:5]
    else:
        qt_ref, k_ref, vt_ref, o_ref = refs[:4]
    scr = refs[5:] if has_sink else refs[4:]
    n_sub = k_ref.shape[2] // KV_SUB
    _init_states(scr, [sink_ref[i] for i in range(N_HEADS)] if has_sink else None)
    items = []
    for g in range(N_KV_HEADS):
        heads = _heads(qt_ref, g)
        k = k_ref[0, g]
        vts = [_vt_part(vt_ref, (0, g), j * (KV_SUB // V_PIECE)) for j in range(n_sub)]
        items += [(g * KV_GROUP + h, k, heads[h], vts, None) for h in range(KV_GROUP)]
    _pipeline(scr, items, fine=True)
    for g in range(N_KV_HEADS):
        o_ref[0, :, g * GROUP_COLS:(g + 1) * GROUP_COLS] = _finish(scr, g * KV_GROUP).astype(o_ref.dtype)


def _attend_ctx(qt, k, vt, sink):
    b, d, l = qt.shape
    has_sink = sink is not None
    in_specs = [
        pl.BlockSpec((1, d, l), lambda bi: (bi, 0, 0)),
        pl.BlockSpec((1, N_KV_HEADS, l, HEAD_DIM), lambda bi: (bi, 0, 0, 0)),
        pl.BlockSpec((1, N_KV_HEADS, l // V_PIECE, ACC_ROWS, V_PIECE), lambda bi: (bi, 0, 0, 0, 0)),
    ]
    args = [qt, k, vt]
    if has_sink:
        in_specs = [pl.BlockSpec(memory_space=pltpu.SMEM)] + in_specs
        args = [sink] + args
    return pl.pallas_call(
        functools.partial(_ctx_body, has_sink=has_sink),
        grid=(b,),
        in_specs=in_specs,
        out_specs=pl.BlockSpec((1, l, d), lambda bi: (bi, 0, 0)),
        out_shape=jax.ShapeDtypeStruct((b, l, d), _BF16),
        scratch_shapes=_attn_scratch(N_HEADS, l, l),
        compiler_params=_params(1),
        name="attn_ctx_sink" if has_sink else "attn_ctx",
    )(*args)


def _cache_layout(cache_k, cache_v):
    b, depth, p, g, hd = cache_v.shape
    k = cache_k.transpose(1, 0, 3, 2, 4).astype(_BF16)
    vt = cache_v.reshape(b, depth, p // V_PIECE, V_PIECE, g, hd).transpose(1, 0, 4, 2, 5, 3).astype(_BF16)
    extra = jnp.zeros((ACC_ROWS - hd, V_PIECE), _BF16).at[0].set(1.0)
    return k, jnp.concatenate([vt, jnp.broadcast_to(extra, vt.shape[:4] + extra.shape)], axis=4)


def _token_tile(t, cap):
    tm = min(cap, t)
    assert t % tm == 0
    return tm


def kernel(x_prompt, x_sample, cache_k, cache_v, c, c_ctx, w_mod, b_mod, norm_g, w_qkv, w_o,
           q_norm_g, k_norm_g, sink, w_ffn_in, w_ffn_out):
    depth = w_mod.shape[0]
    d = D_MODEL
    bp, lp, _ = x_prompt.shape
    bs, ls, _ = x_sample.shape
    past = cache_k.shape[2]
    assert lp % KV_SUB == 0 and ls % KV_TILE == 0 and past % KV_TILE == 0

    w_in = w_ffn_in.astype(_BF16)
    w_out = w_ffn_out.astype(_BF16).reshape(depth, 2, N_FF_CHUNKS, FF_CHUNK, d)
    w_qkv_b = w_qkv.astype(_BF16)
    w_o_b = w_o.astype(_BF16)

    n_rows = -(-(bs + 1) // 8) * 8
    cond = jnp.concatenate([c, c_ctx[None, :], jnp.zeros((n_rows - bs - 1, d), _F32)], axis=0)
    mods = _modulation(cond, w_mod, b_mod)

    tp = bp * lp
    tm_p = _token_tile(tp, FFN_TILE)
    tq_p = _token_tile(lp, QKV_TILE)
    ctx_row = lambda i: bs
    qg_p = jnp.broadcast_to(q_norm_g[:, :, None], (depth, HEAD_DIM, tq_p))
    kg_p = jnp.broadcast_to(k_norm_g[:, :, None], (depth, HEAD_DIM, tq_p))
    h = x_prompt.reshape(tp, d)
    new_ks, new_vs = [], []
    for i in range(depth):
        h = _ffn(h, mods, norm_g, w_in, w_out, i, 0, ctx_row, tm_p)
        qt, ka, vt, k, v = _qkv(h, mods, norm_g, w_qkv_b, qg_p, kg_p, i, ctx_row, tq_p, lp, True)
        new_ks.append(k.reshape(bp, lp, N_KV_HEADS, HEAD_DIM))
        new_vs.append(v.reshape(bp, lp, N_KV_HEADS, HEAD_DIM))
        o = _attend_ctx(qt, ka, vt, sink[i // 2] if i % 2 == 1 else None)
        h = _ffn(h, mods, norm_g, w_in, w_out, i, 1, ctx_row, tm_p, o=o.reshape(tp, d), w_o=w_o_b)
    y_prompt = h.reshape(bp, lp, d)
    new_k = jnp.stack(new_ks, axis=1)
    new_v = jnp.stack(new_vs, axis=1)

    ts = bs * ls
    tm_s = _token_tile(ls, FFN_TILE)
    tq_s = _token_tile(ls, QKV_TILE)
    lat_row = lambda i: i // (ls // tm_s)
    lat_row_q = lambda i: i // (ls // tq_s)
    rope_t = _rope_table_t(ls)
    qg_s = jnp.broadcast_to(q_norm_g[:, :, None], (depth, HEAD_DIM, tq_s))
    kg_s = jnp.broadcast_to(k_norm_g[:, :, None], (depth, HEAD_DIM, tq_s))
    k_ctx, vt_ctx = _cache_layout(cache_k, cache_v)
    blocks_per_step = lambda want: want if ls % (want * Q_TILE) == 0 else 1
    h = x_sample.reshape(ts, d)
    for i in range(depth):
        h = _ffn(h, mods, norm_g, w_in, w_out, i, 0, lat_row, tm_s)
        qt, ka, vt = _qkv(h, mods, norm_g, w_qkv_b, qg_s, kg_s, i, lat_row_q, tq_s, ls, False, rope_t)
        if i % 2 == 0:
            o = _attend_full(qt, k_ctx[i], vt_ctx[i], ka, vt, Q_TILE, KV_TILE, blocks_per_step(FULL_Q_BLOCKS))
        else:
            o = _attend_window(qt, k_ctx[i], vt_ctx[i], ka, vt, sink[i // 2], Q_TILE,
                               blocks_per_step(WINDOW_Q_BLOCKS))
        h = _ffn(h, mods, norm_g, w_in, w_out, i, 1, lat_row, tm_s, o=o.reshape(ts, d), w_o=w_o_b)
    y_sample = h.reshape(bs, ls, d)

    return (y_prompt, y_sample, new_k, new_v)
```

```python
import functools

import jax
import jax.numpy as jnp
from jax import lax
from jax.experimental import pallas as pl
from jax.experimental.pallas import tpu as pltpu

D_MODEL = 1024
N_HEADS = 16
N_KV_HEADS = 4
HEAD_DIM = 64
KV_GROUP = N_HEADS // N_KV_HEADS
GROUP_COLS = KV_GROUP * HEAD_DIM
KV_COLS = N_KV_HEADS * HEAD_DIM
QKV_COLS = (N_HEADS + 2 * N_KV_HEADS) * HEAD_DIM
D_FF = 2816
N_MOD = 9
GRID_W = 64
WINDOW = 128
ROPE_FREQS = HEAD_DIM // 4
ROPE_THETA = 10000.0
ATTN_SCALE = HEAD_DIM ** -0.5
LOG2_E = 1.4426950408889634
Q_SCALE = ATTN_SCALE * LOG2_E
EPS = 1e-6
NEG_INF = -1e30

MOD_COL_TILES = 8
FF_CHUNK = 256
N_FF_CHUNKS = D_FF // FF_CHUNK
FFN_TILE = 1024
QKV_TILE = 1024
QKV_SUB = 512
Q_TILE = 256
FULL_Q_BLOCKS = 8
WINDOW_Q_BLOCKS = 8
KV_TILE = 512
VMEM_LIMIT_BYTES = 52 * 1024 * 1024

_BF16 = jnp.bfloat16
_F32 = jnp.float32


def _params(n_axes):
    return pltpu.CompilerParams(dimension_semantics=("arbitrary",) * n_axes,
                                vmem_limit_bytes=VMEM_LIMIT_BYTES)


def _resident(block_shape, index_map):
    return pl.BlockSpec(block_shape, index_map, pipeline_mode=pl.Buffered(1))


def _silu(x):
    return x * jax.nn.sigmoid(x)


def _modln(x, g, shift, scale):
    y = x * lax.rsqrt(jnp.mean(x * x, axis=-1, keepdims=True) + EPS)
    return y * (g * (1.0 + scale)) + shift


def _mod_body(cond_ref, w_ref, b_ref, o_ref):
    a = _silu(cond_ref[...]).astype(_BF16)
    w = w_ref[0].astype(_BF16)
    o_ref[0] = jnp.dot(a, w, preferred_element_type=_F32) + b_ref[0]


def _modulation(cond, w_mod, b_mod):
    depth, d, n = w_mod.shape
    r = cond.shape[0]
    tn = n // MOD_COL_TILES
    out = pl.pallas_call(
        _mod_body,
        grid=(depth, n // tn),
        in_specs=[
            pl.BlockSpec((r, d), lambda i, j: (0, 0)),
            pl.BlockSpec((1, d, tn), lambda i, j: (i, 0, j)),
            pl.BlockSpec((1, 1, tn), lambda i, j: (i, 0, j)),
        ],
        out_specs=pl.BlockSpec((1, r, tn), lambda i, j: (i, 0, j)),
        out_shape=jax.ShapeDtypeStruct((depth, r, n), _F32),
        compiler_params=_params(2),
        name="modulation",
    )(cond, w_mod, b_mod.reshape(depth, 1, n))
    return out.reshape(depth, r, N_MOD, d)


def _ffn_body(*refs, sub, has_proj):
    if has_proj:
        x_ref, mods_ref, ng_ref, o_ref, wo_ref, win_ref, wout_ref, out_ref, xn_ref, acc_ref = refs
    else:
        x_ref, mods_ref, ng_ref, win_ref, wout_ref, out_ref, xn_ref, acc_ref = refs
    m = mods_ref[0, 0]
    x = x_ref[...]
    if has_proj:
        x = x + m[5:6] * jnp.dot(o_ref[...], wo_ref[0], preferred_element_type=_F32)
    k = 3 * sub
    xn_ref[...] = _modln(x, ng_ref[0, sub:sub + 1], m[k:k + 1], m[k + 1:k + 2]).astype(_BF16)
    if has_proj:
        out_ref[...] = x

    def ffn_chunk(c):
        xn = xn_ref[...]
        cols = slice(c * FF_CHUNK, (c + 1) * FF_CHUNK)
        gate = jnp.dot(xn, win_ref[0, 0, :, cols], preferred_element_type=_F32)
        up = jnp.dot(xn, win_ref[0, 0, :, D_FF + c * FF_CHUNK:D_FF + (c + 1) * FF_CHUNK],
                     preferred_element_type=_F32)
        act = (_silu(gate) * up).astype(_BF16)
        return jnp.dot(act, wout_ref[0, 0, c], preferred_element_type=_F32)

    acc_ref[...] = ffn_chunk(0)

    assert N_FF_CHUNKS % 2 == 1
    for c in range(1, N_FF_CHUNKS, 2):
        acc_ref[...] += ffn_chunk(c) + ffn_chunk(c + 1)
    h = out_ref[...] if has_proj else x_ref[...]
    out_ref[...] = h + (0.5 * m[k + 2:k + 3]) * acc_ref[...]


def _ffn(x, mods, norm_g, w_in, w_out, layer, which, row_of_tile, tm, o=None, w_o=None):
    t, d = x.shape
    sub = 2 * which
    has_proj = o is not None
    in_specs = [
        pl.BlockSpec((tm, d), lambda i: (i, 0)),
        pl.BlockSpec((1, 1, N_MOD, d), lambda i: (layer, row_of_tile(i), 0, 0)),
        pl.BlockSpec((1, 3, d), lambda i: (layer, 0, 0)),
    ]
    args = [x, mods, norm_g]
    if has_proj:
        in_specs += [pl.BlockSpec((tm, d), lambda i: (i, 0)),
                     _resident((1, d, d), lambda i: (layer, 0, 0))]
        args += [o, w_o]
    in_specs += [
        _resident((1, 1, d, 2 * D_FF), lambda i: (layer, which, 0, 0)),
        _resident((1, 1, N_FF_CHUNKS, FF_CHUNK, d), lambda i: (layer, which, 0, 0, 0)),
    ]
    args += [w_in, w_out]
    return pl.pallas_call(
        functools.partial(_ffn_body, sub=sub, has_proj=has_proj),
        grid=(t // tm,),
        in_specs=in_specs,
        out_specs=pl.BlockSpec((tm, d), lambda i: (i, 0)),
        out_shape=jax.ShapeDtypeStruct((t, d), _F32),
        scratch_shapes=[pltpu.VMEM((tm, d), _BF16), pltpu.VMEM((tm, d), _F32)],
        compiler_params=_params(1),
        name="ffn_proj" if has_proj else "ffn",
    )(*args)


def _head_norm_rope_t(yt, g_t, rope_ref):
    ss = jnp.sum(yt * yt, axis=0, keepdims=True)
    yn = (yt * lax.rsqrt(ss * (1.0 / HEAD_DIM) + EPS)) * g_t
    if rope_ref is None:
        return yn
    f = ROPE_FREQS
    x1r, x2r, x1c, x2c = yn[0:f], yn[f:2 * f], yn[2 * f:3 * f], yn[3 * f:4 * f]
    cr, sr, cc, sc = rope_ref[0], rope_ref[1], rope_ref[2], rope_ref[3]
    return jnp.concatenate([x1r * cr - x2r * sr, x2r * cr + x1r * sr,
                            x1c * cc - x2c * sc, x2c * cc + x1c * sc], axis=0)


def _qkv_body(*refs, rope, keep_f32, sub):
    n_in = 7 if rope else 6
    x_ref, mods_ref, ng_ref, w_ref, qg_ref, kg_ref = refs[:6]
    rope_ref = refs[6] if rope else None
    qt_ref, ka_ref, vt_ref = refs[n_in:n_in + 3]
    m = mods_ref[0, 0]
    tm = x_ref.shape[0]
    k0 = N_HEADS * HEAD_DIM
    row = lax.broadcasted_iota(jnp.int32, (ACC_ROWS - HEAD_DIM, V_PIECE), 0)
    extra = jnp.where(row == 0, 1.0, 0.0).astype(vt_ref.dtype)

    def project(r):
        rows = slice(r * sub, (r + 1) * sub)
        a = _modln(x_ref[rows, :], ng_ref[0, 1:2], m[3:4], m[4:5]).astype(_BF16)
        return jnp.dot(a, w_ref[0], preferred_element_type=_F32)

    def finish(r, proj):
        toks = slice(r * sub, (r + 1) * sub)
        qg, kg = qg_ref[0, :, toks], kg_ref[0, :, toks]
        rope_r = None if rope_ref is None else rope_ref.at[:, :, toks]
        for c in range(k0 // GROUP_COLS):
            yt = proj[:, c * GROUP_COLS:(c + 1) * GROUP_COLS].T
            for h in range(KV_GROUP):
                qn = _head_norm_rope_t(yt[h * HEAD_DIM:(h + 1) * HEAD_DIM], qg, rope_r)
                r0 = c * GROUP_COLS + h * HEAD_DIM
                qt_ref[0, r0:r0 + HEAD_DIM, toks] = (qn * Q_SCALE).astype(qt_ref.dtype)
        kt = proj[:, k0:k0 + KV_COLS].T
        kn = jnp.concatenate([_head_norm_rope_t(kt[g * HEAD_DIM:(g + 1) * HEAD_DIM], kg, rope_r)
                              for g in range(N_KV_HEADS)], axis=0)
        k_tok = kn.T
        v_tok = proj[:, k0 + KV_COLS:]
        vt = v_tok.T
        for g in range(N_KV_HEADS):
            ka_ref[0, g, toks, :] = k_tok[:, g * HEAD_DIM:(g + 1) * HEAD_DIM].astype(ka_ref.dtype)
            for j in range(sub // V_PIECE):
                piece = r * (sub // V_PIECE) + j
                vt_ref[0, g, piece, 0:HEAD_DIM, :] = (
                    vt[g * HEAD_DIM:(g + 1) * HEAD_DIM, j * V_PIECE:(j + 1) * V_PIECE].astype(vt_ref.dtype))
                vt_ref[0, g, piece, HEAD_DIM:ACC_ROWS, :] = extra
        if keep_f32:
            k_ref, v_ref = refs[n_in + 3:n_in + 5]
            k_ref[toks, :] = k_tok
            v_ref[toks, :] = v_tok

    n_sub = tm // sub
    proj = project(0)
    for r in range(n_sub):
        nxt = project(r + 1) if r + 1 < n_sub else None
        finish(r, proj)
        proj = nxt


def _qkv(x, mods, norm_g, w_qkv, qg_t, kg_t, layer, row_of_tile, tm, seq_len, keep_f32, rope_t=None):
    t, d = x.shape
    rope = rope_t is not None
    tiles_per_seq = seq_len // tm
    n_seq = t // seq_len
    seq_tile = lambda i: (i // tiles_per_seq, 0, i % tiles_per_seq)
    out_specs = [pl.BlockSpec((1, N_HEADS * HEAD_DIM, tm), seq_tile),
                 pl.BlockSpec((1, N_KV_HEADS, tm, HEAD_DIM), lambda i: seq_tile(i) + (0,)),
                 pl.BlockSpec((1, N_KV_HEADS, tm // V_PIECE, ACC_ROWS, V_PIECE), lambda i: seq_tile(i) + (0, 0))]
    out_shape = [jax.ShapeDtypeStruct((n_seq, N_HEADS * HEAD_DIM, seq_len), _BF16),
                 jax.ShapeDtypeStruct((n_seq, N_KV_HEADS, seq_len, HEAD_DIM), _BF16),
                 jax.ShapeDtypeStruct((n_seq, N_KV_HEADS, seq_len // V_PIECE, ACC_ROWS, V_PIECE), _BF16)]
    if keep_f32:
        out_specs += [pl.BlockSpec((tm, KV_COLS), lambda i: (i, 0))] * 2
        out_shape += [jax.ShapeDtypeStruct((t, KV_COLS), _F32)] * 2
    in_specs = [
        pl.BlockSpec((tm, d), lambda i: (i, 0)),
        pl.BlockSpec((1, 1, N_MOD, d), lambda i: (layer, row_of_tile(i), 0, 0)),
        pl.BlockSpec((1, 3, d), lambda i: (layer, 0, 0)),
        _resident((1, d, QKV_COLS), lambda i: (layer, 0, 0)),
        pl.BlockSpec((1, HEAD_DIM, tm), lambda i: (layer, 0, 0)),
        pl.BlockSpec((1, HEAD_DIM, tm), lambda i: (layer, 0, 0)),
    ]
    args = [x, mods, norm_g, w_qkv, qg_t, kg_t]
    if rope:
        in_specs += [pl.BlockSpec((4, ROPE_FREQS, tm), lambda i: (0, 0, i % tiles_per_seq))]
        args += [rope_t]
    return pl.pallas_call(
        functools.partial(_qkv_body, rope=rope, keep_f32=keep_f32, sub=min(tm, QKV_SUB)),
        grid=(t // tm,),
        in_specs=in_specs,
        out_specs=out_specs,
        out_shape=out_shape,
        compiler_params=_params(1),
        name="qkv_rope" if rope else "qkv",
    )(*args)


def _rope_table_t(n_tok):
    pos = jnp.arange(n_tok)
    row = (pos // GRID_W).astype(_F32)
    col = (pos % GRID_W).astype(_F32)
    freqs = 1.0 / jnp.power(ROPE_THETA, jnp.arange(ROPE_FREQS, dtype=_F32) / ROPE_FREQS)
    ang_r = row[:, None] * freqs
    ang_c = col[:, None] * freqs
    return jnp.stack([jnp.cos(ang_r).T, jnp.sin(ang_r).T, jnp.cos(ang_c).T, jnp.sin(ang_c).T])


KV_SUB = 256
V_PIECE = 128
SCORE_ROWS = 128
BF16_SUBLANES = 16
ACC_ROWS = HEAD_DIM + BF16_SUBLANES


SCORE_AHEAD = 2
VALUE_BEHIND = 1
S_SLOTS = SCORE_AHEAD + 1
P_SLOTS = VALUE_BEHIND + 1


def _attn_scratch(n_states, tq, tk):
    return [pltpu.VMEM((S_SLOTS, tk, tq), _F32),
            pltpu.VMEM((S_SLOTS, 1, tq), _F32),
            pltpu.VMEM((P_SLOTS, tk, tq), _BF16),
            pltpu.VMEM((P_SLOTS, 1, tq), _F32),
            pltpu.VMEM((n_states, 1, tq), _F32),
            pltpu.VMEM((n_states, ACC_ROWS, tq), _F32)]


def _init_states(scr, sinks=None):
    m_ref, acc_ref = scr[4], scr[5]
    tq = m_ref.shape[-1]
    if sinks is None:
        m_ref[...] = jnp.full(m_ref.shape, NEG_INF, _F32)
        acc_ref[...] = jnp.zeros(acc_ref.shape, _F32)
    else:
        for i, sk in enumerate(sinks):
            m_ref[i] = jnp.full((1, tq), sk * LOG2_E, _F32)
        row = lax.broadcasted_iota(jnp.int32, (ACC_ROWS, tq), 0)
        acc_ref[...] = jnp.broadcast_to(jnp.where(row == HEAD_DIM, 1.0, 0.0).astype(_F32), acc_ref.shape)


def _part(j, item):
    part = item[3][0].shape[1]
    return slice(j * part, (j + 1) * part)


def _sub(j, subs, item):
    r0 = _part(j, item).start
    return slice(r0 + subs[0] * SCORE_ROWS, r0 + (subs[-1] + 1) * SCORE_ROWS)


def _scores(scr, slot, j, subs, item):
    s_ref, mc_ref = scr[0], scr[1]
    _, k, qt, _, bias = item
    rows = _sub(j, subs, item)
    s = jnp.concatenate([jnp.dot(k[_sub(j, (u,), item)], qt, preferred_element_type=_F32)
                         for u in subs], axis=0)
    if bias is not None:
        s = s + bias[rows]
    s_ref[slot, rows] = s
    mx = jnp.max(s, axis=0, keepdims=True)
    mc_ref[slot] = mx if rows.start == 0 else jnp.maximum(mc_ref[slot], mx)


def _new_max(scr, s_slot, p_slot, st):
    mc_ref, al_ref, m_ref = scr[1], scr[3], scr[4]
    m_old = m_ref[st]
    m_new = jnp.maximum(m_old, mc_ref[s_slot])
    al_ref[p_slot] = jnp.exp2(m_old - m_new)
    m_ref[st] = m_new


def _softmax(scr, s_slot, p_slot, j, subs, item):
    s_ref, p_ref, m_ref = scr[0], scr[2], scr[4]
    rows = _sub(j, subs, item)
    p_ref[p_slot, rows] = jnp.exp2(s_ref[s_slot, rows] - m_ref[item[0]]).astype(_BF16)


def _values(scr, slot, j, item):
    p_ref, al_ref, acc_ref = scr[2], scr[3], scr[5]
    st, vts = item[0], item[3]
    pv = jnp.dot(vts[j], p_ref[slot, _part(j, item)], preferred_element_type=_F32)
    acc = acc_ref[st]
    acc_ref[st] = (al_ref[slot] * acc if j == 0 else acc) + pv


def _pipeline(scr, items, fine):
    n = len(items)
    n_parts = len(items[0][3])
    subs = tuple(range(items[0][3][0].shape[1] // SCORE_ROWS))
    groups = [(u,) for u in subs] if fine else [subs]
    for i0 in range(min(SCORE_AHEAD, n)):
        for j in range(n_parts):
            _scores(scr, i0 % S_SLOTS, j, subs, items[i0])
    for i in range(n + VALUE_BEHIND):
        ahead = i + SCORE_AHEAD
        if i < n:
            _new_max(scr, i % S_SLOTS, i % P_SLOTS, items[i][0])
        for j in range(n_parts):
            for group in groups:
                if ahead < n:
                    _scores(scr, ahead % S_SLOTS, j, group, items[ahead])
                if i < n:
                    _softmax(scr, i % S_SLOTS, i % P_SLOTS, j, group, items[i])
            if i >= VALUE_BEHIND:
                _values(scr, (i - VALUE_BEHIND) % P_SLOTS, j, items[i - VALUE_BEHIND])


def _finish(scr, first):
    acc_ref = scr[5]
    ot = jnp.concatenate([acc_ref[first + h, 0:HEAD_DIM] / acc_ref[first + h, HEAD_DIM:HEAD_DIM + 1]
                          for h in range(KV_GROUP)], axis=0)
    return ot.T


def _heads(qt_ref, g=None):
    base = 0 if g is None else g * GROUP_COLS
    return [qt_ref[0, base + h * HEAD_DIM: base + (h + 1) * HEAD_DIM, :] for h in range(KV_GROUP)]


def _vt_part(vt_ref, lead, piece, part=KV_SUB):
    return jnp.concatenate([vt_ref[lead + (piece + i,)] for i in range(part // V_PIECE)], axis=1)


def _query_blocks(qt_ref, tq):
    return [[qt_ref[0, h * HEAD_DIM:(h + 1) * HEAD_DIM, b * tq:(b + 1) * tq] for h in range(KV_GROUP)]
            for b in range(qt_ref.shape[2] // tq)]


def _chunk_items(k_ref, vt_ref, blocks, tk, part=KV_SUB):
    items = []
    for c in range(k_ref.shape[2] // tk):
        k = k_ref[0, 0, c * tk:(c + 1) * tk, :]
        vts = [_vt_part(vt_ref, (0, 0), (c * tk + j * part) // V_PIECE, part) for j in range(tk // part)]
        for b, heads in enumerate(blocks):
            items += [(b * KV_GROUP + h, k, heads[h], vts, None) for h in range(KV_GROUP)]
    return items


def _store_blocks(scr, o_ref, tq, n_blocks):
    for b in range(n_blocks):
        o_ref[0, b * tq:(b + 1) * tq, :] = _finish(scr, b * KV_GROUP).astype(o_ref.dtype)


def _full_body(qt_ref, kc_ref, vtc_ref, kl_ref, vtl_ref, o_ref, *scr, tq, tk):
    blocks = _query_blocks(qt_ref, tq)
    _init_states(scr)
    _pipeline(scr, _chunk_items(kc_ref, vtc_ref, blocks, tk) + _chunk_items(kl_ref, vtl_ref, blocks, tk),
              fine=False)
    _store_blocks(scr, o_ref, tq, len(blocks))


def _kv_specs(lk, index_map):
    return [pl.BlockSpec((1, 1, lk, HEAD_DIM), lambda *i: index_map(*i) + (0, 0)),
            pl.BlockSpec((1, 1, lk // V_PIECE, ACC_ROWS, V_PIECE), lambda *i: index_map(*i) + (0, 0, 0))]


def _attend_full(qt, k_ctx, vt_ctx, k_lat, vt_lat, tq, tk, n_blocks):
    b, d, l = qt.shape
    ts = tq * n_blocks
    by_group = lambda bi, g, qi: (bi, g)
    return pl.pallas_call(
        functools.partial(_full_body, tq=tq, tk=tk),
        grid=(b, N_KV_HEADS, l // ts),
        in_specs=([pl.BlockSpec((1, GROUP_COLS, ts), lambda bi, g, qi: (bi, g, qi))]
                  + _kv_specs(k_ctx.shape[2], by_group) + _kv_specs(l, by_group)),
        out_specs=pl.BlockSpec((1, ts, GROUP_COLS), lambda bi, g, qi: (bi, qi, g)),
        out_shape=jax.ShapeDtypeStruct((b, l, d), _BF16),
        scratch_shapes=_attn_scratch(KV_GROUP * n_blocks, tq, tk),
        compiler_params=_params(3),
        name="attn_full",
    )(qt, k_ctx, vt_ctx, k_lat, vt_lat)


def _band_start(q0, l, tq):
    return jnp.clip(q0 - WINDOW, 0, l - 2 * tq)


def _window_body(sink_ref, qt_ref, kc_ref, vtc_ref, kl_ref, vtl_ref, bias_ref, o_ref, *scr, tq):
    g = pl.program_id(1)
    blocks = _query_blocks(qt_ref, tq)
    _init_states(scr, [sink_ref[g * KV_GROUP + h] for _ in blocks for h in range(KV_GROUP)])
    items = _chunk_items(kc_ref, vtc_ref, blocks, 2 * tq)
    for b, heads in enumerate(blocks):
        q0 = (pl.program_id(2) * len(blocks) + b) * tq
        start = pl.multiple_of(_band_start(q0, kl_ref.shape[2], tq), V_PIECE)
        k_band = kl_ref[0, 0, pl.ds(start, 2 * tq), :]
        vt_band = [_vt_part(vtl_ref, (0, 0), start // V_PIECE + j * (KV_SUB // V_PIECE)) for j in range(2)]
        items += [(b * KV_GROUP + h, k_band, heads[h], vt_band, bias_ref[b]) for h in range(KV_GROUP)]
    _pipeline(scr, items, fine=True)
    _store_blocks(scr, o_ref, tq, len(blocks))


def _window_bias(l, tq):
    q0 = (jnp.arange(l // tq) * tq)[:, None, None]
    kpos = _band_start(q0, l, tq) + jnp.arange(2 * tq)[None, :, None]
    qpos = q0 + jnp.arange(tq)[None, None, :]
    return jnp.where(jnp.abs(qpos - kpos) <= WINDOW, 0.0, NEG_INF).astype(_F32)


def _attend_window(qt, k_ctx, vt_ctx, k_lat, vt_lat, sink, tq, n_blocks):
    b, d, l = qt.shape
    lc = k_ctx.shape[2]
    ts = tq * n_blocks
    assert tq == KV_SUB and tq >= 2 * WINDOW and lc == 2 * tq and l >= 2 * tq
    by_group = lambda bi, g, qi: (bi, g)
    return pl.pallas_call(
        functools.partial(_window_body, tq=tq),
        grid=(b, N_KV_HEADS, l // ts),
        in_specs=([pl.BlockSpec(memory_space=pltpu.SMEM),
                   pl.BlockSpec((1, GROUP_COLS, ts), lambda bi, g, qi: (bi, g, qi))]
                  + _kv_specs(lc, by_group) + _kv_specs(l, by_group)
                  + [pl.BlockSpec((n_blocks, 2 * tq, tq), lambda bi, g, qi: (qi, 0, 0))]),
        out_specs=pl.BlockSpec((1, ts, GROUP_COLS), lambda bi, g, qi: (bi, qi, g)),
        out_shape=jax.ShapeDtypeStruct((b, l, d), _BF16),
        scratch_shapes=_attn_scratch(KV_GROUP * n_blocks, tq, 2 * tq),
        compiler_params=_params(3),
        name="attn_window",
    )(sink, qt, k_ctx, vt_ctx, k_lat, vt_lat, _window_bias(l, tq))


def _ctx_body(*refs, has_sink):
    if has_sink:
        sink_ref, qt_ref, k_ref, vt_ref, o_ref = refs[:5]
    else:
        qt_ref, k_ref, vt_ref, o_ref = refs[:4]
    scr = refs[5:] if has_sink else refs[4:]
    n_sub = k_ref.shape[2] // KV_SUB
    _init_states(scr, [sink_ref[i] for i in range(N_HEADS)] if has_sink else None)
    items = []
    for g in range(N_KV_HEADS):
        heads = _heads(qt_ref, g)
        k = k_ref[0, g]
        vts = [_vt_part(vt_ref, (0, g), j * (KV_SUB // V_PIECE)) for j in range(n_sub)]
        items += [(g * KV_GROUP + h, k, heads[h], vts, None) for h in range(KV_GROUP)]
    _pipeline(scr, items, fine=True)
    for g in range(N_KV_HEADS):
        o_ref[0, :, g * GROUP_COLS:(g + 1) * GROUP_COLS] = _finish(scr, g * KV_GROUP).astype(o_ref.dtype)


def _attend_ctx(qt, k, vt, sink):
    b, d, l = qt.shape
    has_sink = sink is not None
    in_specs = [
        pl.BlockSpec((1, d, l), lambda bi: (bi, 0, 0)),
        pl.BlockSpec((1, N_KV_HEADS, l, HEAD_DIM), lambda bi: (bi, 0, 0, 0)),
        pl.BlockSpec((1, N_KV_HEADS, l // V_PIECE, ACC_ROWS, V_PIECE), lambda bi: (bi, 0, 0, 0, 0)),
    ]
    args = [qt, k, vt]
    if has_sink:
        in_specs = [pl.BlockSpec(memory_space=pltpu.SMEM)] + in_specs
        args = [sink] + args
    return pl.pallas_call(
        functools.partial(_ctx_body, has_sink=has_sink),
        grid=(b,),
        in_specs=in_specs,
        out_specs=pl.BlockSpec((1, l, d), lambda bi: (bi, 0, 0)),
        out_shape=jax.ShapeDtypeStruct((b, l, d), _BF16),
        scratch_shapes=_attn_scratch(N_HEADS, l, l),
        compiler_params=_params(1),
        name="attn_ctx_sink" if has_sink else "attn_ctx",
    )(*args)


def _cache_layout(cache_k, cache_v):
    b, depth, p, g, hd = cache_v.shape
    k = cache_k.transpose(1, 0, 3, 2, 4).astype(_BF16)
    vt = cache_v.reshape(b, depth, p // V_PIECE, V_PIECE, g, hd).transpose(1, 0, 4, 2, 5, 3).astype(_BF16)
    extra = jnp.zeros((ACC_ROWS - hd, V_PIECE), _BF16).at[0].set(1.0)
    return k, jnp.concatenate([vt, jnp.broadcast_to(extra, vt.shape[:4] + extra.shape)], axis=4)


def _token_tile(t, cap):
    tm = min(cap, t)
    assert t % tm == 0
    return tm


def kernel(x_prompt, x_sample, cache_k, cache_v, c, c_ctx, w_mod, b_mod, norm_g, w_qkv, w_o,
           q_norm_g, k_norm_g, sink, w_ffn_in, w_ffn_out):
    depth = w_mod.shape[0]
    d = D_MODEL
    bp, lp, _ = x_prompt.shape
    bs, ls, _ = x_sample.shape
    past = cache_k.shape[2]
    assert lp % KV_SUB == 0 and ls % KV_TILE == 0 and past % KV_TILE == 0

    w_in = w_ffn_in.astype(_BF16)
    w_out = w_ffn_out.astype(_BF16).reshape(depth, 2, N_FF_CHUNKS, FF_CHUNK, d)
    w_qkv_b = w_qkv.astype(_BF16)
    w_o_b = w_o.astype(_BF16)

    n_rows = -(-(bs + 1) // 8) * 8
    cond = jnp.concatenate([c, c_ctx[None, :], jnp.zeros((n_rows - bs - 1, d), _F32)], axis=0)
    mods = _modulation(cond, w_mod, b_mod)

    tp = bp * lp
    tm_p = _token_tile(tp, FFN_TILE)
    tq_p = _token_tile(lp, QKV_TILE)
    ctx_row = lambda i: bs
    qg_p = jnp.broadcast_to(q_norm_g[:, :, None], (depth, HEAD_DIM, tq_p))
    kg_p = jnp.broadcast_to(k_norm_g[:, :, None], (depth, HEAD_DIM, tq_p))
    h = x_prompt.reshape(tp, d)
    new_ks, new_vs = [], []
    for i in range(depth):
        h = _ffn(h, mods, norm_g, w_in, w_out, i, 0, ctx_row, tm_p)
        qt, ka, vt, k, v = _qkv(h, mods, norm_g, w_qkv_b, qg_p, kg_p, i, ctx_row, tq_p, lp, True)
        new_ks.append(k.reshape(bp, lp, N_KV_HEADS, HEAD_DIM))
        new_vs.append(v.reshape(bp, lp, N_KV_HEADS, HEAD_DIM))
        o = _attend_ctx(qt, ka, vt, sink[i // 2] if i % 2 == 1 else None)
        h = _ffn(h, mods, norm_g, w_in, w_out, i, 1, ctx_row, tm_p, o=o.reshape(tp, d), w_o=w_o_b)
    y_prompt = h.reshape(bp, lp, d)
    new_k = jnp.stack(new_ks, axis=1)
    new_v = jnp.stack(new_vs, axis=1)

    ts = bs * ls
    tm_s = _token_tile(ls, FFN_TILE)
    tq_s = _token_tile(ls, QKV_TILE)
    lat_row = lambda i: i // (ls // tm_s)
    lat_row_q = lambda i: i // (ls // tq_s)
    rope_t = _rope_table_t(ls)
    qg_s = jnp.broadcast_to(q_norm_g[:, :, None], (depth, HEAD_DIM, tq_s))
    kg_s = jnp.broadcast_to(k_norm_g[:, :, None], (depth, HEAD_DIM, tq_s))
    k_ctx, vt_ctx = _cache_layout(cache_k, cache_v)
    blocks_per_step = lambda want: want if ls % (want * Q_TILE) == 0 else 1
    h = x_sample.reshape(ts, d)
    for i in range(depth):
        h = _ffn(h, mods, norm_g, w_in, w_out, i, 0, lat_row, tm_s)
        qt, ka, vt = _qkv(h, mods, norm_g, w_qkv_b, qg_s, kg_s, i, lat_row_q, tq_s, ls, False, rope_t)
        if i % 2 == 0:
            o = _attend_full(qt, k_ctx[i], vt_ctx[i], ka, vt, Q_TILE, KV_TILE, blocks_per_step(FULL_Q_BLOCKS))
        else:
            o = _attend_window(qt, k_ctx[i], vt_ctx[i], ka, vt, sink[i // 2], Q_TILE,
                               blocks_per_step(WINDOW_Q_BLOCKS))
        h = _ffn(h, mods, norm_g, w_in, w_out, i, 1, lat_row, tm_s, o=o.reshape(ts, d), w_o=w_o_b)
    y_sample = h.reshape(bs, ls, d)

    return (y_prompt, y_sample, new_k, new_v)
```

```python
import functools

import jax
import jax.numpy as jnp
from jax import lax
from jax.experimental import pallas as pl
from jax.experimental.pallas import tpu as pltpu

D_MODEL = 1024
N_HEADS = 16
N_KV_HEADS = 4
HEAD_DIM = 64
KV_GROUP = N_HEADS // N_KV_HEADS
GROUP_COLS = KV_GROUP * HEAD_DIM
KV_COLS = N_KV_HEADS * HEAD_DIM
QKV_COLS = (N_HEADS + 2 * N_KV_HEADS) * HEAD_DIM
D_FF = 2816
N_MOD = 9
GRID_W = 64
WINDOW = 128
ROPE_FREQS = HEAD_DIM // 4
ROPE_THETA = 10000.0
ATTN_SCALE = HEAD_DIM ** -0.5
LOG2_E = 1.4426950408889634
Q_SCALE = ATTN_SCALE * LOG2_E
EPS = 1e-6
NEG_INF = -1e30

MOD_COL_TILES = 8
FF_CHUNK = 256
N_FF_CHUNKS = D_FF // FF_CHUNK
FFN_TILE = 1024
QKV_TILE = 1024
QKV_SUB = 512
Q_TILE = 256
FULL_Q_BLOCKS = 4
WINDOW_Q_BLOCKS = 4
KV_TILE = 512
VMEM_LIMIT_BYTES = 52 * 1024 * 1024

_BF16 = jnp.bfloat16
_F32 = jnp.float32


def _params(n_axes):
    return pltpu.CompilerParams(dimension_semantics=("arbitrary",) * n_axes,
                                vmem_limit_bytes=VMEM_LIMIT_BYTES)


def _resident(block_shape, index_map):
    return pl.BlockSpec(block_shape, index_map, pipeline_mode=pl.Buffered(1))


def _silu(x):
    return x * jax.nn.sigmoid(x)


def _modln(x, g, shift, scale):
    y = x * lax.rsqrt(jnp.mean(x * x, axis=-1, keepdims=True) + EPS)
    return y * (g * (1.0 + scale)) + shift


def _mod_body(cond_ref, w_ref, b_ref, o_ref):
    a = _silu(cond_ref[...]).astype(_BF16)
    w = w_ref[0].astype(_BF16)
    o_ref[0] = jnp.dot(a, w, preferred_element_type=_F32) + b_ref[0]


def _modulation(cond, w_mod, b_mod):
    depth, d, n = w_mod.shape
    r = cond.shape[0]
    tn = n // MOD_COL_TILES
    out = pl.pallas_call(
        _mod_body,
        grid=(depth, n // tn),
        in_specs=[
            pl.BlockSpec((r, d), lambda i, j: (0, 0)),
            pl.BlockSpec((1, d, tn), lambda i, j: (i, 0, j)),
            pl.BlockSpec((1, 1, tn), lambda i, j: (i, 0, j)),
        ],
        out_specs=pl.BlockSpec((1, r, tn), lambda i, j: (i, 0, j)),
        out_shape=jax.ShapeDtypeStruct((depth, r, n), _F32),
        compiler_params=_params(2),
        name="modulation",
    )(cond, w_mod, b_mod.reshape(depth, 1, n))
    return out.reshape(depth, r, N_MOD, d)


def _ffn_body(*refs, sub, has_proj):
    if has_proj:
        x_ref, mods_ref, ng_ref, o_ref, wo_ref, win_ref, wout_ref, out_ref, xn_ref, acc_ref = refs
    else:
        x_ref, mods_ref, ng_ref, win_ref, wout_ref, out_ref, xn_ref, acc_ref = refs
    m = mods_ref[0, 0]
    x = x_ref[...]
    if has_proj:
        x = x + m[5:6] * jnp.dot(o_ref[...], wo_ref[0], preferred_element_type=_F32)
    k = 3 * sub
    xn_ref[...] = _modln(x, ng_ref[0, sub:sub + 1], m[k:k + 1], m[k + 1:k + 2]).astype(_BF16)
    if has_proj:
        out_ref[...] = x

    def ffn_chunk(c):
        xn = xn_ref[...]
        cols = slice(c * FF_CHUNK, (c + 1) * FF_CHUNK)
        gate = jnp.dot(xn, win_ref[0, 0, :, cols], preferred_element_type=_F32)
        up = jnp.dot(xn, win_ref[0, 0, :, D_FF + c * FF_CHUNK:D_FF + (c + 1) * FF_CHUNK],
                     preferred_element_type=_F32)
        act = (_silu(gate) * up).astype(_BF16)
        return jnp.dot(act, wout_ref[0, 0, c], preferred_element_type=_F32)

    acc_ref[...] = ffn_chunk(0)

    assert N_FF_CHUNKS % 2 == 1
    for c in range(1, N_FF_CHUNKS, 2):
        acc_ref[...] += ffn_chunk(c) + ffn_chunk(c + 1)
    h = out_ref[...] if has_proj else x_ref[...]
    out_ref[...] = h + (0.5 * m[k + 2:k + 3]) * acc_ref[...]


def _ffn(x, mods, norm_g, w_in, w_out, layer, which, row_of_tile, tm, o=None, w_o=None):
    t, d = x.shape
    sub = 2 * which
    has_proj = o is not None
    in_specs = [
        pl.BlockSpec((tm, d), lambda i: (i, 0)),
        pl.BlockSpec((1, 1, N_MOD, d), lambda i: (layer, row_of_tile(i), 0, 0)),
        pl.BlockSpec((1, 3, d), lambda i: (layer, 0, 0)),
    ]
    args = [x, mods, norm_g]
    if has_proj:
        in_specs += [pl.BlockSpec((tm, d), lambda i: (i, 0)),
                     _resident((1, d, d), lambda i: (layer, 0, 0))]
        args += [o, w_o]
    in_specs += [
        _resident((1, 1, d, 2 * D_FF), lambda i: (layer, which, 0, 0)),
        _resident((1, 1, N_FF_CHUNKS, FF_CHUNK, d), lambda i: (layer, which, 0, 0, 0)),
    ]
    args += [w_in, w_out]
    return pl.pallas_call(
        functools.partial(_ffn_body, sub=sub, has_proj=has_proj),
        grid=(t // tm,),
        in_specs=in_specs,
        out_specs=pl.BlockSpec((tm, d), lambda i: (i, 0)),
        out_shape=jax.ShapeDtypeStruct((t, d), _F32),
        scratch_shapes=[pltpu.VMEM((tm, d), _BF16), pltpu.VMEM((tm, d), _F32)],
        compiler_params=_params(1),
        name="ffn_proj" if has_proj else "ffn",
    )(*args)


def _head_norm_rope_t(yt, g_t, rope_ref):
    ss = jnp.sum(yt * yt, axis=0, keepdims=True)
    yn = (yt * lax.rsqrt(ss * (1.0 / HEAD_DIM) + EPS)) * g_t
    if rope_ref is None:
        return yn
    f = ROPE_FREQS
    x1r, x2r, x1c, x2c = yn[0:f], yn[f:2 * f], yn[2 * f:3 * f], yn[3 * f:4 * f]
    cr, sr, cc, sc = rope_ref[0], rope_ref[1], rope_ref[2], rope_ref[3]
    return jnp.concatenate([x1r * cr - x2r * sr, x2r * cr + x1r * sr,
                            x1c * cc - x2c * sc, x2c * cc + x1c * sc], axis=0)


def _qkv_body(*refs, rope, keep_f32, sub):
    n_in = 6 + int(rope) + 2 * int(keep_f32)
    x_ref, mods_ref, ng_ref, w_ref, qg_ref, kg_ref = refs[:6]
    rope_ref = refs[6] if rope else None
    qt_ref, ka_ref, vt_ref = refs[n_in:n_in + 3]
    m = mods_ref[0, 0]
    tm = x_ref.shape[0]
    k0 = N_HEADS * HEAD_DIM
    row = lax.broadcasted_iota(jnp.int32, (ACC_ROWS - HEAD_DIM, V_PIECE), 0)
    extra = jnp.where(row == 0, 1.0, 0.0).astype(vt_ref.dtype)

    def project(r):
        rows = slice(r * sub, (r + 1) * sub)
        a = _modln(x_ref[rows, :], ng_ref[0, 1:2], m[3:4], m[4:5]).astype(_BF16)
        return jnp.dot(a, w_ref[0], preferred_element_type=_F32)

    def finish(r, proj):
        toks = slice(r * sub, (r + 1) * sub)
        qg, kg = qg_ref[0, :, toks], kg_ref[0, :, toks]
        rope_r = None if rope_ref is None else rope_ref.at[:, :, toks]
        for c in range(k0 // GROUP_COLS):
            yt = proj[:, c * GROUP_COLS:(c + 1) * GROUP_COLS].T
            for h in range(KV_GROUP):
                qn = _head_norm_rope_t(yt[h * HEAD_DIM:(h + 1) * HEAD_DIM], qg, rope_r)
                r0 = c * GROUP_COLS + h * HEAD_DIM
                qt_ref[0, r0:r0 + HEAD_DIM, toks] = (qn * Q_SCALE).astype(qt_ref.dtype)
        kt = proj[:, k0:k0 + KV_COLS].T
        kn = jnp.concatenate([_head_norm_rope_t(kt[g * HEAD_DIM:(g + 1) * HEAD_DIM], kg, rope_r)
                              for g in range(N_KV_HEADS)], axis=0)
        k_tok = kn.T
        v_tok = proj[:, k0 + KV_COLS:]
        vt = v_tok.T
        for g in range(N_KV_HEADS):
            ka_ref[0, g, toks, :] = k_tok[:, g * HEAD_DIM:(g + 1) * HEAD_DIM].astype(ka_ref.dtype)
            for j in range(sub // V_PIECE):
                piece = r * (sub // V_PIECE) + j
                vt_ref[0, g, piece, 0:HEAD_DIM, :] = (
                    vt[g * HEAD_DIM:(g + 1) * HEAD_DIM, j * V_PIECE:(j + 1) * V_PIECE].astype(vt_ref.dtype))
                vt_ref[0, g, piece, HEAD_DIM:ACC_ROWS, :] = extra
        if keep_f32:
            k_ref, v_ref = refs[n_in + 3:n_in + 5]
            k_ref[0, 0, toks, :] = k_tok
            v_ref[0, 0, toks, :] = v_tok

    n_sub = tm // sub
    proj = project(0)
    for r in range(n_sub):
        nxt = project(r + 1) if r + 1 < n_sub else None
        finish(r, proj)
        proj = nxt


def _qkv(x, mods, norm_g, w_qkv, qg_t, kg_t, layer, row_of_tile, tm, seq_len, kv_f32=None, rope_t=None):
    t, d = x.shape
    rope = rope_t is not None
    keep_f32 = kv_f32 is not None
    tiles_per_seq = seq_len // tm
    n_seq = t // seq_len
    seq_tile = lambda i: (i // tiles_per_seq, 0, i % tiles_per_seq)
    out_specs = [pl.BlockSpec((1, N_HEADS * HEAD_DIM, tm), seq_tile),
                 pl.BlockSpec((1, N_KV_HEADS, tm, HEAD_DIM), lambda i: seq_tile(i) + (0,)),
                 pl.BlockSpec((1, N_KV_HEADS, tm // V_PIECE, ACC_ROWS, V_PIECE), lambda i: seq_tile(i) + (0, 0))]
    out_shape = [jax.ShapeDtypeStruct((n_seq, N_HEADS * HEAD_DIM, seq_len), _BF16),
                 jax.ShapeDtypeStruct((n_seq, N_KV_HEADS, seq_len, HEAD_DIM), _BF16),
                 jax.ShapeDtypeStruct((n_seq, N_KV_HEADS, seq_len // V_PIECE, ACC_ROWS, V_PIECE), _BF16)]
    if keep_f32:
        out_specs += [pl.BlockSpec((1, 1, tm, KV_COLS),
                                   lambda i: (i // tiles_per_seq, layer, i % tiles_per_seq, 0))] * 2
        out_shape += [jax.ShapeDtypeStruct(buf.shape, buf.dtype) for buf in kv_f32]
    in_specs = [
        pl.BlockSpec((tm, d), lambda i: (i, 0)),
        pl.BlockSpec((1, 1, N_MOD, d), lambda i: (layer, row_of_tile(i), 0, 0)),
        pl.BlockSpec((1, 3, d), lambda i: (layer, 0, 0)),
        _resident((1, d, QKV_COLS), lambda i: (layer, 0, 0)),
        pl.BlockSpec((1, HEAD_DIM, tm), lambda i: (layer, 0, 0)),
        pl.BlockSpec((1, HEAD_DIM, tm), lambda i: (layer, 0, 0)),
    ]
    args = [x, mods, norm_g, w_qkv, qg_t, kg_t]
    if rope:
        in_specs += [pl.BlockSpec((4, ROPE_FREQS, tm), lambda i: (0, 0, i % tiles_per_seq))]
        args += [rope_t]
    aliases = {}
    if keep_f32:
        aliases = {len(args): 3, len(args) + 1: 4}
        in_specs += [pl.BlockSpec(memory_space=pl.ANY)] * 2
        args += list(kv_f32)
    return pl.pallas_call(
        functools.partial(_qkv_body, rope=rope, keep_f32=keep_f32, sub=min(tm, QKV_SUB)),
        grid=(t // tm,),
        in_specs=in_specs,
        out_specs=out_specs,
        out_shape=out_shape,
        input_output_aliases=aliases,
        compiler_params=_params(1),
        name="qkv_rope" if rope else "qkv",
    )(*args)


def _rope_table_t(n_tok):
    pos = jnp.arange(n_tok)
    row = (pos // GRID_W).astype(_F32)
    col = (pos % GRID_W).astype(_F32)
    freqs = 1.0 / jnp.power(ROPE_THETA, jnp.arange(ROPE_FREQS, dtype=_F32) / ROPE_FREQS)
    ang_r = row[:, None] * freqs
    ang_c = col[:, None] * freqs
    return jnp.stack([jnp.cos(ang_r).T, jnp.sin(ang_r).T, jnp.cos(ang_c).T, jnp.sin(ang_c).T])


KV_SUB = 256
V_PIECE = 128
SCORE_ROWS = 128
BF16_SUBLANES = 16
ACC_ROWS = HEAD_DIM + BF16_SUBLANES


SCORE_AHEAD = 2
VALUE_BEHIND = 1
S_SLOTS = SCORE_AHEAD + 1
P_SLOTS = VALUE_BEHIND + 1


def _attn_scratch(n_states, tq, tk):
    return [pltpu.VMEM((S_SLOTS, tk, tq), _F32),
            pltpu.VMEM((S_SLOTS, 1, tq), _F32),
            pltpu.VMEM((P_SLOTS, tk, tq), _BF16),
            pltpu.VMEM((P_SLOTS, 1, tq), _F32),
            pltpu.VMEM((n_states, 1, tq), _F32),
            pltpu.VMEM((n_states, ACC_ROWS, tq), _F32)]


def _init_states(scr, sinks=None):
    m_ref, acc_ref = scr[4], scr[5]
    tq = m_ref.shape[-1]
    if sinks is None:
        m_ref[...] = jnp.full(m_ref.shape, NEG_INF, _F32)
        acc_ref[...] = jnp.zeros(acc_ref.shape, _F32)
    else:
        for i, sk in enumerate(sinks):
            m_ref[i] = jnp.full((1, tq), sk * LOG2_E, _F32)
        row = lax.broadcasted_iota(jnp.int32, (ACC_ROWS, tq), 0)
        acc_ref[...] = jnp.broadcast_to(jnp.where(row == HEAD_DIM, 1.0, 0.0).astype(_F32), acc_ref.shape)


def _part(j, item):
    part = item[3][0].shape[1]
    return slice(j * part, (j + 1) * part)


def _sub(j, subs, item):
    r0 = _part(j, item).start
    return slice(r0 + subs[0] * SCORE_ROWS, r0 + (subs[-1] + 1) * SCORE_ROWS)


def _scores(scr, slot, j, subs, item):
    s_ref, mc_ref = scr[0], scr[1]
    _, k, qt, _, bias = item
    rows = _sub(j, subs, item)
    s = jnp.concatenate([jnp.dot(k[_sub(j, (u,), item)], qt, preferred_element_type=_F32)
                         for u in subs], axis=0)
    if bias is not None:
        s = s + bias[rows]
    s_ref[slot, rows] = s
    mx = jnp.max(s, axis=0, keepdims=True)
    mc_ref[slot] = mx if rows.start == 0 else jnp.maximum(mc_ref[slot], mx)


def _new_max(scr, s_slot, p_slot, st):
    mc_ref, al_ref, m_ref = scr[1], scr[3], scr[4]
    m_old = m_ref[st]
    m_new = jnp.maximum(m_old, mc_ref[s_slot])
    al_ref[p_slot] = jnp.exp2(m_old - m_new)
    m_ref[st] = m_new


def _softmax(scr, s_slot, p_slot, j, subs, item):
    s_ref, p_ref, m_ref = scr[0], scr[2], scr[4]
    rows = _sub(j, subs, item)
    p_ref[p_slot, rows] = jnp.exp2(s_ref[s_slot, rows] - m_ref[item[0]]).astype(_BF16)


def _values(scr, slot, j, item):
    p_ref, al_ref, acc_ref = scr[2], scr[3], scr[5]
    st, vts = item[0], item[3]
    pv = jnp.dot(vts[j], p_ref[slot, _part(j, item)], preferred_element_type=_F32)
    acc = acc_ref[st]
    acc_ref[st] = (al_ref[slot] * acc if j == 0 else acc) + pv


def _pipeline(scr, items, fine):
    n = len(items)
    n_parts = len(items[0][3])
    subs = tuple(range(items[0][3][0].shape[1] // SCORE_ROWS))
    groups = [(u,) for u in subs] if fine else [subs]
    for i0 in range(min(SCORE_AHEAD, n)):
        for j in range(n_parts):
            _scores(scr, i0 % S_SLOTS, j, subs, items[i0])
    for i in range(n + VALUE_BEHIND):
        ahead = i + SCORE_AHEAD
        if i < n:
            _new_max(scr, i % S_SLOTS, i % P_SLOTS, items[i][0])
        for j in range(n_parts):
            for group in groups:
                if ahead < n:
                    _scores(scr, ahead % S_SLOTS, j, group, items[ahead])
                if i < n:
                    _softmax(scr, i % S_SLOTS, i % P_SLOTS, j, group, items[i])
            if i >= VALUE_BEHIND:
                _values(scr, (i - VALUE_BEHIND) % P_SLOTS, j, items[i - VALUE_BEHIND])


def _finish(scr, first):
    acc_ref = scr[5]
    ot = jnp.concatenate([acc_ref[first + h, 0:HEAD_DIM] / acc_ref[first + h, HEAD_DIM:HEAD_DIM + 1]
                          for h in range(KV_GROUP)], axis=0)
    return ot.T


def _heads(qt_ref, g=None):
    base = 0 if g is None else g * GROUP_COLS
    return [qt_ref[0, base + h * HEAD_DIM: base + (h + 1) * HEAD_DIM, :] for h in range(KV_GROUP)]


def _vt_part(vt_ref, lead, piece, part=KV_SUB):
    return jnp.concatenate([vt_ref[lead + (piece + i,)] for i in range(part // V_PIECE)], axis=1)


def _query_blocks(qt_ref, tq):
    return [[qt_ref[0, h * HEAD_DIM:(h + 1) * HEAD_DIM, b * tq:(b + 1) * tq] for h in range(KV_GROUP)]
            for b in range(qt_ref.shape[2] // tq)]


def _chunk_items(k_ref, vt_ref, blocks, tk, part=KV_SUB):
    items = []
    for c in range(k_ref.shape[2] // tk):
        k = k_ref[0, 0, c * tk:(c + 1) * tk, :]
        vts = [_vt_part(vt_ref, (0, 0), (c * tk + j * part) // V_PIECE, part) for j in range(tk // part)]
        for b, heads in enumerate(blocks):
            items += [(b * KV_GROUP + h, k, heads[h], vts, None) for h in range(KV_GROUP)]
    return items


def _store_blocks(scr, o_ref, tq, n_blocks):
    for b in range(n_blocks):
        o_ref[0, b * tq:(b + 1) * tq, :] = _finish(scr, b * KV_GROUP).astype(o_ref.dtype)


def _full_body(qt_ref, kc_ref, vtc_ref, kl_ref, vtl_ref, o_ref, *scr, tq, tk):
    blocks = _query_blocks(qt_ref, tq)
    _init_states(scr)
    _pipeline(scr, _chunk_items(kc_ref, vtc_ref, blocks, tk) + _chunk_items(kl_ref, vtl_ref, blocks, tk),
              fine=False)
    _store_blocks(scr, o_ref, tq, len(blocks))


def _kv_specs(lk, index_map):
    return [pl.BlockSpec((1, 1, lk, HEAD_DIM), lambda *i: index_map(*i) + (0, 0)),
            pl.BlockSpec((1, 1, lk // V_PIECE, ACC_ROWS, V_PIECE), lambda *i: index_map(*i) + (0, 0, 0))]


def _attend_full(qt, k_ctx, vt_ctx, k_lat, vt_lat, tq, tk, n_blocks):
    b, d, l = qt.shape
    ts = tq * n_blocks
    by_group = lambda bi, g, qi: (bi, g)
    return pl.pallas_call(
        functools.partial(_full_body, tq=tq, tk=tk),
        grid=(b, N_KV_HEADS, l // ts),
        in_specs=([pl.BlockSpec((1, GROUP_COLS, ts), lambda bi, g, qi: (bi, g, qi))]
                  + _kv_specs(k_ctx.shape[2], by_group) + _kv_specs(l, by_group)),
        out_specs=pl.BlockSpec((1, ts, GROUP_COLS), lambda bi, g, qi: (bi, qi, g)),
        out_shape=jax.ShapeDtypeStruct((b, l, d), _BF16),
        scratch_shapes=_attn_scratch(KV_GROUP * n_blocks, tq, tk),
        compiler_params=_params(3),
        name="attn_full",
    )(qt, k_ctx, vt_ctx, k_lat, vt_lat)


def _band_start(q0, l, tq):
    return jnp.clip(q0 - WINDOW, 0, l - 2 * tq)


def _window_body(sink_ref, qt_ref, kc_ref, vtc_ref, kl_ref, vtl_ref, bias_ref, o_ref, *scr, tq):
    g = pl.program_id(1)
    blocks = _query_blocks(qt_ref, tq)
    _init_states(scr, [sink_ref[g * KV_GROUP + h] for _ in blocks for h in range(KV_GROUP)])
    items = _chunk_items(kc_ref, vtc_ref, blocks, 2 * tq)
    for b, heads in enumerate(blocks):
        q0 = (pl.program_id(2) * len(blocks) + b) * tq
        start = pl.multiple_of(_band_start(q0, kl_ref.shape[2], tq), V_PIECE)
        k_band = kl_ref[0, 0, pl.ds(start, 2 * tq), :]
        vt_band = [_vt_part(vtl_ref, (0, 0), start // V_PIECE + j * (KV_SUB // V_PIECE)) for j in range(2)]
        items += [(b * KV_GROUP + h, k_band, heads[h], vt_band, bias_ref[b]) for h in range(KV_GROUP)]
    _pipeline(scr, items, fine=True)
    _store_blocks(scr, o_ref, tq, len(blocks))


def _window_bias(l, tq):
    q0 = (jnp.arange(l // tq) * tq)[:, None, None]
    kpos = _band_start(q0, l, tq) + jnp.arange(2 * tq)[None, :, None]
    qpos = q0 + jnp.arange(tq)[None, None, :]
    return jnp.where(jnp.abs(qpos - kpos) <= WINDOW, 0.0, NEG_INF).astype(_F32)


def _attend_window(qt, k_ctx, vt_ctx, k_lat, vt_lat, sink, tq, n_blocks):
    b, d, l = qt.shape
    lc = k_ctx.shape[2]
    ts = tq * n_blocks
    assert tq == KV_SUB and tq >= 2 * WINDOW and lc == 2 * tq and l >= 2 * tq
    by_group = lambda bi, g, qi: (bi, g)
    return pl.pallas_call(
        functools.partial(_window_body, tq=tq),
        grid=(b, N_KV_HEADS, l // ts),
        in_specs=([pl.BlockSpec(memory_space=pltpu.SMEM),
                   pl.BlockSpec((1, GROUP_COLS, ts), lambda bi, g, qi: (bi, g, qi))]
                  + _kv_specs(lc, by_group) + _kv_specs(l, by_group)
                  + [pl.BlockSpec((n_blocks, 2 * tq, tq), lambda bi, g, qi: (qi, 0, 0))]),
        out_specs=pl.BlockSpec((1, ts, GROUP_COLS), lambda bi, g, qi: (bi, qi, g)),
        out_shape=jax.ShapeDtypeStruct((b, l, d), _BF16),
        scratch_shapes=_attn_scratch(KV_GROUP * n_blocks, tq, 2 * tq),
        compiler_params=_params(3),
        name="attn_window",
    )(sink, qt, k_ctx, vt_ctx, k_lat, vt_lat, _window_bias(l, tq))


def _ctx_body(*refs, has_sink):
    if has_sink:
        sink_ref, qt_ref, k_ref, vt_ref, o_ref = refs[:5]
    else:
        qt_ref, k_ref, vt_ref, o_ref = refs[:4]
    scr = refs[5:] if has_sink else refs[4:]
    n_sub = k_ref.shape[2] // KV_SUB
    _init_states(scr, [sink_ref[i] for i in range(N_HEADS)] if has_sink else None)
    items = []
    for g in range(N_KV_HEADS):
        heads = _heads(qt_ref, g)
        k = k_ref[0, g]
        vts = [_vt_part(vt_ref, (0, g), j * (KV_SUB // V_PIECE)) for j in range(n_sub)]
        items += [(g * KV_GROUP + h, k, heads[h], vts, None) for h in range(KV_GROUP)]
    _pipeline(scr, items, fine=True)
    for g in range(N_KV_HEADS):
        o_ref[0, :, g * GROUP_COLS:(g + 1) * GROUP_COLS] = _finish(scr, g * KV_GROUP).astype(o_ref.dtype)


def _attend_ctx(qt, k, vt, sink):
    b, d, l = qt.shape
    has_sink = sink is not None
    in_specs = [
        pl.BlockSpec((1, d, l), lambda bi: (bi, 0, 0)),
        pl.BlockSpec((1, N_KV_HEADS, l, HEAD_DIM), lambda bi: (bi, 0, 0, 0)),
        pl.BlockSpec((1, N_KV_HEADS, l // V_PIECE, ACC_ROWS, V_PIECE), lambda bi: (bi, 0, 0, 0, 0)),
    ]
    args = [qt, k, vt]
    if has_sink:
        in_specs = [pl.BlockSpec(memory_space=pltpu.SMEM)] + in_specs
        args = [sink] + args
    return pl.pallas_call(
        functools.partial(_ctx_body, has_sink=has_sink),
        grid=(b,),
        in_specs=in_specs,
        out_specs=pl.BlockSpec((1, l, d), lambda bi: (bi, 0, 0)),
        out_shape=jax.ShapeDtypeStruct((b, l, d), _BF16),
        scratch_shapes=_attn_scratch(N_HEADS, l, l),
        compiler_params=_params(1),
        name="attn_ctx_sink" if has_sink else "attn_ctx",
    )(*args)


def _cache_layout(cache_k, cache_v):
    b, depth, p, g, hd = cache_v.shape
    k = cache_k.transpose(1, 0, 3, 2, 4).astype(_BF16)
    vt = cache_v.reshape(b, depth, p // V_PIECE, V_PIECE, g, hd).transpose(1, 0, 4, 2, 5, 3).astype(_BF16)
    extra = jnp.zeros((ACC_ROWS - hd, V_PIECE), _BF16).at[0].set(1.0)
    return k, jnp.concatenate([vt, jnp.broadcast_to(extra, vt.shape[:4] + extra.shape)], axis=4)


def _token_tile(t, cap):
    tm = min(cap, t)
    assert t % tm == 0
    return tm


def kernel(x_prompt, x_sample, cache_k, cache_v, c, c_ctx, w_mod, b_mod, norm_g, w_qkv, w_o,
           q_norm_g, k_norm_g, sink, w_ffn_in, w_ffn_out):
    depth = w_mod.shape[0]
    d = D_MODEL
    bp, lp, _ = x_prompt.shape
    bs, ls, _ = x_sample.shape
    past = cache_k.shape[2]
    assert lp % KV_SUB == 0 and ls % KV_TILE == 0 and past % KV_TILE == 0

    w_in = w_ffn_in.astype(_BF16)
    w_out = w_ffn_out.astype(_BF16).reshape(depth, 2, N_FF_CHUNKS, FF_CHUNK, d)
    w_qkv_b = w_qkv.astype(_BF16)
    w_o_b = w_o.astype(_BF16)

    n_rows = -(-(bs + 1) // 8) * 8
    cond = jnp.concatenate([c, c_ctx[None, :], jnp.zeros((n_rows - bs - 1, d), _F32)], axis=0)
    mods = _modulation(cond, w_mod, b_mod)

    tp = bp * lp
    tm_p = _token_tile(tp, FFN_TILE)
    tq_p = _token_tile(lp, QKV_TILE)
    ctx_row = lambda i: bs
    qg_p = jnp.broadcast_to(q_norm_g[:, :, None], (depth, HEAD_DIM, tq_p))
    kg_p = jnp.broadcast_to(k_norm_g[:, :, None], (depth, HEAD_DIM, tq_p))
    h = x_prompt.reshape(tp, d)
    kv_f32 = (jnp.zeros((bp, depth, lp, KV_COLS), _F32), jnp.zeros((bp, depth, lp, KV_COLS), _F32))
    for i in range(depth):
        h = _ffn(h, mods, norm_g, w_in, w_out, i, 0, ctx_row, tm_p)
        qt, ka, vt, *kv_f32 = _qkv(h, mods, norm_g, w_qkv_b, qg_p, kg_p, i, ctx_row, tq_p, lp, kv_f32)
        o = _attend_ctx(qt, ka, vt, sink[i // 2] if i % 2 == 1 else None)
        h = _ffn(h, mods, norm_g, w_in, w_out, i, 1, ctx_row, tm_p, o=o.reshape(tp, d), w_o=w_o_b)
    y_prompt = h.reshape(bp, lp, d)
    new_k = kv_f32[0].reshape(bp, depth, lp, N_KV_HEADS, HEAD_DIM)
    new_v = kv_f32[1].reshape(bp, depth, lp, N_KV_HEADS, HEAD_DIM)

    ts = bs * ls
    tm_s = _token_tile(ls, FFN_TILE)
    tq_s = _token_tile(ls, QKV_TILE)
    lat_row = lambda i: i // (ls // tm_s)
    lat_row_q = lambda i: i // (ls // tq_s)
    rope_t = _rope_table_t(ls)
    qg_s = jnp.broadcast_to(q_norm_g[:, :, None], (depth, HEAD_DIM, tq_s))
    kg_s = jnp.broadcast_to(k_norm_g[:, :, None], (depth, HEAD_DIM, tq_s))
    k_ctx, vt_ctx = _cache_layout(cache_k, cache_v)
    blocks_per_step = lambda want: want if ls % (want * Q_TILE) == 0 else 1
    h = x_sample.reshape(ts, d)
    for i in range(depth):
        h = _ffn(h, mods, norm_g, w_in, w_out, i, 0, lat_row, tm_s)
        qt, ka, vt = _qkv(h, mods, norm_g, w_qkv_b, qg_s, kg_s, i, lat_row_q, tq_s, ls, None, rope_t)
        if i % 2 == 0:
            o = _attend_full(qt, k_ctx[i], vt_ctx[i], ka, vt, Q_TILE, KV_TILE, blocks_per_step(FULL_Q_BLOCKS))
        else:
            o = _attend_window(qt, k_ctx[i], vt_ctx[i], ka, vt, sink[i // 2], Q_TILE,
                               blocks_per_step(WINDOW_Q_BLOCKS))
        h = _ffn(h, mods, norm_g, w_in, w_out, i, 1, lat_row, tm_s, o=o.reshape(ts, d), w_o=w_o_b)
    y_sample = h.reshape(bs, ls, d)

    return (y_prompt, y_sample, new_k, new_v)
```
